```python
import jax, jax.numpy as jnp
from jax import lax
import numpy as np

D_MODEL = 1024
BATCH = 8
SEQ = 2048
DEPTH = 1

HEAD_DIM = 64
FOX_HEADS = 8
DIL_GROUPS = ((128, 1), (512, 4), (2048, 16))
DIL_HEADS_PER_GROUP = 4
DIL_HEADS = DIL_HEADS_PER_GROUP * len(DIL_GROUPS)
D_FF = 4 * D_MODEL
ROPE_THETA = 500000.0
ROPE_DIM = HEAD_DIM // 4
Q_BLOCK = 128
EPS = 1e-6
NEG_INF = -1e30
FOX_W = FOX_HEADS * HEAD_DIM
DIL_W = DIL_HEADS * HEAD_DIM
DIL_OUT_W = DIL_HEADS_PER_GROUP * HEAD_DIM
IN_SPLIT_SIZES = (FOX_W, FOX_W, FOX_W, FOX_HEADS, DIL_W, DIL_W, DIL_W, D_MODEL, D_MODEL)
D_IN = sum(IN_SPLIT_SIZES)

kernel_name = 'hybrid_fox_dilated_gated_block'


def _rms_norm(x, g):
    xf = x.astype(jnp.float32)
    y = xf * lax.rsqrt(jnp.mean(xf * xf, axis=-1, keepdims=True) + EPS)
    return (y * g.astype(jnp.float32)).astype(x.dtype)


def _partial_rope(x, positions):
    half = ROPE_DIM // 2
    inv_freq = jnp.power(jnp.float32(ROPE_THETA), -jnp.arange(half, dtype=jnp.float32) * 2.0 / ROPE_DIM)
    ang = positions.astype(jnp.float32)[:, None] * inv_freq[None, :]
    cos = jnp.cos(ang)[None, :, None, :]
    sin = jnp.sin(ang)[None, :, None, :]
    xf = x.astype(jnp.float32)
    x1 = xf[..., :half]
    x2 = xf[..., half:ROPE_DIM]
    out = jnp.concatenate([x1 * cos - x2 * sin, x2 * cos + x1 * sin, xf[..., ROPE_DIM:]], axis=-1)
    return out.astype(x.dtype)


def _fox_attention(q, k, v, log_f):
    B, S, H, D = q.shape
    nb = S // Q_BLOCK
    scale = D ** -0.5
    F = jnp.cumsum(log_f, axis=1)
    F_k = jnp.transpose(F, (0, 2, 1))[:, :, None, :]
    k_pos = jnp.arange(S)
    q_blocks = jnp.swapaxes(q.reshape(B, nb, Q_BLOCK, H, D), 0, 1)
    F_blocks = jnp.swapaxes(F.reshape(B, nb, Q_BLOCK, H), 0, 1)

    def one_block(args):
        q_blk, F_blk, i = args
        s = jnp.einsum('bqhd,bkhd->bhqk', q_blk, k, preferred_element_type=jnp.float32) * scale
        s = s + jnp.transpose(F_blk, (0, 2, 1))[..., None] - F_k
        q_pos = i * Q_BLOCK + jnp.arange(Q_BLOCK)
        mask = k_pos[None, :] <= q_pos[:, None]
        s = jnp.where(mask[None, None], s, NEG_INF)
        p = jax.nn.softmax(s, axis=-1)
        return jnp.einsum('bhqk,bkhd->bqhd', p.astype(v.dtype), v)

    out = lax.map(one_block, (q_blocks, F_blocks, jnp.arange(nb)))
    return jnp.swapaxes(out, 0, 1).reshape(B, S, H, D)


def _dilated_group(q, k, v, window, dilation):
    B, S, H, D = q.shape
    L = S // dilation
    W = window // dilation
    blk = W
    Lp = -(-L // blk) * blk
    nb = Lp // blk
    N = B * dilation
    scale = D ** -0.5

    def to_sub(t):
        t = jnp.transpose(t.reshape(B, L, dilation, H, D), (0, 2, 1, 3, 4)).reshape(N, L, H, D)
        t = jnp.pad(t, ((0, 0), (0, Lp - L), (0, 0), (0, 0)))
        return t.reshape(N, nb, blk, H, D)

    def with_prev(t):
        prev = jnp.pad(t[:, :-1], ((0, 0), (1, 0), (0, 0), (0, 0), (0, 0)))
        return jnp.concatenate([prev, t], axis=2)

    qs = to_sub(q)
    k2 = with_prev(to_sub(k))
    v2 = with_prev(to_sub(v))
    s = jnp.einsum('nbqhd,nbkhd->nbhqk', qs, k2, preferred_element_type=jnp.float32) * scale
    a = jnp.arange(blk)
    kk = jnp.arange(2 * blk)
    bidx = jnp.arange(nb)
    diff = blk + a[:, None] - kk[None, :]
    key_idx = bidx[:, None] * blk - blk + kk[None, :]
    mask = ((diff >= 0) & (diff <= W))[None, :, :] & (key_idx >= 0)[:, None, :]
    s = jnp.where(mask[None, :, None], s, NEG_INF)
    m = jnp.max(s, axis=-1, keepdims=True)
    p = jnp.exp(s - m)
    denom = jnp.sum(p, axis=-1, keepdims=True)
    lse = (m + jnp.log(denom))[..., 0]
    out = jnp.einsum('nbhqk,nbkhd->nbqhd', (p / denom).astype(v.dtype), v2)
    out = out.reshape(N, Lp, H, D)[:, :L]
    out = jnp.transpose(out.reshape(B, dilation, L, H, D), (0, 2, 1, 3, 4)).reshape(B, S, H, D)
    lse = jnp.transpose(lse, (0, 1, 3, 2)).reshape(N, Lp, H)[:, :L]
    lse = jnp.transpose(lse.reshape(B, dilation, L, H), (0, 2, 1, 3)).reshape(B, S, H)
    return out, lse


def _dilated_mixture(q, k, v):
    outs, lses = [], []
    for g, (window, dilation) in enumerate(DIL_GROUPS):
        sl = slice(g * DIL_HEADS_PER_GROUP, (g + 1) * DIL_HEADS_PER_GROUP)
        o, lse = _dilated_group(q[:, :, sl], k[:, :, sl], v[:, :, sl], window, dilation)
        outs.append(o)
        lses.append(lse)
    outs = jnp.stack(outs, axis=0)
    alpha = jax.nn.softmax(jnp.stack(lses, axis=0), axis=0)
    return jnp.sum(alpha[..., None].astype(outs.dtype) * outs, axis=0)


def setup_inputs(seed: int = 0) -> dict:
    key = jax.random.key(seed)
    ks = jax.random.split(key, 12)
    f32 = jnp.float32

    def nrm(k, shape, fan_in):
        return jax.random.normal(k, shape, f32) * (fan_in ** -0.5)

    return {
        'x': jax.random.normal(ks[0], (BATCH, SEQ, D_MODEL), f32),
        'norm_attn_g': 1.0 + 0.02 * jax.random.normal(ks[1], (DEPTH, D_MODEL), f32),
        'w_in': nrm(ks[2], (DEPTH, D_MODEL, D_IN), D_MODEL),
        'b_forget': 2.0 + 0.5 * jax.random.normal(ks[3], (DEPTH, FOX_HEADS), f32),
        'w_branch_a': nrm(ks[4], (DEPTH, FOX_W, D_MODEL), FOX_W),
        'w_branch_b': nrm(ks[5], (DEPTH, DIL_OUT_W, D_MODEL), DIL_OUT_W),
        'w_out': nrm(ks[6], (DEPTH, D_MODEL, D_MODEL), D_MODEL),
        'norm_mlp_g': 1.0 + 0.02 * jax.random.normal(ks[7], (DEPTH, D_MODEL), f32),
        'w_up': nrm(ks[8], (DEPTH, D_MODEL, D_FF), D_MODEL),
        'w_down': nrm(ks[9], (DEPTH, D_FF, D_MODEL), D_FF),
        'norm_final_g': 1.0 + 0.02 * jax.random.normal(ks[10], (D_MODEL,), f32),
    }


def reference(x, norm_attn_g, w_in, b_forget, w_branch_a, w_branch_b, w_out,
              norm_mlp_g, w_up, w_down, norm_final_g):
    B, S, _ = x.shape
    positions = jnp.arange(S)
    split_points = [int(v) for v in np.cumsum(IN_SPLIT_SIZES)[:-1]]
    for l in range(DEPTH):
        h = _rms_norm(x, norm_attn_g[l])
        proj = h @ w_in[l]
        qa, ka, va, fa, qb, kb, vb, ga, gb = jnp.split(proj, split_points, axis=-1)
        log_f = jax.nn.log_sigmoid(fa.astype(jnp.float32) + b_forget[l].astype(jnp.float32))
        oa = _fox_attention(qa.reshape(B, S, FOX_HEADS, HEAD_DIM),
                            ka.reshape(B, S, FOX_HEADS, HEAD_DIM),
                            va.reshape(B, S, FOX_HEADS, HEAD_DIM), log_f)
        qb = _partial_rope(qb.reshape(B, S, DIL_HEADS, HEAD_DIM), positions)
        kb = _partial_rope(kb.reshape(B, S, DIL_HEADS, HEAD_DIM), positions)
        ob = _dilated_mixture(qb, kb, vb.reshape(B, S, DIL_HEADS, HEAD_DIM))
        ya = oa.reshape(B, S, FOX_W) @ w_branch_a[l]
        yb = ob.reshape(B, S, DIL_OUT_W) @ w_branch_b[l]
        mixed = jax.nn.sigmoid(ga) * ya + jax.nn.sigmoid(gb) * yb
        x = x + mixed @ w_out[l]
        h = _rms_norm(x, norm_mlp_g[l])
        x = x + jnp.square(jax.nn.relu(h @ w_up[l])) @ w_down[l]
    return _rms_norm(x, norm_final_g)
```

```python
import functools

import numpy as np
import jax
import jax.numpy as jnp
from jax import lax
from jax.experimental import pallas as pl
from jax.experimental.pallas import tpu as pltpu

F32 = jnp.float32
BF16 = jnp.bfloat16

HEAD_DIM = 64
FOX_HEADS = 8
FOX_W = FOX_HEADS * HEAD_DIM
DIL_GROUPS = ((128, 1), (512, 4), (2048, 16))
DIL_HEADS_PER_GROUP = 4
GROUP_W = DIL_HEADS_PER_GROUP * HEAD_DIM
N_GROUPS = len(DIL_GROUPS)
DIL_W = N_GROUPS * GROUP_W
ROPE_THETA = 500000.0
ROPE_DIM = HEAD_DIM // 4
ROPE_HALF = ROPE_DIM // 2
EPS = 1e-6
NEG_INF = -1e30
Q_SCALE = HEAD_DIM ** -0.5

DIL_BLK = 128
TM_IN = 512
TM_OUT = 512
TQ = 256
TK = 256
FF_CHUNK = 1024
VMEM_LIMIT = 56 * 1024 * 1024

_NT = (((1,), (1,)), ((), ()))


def _resident(shape):
    nd = len(shape)
    return pl.BlockSpec(shape, lambda *_: (0,) * nd, pipeline_mode=pl.Buffered(1))


def _rms(x, g):
    ms = jnp.mean(x * x, axis=-1, keepdims=True)
    return x * lax.rsqrt(ms + EPS) * g


def _rope(x, cos, sin_lo, sin_hi):
    w = x.shape[1]
    up = pltpu.roll(x, w - ROPE_HALF, 1)
    dn = pltpu.roll(x, ROPE_HALF, 1)
    return x * cos + up * sin_lo + dn * sin_hi


def _inproj_kernel(x_ref, g_ref, wfox_ref, wft_ref, bf_ref, wd_ref, wg_ref,
                   cos_ref, slo_ref, shi_ref,
                   fox_ref, lf_ref, d0_ref, d1_ref, d2_ref, gate_ref, perm_scr):
    tm = x_ref.shape[1]
    h = _rms(x_ref[0], g_ref[...]).astype(BF16)

    for c in range(3):
        r = jnp.dot(h, wfox_ref[:, c * FOX_W:(c + 1) * FOX_W], preferred_element_type=F32)
        if c == 0:
            r = r * Q_SCALE
        fox_ref[0, :, c * FOX_W:(c + 1) * FOX_W] = r.astype(BF16)

    z = lax.dot_general(wft_ref[...], h, _NT, preferred_element_type=F32) + bf_ref[...]
    lf_ref[0] = -(jnp.maximum(-z, 0.0) + jnp.log1p(jnp.exp(-jnp.abs(z))))

    cos = cos_ref[...]
    slo = slo_ref[...]
    shi = shi_ref[...]
    outs = (d0_ref, d1_ref, d2_ref)
    for g, (_, dil) in enumerate(DIL_GROUPS):
        r = jnp.dot(h, wd_ref[g], preferred_element_type=F32)
        q = _rope(r[:, 0:GROUP_W], cos, slo, shi) * Q_SCALE
        k = _rope(r[:, GROUP_W:2 * GROUP_W], cos, slo, shi)
        v = r[:, 2 * GROUP_W:3 * GROUP_W]
        if dil == 1:
            outs[g][0, :, 0:GROUP_W] = q.astype(BF16)
            outs[g][0, :, GROUP_W:2 * GROUP_W] = k.astype(BF16)
            outs[g][0, :, 2 * GROUP_W:3 * GROUP_W] = v.astype(BF16)
        else:
            for c, t in enumerate((q, k, v)):
                perm_scr[2 * c] = t[:, 0:128]
                perm_scr[2 * c + 1] = t[:, 128:256]
            for res in range(dil):
                for c in range(6):
                    outs[g][0, res, :, c * 128:(c + 1) * 128] = (
                        perm_scr[c, pl.ds(res, tm // dil, stride=dil), :].astype(BF16))

    n_gate = wg_ref.shape[1]
    for c in range(n_gate // 512):
        r = jnp.dot(h, wg_ref[:, c * 512:(c + 1) * 512], preferred_element_type=F32)
        gate_ref[0, :, c * 512:(c + 1) * 512] = jax.nn.sigmoid(r).astype(BF16)


def _inproj(x, g, w_fox, w_ft, b_f, w_d, w_g, cos, slo, shi):
    B, S, D = x.shape
    tm = TM_IN
    grid = (B, S // tm)
    tok = lambda w: pl.BlockSpec((1, tm, w), lambda b, i: (b, i, 0))
    tab = pl.BlockSpec((tm, GROUP_W), lambda b, i: (i, 0))
    out_shape = (
        jax.ShapeDtypeStruct((B, S, 3 * FOX_W), BF16),
        jax.ShapeDtypeStruct((B, FOX_HEADS, S), F32),
        jax.ShapeDtypeStruct((B, S, 3 * GROUP_W), BF16),
        jax.ShapeDtypeStruct((B, 4, S // 4, 3 * GROUP_W), BF16),
        jax.ShapeDtypeStruct((B, 16, S // 16, 3 * GROUP_W), BF16),
        jax.ShapeDtypeStruct((B, S, 2 * D), BF16),
    )
    out_specs = (
        tok(3 * FOX_W),
        pl.BlockSpec((1, FOX_HEADS, tm), lambda b, i: (b, 0, i)),
        tok(3 * GROUP_W),
        pl.BlockSpec((1, 4, tm // 4, 3 * GROUP_W), lambda b, i: (b, 0, i, 0)),
        pl.BlockSpec((1, 16, tm // 16, 3 * GROUP_W), lambda b, i: (b, 0, i, 0)),
        tok(2 * D),
    )
    in_specs = [
        tok(D), _resident(g.shape), _resident(w_fox.shape), _resident(w_ft.shape),
        _resident(b_f.shape), _resident(w_d.shape), _resident(w_g.shape), tab, tab, tab,
    ]
    return pl.pallas_call(
        _inproj_kernel,
        grid=grid,
        in_specs=in_specs,
        out_specs=out_specs,
        out_shape=out_shape,
        scratch_shapes=[pltpu.VMEM((6, tm, 128), F32)],
        compiler_params=pltpu.CompilerParams(
            dimension_semantics=("arbitrary", "arbitrary"), vmem_limit_bytes=VMEM_LIMIT),
        name="inproj",
    )(x, g, w_fox, w_ft, b_f, w_d, w_g, cos, slo, shi)


def _fox_kernel(q_ref, k_ref, v_ref, lf_ref, o_ref, nf_scr):
    S = q_ref.shape[1]
    hp = pl.program_id(1)

    @pl.when(hp == 0)
    def _cumulative_forget():
        row = lax.broadcasted_iota(jnp.int32, (128, 128), 0)
        col = lax.broadcasted_iota(jnp.int32, (128, 128), 1)
        tri = jnp.where(row <= col, 1.0, 0.0).astype(BF16)
        carry = jnp.zeros((FOX_HEADS, 1), F32)
        for c in range(S // 128):
            xc = lf_ref[0, :, c * 128:(c + 1) * 128]
            hi = xc.astype(BF16)
            r1 = xc - hi.astype(F32)
            mid = r1.astype(BF16)
            lo = (r1 - mid.astype(F32)).astype(BF16)
            loc = (jnp.dot(hi, tri, preferred_element_type=F32)
                   + jnp.dot(mid, tri, preferred_element_type=F32)
                   + jnp.dot(lo, tri, preferred_element_type=F32))
            fc = loc + carry
            nf_scr[:, c * 128:(c + 1) * 128] = -fc
            carry = fc[:, 127:128]

    lane = lax.broadcasted_iota(jnp.int32, (1, 2 * HEAD_DIM), 1)
    first = lane < HEAD_DIM
    rows = lax.broadcasted_iota(jnp.int32, (TQ, TK), 0)
    cols = lax.broadcasted_iota(jnp.int32, (TQ, TK), 1)
    causal = cols <= rows

    def q_block(qi, _):
        q0 = pl.multiple_of(qi * TQ, TQ)
        q2 = q_ref[0, pl.ds(q0, TQ), :]
        zero = jnp.zeros_like(q2)
        outs = []
        for hh in range(2):
            qh = jnp.where(first, q2, zero) if hh == 0 else jnp.where(first, zero, q2)
            hrow = hp * 2 + hh

            def kv_step(kj, carry, masked):
                m, l, acc = carry
                k0 = pl.multiple_of(kj * TK, TK)
                k2 = k_ref[0, pl.ds(k0, TK), :]
                v2 = v_ref[0, pl.ds(k0, TK), :]
                s = lax.dot_general(qh, k2, _NT, preferred_element_type=F32)
                s = s + nf_scr[pl.ds(hrow, 1), pl.ds(k0, TK)]
                if masked:
                    s = jnp.where(causal, s, NEG_INF)
                m_new = jnp.maximum(m, jnp.max(s, axis=1, keepdims=True))
                alpha = jnp.exp(m - m_new)
                p = jnp.exp(s - m_new)
                l = alpha * l + jnp.sum(p, axis=1, keepdims=True)
                acc = alpha * acc + jnp.dot(p.astype(BF16), v2, preferred_element_type=F32)
                return m_new, l, acc

            init = (jnp.full((TQ, 1), NEG_INF, F32), jnp.zeros((TQ, 1), F32),
                    jnp.zeros((TQ, 2 * HEAD_DIM), F32))
            carry = lax.fori_loop(0, qi, functools.partial(kv_step, masked=False), init)
            _, l, acc = kv_step(qi, carry, True)
            outs.append(acc / l)
        o_ref[0, pl.ds(q0, TQ), :] = jnp.where(first, outs[0], outs[1]).astype(BF16)
        return 0

    lax.fori_loop(0, S // TQ, q_block, 0)


def _fox(qkv, lf):
    B, S, _ = qkv.shape
    n_pairs = FOX_HEADS // 2
    pw = 2 * HEAD_DIM
    spec = lambda off: pl.BlockSpec((1, S, pw), lambda b, p: (b, 0, off + p))
    return pl.pallas_call(
        _fox_kernel,
        grid=(B, n_pairs),
        in_specs=[spec(0), spec(n_pairs), spec(2 * n_pairs),
                  pl.BlockSpec((1, FOX_HEADS, S), lambda b, p: (b, 0, 0))],
        out_specs=pl.BlockSpec((1, S, pw), lambda b, p: (b, 0, p)),
        out_shape=jax.ShapeDtypeStruct((B, S, FOX_W), BF16),
        scratch_shapes=[pltpu.VMEM((FOX_HEADS, S), F32)],
        compiler_params=pltpu.CompilerParams(
            dimension_semantics=("arbitrary", "arbitrary"), vmem_limit_bytes=VMEM_LIMIT),
        name="fox",
    )(qkv, qkv, qkv, lf)


def _dil_kernel(d0_ref, d1_ref, d2_ref, o_ref, acc_scr, m_scr, l_scr):
    S = d0_ref.shape[1]
    nh = DIL_HEADS_PER_GROUP
    blk = DIL_BLK
    n_blocks = S // blk
    lane_head = lax.broadcasted_iota(jnp.int32, (1, GROUP_W), 1) // HEAD_DIM
    a2 = lax.broadcasted_iota(jnp.int32, (nh * blk, 2 * blk), 0) % blk
    k2 = lax.broadcasted_iota(jnp.int32, (nh * blk, 2 * blk), 1)
    bias_two = jnp.where((k2 >= a2) & (k2 <= a2 + blk), 0.0, NEG_INF)
    prev_cols = k2 < blk
    a1 = lax.broadcasted_iota(jnp.int32, (nh * blk, blk), 0) % blk
    k1 = lax.broadcasted_iota(jnp.int32, (nh * blk, blk), 1)
    bias_one = jnp.where(k1 <= a1, 0.0, NEG_INF)

    def assemble(x):
        out = x[(nh - 1) * blk:nh * blk]
        for h in range(nh - 2, -1, -1):
            out = jnp.where(lane_head == h, x[h * blk:(h + 1) * blk], out)
        return out

    refs = (d0_ref, d1_ref, d2_ref)
    for g, (_, dil) in enumerate(DIL_GROUPS):
        ref = refs[g]
        nbs = n_blocks // dil

        def block(n, _, ref=ref, dil=dil, nbs=nbs, g=g):
            r0 = pl.multiple_of(n * blk, blk)
            qs = ref[0, pl.ds(r0, blk), 0:GROUP_W]
            zero = jnp.zeros_like(qs)
            qstack = jnp.concatenate(
                [jnp.where(lane_head == h, qs, zero) for h in range(nh)], axis=0)
            kcur = ref[0, pl.ds(r0, blk), GROUP_W:2 * GROUP_W]
            vcur = ref[0, pl.ds(r0, blk), 2 * GROUP_W:3 * GROUP_W]
            pos = n % nbs
            if nbs > 1:
                rp = pl.multiple_of(jnp.maximum(n - 1, 0) * blk, blk)
                kcat = jnp.concatenate([ref[0, pl.ds(rp, blk), GROUP_W:2 * GROUP_W], kcur], axis=0)
                vcat = jnp.concatenate([ref[0, pl.ds(rp, blk), 2 * GROUP_W:3 * GROUP_W], vcur], axis=0)
                s = lax.dot_general(qstack, kcat, _NT, preferred_element_type=F32) + bias_two
                no_prev = jnp.where(pos == 0, NEG_INF, 0.0)
                s = s + jnp.where(prev_cols, no_prev, 0.0)
            else:
                vcat = vcur
                s = lax.dot_general(qstack, kcur, _NT, preferred_element_type=F32) + bias_one
            m = jnp.max(s, axis=1, keepdims=True)
            p = jnp.exp(s - m)
            l = jnp.sum(p, axis=1, keepdims=True)
            pv = jnp.dot(p.astype(BF16), vcat, preferred_element_type=F32)
            acc_b = assemble(pv)
            m_b = assemble(jnp.broadcast_to(m, pv.shape))
            l_b = assemble(jnp.broadcast_to(l, pv.shape))
            if dil == 1:
                idx = pl.ds(r0, blk)
            else:
                idx = pl.ds(dil * pos * blk + n // nbs, blk, stride=dil)
            for c in range(GROUP_W // 128):
                lanes = slice(c * 128, (c + 1) * 128)
                if g == 0:
                    acc_scr[c, idx, :] = acc_b[:, lanes]
                    m_scr[c, idx, :] = m_b[:, lanes]
                    l_scr[c, idx, :] = l_b[:, lanes]
                else:
                    m_o = m_scr[c, idx, :]
                    m_n = jnp.maximum(m_o, m_b[:, lanes])
                    e_o = jnp.exp(m_o - m_n)
                    e_b = jnp.exp(m_b[:, lanes] - m_n)
                    acc_scr[c, idx, :] = acc_scr[c, idx, :] * e_o + acc_b[:, lanes] * e_b
                    l_scr[c, idx, :] = l_scr[c, idx, :] * e_o + l_b[:, lanes] * e_b
                    m_scr[c, idx, :] = m_n
            return 0

        lax.fori_loop(0, n_blocks, block, 0)

    def finish(i, _):
        r0 = pl.multiple_of(i * blk, blk)
        for c in range(GROUP_W // 128):
            o_ref[0, pl.ds(r0, blk), c * 128:(c + 1) * 128] = (
                acc_scr[c, pl.ds(r0, blk), :] / l_scr[c, pl.ds(r0, blk), :]).astype(BF16)
        return 0

    lax.fori_loop(0, n_blocks, finish, 0)


def _dilated(d0, d1, d2):
    B, S, W = d0.shape
    spec = pl.BlockSpec((1, S, W), lambda b: (b, 0, 0))
    return pl.pallas_call(
        _dil_kernel,
        grid=(B,),
        in_specs=[spec, spec, spec],
        out_specs=pl.BlockSpec((1, S, GROUP_W), lambda b: (b, 0, 0)),
        out_shape=jax.ShapeDtypeStruct((B, S, GROUP_W), BF16),
        scratch_shapes=[pltpu.VMEM((GROUP_W // 128, S, 128), F32)] * 3,
        compiler_params=pltpu.CompilerParams(
            dimension_semantics=("arbitrary",), vmem_limit_bytes=VMEM_LIMIT),
        name="dilated",
    )(d0, d1, d2)


def _outmlp_kernel(x_ref, oa_ref, ob_ref, gate_ref, wa_ref, wb_ref, wo_ref, g2_ref,
                   wu_ref, wdn_ref, g3_ref, o_ref, *, final_norm):
    d = x_ref.shape[1]
    ya = jnp.dot(oa_ref[...], wa_ref[...], preferred_element_type=F32)
    yb = jnp.dot(ob_ref[...], wb_ref[...], preferred_element_type=F32)
    mixed = gate_ref[:, 0:d].astype(F32) * ya + gate_ref[:, d:2 * d].astype(F32) * yb
    x1 = x_ref[...] + jnp.dot(mixed.astype(BF16), wo_ref[...], preferred_element_type=F32)
    h = _rms(x1, g2_ref[...]).astype(BF16)
    x2 = x1
    d_ff = wu_ref.shape[1]
    for c in range(d_ff // FF_CHUNK):
        u = jnp.dot(h, wu_ref[:, c * FF_CHUNK:(c + 1) * FF_CHUNK], preferred_element_type=F32)
        a = jnp.square(jnp.maximum(u, 0.0)).astype(BF16)
        x2 = x2 + jnp.dot(a, wdn_ref[c * FF_CHUNK:(c + 1) * FF_CHUNK, :], preferred_element_type=F32)
    o_ref[...] = _rms(x2, g3_ref[...]) if final_norm else x2


def _outmlp(x, oa, ob, gates, w_a, w_b, w_o, g2, w_u, w_dn, g3, final_norm):
    N, D = x.shape
    tm = TM_OUT
    tok = lambda w: pl.BlockSpec((tm, w), lambda i: (i, 0))
    return pl.pallas_call(
        functools.partial(_outmlp_kernel, final_norm=final_norm),
        grid=(N // tm,),
        in_specs=[tok(D), tok(oa.shape[1]), tok(ob.shape[1]), tok(gates.shape[1]),
                  _resident(w_a.shape), _resident(w_b.shape), _resident(w_o.shape),
                  _resident(g2.shape), _resident(w_u.shape), _resident(w_dn.shape),
                  _resident(g3.shape)],
        out_specs=tok(D),
        out_shape=jax.ShapeDtypeStruct((N, D), F32),
        compiler_params=pltpu.CompilerParams(
            dimension_semantics=("arbitrary",), vmem_limit_bytes=VMEM_LIMIT),
        name="outmlp",
    )(x, oa, ob, gates, w_a, w_b, w_o, g2, w_u, w_dn, g3)


def _rope_tables(S):
    inv_freq = jnp.power(jnp.float32(ROPE_THETA),
                         -jnp.arange(ROPE_HALF, dtype=F32) * 2.0 / ROPE_DIM)
    ang = jnp.arange(S).astype(F32)[:, None] * inv_freq[None, :]
    cos, sin = jnp.cos(ang), jnp.sin(ang)
    rest = HEAD_DIM - ROPE_DIM
    one = jnp.ones((S, rest), F32)
    zero = lambda w: jnp.zeros((S, w), F32)
    cos_h = jnp.concatenate([cos, cos, one], axis=1)
    slo_h = jnp.concatenate([-sin, zero(ROPE_HALF + rest)], axis=1)
    shi_h = jnp.concatenate([zero(ROPE_HALF), sin, zero(rest)], axis=1)
    rep = lambda t: jnp.tile(t, (1, DIL_HEADS_PER_GROUP))
    return rep(cos_h), rep(slo_h), rep(shi_h)


def kernel(x, norm_attn_g, w_in, b_forget, w_branch_a, w_branch_b, w_out, norm_mlp_g, w_up,
           w_down, norm_final_g):
    B, S, D = x.shape
    depth = w_in.shape[0]
    assert S % (16 * DIL_BLK) == 0 and S // 16 == DIL_BLK, "sequence length fixed by the dilation groups"
    cos, slo, shi = _rope_tables(S)
    sizes = (FOX_W, FOX_W, FOX_W, FOX_HEADS, DIL_W, DIL_W, DIL_W, D, D)
    offs = np.concatenate([[0], np.cumsum(sizes)])
    col = lambda w, i: w[:, offs[i]:offs[i + 1]]
    for l in range(depth):
        w = w_in[l]
        w_fox = w[:, :offs[3]].astype(BF16)
        w_ft = col(w, 3).T.astype(BF16)
        qb, kb, vb = col(w, 4), col(w, 5), col(w, 6)
        grp = lambda t, g: t[:, g * GROUP_W:(g + 1) * GROUP_W]
        w_d = jnp.stack([jnp.concatenate([grp(qb, g), grp(kb, g), grp(vb, g)], axis=1)
                         for g in range(N_GROUPS)]).astype(BF16)
        w_g = w[:, offs[7]:].astype(BF16)
        b_f = b_forget[l].astype(F32)[:, None]

        qkv, lf, d0, d1, d2, gates = _inproj(
            x, norm_attn_g[l][None, :], w_fox, w_ft, b_f, w_d, w_g, cos, slo, shi)
        oa = _fox(qkv, lf)
        ob = _dilated(d0, d1.reshape(B, S, -1), d2.reshape(B, S, -1))
        y = _outmlp(
            x.reshape(B * S, D), oa.reshape(B * S, -1), ob.reshape(B * S, -1),
            gates.reshape(B * S, -1),
            w_branch_a[l].astype(BF16), w_branch_b[l].astype(BF16), w_out[l].astype(BF16),
            norm_mlp_g[l][None, :], w_up[l].astype(BF16), w_down[l].astype(BF16),
            norm_final_g[None, :], final_norm=(l == depth - 1))
        x = y.reshape(B, S, D)
    return x
```

```python
import functools

import numpy as np
import jax
import jax.numpy as jnp
from jax import lax
from jax.experimental import pallas as pl
from jax.experimental.pallas import tpu as pltpu

F32 = jnp.float32
BF16 = jnp.bfloat16

HEAD_DIM = 64
FOX_HEADS = 8
FOX_W = FOX_HEADS * HEAD_DIM
DIL_GROUPS = ((128, 1), (512, 4), (2048, 16))
DIL_HEADS_PER_GROUP = 4
GROUP_W = DIL_HEADS_PER_GROUP * HEAD_DIM
N_GROUPS = len(DIL_GROUPS)
DIL_W = N_GROUPS * GROUP_W
ROPE_THETA = 500000.0
ROPE_DIM = HEAD_DIM // 4
ROPE_HALF = ROPE_DIM // 2
EPS = 1e-6
NEG_INF = -1e30
Q_SCALE = HEAD_DIM ** -0.5
LOG2E = 1.4426950408889634

DIL_BLK = 128
TM_IN = 512
TM_OUT = 512
TQ = 256
TK = 256
FF_CHUNK = 1024
VMEM_LIMIT = 56 * 1024 * 1024

_NT = (((1,), (1,)), ((), ()))


def _resident(shape):
    nd = len(shape)
    return pl.BlockSpec(shape, lambda *_: (0,) * nd, pipeline_mode=pl.Buffered(1))


def _rms(x, g):
    ms = jnp.mean(x * x, axis=-1, keepdims=True)
    return x * lax.rsqrt(ms + EPS) * g


def _rope(x, cos, sin_lo, sin_hi):
    w = x.shape[1]
    up = pltpu.roll(x, w - ROPE_HALF, 1)
    dn = pltpu.roll(x, ROPE_HALF, 1)
    return x * cos + up * sin_lo + dn * sin_hi


def _split3(x):
    hi = x.astype(BF16)
    r1 = x - hi.astype(F32)
    mid = r1.astype(BF16)
    lo = (r1 - mid.astype(F32)).astype(BF16)
    return hi, mid, lo


def _inproj_kernel(x_ref, g_ref, wqk_ref, wvt_ref, wf_ref, bf_ref, wd_ref, wg_ref,
                   cos_ref, slo_ref, shi_ref,
                   qk_ref, vt_ref, kx_ref, d0_ref, d1_ref, d2_ref, gate_ref,
                   perm_scr, carry_scr):
    tm = x_ref.shape[1]
    h = _rms(x_ref[0], g_ref[...]).astype(BF16)

    for c in range(2):
        r = jnp.dot(h, wqk_ref[:, c * FOX_W:(c + 1) * FOX_W], preferred_element_type=F32)
        if c == 0:
            r = r * (Q_SCALE * LOG2E)
        qk_ref[0, :, c * FOX_W:(c + 1) * FOX_W] = r.astype(BF16)
    vt_ref[0] = lax.dot_general(wvt_ref[...], h, _NT, preferred_element_type=F32).astype(BF16)

    z = jnp.dot(h, wf_ref[...], preferred_element_type=F32) + bf_ref[...]
    lane = lax.broadcasted_iota(jnp.int32, (1, 128), 1)
    log_f = jnp.where(lane < 3 * FOX_HEADS,
                      -(jnp.maximum(-z, 0.0) + jnp.log1p(jnp.exp(-jnp.abs(z)))), 0.0)

    @pl.when(pl.program_id(1) == 0)
    def _reset():
        carry_scr[...] = jnp.zeros_like(carry_scr)

    row = lax.broadcasted_iota(jnp.int32, (tm, tm), 0)
    col = lax.broadcasted_iota(jnp.int32, (tm, tm), 1)
    tri = jnp.where(col <= row, 1.0, 0.0).astype(BF16)
    cum = carry_scr[0:1, :]
    for part in _split3(log_f):
        cum = cum + jnp.dot(tri, part, preferred_element_type=F32)
    carry_scr[0:1, :] = cum[tm - 1:tm, :]
    hi, mid, lo = _split3(-LOG2E * cum)
    kx_ref[0] = jnp.where(lane % 3 == 0, hi, jnp.where(lane % 3 == 1, mid, lo))

    cos = cos_ref[...]
    slo = slo_ref[...]
    shi = shi_ref[...]
    outs = (d0_ref, d1_ref, d2_ref)
    for g, (_, dil) in enumerate(DIL_GROUPS):
        r = jnp.dot(h, wd_ref[g], preferred_element_type=F32)
        q = _rope(r[:, 0:GROUP_W], cos, slo, shi) * Q_SCALE
        k = _rope(r[:, GROUP_W:2 * GROUP_W], cos, slo, shi)
        v = r[:, 2 * GROUP_W:3 * GROUP_W]
        if dil == 1:
            outs[g][0, :, 0:GROUP_W] = q.astype(BF16)
            outs[g][0, :, GROUP_W:2 * GROUP_W] = k.astype(BF16)
            outs[g][0, :, 2 * GROUP_W:3 * GROUP_W] = v.astype(BF16)
        else:
            for c, t in enumerate((q, k, v)):
                perm_scr[2 * c] = t[:, 0:128]
                perm_scr[2 * c + 1] = t[:, 128:256]
            for res in range(dil):
                for c in range(6):
                    outs[g][0, res, :, c * 128:(c + 1) * 128] = (
                        perm_scr[c, pl.ds(res, tm // dil, stride=dil), :].astype(BF16))

    n_gate = wg_ref.shape[1]
    for c in range(n_gate // 512):
        r = jnp.dot(h, wg_ref[:, c * 512:(c + 1) * 512], preferred_element_type=F32)
        gate_ref[0, :, c * 512:(c + 1) * 512] = jax.nn.sigmoid(r).astype(BF16)


def _inproj(x, g, w_qk, w_vt, w_f, b_f, w_d, w_g, cos, slo, shi):
    B, S, D = x.shape
    tm = TM_IN
    grid = (B, S // tm)
    tok = lambda w: pl.BlockSpec((1, tm, w), lambda b, i: (b, i, 0))
    tab = pl.BlockSpec((tm, GROUP_W), lambda b, i: (i, 0))
    out_shape = (
        jax.ShapeDtypeStruct((B, S, 2 * FOX_W), BF16),
        jax.ShapeDtypeStruct((B, FOX_W, S), BF16),
        jax.ShapeDtypeStruct((B, S, 128), BF16),
        jax.ShapeDtypeStruct((B, S, 3 * GROUP_W), BF16),
        jax.ShapeDtypeStruct((B, 4, S // 4, 3 * GROUP_W), BF16),
        jax.ShapeDtypeStruct((B, 16, S // 16, 3 * GROUP_W), BF16),
        jax.ShapeDtypeStruct((B, S, 2 * D), BF16),
    )
    out_specs = (
        tok(2 * FOX_W),
        pl.BlockSpec((1, FOX_W, tm), lambda b, i: (b, 0, i)),
        tok(128),
        tok(3 * GROUP_W),
        pl.BlockSpec((1, 4, tm // 4, 3 * GROUP_W), lambda b, i: (b, 0, i, 0)),
        pl.BlockSpec((1, 16, tm // 16, 3 * GROUP_W), lambda b, i: (b, 0, i, 0)),
        tok(2 * D),
    )
    in_specs = [
        tok(D), _resident(g.shape), _resident(w_qk.shape), _resident(w_vt.shape),
        _resident(w_f.shape), _resident(b_f.shape), _resident(w_d.shape), _resident(w_g.shape),
        tab, tab, tab,
    ]
    return pl.pallas_call(
        _inproj_kernel,
        grid=grid,
        in_specs=in_specs,
        out_specs=out_specs,
        out_shape=out_shape,
        scratch_shapes=[pltpu.VMEM((6, tm, 128), F32), pltpu.VMEM((8, 128), F32)],
        compiler_params=pltpu.CompilerParams(
            dimension_semantics=("arbitrary", "arbitrary"), vmem_limit_bytes=VMEM_LIMIT),
        name="inproj",
    )(x, g, w_qk, w_vt, w_f, b_f, w_d, w_g, cos, slo, shi)


def _fox_kernel(q_ref, k_ref, kx_ref, vt_ref, o_ref, s_scr):
    S = q_ref.shape[1]
    pw = 2 * HEAD_DIM
    hp = pl.program_id(1)
    lane = lax.broadcasted_iota(jnp.int32, (1, pw), 1)
    first = lane < HEAD_DIM
    zeros = jnp.zeros((TQ, pw), BF16)
    sel = [jnp.broadcast_to(
        jnp.where((lane >= 3 * (2 * hp + hh)) & (lane < 3 * (2 * hp + hh) + 3), 1.0, 0.0),
        (TQ, pw)).astype(BF16) for hh in range(2)]
    key = lax.broadcasted_iota(jnp.int32, (TK, 2 * TQ), 0)
    qry = lax.broadcasted_iota(jnp.int32, (TK, 2 * TQ), 1) % TQ
    causal = key <= qry

    for qi in range(S // TQ):
        q2 = q_ref[0, qi * TQ:(qi + 1) * TQ, :]
        qmix = jnp.concatenate([
            jnp.concatenate([jnp.where(first, q2, zeros), sel[0]], axis=1),
            jnp.concatenate([jnp.where(first, zeros, q2), sel[1]], axis=1)], axis=0)
        n_chunks = qi + 1
        m = jnp.full((1, 2 * TQ), NEG_INF, F32)
        for c in range(n_chunks):
            kaug = jnp.concatenate([k_ref[0, c * TK:(c + 1) * TK, :],
                                    kx_ref[0, c * TK:(c + 1) * TK, :]], axis=1)
            s = lax.dot_general(kaug, qmix, _NT, preferred_element_type=F32)
            if c == qi:
                s = jnp.where(causal, s, NEG_INF)
            s_scr[c] = s
            m = jnp.maximum(m, jnp.max(s, axis=0, keepdims=True))
        l = jnp.zeros((1, 2 * TQ), F32)
        acc = jnp.zeros((pw, 2 * TQ), F32)
        for c in range(n_chunks):
            p = jnp.exp2(s_scr[c] - m)
            l = l + jnp.sum(p, axis=0, keepdims=True)
            acc = acc + jnp.dot(vt_ref[0, :, c * TK:(c + 1) * TK], p.astype(BF16),
                                preferred_element_type=F32)
        ot = (acc / l).T
        o_ref[0, qi * TQ:(qi + 1) * TQ, :] = jnp.where(first, ot[0:TQ], ot[TQ:2 * TQ]).astype(BF16)


def _fox(qk, vt, kx):
    B, S, _ = qk.shape
    n_pairs = FOX_HEADS // 2
    pw = 2 * HEAD_DIM
    spec = lambda off: pl.BlockSpec((1, S, pw), lambda b, p: (b, 0, off + p))
    return pl.pallas_call(
        _fox_kernel,
        grid=(B, n_pairs),
        in_specs=[spec(0), spec(n_pairs),
                  pl.BlockSpec((1, S, 128), lambda b, p: (b, 0, 0)),
                  pl.BlockSpec((1, pw, S), lambda b, p: (b, p, 0))],
        out_specs=pl.BlockSpec((1, S, pw), lambda b, p: (b, 0, p)),
        out_shape=jax.ShapeDtypeStruct((B, S, FOX_W), BF16),
        scratch_shapes=[pltpu.VMEM((S // TK, TK, 2 * TQ), F32)],
        compiler_params=pltpu.CompilerParams(
            dimension_semantics=("arbitrary", "arbitrary"), vmem_limit_bytes=VMEM_LIMIT),
        name="fox",
    )(qk, qk, kx, vt)


def _dil_kernel(d0_ref, d1_ref, d2_ref, o_ref, acc_scr, m_scr, l_scr):
    S = d0_ref.shape[1]
    nh = DIL_HEADS_PER_GROUP
    blk = DIL_BLK
    n_blocks = S // blk
    lane_head = lax.broadcasted_iota(jnp.int32, (1, GROUP_W), 1) // HEAD_DIM
    a2 = lax.broadcasted_iota(jnp.int32, (nh * blk, 2 * blk), 0) % blk
    k2 = lax.broadcasted_iota(jnp.int32, (nh * blk, 2 * blk), 1)
    bias_two = jnp.where((k2 >= a2) & (k2 <= a2 + blk), 0.0, NEG_INF)
    prev_cols = k2 < blk
    a1 = lax.broadcasted_iota(jnp.int32, (nh * blk, blk), 0) % blk
    k1 = lax.broadcasted_iota(jnp.int32, (nh * blk, blk), 1)
    bias_one = jnp.where(k1 <= a1, 0.0, NEG_INF)

    def assemble(x):
        out = x[(nh - 1) * blk:nh * blk]
        for h in range(nh - 2, -1, -1):
            out = jnp.where(lane_head == h, x[h * blk:(h + 1) * blk], out)
        return out

    refs = (d0_ref, d1_ref, d2_ref)
    for g, (_, dil) in enumerate(DIL_GROUPS):
        ref = refs[g]
        nbs = n_blocks // dil

        def block(n, _, ref=ref, dil=dil, nbs=nbs, g=g):
            r0 = pl.multiple_of(n * blk, blk)
            qs = ref[0, pl.ds(r0, blk), 0:GROUP_W]
            zero = jnp.zeros_like(qs)
            qstack = jnp.concatenate(
                [jnp.where(lane_head == h, qs, zero) for h in range(nh)], axis=0)
            kcur = ref[0, pl.ds(r0, blk), GROUP_W:2 * GROUP_W]
            vcur = ref[0, pl.ds(r0, blk), 2 * GROUP_W:3 * GROUP_W]
            pos = n % nbs
            if nbs > 1:
                rp = pl.multiple_of(jnp.maximum(n - 1, 0) * blk, blk)
                kcat = jnp.concatenate([ref[0, pl.ds(rp, blk), GROUP_W:2 * GROUP_W], kcur], axis=0)
                vcat = jnp.concatenate([ref[0, pl.ds(rp, blk), 2 * GROUP_W:3 * GROUP_W], vcur], axis=0)
                s = lax.dot_general(qstack, kcat, _NT, preferred_element_type=F32) + bias_two
                no_prev = jnp.where(pos == 0, NEG_INF, 0.0)
                s = s + jnp.where(prev_cols, no_prev, 0.0)
            else:
                vcat = vcur
                s = lax.dot_general(qstack, kcur, _NT, preferred_element_type=F32) + bias_one
            m = jnp.max(s, axis=1, keepdims=True)
            p = jnp.exp(s - m)
            l = jnp.sum(p, axis=1, keepdims=True)
            pv = jnp.dot(p.astype(BF16), vcat, preferred_element_type=F32)
            acc_b = assemble(pv)
            m_b = assemble(jnp.broadcast_to(m, pv.shape))
            l_b = assemble(jnp.broadcast_to(l, pv.shape))
            if dil == 1:
                idx = pl.ds(r0, blk)
            else:
                idx = pl.ds(dil * pos * blk + n // nbs, blk, stride=dil)
            for c in range(GROUP_W // 128):
                lanes = slice(c * 128, (c + 1) * 128)
                if g == 0:
                    acc_scr[c, idx, :] = acc_b[:, lanes]
                    m_scr[c, idx, :] = m_b[:, lanes]
                    l_scr[c, idx, :] = l_b[:, lanes]
                else:
                    m_o = m_scr[c, idx, :]
                    m_n = jnp.maximum(m_o, m_b[:, lanes])
                    e_o = jnp.exp(m_o - m_n)
                    e_b = jnp.exp(m_b[:, lanes] - m_n)
                    acc_scr[c, idx, :] = acc_scr[c, idx, :] * e_o + acc_b[:, lanes] * e_b
                    l_scr[c, idx, :] = l_scr[c, idx, :] * e_o + l_b[:, lanes] * e_b
                    m_scr[c, idx, :] = m_n
            return 0

        lax.fori_loop(0, n_blocks, block, 0, unroll=4)

    def finish(i, _):
        r0 = pl.multiple_of(i * blk, blk)
        for c in range(GROUP_W // 128):
            o_ref[0, pl.ds(r0, blk), c * 128:(c + 1) * 128] = (
                acc_scr[c, pl.ds(r0, blk), :] / l_scr[c, pl.ds(r0, blk), :]).astype(BF16)
        return 0

    lax.fori_loop(0, n_blocks, finish, 0)


def _dilated(d0, d1, d2):
    B, S, W = d0.shape
    spec = pl.BlockSpec((1, S, W), lambda b: (b, 0, 0))
    return pl.pallas_call(
        _dil_kernel,
        grid=(B,),
        in_specs=[spec, spec, spec],
        out_specs=pl.BlockSpec((1, S, GROUP_W), lambda b: (b, 0, 0)),
        out_shape=jax.ShapeDtypeStruct((B, S, GROUP_W), BF16),
        scratch_shapes=[pltpu.VMEM((GROUP_W // 128, S, 128), F32)] * 3,
        compiler_params=pltpu.CompilerParams(
            dimension_semantics=("arbitrary",), vmem_limit_bytes=VMEM_LIMIT),
        name="dilated",
    )(d0, d1, d2)


def _outmlp_kernel(x_ref, oa_ref, ob_ref, gate_ref, wa_ref, wb_ref, wo_ref, g2_ref,
                   wu_ref, wdn_ref, g3_ref, o_ref, *, final_norm):
    d = x_ref.shape[1]
    ya = jnp.dot(oa_ref[...], wa_ref[...], preferred_element_type=F32)
    yb = jnp.dot(ob_ref[...], wb_ref[...], preferred_element_type=F32)
    mixed = gate_ref[:, 0:d].astype(F32) * ya + gate_ref[:, d:2 * d].astype(F32) * yb
    x1 = x_ref[...] + jnp.dot(mixed.astype(BF16), wo_ref[...], preferred_element_type=F32)
    h = _rms(x1, g2_ref[...]).astype(BF16)
    x2 = x1
    d_ff = wu_ref.shape[1]
    for c in range(d_ff // FF_CHUNK):
        u = jnp.dot(h, wu_ref[:, c * FF_CHUNK:(c + 1) * FF_CHUNK], preferred_element_type=F32)
        a = jnp.square(jnp.maximum(u, 0.0)).astype(BF16)
        x2 = x2 + jnp.dot(a, wdn_ref[c * FF_CHUNK:(c + 1) * FF_CHUNK, :], preferred_element_type=F32)
    o_ref[...] = _rms(x2, g3_ref[...]) if final_norm else x2


def _outmlp(x, oa, ob, gates, w_a, w_b, w_o, g2, w_u, w_dn, g3, final_norm):
    N, D = x.shape
    tm = TM_OUT
    tok = lambda w: pl.BlockSpec((tm, w), lambda i: (i, 0))
    return pl.pallas_call(
        functools.partial(_outmlp_kernel, final_norm=final_norm),
        grid=(N // tm,),
        in_specs=[tok(D), tok(oa.shape[1]), tok(ob.shape[1]), tok(gates.shape[1]),
                  _resident(w_a.shape), _resident(w_b.shape), _resident(w_o.shape),
                  _resident(g2.shape), _resident(w_u.shape), _resident(w_dn.shape),
                  _resident(g3.shape)],
        out_specs=tok(D),
        out_shape=jax.ShapeDtypeStruct((N, D), F32),
        compiler_params=pltpu.CompilerParams(
            dimension_semantics=("arbitrary",), vmem_limit_bytes=VMEM_LIMIT),
        name="outmlp",
    )(x, oa, ob, gates, w_a, w_b, w_o, g2, w_u, w_dn, g3)


def _rope_tables(S):
    inv_freq = jnp.power(jnp.float32(ROPE_THETA),
                         -jnp.arange(ROPE_HALF, dtype=F32) * 2.0 / ROPE_DIM)
    ang = jnp.arange(S).astype(F32)[:, None] * inv_freq[None, :]
    cos, sin = jnp.cos(ang), jnp.sin(ang)
    rest = HEAD_DIM - ROPE_DIM
    one = jnp.ones((S, rest), F32)
    zero = lambda w: jnp.zeros((S, w), F32)
    cos_h = jnp.concatenate([cos, cos, one], axis=1)
    slo_h = jnp.concatenate([-sin, zero(ROPE_HALF + rest)], axis=1)
    shi_h = jnp.concatenate([zero(ROPE_HALF), sin, zero(rest)], axis=1)
    rep = lambda t: jnp.tile(t, (1, DIL_HEADS_PER_GROUP))
    return rep(cos_h), rep(slo_h), rep(shi_h)


def kernel(x, norm_attn_g, w_in, b_forget, w_branch_a, w_branch_b, w_out, norm_mlp_g, w_up,
           w_down, norm_final_g):
    B, S, D = x.shape
    depth = w_in.shape[0]
    assert S % (16 * DIL_BLK) == 0 and S // 16 == DIL_BLK, "sequence length fixed by the dilation groups"
    cos, slo, shi = _rope_tables(S)
    sizes = (FOX_W, FOX_W, FOX_W, FOX_HEADS, DIL_W, DIL_W, DIL_W, D, D)
    offs = np.concatenate([[0], np.cumsum(sizes)])
    col = lambda w, i: w[:, offs[i]:offs[i + 1]]
    for l in range(depth):
        w = w_in[l]
        w_qk = w[:, :offs[2]].astype(BF16)
        w_vt = col(w, 2).T.astype(BF16)
        pad = 128 - 3 * FOX_HEADS
        w_f = jnp.pad(jnp.repeat(col(w, 3), 3, axis=1), ((0, 0), (0, pad))).astype(BF16)
        b_f = jnp.pad(jnp.repeat(b_forget[l].astype(F32), 3), (0, pad))[None, :]
        qb, kb, vb = col(w, 4), col(w, 5), col(w, 6)
        grp = lambda t, g: t[:, g * GROUP_W:(g + 1) * GROUP_W]
        w_d = jnp.stack([jnp.concatenate([grp(qb, g), grp(kb, g), grp(vb, g)], axis=1)
                         for g in range(N_GROUPS)]).astype(BF16)
        w_g = w[:, offs[7]:].astype(BF16)

        qk, vt, kx, d0, d1, d2, gates = _inproj(
            x, norm_attn_g[l][None, :], w_qk, w_vt, w_f, b_f, w_d, w_g, cos, slo, shi)
        oa = _fox(qk, vt, kx)
        ob = _dilated(d0, d1.reshape(B, S, -1), d2.reshape(B, S, -1))
        y = _outmlp(
            x.reshape(B * S, D), oa.reshape(B * S, -1), ob.reshape(B * S, -1),
            gates.reshape(B * S, -1),
            w_branch_a[l].astype(BF16), w_branch_b[l].astype(BF16), w_out[l].astype(BF16),
            norm_mlp_g[l][None, :], w_up[l].astype(BF16), w_down[l].astype(BF16),
            norm_final_g[None, :], final_norm=(l == depth - 1))
        x = y.reshape(B, S, D)
    return x
```

```python
import functools

import numpy as np
import jax
import jax.numpy as jnp
from jax import lax
from jax.experimental import pallas as pl
from jax.experimental.pallas import tpu as pltpu

F32 = jnp.float32
BF16 = jnp.bfloat16

HEAD_DIM = 64
FOX_HEADS = 8
FOX_W = FOX_HEADS * HEAD_DIM
DIL_GROUPS = ((128, 1), (512, 4), (2048, 16))
DIL_HEADS_PER_GROUP = 4
GROUP_W = DIL_HEADS_PER_GROUP * HEAD_DIM
N_GROUPS = len(DIL_GROUPS)
DIL_W = N_GROUPS * GROUP_W
ROPE_THETA = 500000.0
ROPE_DIM = HEAD_DIM // 4
ROPE_HALF = ROPE_DIM // 2
EPS = 1e-6
NEG_INF = -1e30
Q_SCALE = HEAD_DIM ** -0.5
LOG2E = 1.4426950408889634

DIL_BLK = 128
TM_IN = 512
TM_OUT = 512
TQ = 256
TK = 256
FF_CHUNK = 1024
VMEM_LIMIT = 56 * 1024 * 1024

_NT = (((1,), (1,)), ((), ()))


def _resident(shape):
    nd = len(shape)
    return pl.BlockSpec(shape, lambda *_: (0,) * nd, pipeline_mode=pl.Buffered(1))


def _rms(x, g):
    ms = jnp.mean(x * x, axis=-1, keepdims=True)
    return x * lax.rsqrt(ms + EPS) * g


def _rope(x, cos, sin_lo, sin_hi):
    w = x.shape[1]
    up = pltpu.roll(x, w - ROPE_HALF, 1)
    dn = pltpu.roll(x, ROPE_HALF, 1)
    return x * cos + up * sin_lo + dn * sin_hi


def _split3(x):
    hi = x.astype(BF16)
    r1 = x - hi.astype(F32)
    mid = r1.astype(BF16)
    lo = (r1 - mid.astype(F32)).astype(BF16)
    return hi, mid, lo


def _inproj_kernel(x_ref, g_ref, wqk_ref, wvt_ref, wf_ref, bf_ref, wd_ref, wg_ref,
                   cos_ref, slo_ref, shi_ref,
                   qk_ref, vt_ref, kx_ref, d0_ref, d1_ref, d2_ref, gate_ref,
                   perm_scr, carry_scr):
    tm = x_ref.shape[1]
    h = _rms(x_ref[0], g_ref[...]).astype(BF16)

    for c in range(2):
        r = jnp.dot(h, wqk_ref[:, c * FOX_W:(c + 1) * FOX_W], preferred_element_type=F32)
        if c == 0:
            r = r * (Q_SCALE * LOG2E)
        qk_ref[0, :, c * FOX_W:(c + 1) * FOX_W] = r.astype(BF16)
    vt_ref[0] = lax.dot_general(wvt_ref[...], h, _NT, preferred_element_type=F32).astype(BF16)

    z = jnp.dot(h, wf_ref[...], preferred_element_type=F32) + bf_ref[...]
    lane = lax.broadcasted_iota(jnp.int32, (1, 128), 1)
    log_f = jnp.where(lane < 3 * FOX_HEADS,
                      -(jnp.maximum(-z, 0.0) + jnp.log1p(jnp.exp(-jnp.abs(z)))), 0.0)

    @pl.when(pl.program_id(1) == 0)
    def _reset():
        carry_scr[...] = jnp.zeros_like(carry_scr)

    row = lax.broadcasted_iota(jnp.int32, (tm, tm), 0)
    col = lax.broadcasted_iota(jnp.int32, (tm, tm), 1)
    tri = jnp.where(col <= row, 1.0, 0.0).astype(BF16)
    cum = carry_scr[0:1, :]
    for part in _split3(log_f):
        cum = cum + jnp.dot(tri, part, preferred_element_type=F32)
    carry_scr[0:1, :] = cum[tm - 1:tm, :]
    hi, mid, lo = _split3(-LOG2E * cum)
    kx_ref[0] = jnp.where(lane % 3 == 0, hi, jnp.where(lane % 3 == 1, mid, lo))

    cos = cos_ref[...]
    slo = slo_ref[...]
    shi = shi_ref[...]
    outs = (d0_ref, d1_ref, d2_ref)
    for g, (_, dil) in enumerate(DIL_GROUPS):
        r = jnp.dot(h, wd_ref[g], preferred_element_type=F32)
        q = _rope(r[:, 0:GROUP_W], cos, slo, shi) * (Q_SCALE * LOG2E)
        k = _rope(r[:, GROUP_W:2 * GROUP_W], cos, slo, shi)
        v = r[:, 2 * GROUP_W:3 * GROUP_W]
        n_res = max(dil, 4)
        for c, t in enumerate((q, k, v)):
            perm_scr[2 * c] = t[:, 0:128]
            perm_scr[2 * c + 1] = t[:, 128:256]
        for res in range(n_res):
            for c in range(6):
                outs[g][0, res, :, c * 128:(c + 1) * 128] = (
                    perm_scr[c, pl.ds(res, tm // n_res, stride=n_res), :].astype(BF16))

    n_gate = wg_ref.shape[1]
    for c in range(n_gate // 512):
        r = jnp.dot(h, wg_ref[:, c * 512:(c + 1) * 512], preferred_element_type=F32)
        gate_ref[0, :, c * 512:(c + 1) * 512] = jax.nn.sigmoid(r).astype(BF16)


def _inproj(x, g, w_qk, w_vt, w_f, b_f, w_d, w_g, cos, slo, shi):
    B, S, D = x.shape
    tm = TM_IN
    grid = (B, S // tm)
    tok = lambda w: pl.BlockSpec((1, tm, w), lambda b, i: (b, i, 0))
    tab = pl.BlockSpec((tm, GROUP_W), lambda b, i: (i, 0))
    out_shape = (
        jax.ShapeDtypeStruct((B, S, 2 * FOX_W), BF16),
        jax.ShapeDtypeStruct((B, FOX_W, S), BF16),
        jax.ShapeDtypeStruct((B, S, 128), BF16),
        jax.ShapeDtypeStruct((B, 4, S // 4, 3 * GROUP_W), BF16),
        jax.ShapeDtypeStruct((B, 4, S // 4, 3 * GROUP_W), BF16),
        jax.ShapeDtypeStruct((B, 16, S // 16, 3 * GROUP_W), BF16),
        jax.ShapeDtypeStruct((B, S, 2 * D), BF16),
    )
    out_specs = (
        tok(2 * FOX_W),
        pl.BlockSpec((1, FOX_W, tm), lambda b, i: (b, 0, i)),
        tok(128),
        pl.BlockSpec((1, 4, tm // 4, 3 * GROUP_W), lambda b, i: (b, 0, i, 0)),
        pl.BlockSpec((1, 4, tm // 4, 3 * GROUP_W), lambda b, i: (b, 0, i, 0)),
        pl.BlockSpec((1, 16, tm // 16, 3 * GROUP_W), lambda b, i: (b, 0, i, 0)),
        tok(2 * D),
    )
    in_specs = [
        tok(D), _resident(g.shape), _resident(w_qk.shape), _resident(w_vt.shape),
        _resident(w_f.shape), _resident(b_f.shape), _resident(w_d.shape), _resident(w_g.shape),
        tab, tab, tab,
    ]
    return pl.pallas_call(
        _inproj_kernel,
        grid=grid,
        in_specs=in_specs,
        out_specs=out_specs,
        out_shape=out_shape,
        scratch_shapes=[pltpu.VMEM((6, tm, 128), F32), pltpu.VMEM((8, 128), F32)],
        compiler_params=pltpu.CompilerParams(
            dimension_semantics=("arbitrary", "arbitrary"), vmem_limit_bytes=VMEM_LIMIT),
        name="inproj",
    )(x, g, w_qk, w_vt, w_f, b_f, w_d, w_g, cos, slo, shi)


def _fox_kernel(q_ref, k_ref, kx_ref, vt_ref, o_ref):
    S = q_ref.shape[1]
    pw = 2 * HEAD_DIM
    hp = pl.program_id(1)
    lane = lax.broadcasted_iota(jnp.int32, (1, pw), 1)
    first = lane < HEAD_DIM
    zeros = jnp.zeros((TQ, pw), BF16)
    sel = [jnp.broadcast_to(
        jnp.where((lane >= 3 * (2 * hp + hh)) & (lane < 3 * (2 * hp + hh) + 3), 1.0, 0.0),
        (TQ, pw)).astype(BF16) for hh in range(2)]
    key = lax.broadcasted_iota(jnp.int32, (TK, 2 * TQ), 0)
    qry = lax.broadcasted_iota(jnp.int32, (TK, 2 * TQ), 1) % TQ
    causal = key <= qry

    def scores(qi):
        q2 = q_ref[0, qi * TQ:(qi + 1) * TQ, :]
        qmix = jnp.concatenate([
            jnp.concatenate([jnp.where(first, q2, zeros), sel[0]], axis=1),
            jnp.concatenate([jnp.where(first, zeros, q2), sel[1]], axis=1)], axis=0)
        n_k = (qi + 1) * TK
        kaug = jnp.concatenate([k_ref[0, 0:n_k, :], kx_ref[0, 0:n_k, :]], axis=1)
        s = lax.dot_general(kaug, qmix, _NT, preferred_element_type=F32)
        chunks = [s[c * TK:(c + 1) * TK] for c in range(qi)]
        chunks.append(jnp.where(causal, s[qi * TK:n_k], NEG_INF))
        m = jnp.max(chunks[0], axis=0, keepdims=True)
        for c in range(1, qi + 1):
            m = jnp.maximum(m, jnp.max(chunks[c], axis=0, keepdims=True))
        return chunks, m

    def outputs(qi, chunks, m):
        n_k = (qi + 1) * TK
        p = [jnp.exp2(s - m) for s in chunks]
        l = jnp.sum(p[0], axis=0, keepdims=True)
        for c in range(1, qi + 1):
            l = l + jnp.sum(p[c], axis=0, keepdims=True)
        pt = jnp.concatenate([t.astype(BF16) for t in p], axis=0)
        acc = jnp.dot(vt_ref[0, :, 0:n_k], pt, preferred_element_type=F32)
        ot = (acc / l).T
        o_ref[0, qi * TQ:(qi + 1) * TQ, :] = jnp.where(first, ot[0:TQ], ot[TQ:2 * TQ]).astype(BF16)

    n_q = S // TQ
    nxt = scores(0)
    for qi in range(n_q):
        cur = nxt
        if qi + 1 < n_q:
            nxt = scores(qi + 1)
        outputs(qi, *cur)


def _fox(qk, vt, kx):
    B, S, _ = qk.shape
    n_pairs = FOX_HEADS // 2
    pw = 2 * HEAD_DIM
    spec = lambda off: pl.BlockSpec((1, S, pw), lambda b, p: (b, 0, off + p))
    return pl.pallas_call(
        _fox_kernel,
        grid=(B, n_pairs),
        in_specs=[spec(0), spec(n_pairs),
                  pl.BlockSpec((1, S, 128), lambda b, p: (b, 0, 0)),
                  pl.BlockSpec((1, pw, S), lambda b, p: (b, p, 0))],
        out_specs=pl.BlockSpec((1, S, pw), lambda b, p: (b, 0, p)),
        out_shape=jax.ShapeDtypeStruct((B, S, FOX_W), BF16),
        compiler_params=pltpu.CompilerParams(
            dimension_semantics=("arbitrary", "arbitrary"), vmem_limit_bytes=VMEM_LIMIT),
        name="fox",
    )(qk, qk, kx, vt)


def _dil_kernel(d0_ref, d1_ref, d2_ref, o_ref, acc_scr, m_scr, l_scr, nat_scr):
    S = d0_ref.shape[1]
    nh = DIL_HEADS_PER_GROUP
    blk = DIL_BLK
    n_blocks = S // blk
    lane_head = lax.broadcasted_iota(jnp.int32, (1, GROUP_W), 1) // HEAD_DIM
    a2 = lax.broadcasted_iota(jnp.int32, (nh * blk, 2 * blk), 0) % blk
    k2 = lax.broadcasted_iota(jnp.int32, (nh * blk, 2 * blk), 1)
    bias_two = jnp.where((k2 >= a2) & (k2 <= a2 + blk), 0.0, NEG_INF)
    a1 =lax.broadcasted_iota(jnp.int32, (nh * blk, blk), 0) % blk
    k1 = lax.broadcasted_iota(jnp.int32, (nh * blk, blk), 1)
    bias_one = jnp.where(k1 <= a1, 0.0, NEG_INF)

    def assemble(x):
        out = x[(nh - 1) * blk:nh * blk]
        for h in range(nh - 2, -1, -1):
            out = jnp.where(lane_head == h, x[h * blk:(h + 1) * blk], out)
        return out

    def attend(qs, kcat, vcat, bias):
        zero = jnp.zeros_like(qs)
        qstack = jnp.concatenate(
            [jnp.where(lane_head == h, qs, zero) for h in range(nh)], axis=0)
        s = lax.dot_general(qstack, kcat, _NT, preferred_element_type=F32) + bias
        m = jnp.max(s, axis=1, keepdims=True)
        p = jnp.exp2(s - m)
        l = jnp.sum(p, axis=1, keepdims=True)
        pv = jnp.dot(p.astype(BF16), vcat, preferred_element_type=F32)
        return (assemble(pv), assemble(jnp.broadcast_to(m, pv.shape)),
                assemble(jnp.broadcast_to(l, pv.shape)))

    def put(res, pieces, first):
        acc_b, m_b, l_b = res
        for rows, idx in pieces:
            for c in range(GROUP_W // 128):
                lanes = slice(c * 128, (c + 1) * 128)
                if first:
                    acc_scr[c, idx, :] = acc_b[rows, lanes]
                    m_scr[c, idx, :] = m_b[rows, lanes]
                    l_scr[c, idx, :] = l_b[rows, lanes]
                else:
                    m_o = m_scr[c, idx, :]
                    m_n = jnp.maximum(m_o, m_b[rows, lanes])
                    e_o = jnp.exp2(m_o - m_n)
                    e_b = jnp.exp2(m_b[rows, lanes] - m_n)
                    acc_scr[c, idx, :] = acc_scr[c, idx, :] * e_o + acc_b[rows, lanes] * e_b
                    l_scr[c, idx, :] = l_scr[c, idx, :] * e_o + l_b[rows, lanes] * e_b
                    m_scr[c, idx, :] = m_n

    q_l, k_l, v_l = (slice(i * GROUP_W, (i + 1) * GROUP_W) for i in range(3))
    whole = slice(0, blk)
    n_res = 4
    slab = S // n_res
    per_slab = slab // blk

    def g1_slab(res, _):
        for pos in range(per_slab):
            r0 = pl.multiple_of(res * slab + pos * blk, blk)
            qs = d1_ref[0, pl.ds(r0, blk), q_l]
            if pos == 0:
                out = attend(qs, d1_ref[0, pl.ds(r0, blk), k_l], d1_ref[0, pl.ds(r0, blk), v_l], bias_one)
            else:
                rp = pl.multiple_of(r0 - blk, blk)
                out = attend(qs, d1_ref[0, pl.ds(rp, 2 * blk), k_l], d1_ref[0, pl.ds(rp, 2 * blk), v_l],
                             bias_two)
            put(out, [(whole, pl.ds(r0, blk))], first=True)
        return 0

    lax.fori_loop(0, n_res, g1_slab, 0)

    sub = blk // n_res
    qa = lax.broadcasted_iota(jnp.int32, (nh * blk, 2 * blk), 0) % blk
    kk = lax.broadcasted_iota(jnp.int32, (nh * blk, 2 * blk), 1)
    tq = blk + n_res * (qa % sub) + qa // sub
    tk = n_res * (kk % (2 * sub)) + kk // (2 * sub)
    bias0_two = jnp.where((tk <= tq) & (tq - tk <= blk), 0.0, NEG_INF)
    qa1 = lax.broadcasted_iota(jnp.int32, (nh * blk, blk), 0) % blk
    kk1 = lax.broadcasted_iota(jnp.int32, (nh * blk, blk), 1)
    bias0_one = jnp.where(n_res * (kk1 % sub) + kk1 // sub <= n_res * (qa1 % sub) + qa1 // sub,
                          0.0, NEG_INF)

    def g0_block(n, has_prev):
        rq = [pl.multiple_of(r * slab + n * sub, sub) for r in range(n_res)]
        qs = jnp.concatenate([d0_ref[0, pl.ds(rq[r], sub), q_l] for r in range(n_res)], axis=0)
        if has_prev:
            rk = [pl.multiple_of(r * slab + (n - 1) * sub, sub) for r in range(n_res)]
            kcat = jnp.concatenate([d0_ref[0, pl.ds(rk[r], 2 * sub), k_l] for r in range(n_res)], axis=0)
            vcat = jnp.concatenate([d0_ref[0, pl.ds(rk[r], 2 * sub), v_l] for r in range(n_res)], axis=0)
            out = attend(qs, kcat, vcat, bias0_two)
        else:
            kcat = jnp.concatenate([d0_ref[0, pl.ds(rq[r], sub), k_l] for r in range(n_res)], axis=0)
            vcat = jnp.concatenate([d0_ref[0, pl.ds(rq[r], sub), v_l] for r in range(n_res)], axis=0)
            out = attend(qs, kcat, vcat, bias0_one)
        put(out, [(slice(r * sub, (r + 1) * sub), pl.ds(rq[r], sub)) for r in range(n_res)],
            first=False)

    g0_block(0, False)

    def g0_rest(n, _):
        g0_block(n, True)
        return 0

    lax.fori_loop(1, n_blocks, g0_rest, 0, unroll=3)

    def g2_block(r16, _):
        r0 = pl.multiple_of(r16 * blk, blk)
        out = attend(d2_ref[0, pl.ds(r0, blk), q_l], d2_ref[0, pl.ds(r0, blk), k_l],
                     d2_ref[0, pl.ds(r0, blk), v_l], bias_one)
        put(out, [(whole, pl.ds((r16 % n_res) * slab + r16 // n_res, blk, stride=n_res))], first=False)
        return 0

    lax.fori_loop(0, n_blocks, g2_block, 0, unroll=4)

    def finish(res, _):
        for pos in range(per_slab):
            r0 = pl.multiple_of(res * slab + pos * blk, blk)
            for c in range(GROUP_W // 128):
                nat_scr[c, pl.ds(n_res * pos * blk + res, blk, stride=n_res), :] = (
                    acc_scr[c, pl.ds(r0, blk), :] / l_scr[c, pl.ds(r0, blk), :])
        return 0

    lax.fori_loop(0, n_res, finish, 0)

    def emit(i, _):
        r0 = pl.multiple_of(i * 2 * blk, 2 * blk)
        for c in range(GROUP_W // 128):
            o_ref[0, pl.ds(r0, 2 * blk), c * 128:(c + 1) * 128] = nat_scr[c, pl.ds(r0, 2 * blk), :].astype(BF16)
        return 0

    lax.fori_loop(0, n_blocks // 2, emit, 0)


def _dilated(d0, d1, d2):
    B, S, W = d0.shape
    spec = pl.BlockSpec((1, S, W), lambda b: (b, 0, 0))
    return pl.pallas_call(
        _dil_kernel,
        grid=(B,),
        in_specs=[spec, spec, spec],
        out_specs=pl.BlockSpec((1, S, GROUP_W), lambda b: (b, 0, 0)),
        out_shape=jax.ShapeDtypeStruct((B, S, GROUP_W), BF16),
        scratch_shapes=[pltpu.VMEM((GROUP_W // 128, S, 128), F32)] * 4,
        compiler_params=pltpu.CompilerParams(
            dimension_semantics=("arbitrary",), vmem_limit_bytes=VMEM_LIMIT),
        name="dilated",
    )(d0, d1, d2)


def _outmlp_kernel(x_ref, oa_ref, ob_ref, gate_ref, wa_ref, wb_ref, wo_ref, g2_ref,
                   wu_ref, wdn_ref, g3_ref, o_ref, *, final_norm):
    d = x_ref.shape[1]
    ya = jnp.dot(oa_ref[...], wa_ref[...], preferred_element_type=F32)
    yb = jnp.dot(ob_ref[...], wb_ref[...], preferred_element_type=F32)
    mixed = gate_ref[:, 0:d].astype(F32) * ya + gate_ref[:, d:2 * d].astype(F32) * yb
    x1 = x_ref[...] + jnp.dot(mixed.astype(BF16), wo_ref[...], preferred_element_type=F32)
    h = _rms(x1, g2_ref[...]).astype(BF16)
    x2 = x1
    d_ff = wu_ref.shape[1]
    for c in range(d_ff // FF_CHUNK):
        u = jnp.dot(h, wu_ref[:, c * FF_CHUNK:(c + 1) * FF_CHUNK], preferred_element_type=F32)
        a = jnp.square(jnp.maximum(u, 0.0)).astype(BF16)
        x2 = x2 + jnp.dot(a, wdn_ref[c * FF_CHUNK:(c + 1) * FF_CHUNK, :], preferred_element_type=F32)
    o_ref[...] = _rms(x2, g3_ref[...]) if final_norm else x2


def _outmlp(x, oa, ob, gates, w_a, w_b, w_o, g2, w_u, w_dn, g3, final_norm):
    N, D = x.shape
    tm = TM_OUT
    tok = lambda w: pl.BlockSpec((tm, w), lambda i: (i, 0))
    return pl.pallas_call(
        functools.partial(_outmlp_kernel, final_norm=final_norm),
        grid=(N // tm,),
        in_specs=[tok(D), tok(oa.shape[1]), tok(ob.shape[1]), tok(gates.shape[1]),
                  _resident(w_a.shape), _resident(w_b.shape), _resident(w_o.shape),
                  _resident(g2.shape), _resident(w_u.shape), _resident(w_dn.shape),
                  _resident(g3.shape)],
        out_specs=tok(D),
        out_shape=jax.ShapeDtypeStruct((N, D), F32),
        compiler_params=pltpu.CompilerParams(
            dimension_semantics=("arbitrary",), vmem_limit_bytes=VMEM_LIMIT),
        name="outmlp",
    )(x, oa, ob, gates, w_a, w_b, w_o, g2, w_u, w_dn, g3)


def _rope_tables(S):
    inv_freq = jnp.power(jnp.float32(ROPE_THETA),
                         -jnp.arange(ROPE_HALF, dtype=F32) * 2.0 / ROPE_DIM)
    ang = jnp.arange(S).astype(F32)[:, None] * inv_freq[None, :]
    cos, sin = jnp.cos(ang), jnp.sin(ang)
    rest = HEAD_DIM - ROPE_DIM
    one = jnp.ones((S, rest), F32)
    zero = lambda w: jnp.zeros((S, w), F32)
    cos_h = jnp.concatenate([cos, cos, one], axis=1)
    slo_h = jnp.concatenate([-sin, zero(ROPE_HALF + rest)], axis=1)
    shi_h = jnp.concatenate([zero(ROPE_HALF), sin, zero(rest)], axis=1)
    rep = lambda t: jnp.tile(t, (1, DIL_HEADS_PER_GROUP))
    return rep(cos_h), rep(slo_h), rep(shi_h)


def kernel(x, norm_attn_g, w_in, b_forget, w_branch_a, w_branch_b, w_out, norm_mlp_g, w_up,
           w_down, norm_final_g):
    B, S, D = x.shape
    depth = w_in.shape[0]
    assert S % (16 * DIL_BLK) == 0 and S // 16 == DIL_BLK, "sequence length fixed by the dilation groups"
    cos, slo, shi = _rope_tables(S)
    sizes = (FOX_W, FOX_W, FOX_W, FOX_HEADS, DIL_W, DIL_W, DIL_W, D, D)
    offs = np.concatenate([[0], np.cumsum(sizes)])
    col = lambda w, i: w[:, offs[i]:offs[i + 1]]
    for l in range(depth):
        w = w_in[l]
        w_qk = w[:, :offs[2]].astype(BF16)
        w_vt = col(w, 2).T.astype(BF16)
        pad = 128 - 3 * FOX_HEADS
        w_f = jnp.pad(jnp.repeat(col(w, 3), 3, axis=1), ((0, 0), (0, pad))).astype(BF16)
        b_f = jnp.pad(jnp.repeat(b_forget[l].astype(F32), 3), (0, pad))[None, :]
        qb, kb, vb = col(w, 4), col(w, 5), col(w, 6)
        grp = lambda t, g: t[:, g * GROUP_W:(g + 1) * GROUP_W]
        w_d = jnp.stack([jnp.concatenate([grp(qb, g), grp(kb, g), grp(vb, g)], axis=1)
                         for g in range(N_GROUPS)]).astype(BF16)
        w_g = w[:, offs[7]:].astype(BF16)

        qk, vt, kx, d0, d1, d2, gates = _inproj(
            x, norm_attn_g[l][None, :], w_qk, w_vt, w_f, b_f, w_d, w_g, cos, slo, shi)
        oa = _fox(qk, vt, kx)
        ob = _dilated(d0.reshape(B, S, -1), d1.reshape(B, S, -1), d2.reshape(B, S, -1))
        y = _outmlp(
            x.reshape(B * S, D), oa.reshape(B * S, -1), ob.reshape(B * S, -1),
            gates.reshape(B * S, -1),
            w_branch_a[l].astype(BF16), w_branch_b[l].astype(BF16), w_out[l].astype(BF16),
            norm_mlp_g[l][None, :], w_up[l].astype(BF16), w_down[l].astype(BF16),
            norm_final_g[None, :], final_norm=(l == depth - 1))
        x = y.reshape(B, S, D)
    return x
```

```python
import functools

import numpy as np
import jax
import jax.numpy as jnp
from jax import lax
from jax.experimental import pallas as pl
from jax.experimental.pallas import tpu as pltpu

F32 = jnp.float32
BF16 = jnp.bfloat16

HEAD_DIM = 64
FOX_HEADS = 8
FOX_W = FOX_HEADS * HEAD_DIM
DIL_GROUPS = ((128, 1), (512, 4), (2048, 16))
DIL_HEADS_PER_GROUP = 4
GROUP_W = DIL_HEADS_PER_GROUP * HEAD_DIM
N_GROUPS = len(DIL_GROUPS)
DIL_W = N_GROUPS * GROUP_W
ROPE_THETA = 500000.0
ROPE_DIM = HEAD_DIM // 4
ROPE_HALF = ROPE_DIM // 2
EPS = 1e-6
NEG_INF = -1e30
Q_SCALE = HEAD_DIM ** -0.5
LOG2E = 1.4426950408889634

DIL_BLK = 128
TM_IN = 512
TM_OUT = 512
TQ = 256
TK = 256
FF_CHUNK = 1024
VMEM_LIMIT = 56 * 1024 * 1024

_NT = (((1,), (1,)), ((), ()))


def _resident(shape):
    nd = len(shape)
    return pl.BlockSpec(shape, lambda *_: (0,) * nd, pipeline_mode=pl.Buffered(1))


def _aligned(x, m):
    return x if isinstance(x, int) else pl.multiple_of(x, m)


def _rms(x, g):
    ms = jnp.mean(x * x, axis=-1, keepdims=True)
    return x * lax.rsqrt(ms + EPS) * g


def _rope(x, cos, sin_lo, sin_hi):
    w = x.shape[1]
    up = pltpu.roll(x, w - ROPE_HALF, 1)
    dn = pltpu.roll(x, ROPE_HALF, 1)
    return x * cos + up * sin_lo + dn * sin_hi


def _split3(x):
    hi = x.astype(BF16)
    r1 = x - hi.astype(F32)
    mid = r1.astype(BF16)
    lo = (r1 - mid.astype(F32)).astype(BF16)
    return hi, mid, lo


def _inproj_kernel(x_ref, g_ref, w_ref, wvt_ref, bf_ref, cos_ref, slo_ref, shi_ref,
                   qk_ref, vt_ref, kx_ref, d0_ref, d1_ref, d2_ref, gate_ref,
                   perm_scr, carry_scr):
    tm = x_ref.shape[1]
    d = x_ref.shape[2]

    @pl.when(pl.program_id(1) == 0)
    def _reset():
        carry_scr[...] = jnp.zeros_like(carry_scr)

    xt = x_ref[0]
    h = (xt * g_ref[...]).astype(BF16)
    inv = lax.rsqrt(jnp.mean(xt * xt, axis=-1, keepdims=True) + EPS)
    inv_row = jnp.broadcast_to(inv, (tm, 128)).T[0:1, :]
    proj = lambda lo, hi: inv * jnp.dot(h, w_ref[:, lo:hi], preferred_element_type=F32)
    f_off = 2 * FOX_W
    d_off = f_off + 128
    g_off = d_off + N_GROUPS * 3 * GROUP_W
    lane =lax.broadcasted_iota(jnp.int32, (1, 128), 1)

    def forget_logits():
        z = proj(f_off, d_off) + bf_ref[...]
        log_f = jnp.where(lane < 3 * FOX_HEADS,
                          -(jnp.maximum(-z, 0.0) + jnp.log1p(jnp.exp(-jnp.abs(z)))), 0.0)
        return _split3(log_f)

    def forget_cumsum(parts):
        row = lax.broadcasted_iota(jnp.int32, (tm, tm), 0)
        col = lax.broadcasted_iota(jnp.int32, (tm, tm), 1)
        tri = jnp.where(col <= row, 1.0, 0.0).astype(BF16)
        cum = carry_scr[0:1, :]
        for part in parts:
            cum = cum + jnp.dot(tri, part, preferred_element_type=F32)
        carry_scr[0:1, :] = cum[tm - 1:tm, :]
        hi, mid, lo = _split3(-LOG2E * cum)
        kx_ref[0] = jnp.where(lane % 3 == 0, hi, jnp.where(lane % 3 == 1, mid, lo))

    def mixer_a():
        qk_ref[0, :, 0:FOX_W] = (proj(0, FOX_W) * (Q_SCALE * LOG2E)).astype(BF16)
        qk_ref[0, :, FOX_W:2 * FOX_W] = proj(FOX_W, 2 * FOX_W).astype(BF16)
        vt_ref[0] = (lax.dot_general(wvt_ref[...], h, _NT, preferred_element_type=F32)
                     * inv_row).astype(BF16)

    outs = (d0_ref, d1_ref, d2_ref)

    def mixer_b(g):
        dil = DIL_GROUPS[g][1]
        cos, slo, shi = cos_ref[...], slo_ref[...], shi_ref[...]
        r = proj(d_off + g * 3 * GROUP_W, d_off + (g + 1) * 3 * GROUP_W)
        q = _rope(r[:, 0:GROUP_W], cos, slo, shi) * (Q_SCALE * LOG2E)
        k = _rope(r[:, GROUP_W:2 * GROUP_W], cos, slo, shi)
        v = r[:, 2 * GROUP_W:3 * GROUP_W]
        n_res = max(dil, 4)
        for c, t in enumerate((q, k, v)):
            perm_scr[g, 2 * c] = t[:, 0:128]
            perm_scr[g, 2 * c + 1] = t[:, 128:256]
        for res in range(n_res):
            for c in range(6):
                outs[g][0, res, :, c * 128:(c + 1) * 128] = (
                    perm_scr[g, c, pl.ds(res, tm // n_res, stride=n_res), :].astype(BF16))

    def gates(c):
        r = proj(g_off + c * 512, g_off + (c + 1) * 512)
        gate_ref[0, :, c * 512:(c + 1) * 512] = jax.nn.sigmoid(r).astype(BF16)

    parts = forget_logits()
    mixer_b(0)
    gates(0)
    gates(1)
    forget_cumsum(parts)
    mixer_b(1)
    gates(2)
    gates(3)
    mixer_b(2)
    mixer_a()


def _inproj(x, g, w_cat, w_vt, b_f, cos, slo, shi):
    B, S, D = x.shape
    tm = TM_IN
    grid = (B, S // tm)
    tok = lambda w: pl.BlockSpec((1, tm, w), lambda b, i: (b, i, 0))
    tab = pl.BlockSpec((tm, GROUP_W), lambda b, i: (i, 0))
    out_shape = (
        jax.ShapeDtypeStruct((B, S, 2 * FOX_W), BF16),
        jax.ShapeDtypeStruct((B, FOX_W, S), BF16),
        jax.ShapeDtypeStruct((B, S, 128), BF16),
        jax.ShapeDtypeStruct((B, 4, S // 4, 3 * GROUP_W), BF16),
        jax.ShapeDtypeStruct((B, 4, S // 4, 3 * GROUP_W), BF16),
        jax.ShapeDtypeStruct((B, 16, S // 16, 3 * GROUP_W), BF16),
        jax.ShapeDtypeStruct((B, S, 2 * D), BF16),
    )
    out_specs = (
        tok(2 * FOX_W),
        pl.BlockSpec((1, FOX_W, tm), lambda b, i: (b, 0, i)),
        tok(128),
        pl.BlockSpec((1, 4, tm // 4, 3 * GROUP_W), lambda b, i: (b, 0, i, 0)),
        pl.BlockSpec((1, 4, tm // 4, 3 * GROUP_W), lambda b, i: (b, 0, i, 0)),
        pl.BlockSpec((1, 16, tm // 16, 3 * GROUP_W), lambda b, i: (b, 0, i, 0)),
        tok(2 * D),
    )
    in_specs = [
        tok(D), _resident(g.shape), _resident(w_cat.shape), _resident(w_vt.shape),
        _resident(b_f.shape), tab, tab, tab,
    ]
    return pl.pallas_call(
        _inproj_kernel,
        grid=grid,
        in_specs=in_specs,
        out_specs=out_specs,
        out_shape=out_shape,
        scratch_shapes=[pltpu.VMEM((N_GROUPS, 6, tm, 128), F32), pltpu.VMEM((8, 128), F32)],
        compiler_params=pltpu.CompilerParams(
            dimension_semantics=("arbitrary", "arbitrary"), vmem_limit_bytes=VMEM_LIMIT),
        name="inproj",
    )(x, g, w_cat, w_vt, b_f, cos, slo, shi)


def _fox_kernel(q_ref, k_ref, kx_ref, vt_ref, o_ref):
    S = q_ref.shape[1]
    pw = 2 * HEAD_DIM
    hp = pl.program_id(1)
    lane = lax.broadcasted_iota(jnp.int32, (1, pw), 1)
    first = lane < HEAD_DIM
    zeros = jnp.zeros((TQ, pw), BF16)
    sel = [jnp.broadcast_to(
        jnp.where((lane >= 3 * (2 * hp + hh)) & (lane < 3 * (2 * hp + hh) + 3), 1.0, 0.0),
        (TQ, pw)).astype(BF16) for hh in range(2)]
    key = lax.broadcasted_iota(jnp.int32, (TK, 2 * TQ), 0)
    qry = lax.broadcasted_iota(jnp.int32, (TK, 2 * TQ), 1) % TQ
    causal = key <= qry

    def scores(qi):
        q2 = q_ref[0, qi * TQ:(qi + 1) * TQ, :]
        qmix = jnp.concatenate([
            jnp.concatenate([jnp.where(first, q2, zeros), sel[0]], axis=1),
            jnp.concatenate([jnp.where(first, zeros, q2), sel[1]], axis=1)], axis=0)
        n_k = (qi + 1) * TK
        kaug = jnp.concatenate([k_ref[0, 0:n_k, :], kx_ref[0, 0:n_k, :]], axis=1)
        s = lax.dot_general(kaug, qmix, _NT, preferred_element_type=F32)
        chunks = [s[c * TK:(c + 1) * TK] for c in range(qi)]
        chunks.append(jnp.where(causal, s[qi * TK:n_k], NEG_INF))
        m = jnp.max(chunks[0], axis=0, keepdims=True)
        for c in range(1, qi + 1):
            m = jnp.maximum(m, jnp.max(chunks[c], axis=0, keepdims=True))
        return chunks, m

    def outputs(qi, chunks, m):
        n_k = (qi + 1) * TK
        p = [jnp.exp2(s - m) for s in chunks]
        l = jnp.sum(p[0], axis=0, keepdims=True)
        for c in range(1, qi + 1):
            l = l + jnp.sum(p[c], axis=0, keepdims=True)
        pt = jnp.concatenate([t.astype(BF16) for t in p], axis=0)
        acc = jnp.dot(vt_ref[0, :, 0:n_k], pt, preferred_element_type=F32)
        ot = (acc / l).T
        o_ref[0, qi * TQ:(qi + 1) * TQ, :] = jnp.where(first, ot[0:TQ], ot[TQ:2 * TQ]).astype(BF16)

    n_q = S // TQ
    nxt = scores(0)
    for qi in range(n_q):
        cur = nxt
        if qi + 1 < n_q:
            nxt = scores(qi + 1)
        outputs(qi, *cur)


def _fox(qk, vt, kx):
    B, S, _ = qk.shape
    n_pairs = FOX_HEADS // 2
    pw = 2 * HEAD_DIM
    spec = lambda off: pl.BlockSpec((1, S, pw), lambda b, p: (b, 0, off + p))
    return pl.pallas_call(
        _fox_kernel,
        grid=(B, n_pairs),
        in_specs=[spec(0), spec(n_pairs),
                  pl.BlockSpec((1, S, 128), lambda b, p: (b, 0, 0)),
                  pl.BlockSpec((1, pw, S), lambda b, p: (b, p, 0))],
        out_specs=pl.BlockSpec((1, S, pw), lambda b, p: (b, 0, p)),
        out_shape=jax.ShapeDtypeStruct((B, S, FOX_W), BF16),
        compiler_params=pltpu.CompilerParams(
            dimension_semantics=("arbitrary", "arbitrary"), vmem_limit_bytes=VMEM_LIMIT),
        name="fox",
    )(qk, qk, kx, vt)


def _dil_kernel(d0_ref, d1_ref, d2_ref, o_ref, acc_scr, m_scr, l_scr, nat_scr):
    S = d0_ref.shape[1]
    nh = DIL_HEADS_PER_GROUP
    blk = DIL_BLK
    n_blocks = S // blk
    lane_head = lax.broadcasted_iota(jnp.int32, (1, GROUP_W), 1) // HEAD_DIM
    a2 = lax.broadcasted_iota(jnp.int32, (nh * blk, 2 * blk), 0) % blk
    k2 = lax.broadcasted_iota(jnp.int32, (nh * blk, 2 * blk), 1)
    bias_two = jnp.where((k2 >= a2) & (k2 <= a2 + blk), 0.0, NEG_INF)
    a1 =lax.broadcasted_iota(jnp.int32, (nh * blk, blk), 0) % blk
    k1 = lax.broadcasted_iota(jnp.int32, (nh * blk, blk), 1)
    bias_one = jnp.where(k1 <= a1, 0.0, NEG_INF)

    def assemble(x):
        out = x[(nh - 1) * blk:nh * blk]
        for h in range(nh - 2, -1, -1):
            out = jnp.where(lane_head == h, x[h * blk:(h + 1) * blk], out)
        return out

    def attend(qs, kcat, vcat, bias):
        zero = jnp.zeros_like(qs)
        qstack = jnp.concatenate(
            [jnp.where(lane_head == h, qs, zero) for h in range(nh)], axis=0)
        s = lax.dot_general(qstack, kcat, _NT, preferred_element_type=F32) + bias
        m = jnp.max(s, axis=1, keepdims=True)
        p = jnp.exp2(s - m)
        l = jnp.sum(p, axis=1, keepdims=True)
        pv = jnp.dot(p.astype(BF16), vcat, preferred_element_type=F32)
        return (assemble(pv), assemble(jnp.broadcast_to(m, pv.shape)),
                assemble(jnp.broadcast_to(l, pv.shape)))

    def put(res, pieces, first):
        acc_b, m_b, l_b = res
        for rows, idx in pieces:
            for c in range(GROUP_W // 128):
                lanes = slice(c * 128, (c + 1) * 128)
                if first:
                    acc_scr[c, idx, :] = acc_b[rows, lanes]
                    m_scr[c, idx, :] = m_b[rows, lanes]
                    l_scr[c, idx, :] = l_b[rows, lanes]
                else:
                    m_o = m_scr[c, idx, :]
                    m_n = jnp.maximum(m_o, m_b[rows, lanes])
                    e_o = jnp.exp2(m_o - m_n)
                    e_b = jnp.exp2(m_b[rows, lanes] - m_n)
                    acc_scr[c, idx, :] = acc_scr[c, idx, :] * e_o + acc_b[rows, lanes] * e_b
                    l_scr[c, idx, :] = l_scr[c, idx, :] * e_o + l_b[rows, lanes] * e_b
                    m_scr[c, idx, :] = m_n

    q_l, k_l, v_l = (slice(i * GROUP_W, (i + 1) * GROUP_W) for i in range(3))
    whole = slice(0, blk)
    n_res = 4
    slab = S // n_res
    per_slab = slab // blk

    def g1_slab(res, _):
        for pos in range(per_slab):
            r0 = _aligned(res * slab + pos * blk, blk)
            qs = d1_ref[0, pl.ds(r0, blk), q_l]
            if pos == 0:
                out = attend(qs, d1_ref[0, pl.ds(r0, blk), k_l], d1_ref[0, pl.ds(r0, blk), v_l], bias_one)
            else:
                rp = _aligned(r0 - blk, blk)
                out = attend(qs, d1_ref[0, pl.ds(rp, 2 * blk), k_l], d1_ref[0, pl.ds(rp, 2 * blk), v_l],
                             bias_two)
            put(out, [(whole, pl.ds(r0, blk))], first=True)
        return 0

    for res in range(n_res):
        g1_slab(res, 0)

    sub = blk // n_res
    qa = lax.broadcasted_iota(jnp.int32, (nh * blk, 2 * blk), 0) % blk
    kk = lax.broadcasted_iota(jnp.int32, (nh * blk, 2 * blk), 1)
    tq = blk + n_res * (qa % sub) + qa // sub
    tk = n_res * (kk % (2 * sub)) + kk // (2 * sub)
    bias0_two = jnp.where((tk <= tq) & (tq - tk <= blk), 0.0, NEG_INF)
    qa1 = lax.broadcasted_iota(jnp.int32, (nh * blk, blk), 0) % blk
    kk1 = lax.broadcasted_iota(jnp.int32, (nh * blk, blk), 1)
    bias0_one = jnp.where(n_res * (kk1 % sub) + kk1 // sub <= n_res * (qa1 % sub) + qa1 // sub,
                          0.0, NEG_INF)

    def g0_block(n, has_prev):
        rq = [_aligned(r * slab + n * sub, sub) for r in range(n_res)]
        qs = jnp.concatenate([d0_ref[0, pl.ds(rq[r], sub), q_l] for r in range(n_res)], axis=0)
        if has_prev:
            rk = [_aligned(r * slab + (n - 1) * sub, sub) for r in range(n_res)]
            kcat = jnp.concatenate([d0_ref[0, pl.ds(rk[r], 2 * sub), k_l] for r in range(n_res)], axis=0)
            vcat = jnp.concatenate([d0_ref[0, pl.ds(rk[r], 2 * sub), v_l] for r in range(n_res)], axis=0)
            out = attend(qs, kcat, vcat, bias0_two)
        else:
            kcat = jnp.concatenate([d0_ref[0, pl.ds(rq[r], sub), k_l] for r in range(n_res)], axis=0)
            vcat = jnp.concatenate([d0_ref[0, pl.ds(rq[r], sub), v_l] for r in range(n_res)], axis=0)
            out = attend(qs, kcat, vcat, bias0_one)
        put(out, [(slice(r * sub, (r + 1) * sub), pl.ds(rq[r], sub)) for r in range(n_res)],
            first=False)

    g0_block(0, False)

    def g0_rest(n, _):
        g0_block(n, True)
        return 0

    for n in range(1, n_blocks):
        g0_rest(n, 0)

    def g2_block(r16, _):
        r0 = _aligned(r16 * blk, blk)
        out = attend(d2_ref[0, pl.ds(r0, blk), q_l], d2_ref[0, pl.ds(r0, blk), k_l],
                     d2_ref[0, pl.ds(r0, blk), v_l], bias_one)
        put(out, [(whole, pl.ds((r16 % n_res) * slab + r16 // n_res, blk, stride=n_res))], first=False)
        return 0

    for r16 in range(n_blocks):
        g2_block(r16, 0)

    def finish(res, _):
        for pos in range(per_slab):
            r0 = _aligned(res * slab + pos * blk, blk)
            for c in range(GROUP_W // 128):
                nat_scr[c, pl.ds(n_res * pos * blk + res, blk, stride=n_res), :] = (
                    acc_scr[c, pl.ds(r0, blk), :] / l_scr[c, pl.ds(r0, blk), :])
        return 0

    for res in range(n_res):
        finish(res, 0)

    def emit(i, _):
        r0 = _aligned(i * 2 * blk, 2 * blk)
        for c in range(GROUP_W // 128):
            o_ref[0, pl.ds(r0, 2 * blk), c * 128:(c + 1) * 128] = nat_scr[c, pl.ds(r0, 2 * blk), :].astype(BF16)
        return 0

    for i in range(n_blocks // 2):
        emit(i, 0)


def _dilated(d0, d1, d2):
    B, S, W = d0.shape
    spec = pl.BlockSpec((1, S, W), lambda b: (b, 0, 0))
    return pl.pallas_call(
        _dil_kernel,
        grid=(B,),
        in_specs=[spec, spec, spec],
        out_specs=pl.BlockSpec((1, S, GROUP_W), lambda b: (b, 0, 0)),
        out_shape=jax.ShapeDtypeStruct((B, S, GROUP_W), BF16),
        scratch_shapes=[pltpu.VMEM((GROUP_W // 128, S, 128), F32)] * 4,
        compiler_params=pltpu.CompilerParams(
            dimension_semantics=("arbitrary",), vmem_limit_bytes=VMEM_LIMIT),
        name="dilated",
    )(d0, d1, d2)


def _outmlp_kernel(x_ref, oa_ref, ob_ref, gate_ref, wa_ref, wb_ref, wo_ref, g2_ref,
                   wu_ref, wdn_ref, g3_ref, o_ref, *, final_norm):
    d = x_ref.shape[1]
    ya = jnp.dot(oa_ref[...], wa_ref[...], preferred_element_type=F32)
    yb = jnp.dot(ob_ref[...], wb_ref[...], preferred_element_type=F32)
    mixed = gate_ref[:, 0:d].astype(F32) * ya + gate_ref[:, d:2 * d].astype(F32) * yb
    x1 = x_ref[...] + jnp.dot(mixed.astype(BF16), wo_ref[...], preferred_element_type=F32)
    h = (x1 * g2_ref[...]).astype(BF16)
    inv = lax.rsqrt(jnp.mean(x1 * x1, axis=-1, keepdims=True) + EPS)
    d_ff = wu_ref.shape[1]
    mlp = None
    for c in range(d_ff // FF_CHUNK):
        u = jnp.dot(h, wu_ref[:, c * FF_CHUNK:(c + 1) * FF_CHUNK], preferred_element_type=F32)
        a = jnp.square(jnp.maximum(u, 0.0)).astype(BF16)
        t = jnp.dot(a, wdn_ref[c * FF_CHUNK:(c + 1) * FF_CHUNK, :], preferred_element_type=F32)
        mlp = t if mlp is None else mlp + t
    x2 = x1 + (inv * inv) * mlp
    o_ref[...] = _rms(x2, g3_ref[...]) if final_norm else x2


def _outmlp(x, oa, ob, gates, w_a, w_b, w_o, g2, w_u, w_dn, g3, final_norm):
    N, D = x.shape
    tm = TM_OUT
    tok = lambda w: pl.BlockSpec((tm, w), lambda i: (i, 0))
    return pl.pallas_call(
        functools.partial(_outmlp_kernel, final_norm=final_norm),
        grid=(N // tm,),
        in_specs=[tok(D), tok(oa.shape[1]), tok(ob.shape[1]), tok(gates.shape[1]),
                  _resident(w_a.shape), _resident(w_b.shape), _resident(w_o.shape),
                  _resident(g2.shape), _resident(w_u.shape), _resident(w_dn.shape),
                  _resident(g3.shape)],
        out_specs=tok(D),
        out_shape=jax.ShapeDtypeStruct((N, D), F32),
        compiler_params=pltpu.CompilerParams(
            dimension_semantics=("arbitrary",), vmem_limit_bytes=VMEM_LIMIT),
        name="outmlp",
    )(x, oa, ob, gates, w_a, w_b, w_o, g2, w_u, w_dn, g3)


def _rope_tables(S):
    inv_freq = jnp.power(jnp.float32(ROPE_THETA),
                         -jnp.arange(ROPE_HALF, dtype=F32) * 2.0 / ROPE_DIM)
    ang = jnp.arange(S).astype(F32)[:, None] * inv_freq[None, :]
    cos, sin = jnp.cos(ang), jnp.sin(ang)
    rest = HEAD_DIM - ROPE_DIM
    one = jnp.ones((S, rest), F32)
    zero = lambda w: jnp.zeros((S, w), F32)
    cos_h = jnp.concatenate([cos, cos, one], axis=1)
    slo_h = jnp.concatenate([-sin, zero(ROPE_HALF + rest)], axis=1)
    shi_h = jnp.concatenate([zero(ROPE_HALF), sin, zero(rest)], axis=1)
    rep = lambda t: jnp.tile(t, (1, DIL_HEADS_PER_GROUP))
    return rep(cos_h), rep(slo_h), rep(shi_h)


def kernel(x, norm_attn_g, w_in, b_forget, w_branch_a, w_branch_b, w_out, norm_mlp_g, w_up,
           w_down, norm_final_g):
    B, S, D = x.shape
    depth = w_in.shape[0]
    assert S % (16 * DIL_BLK) == 0 and S // 16 == DIL_BLK, "sequence length fixed by the dilation groups"
    cos, slo, shi = _rope_tables(S)
    sizes = (FOX_W, FOX_W, FOX_W, FOX_HEADS, DIL_W, DIL_W, DIL_W, D, D)
    offs = np.concatenate([[0], np.cumsum(sizes)])
    col = lambda w, i: w[:, offs[i]:offs[i + 1]]
    for l in range(depth):
        w = w_in[l]
        w_vt = col(w, 2).T.astype(BF16)
        pad = 128 - 3 * FOX_HEADS
        w_f = jnp.pad(jnp.repeat(col(w, 3), 3, axis=1), ((0, 0), (0, pad)))
        b_f = jnp.pad(jnp.repeat(b_forget[l].astype(F32), 3), (0, pad))[None, :]
        qb, kb, vb = col(w, 4), col(w, 5), col(w, 6)
        grp = lambda t, g: t[:, g * GROUP_W:(g + 1) * GROUP_W]
        w_cat = jnp.concatenate(
            [col(w, 0), col(w, 1), w_f]
            + [grp(t, g) for g in range(N_GROUPS) for t in (qb, kb, vb)]
            + [col(w, 7), col(w, 8)], axis=1).astype(BF16)

        qk, vt, kx, d0, d1, d2, gates = _inproj(
            x, norm_attn_g[l][None, :], w_cat, w_vt, b_f, cos, slo, shi)
        oa = _fox(qk, vt, kx)
        ob = _dilated(d0.reshape(B, S, -1), d1.reshape(B, S, -1), d2.reshape(B, S, -1))
        y = _outmlp(
            x.reshape(B * S, D), oa.reshape(B * S, -1), ob.reshape(B * S, -1),
            gates.reshape(B * S, -1),
            w_branch_a[l].astype(BF16), w_branch_b[l].astype(BF16), w_out[l].astype(BF16),
            norm_mlp_g[l][None, :], w_up[l].astype(BF16), w_down[l].astype(BF16),
            norm_final_g[None, :], final_norm=(l == depth - 1))
        x = y.reshape(B, S, D)
    return x
```

```python
import functools

import numpy as np
import jax
import jax.numpy as jnp
from jax import lax
from jax.experimental import pallas as pl
from jax.experimental.pallas import tpu as pltpu

F32 = jnp.float32
BF16 = jnp.bfloat16

HEAD_DIM = 64
FOX_HEADS = 8
FOX_W = FOX_HEADS * HEAD_DIM
DIL_GROUPS = ((128, 1), (512, 4), (2048, 16))
DIL_HEADS_PER_GROUP = 4
GROUP_W = DIL_HEADS_PER_GROUP * HEAD_DIM
N_GROUPS = len(DIL_GROUPS)
DIL_W = N_GROUPS * GROUP_W
ROPE_THETA = 500000.0
ROPE_DIM = HEAD_DIM // 4
ROPE_HALF = ROPE_DIM // 2
EPS = 1e-6
NEG_INF = -1e30
Q_SCALE = HEAD_DIM ** -0.5
LOG2E = 1.4426950408889634

DIL_BLK = 128
TM_IN = 512
TM_OUT = 512
TQ = 256
TK = 256
FF_CHUNK = 1024
VMEM_LIMIT = 56 * 1024 * 1024

_NT = (((1,), (1,)), ((), ()))


def _resident(shape):
    nd = len(shape)
    return pl.BlockSpec(shape, lambda *_: (0,) * nd, pipeline_mode=pl.Buffered(1))


def _aligned(x, m):
    return x if isinstance(x, int) else pl.multiple_of(x, m)


def _rms(x, g):
    ms = jnp.mean(x * x, axis=-1, keepdims=True)
    return x * lax.rsqrt(ms + EPS) * g


def _rope(x, cos, sin_lo, sin_hi):
    w = x.shape[1]
    up = pltpu.roll(x, w - ROPE_HALF, 1)
    dn = pltpu.roll(x, ROPE_HALF, 1)
    return x * cos + up * sin_lo + dn * sin_hi


def _split3(x):
    hi = x.astype(BF16)
    r1 = x - hi.astype(F32)
    mid = r1.astype(BF16)
    lo = (r1 - mid.astype(F32)).astype(BF16)
    return hi, mid, lo


def _inproj_kernel(x_ref, g_ref, wqk_ref, wt_ref, wf_ref, wvt_ref, bf_ref,
                   cos_ref, slo_ref, shi_ref,
                   qk_ref, vt_ref, kx_ref, d0_ref, d1_ref, d2_ref, gate_ref,
                   perm_scr, carry_scr):
    tm = x_ref.shape[1]
    d = x_ref.shape[2]

    @pl.when(pl.program_id(1) == 0)
    def _reset():
        carry_scr[...] = jnp.zeros_like(carry_scr)

    xt = x_ref[0]
    h = (xt * g_ref[...]).astype(BF16)
    inv = lax.rsqrt(jnp.mean(xt * xt, axis=-1, keepdims=True) + EPS)
    inv_row = jnp.broadcast_to(inv, (tm, 128)).T[0:1, :]
    proj = lambda ref, lo, hi: inv * jnp.dot(h, ref[:, lo:hi], preferred_element_type=F32)
    g_off = 3 * DIL_W
    lane = lax.broadcasted_iota(jnp.int32, (1, 128), 1)

    def forget_logits():
        z = proj(wf_ref, 0, 128) + bf_ref[...]
        log_f = jnp.where(lane < 3 * FOX_HEADS,
                          -(jnp.maximum(-z, 0.0) + jnp.log1p(jnp.exp(-jnp.abs(z)))), 0.0)
        return _split3(log_f)

    def forget_cumsum(parts):
        row = lax.broadcasted_iota(jnp.int32, (tm, tm), 0)
        col = lax.broadcasted_iota(jnp.int32, (tm, tm), 1)
        tri = jnp.where(col <= row, 1.0, 0.0).astype(BF16)
        cum = carry_scr[0:1, :]
        for part in parts:
            cum = cum + jnp.dot(tri, part, preferred_element_type=F32)
        carry_scr[0:1, :] = cum[tm - 1:tm, :]
        hi, mid, lo = _split3(-LOG2E * cum)
        kx_ref[0] = jnp.where(lane % 3 == 0, hi, jnp.where(lane % 3 == 1, mid, lo))

    def mixer_a():
        qk_ref[0, :, 0:FOX_W] = (proj(wqk_ref, 0, FOX_W) * (Q_SCALE * LOG2E)).astype(BF16)
        qk_ref[0, :, FOX_W:2 * FOX_W] = proj(wqk_ref, FOX_W, 2 * FOX_W).astype(BF16)
        vt_ref[0] = (lax.dot_general(wvt_ref[...], h, _NT, preferred_element_type=F32)
                     * inv_row).astype(BF16)

    outs = (d0_ref, d1_ref, d2_ref)

    def mixer_b(g):
        dil = DIL_GROUPS[g][1]
        cos, slo, shi = cos_ref[...], slo_ref[...], shi_ref[...]
        part = lambda t: proj(wt_ref, t * DIL_W + g * GROUP_W, t * DIL_W + (g + 1) * GROUP_W)
        q = _rope(part(0), cos, slo, shi) * (Q_SCALE * LOG2E)
        k = _rope(part(1), cos, slo, shi)
        v = part(2)
        n_res = max(dil, 4)
        for c, t in enumerate((q, k, v)):
            perm_scr[g, 2 * c] = t[:, 0:128]
            perm_scr[g, 2 * c + 1] = t[:, 128:256]
        for res in range(n_res):
            for c in range(6):
                outs[g][0, res, :, c * 128:(c + 1) * 128] = (
                    perm_scr[g, c, pl.ds(res, tm // n_res, stride=n_res), :].astype(BF16))

    def gates(c):
        r = proj(wt_ref, g_off + c * 512, g_off + (c + 1) * 512)
        gate_ref[0, :, c * 512:(c + 1) * 512] = jax.nn.sigmoid(r).astype(BF16)

    parts = forget_logits()
    mixer_b(0)
    gates(0)
    gates(1)
    forget_cumsum(parts)
    mixer_b(1)
    gates(2)
    gates(3)
    mixer_b(2)
    mixer_a()


def _inproj(x, g, w_qk, w_t, w_f, w_vt, b_f, cos, slo, shi):
    B, S, D = x.shape
    tm = TM_IN
    grid = (B, S // tm)
    tok = lambda w: pl.BlockSpec((1, tm, w), lambda b, i: (b, i, 0))
    tab = pl.BlockSpec((tm, GROUP_W), lambda b, i: (i, 0))
    out_shape = (
        jax.ShapeDtypeStruct((B, S, 2 * FOX_W), BF16),
        jax.ShapeDtypeStruct((B, FOX_W, S), BF16),
        jax.ShapeDtypeStruct((B, S, 128), BF16),
        jax.ShapeDtypeStruct((B, 4, S // 4, 3 * GROUP_W), BF16),
        jax.ShapeDtypeStruct((B, 4, S // 4, 3 * GROUP_W), BF16),
        jax.ShapeDtypeStruct((B, 16, S // 16, 3 * GROUP_W), BF16),
        jax.ShapeDtypeStruct((B, S, 2 * D), BF16),
    )
    out_specs = (
        tok(2 * FOX_W),
        pl.BlockSpec((1, FOX_W, tm), lambda b, i: (b, 0, i)),
        tok(128),
        pl.BlockSpec((1, 4, tm // 4, 3 * GROUP_W), lambda b, i: (b, 0, i, 0)),
        pl.BlockSpec((1, 4, tm // 4, 3 * GROUP_W), lambda b, i: (b, 0, i, 0)),
        pl.BlockSpec((1, 16, tm // 16, 3 * GROUP_W), lambda b, i: (b, 0, i, 0)),
        tok(2 * D),
    )
    in_specs = [
        tok(D), _resident(g.shape), _resident(w_qk.shape), _resident(w_t.shape),
        _resident(w_f.shape), _resident(w_vt.shape), _resident(b_f.shape), tab, tab, tab,
    ]
    return pl.pallas_call(
        _inproj_kernel,
        grid=grid,
        in_specs=in_specs,
        out_specs=out_specs,
        out_shape=out_shape,
        scratch_shapes=[pltpu.VMEM((N_GROUPS, 6, tm, 128), F32), pltpu.VMEM((8, 128), F32)],
        compiler_params=pltpu.CompilerParams(
            dimension_semantics=("arbitrary", "arbitrary"), vmem_limit_bytes=VMEM_LIMIT),
        name="inproj",
    )(x, g, w_qk, w_t, w_f, w_vt, b_f, cos, slo, shi)


def _fox_kernel(q_ref, k_ref, kx_ref, vt_ref, o_ref):
    S = q_ref.shape[1]
    pw = 2 * HEAD_DIM
    hp = pl.program_id(1)
    lane = lax.broadcasted_iota(jnp.int32, (1, pw), 1)
    first = lane < HEAD_DIM
    zeros = jnp.zeros((TQ, pw), BF16)
    sel = [jnp.broadcast_to(
        jnp.where((lane >= 3 * (2 * hp + hh)) & (lane < 3 * (2 * hp + hh) + 3), 1.0, 0.0),
        (TQ, pw)).astype(BF16) for hh in range(2)]
    key = lax.broadcasted_iota(jnp.int32, (TK, 2 * TQ), 0)
    qry = lax.broadcasted_iota(jnp.int32, (TK, 2 * TQ), 1) % TQ
    causal = key <= qry

    def scores(qi):
        q2 = q_ref[0, qi * TQ:(qi + 1) * TQ, :]
        qmix = jnp.concatenate([
            jnp.concatenate([jnp.where(first, q2, zeros), sel[0]], axis=1),
            jnp.concatenate([jnp.where(first, zeros, q2), sel[1]], axis=1)], axis=0)
        n_k = (qi + 1) * TK
        kaug = jnp.concatenate([k_ref[0, 0:n_k, :], kx_ref[0, 0:n_k, :]], axis=1)
        s = lax.dot_general(kaug, qmix, _NT, preferred_element_type=F32)
        chunks = [s[c * TK:(c + 1) * TK] for c in range(qi)]
        chunks.append(jnp.where(causal, s[qi * TK:n_k], NEG_INF))
        m = jnp.max(chunks[0], axis=0, keepdims=True)
        for c in range(1, qi + 1):
            m = jnp.maximum(m, jnp.max(chunks[c], axis=0, keepdims=True))
        return chunks, m

    def outputs(qi, chunks, m):
        n_k = (qi + 1) * TK
        p = [jnp.exp2(s - m) for s in chunks]
        l = jnp.sum(p[0], axis=0, keepdims=True)
        for c in range(1, qi + 1):
            l = l + jnp.sum(p[c], axis=0, keepdims=True)
        pt = jnp.concatenate([t.astype(BF16) for t in p], axis=0)
        acc = jnp.dot(vt_ref[0, :, 0:n_k], pt, preferred_element_type=F32)
        ot = (acc / l).T
        o_ref[0, qi * TQ:(qi + 1) * TQ, :] = jnp.where(first, ot[0:TQ], ot[TQ:2 * TQ]).astype(BF16)

    n_q = S // TQ
    nxt = scores(0)
    for qi in range(n_q):
        cur = nxt
        if qi + 1 < n_q:
            nxt = scores(qi + 1)
        outputs(qi, *cur)


def _fox(qk, vt, kx):
    B, S, _ = qk.shape
    n_pairs = FOX_HEADS // 2
    pw = 2 * HEAD_DIM
    spec = lambda off: pl.BlockSpec((1, S, pw), lambda b, p: (b, 0, off + p))
    return pl.pallas_call(
        _fox_kernel,
        grid=(B, n_pairs),
        in_specs=[spec(0), spec(n_pairs),
                  pl.BlockSpec((1, S, 128), lambda b, p: (b, 0, 0)),
                  pl.BlockSpec((1, pw, S), lambda b, p: (b, p, 0))],
        out_specs=pl.BlockSpec((1, S, pw), lambda b, p: (b, 0, p)),
        out_shape=jax.ShapeDtypeStruct((B, S, FOX_W), BF16),
        compiler_params=pltpu.CompilerParams(
            dimension_semantics=("arbitrary", "arbitrary"), vmem_limit_bytes=VMEM_LIMIT),
        name="fox",
    )(qk, qk, kx, vt)


def _dil_kernel(d0_ref, d1_ref, d2_ref, o_ref, acc_scr, m_scr, l_scr, nat_scr):
    S = d0_ref.shape[1]
    nh = DIL_HEADS_PER_GROUP
    blk = DIL_BLK
    n_blocks = S // blk
    lane_head = lax.broadcasted_iota(jnp.int32, (1, GROUP_W), 1) // HEAD_DIM
    a2 = lax.broadcasted_iota(jnp.int32, (nh * blk, 2 * blk), 0) % blk
    k2 = lax.broadcasted_iota(jnp.int32, (nh * blk, 2 * blk), 1)
    bias_two = jnp.where((k2 >= a2) & (k2 <= a2 + blk), 0.0, NEG_INF)
    a1 =lax.broadcasted_iota(jnp.int32, (nh * blk, blk), 0) % blk
    k1 = lax.broadcasted_iota(jnp.int32, (nh * blk, blk), 1)
    bias_one = jnp.where(k1 <= a1, 0.0, NEG_INF)

    def assemble(x):
        out = x[(nh - 1) * blk:nh * blk]
        for h in range(nh - 2, -1, -1):
            out = jnp.where(lane_head == h, x[h * blk:(h + 1) * blk], out)
        return out

    def attend(qs, kcat, vcat, bias):
        zero = jnp.zeros_like(qs)
        qstack = jnp.concatenate(
            [jnp.where(lane_head == h, qs, zero) for h in range(nh)], axis=0)
        s = lax.dot_general(qstack, kcat, _NT, preferred_element_type=F32) + bias
        m = jnp.max(s, axis=1, keepdims=True)
        p = jnp.exp2(s - m)
        l = jnp.sum(p, axis=1, keepdims=True)
        pv = jnp.dot(p.astype(BF16), vcat, preferred_element_type=F32)
        return (assemble(pv), assemble(jnp.broadcast_to(m, pv.shape)),
                assemble(jnp.broadcast_to(l, pv.shape)))

    def put(res, pieces, first):
        acc_b, m_b, l_b = res
        for rows, idx in pieces:
            for c in range(GROUP_W // 128):
                lanes = slice(c * 128, (c + 1) * 128)
                if first:
                    acc_scr[c, idx, :] = acc_b[rows, lanes]
                    m_scr[c, idx, :] = m_b[rows, lanes]
                    l_scr[c, idx, :] = l_b[rows, lanes]
                else:
                    m_o = m_scr[c, idx, :]
                    m_n = jnp.maximum(m_o, m_b[rows, lanes])
                    e_o = jnp.exp2(m_o - m_n)
                    e_b = jnp.exp2(m_b[rows, lanes] - m_n)
                    acc_scr[c, idx, :] = acc_scr[c, idx, :] * e_o + acc_b[rows, lanes] * e_b
                    l_scr[c, idx, :] = l_scr[c, idx, :] * e_o + l_b[rows, lanes] * e_b
                    m_scr[c, idx, :] = m_n

    q_l, k_l, v_l = (slice(i * GROUP_W, (i + 1) * GROUP_W) for i in range(3))
    whole = slice(0, blk)
    n_res = 4
    slab = S // n_res
    per_slab = slab // blk

    def g1_slab(res, _):
        for pos in range(per_slab):
            r0 = _aligned(res * slab + pos * blk, blk)
            qs = d1_ref[0, pl.ds(r0, blk), q_l]
            if pos == 0:
                out = attend(qs, d1_ref[0, pl.ds(r0, blk), k_l], d1_ref[0, pl.ds(r0, blk), v_l], bias_one)
            else:
                rp = _aligned(r0 - blk, blk)
                out = attend(qs, d1_ref[0, pl.ds(rp, 2 * blk), k_l], d1_ref[0, pl.ds(rp, 2 * blk), v_l],
                             bias_two)
            put(out, [(whole, pl.ds(r0, blk))], first=True)
        return 0

    for res in range(n_res):
        g1_slab(res, 0)

    sub = blk // n_res
    qa = lax.broadcasted_iota(jnp.int32, (nh * blk, 2 * blk), 0) % blk
    kk = lax.broadcasted_iota(jnp.int32, (nh * blk, 2 * blk), 1)
    tq = blk + n_res * (qa % sub) + qa // sub
    tk = n_res * (kk % (2 * sub)) + kk // (2 * sub)
    bias0_two = jnp.where((tk <= tq) & (tq - tk <= blk), 0.0, NEG_INF)
    qa1 = lax.broadcasted_iota(jnp.int32, (nh * blk, blk), 0) % blk
    kk1 = lax.broadcasted_iota(jnp.int32, (nh * blk, blk), 1)
    bias0_one = jnp.where(n_res * (kk1 % sub) + kk1 // sub <= n_res * (qa1 % sub) + qa1 // sub,
                          0.0, NEG_INF)

    def g0_block(n, has_prev):
        rq = [_aligned(r * slab + n * sub, sub) for r in range(n_res)]
        qs = jnp.concatenate([d0_ref[0, pl.ds(rq[r], sub), q_l] for r in range(n_res)], axis=0)
        if has_prev:
            rk = [_aligned(r * slab + (n - 1) * sub, sub) for r in range(n_res)]
            kcat = jnp.concatenate([d0_ref[0, pl.ds(rk[r], 2 * sub), k_l] for r in range(n_res)], axis=0)
            vcat = jnp.concatenate([d0_ref[0, pl.ds(rk[r], 2 * sub), v_l] for r in range(n_res)], axis=0)
            out = attend(qs, kcat, vcat, bias0_two)
        else:
            kcat = jnp.concatenate([d0_ref[0, pl.ds(rq[r], sub), k_l] for r in range(n_res)], axis=0)
            vcat = jnp.concatenate([d0_ref[0, pl.ds(rq[r], sub), v_l] for r in range(n_res)], axis=0)
            out = attend(qs, kcat, vcat, bias0_one)
        put(out, [(slice(r * sub, (r + 1) * sub), pl.ds(rq[r], sub)) for r in range(n_res)],
            first=False)

    g0_block(0, False)

    def g0_rest(n, _):
        g0_block(n, True)
        return 0

    for n in range(1, n_blocks):
        g0_rest(n, 0)

    def g2_block(r16, _):
        r0 = _aligned(r16 * blk, blk)
        out = attend(d2_ref[0, pl.ds(r0, blk), q_l], d2_ref[0, pl.ds(r0, blk), k_l],
                     d2_ref[0, pl.ds(r0, blk), v_l], bias_one)
        put(out, [(whole, pl.ds((r16 % n_res) * slab + r16 // n_res, blk, stride=n_res))], first=False)
        return 0

    for r16 in range(n_blocks):
        g2_block(r16, 0)

    def finish(res, _):
        for pos in range(per_slab):
            r0 = _aligned(res * slab + pos * blk, blk)
            for c in range(GROUP_W // 128):
                nat_scr[c, pl.ds(n_res * pos * blk + res, blk, stride=n_res), :] = (
                    acc_scr[c, pl.ds(r0, blk), :] / l_scr[c, pl.ds(r0, blk), :])
        return 0

    for res in range(n_res):
        finish(res, 0)

    def emit(i, _):
        r0 = _aligned(i * 2 * blk, 2 * blk)
        for c in range(GROUP_W // 128):
            o_ref[0, pl.ds(r0, 2 * blk), c * 128:(c + 1) * 128] = nat_scr[c, pl.ds(r0, 2 * blk), :].astype(BF16)
        return 0

    for i in range(n_blocks // 2):
        emit(i, 0)


def _dilated(d0, d1, d2):
    B, S, W = d0.shape
    spec = pl.BlockSpec((1, S, W), lambda b: (b, 0, 0))
    return pl.pallas_call(
        _dil_kernel,
        grid=(B,),
        in_specs=[spec, spec, spec],
        out_specs=pl.BlockSpec((1, S, GROUP_W), lambda b: (b, 0, 0)),
        out_shape=jax.ShapeDtypeStruct((B, S, GROUP_W), BF16),
        scratch_shapes=[pltpu.VMEM((GROUP_W // 128, S, 128), F32)] * 4,
        compiler_params=pltpu.CompilerParams(
            dimension_semantics=("arbitrary",), vmem_limit_bytes=VMEM_LIMIT),
        name="dilated",
    )(d0, d1, d2)


def _outmlp_kernel(x_ref, oa_ref, ob_ref, gate_ref, wa_ref, wb_ref, wo_ref, g2_ref,
                   wu_ref, wdn_ref, g3_ref, o_ref, *, final_norm):
    d = x_ref.shape[1]
    ya = jnp.dot(oa_ref[...], wa_ref[...], preferred_element_type=F32)
    yb = jnp.dot(ob_ref[...], wb_ref[...], preferred_element_type=F32)
    mixed = gate_ref[:, 0:d].astype(F32) * ya + gate_ref[:, d:2 * d].astype(F32) * yb
    x1 = x_ref[...] + jnp.dot(mixed.astype(BF16), wo_ref[...], preferred_element_type=F32)
    h = (x1 * g2_ref[...]).astype(BF16)
    inv = lax.rsqrt(jnp.mean(x1 * x1, axis=-1, keepdims=True) + EPS)
    d_ff = wu_ref.shape[1]
    mlp = None
    for c in range(d_ff // FF_CHUNK):
        u = jnp.dot(h, wu_ref[:, c * FF_CHUNK:(c + 1) * FF_CHUNK], preferred_element_type=F32)
        a = jnp.square(jnp.maximum(u, 0.0)).astype(BF16)
        t = jnp.dot(a, wdn_ref[c * FF_CHUNK:(c + 1) * FF_CHUNK, :], preferred_element_type=F32)
        mlp = t if mlp is None else mlp + t
    x2 = x1 + (inv * inv) * mlp
    o_ref[...] = _rms(x2, g3_ref[...]) if final_norm else x2


def _outmlp(x, oa, ob, gates, w_a, w_b, w_o, g2, w_u, w_dn, g3, final_norm):
    N, D = x.shape
    tm = TM_OUT
    tok = lambda w: pl.BlockSpec((tm, w), lambda i: (i, 0))
    return pl.pallas_call(
        functools.partial(_outmlp_kernel, final_norm=final_norm),
        grid=(N // tm,),
        in_specs=[tok(D), tok(oa.shape[1]), tok(ob.shape[1]), tok(gates.shape[1]),
                  _resident(w_a.shape), _resident(w_b.shape), _resident(w_o.shape),
                  _resident(g2.shape), _resident(w_u.shape), _resident(w_dn.shape),
                  _resident(g3.shape)],
        out_specs=tok(D),
        out_shape=jax.ShapeDtypeStruct((N, D), F32),
        compiler_params=pltpu.CompilerParams(
            dimension_semantics=("arbitrary",), vmem_limit_bytes=VMEM_LIMIT),
        name="outmlp",
    )(x, oa, ob, gates, w_a, w_b, w_o, g2, w_u, w_dn, g3)


def _rope_tables(S):
    inv_freq = jnp.power(jnp.float32(ROPE_THETA),
                         -jnp.arange(ROPE_HALF, dtype=F32) * 2.0 / ROPE_DIM)
    ang = jnp.arange(S).astype(F32)[:, None] * inv_freq[None, :]
    cos, sin = jnp.cos(ang), jnp.sin(ang)
    rest = HEAD_DIM - ROPE_DIM
    one = jnp.ones((S, rest), F32)
    zero = lambda w: jnp.zeros((S, w), F32)
    cos_h = jnp.concatenate([cos, cos, one], axis=1)
    slo_h = jnp.concatenate([-sin, zero(ROPE_HALF + rest)], axis=1)
    shi_h = jnp.concatenate([zero(ROPE_HALF), sin, zero(rest)], axis=1)
    rep = lambda t: jnp.tile(t, (1, DIL_HEADS_PER_GROUP))
    return rep(cos_h), rep(slo_h), rep(shi_h)


def kernel(x, norm_attn_g, w_in, b_forget, w_branch_a, w_branch_b, w_out, norm_mlp_g, w_up,
           w_down, norm_final_g):
    B, S, D = x.shape
    depth = w_in.shape[0]
    assert S % (16 * DIL_BLK) == 0 and S // 16 == DIL_BLK, "sequence length fixed by the dilation groups"
    cos, slo, shi = _rope_tables(S)
    sizes = (FOX_W, FOX_W, FOX_W, FOX_HEADS, DIL_W, DIL_W, DIL_W, D, D)
    offs = np.concatenate([[0], np.cumsum(sizes)])
    col = lambda w, i: w[:, offs[i]:offs[i + 1]]
    for l in range(depth):
        w = w_in[l]
        w_qk = w[:, :offs[2]].astype(BF16)
        w_vt = col(w, 2).T.astype(BF16)
        w_t = w[:, offs[4]:].astype(BF16)
        pad = 128 - 3 * FOX_HEADS
        w_f = jnp.pad(jnp.repeat(col(w, 3), 3, axis=1), ((0, 0), (0, pad))).astype(BF16)
        b_f = jnp.pad(jnp.repeat(b_forget[l].astype(F32), 3), (0, pad))[None, :]

        qk, vt, kx, d0, d1, d2, gates = _inproj(
            x, norm_attn_g[l][None, :], w_qk, w_t, w_f, w_vt, b_f, cos, slo, shi)
        oa = _fox(qk, vt, kx)
        ob = _dilated(d0.reshape(B, S, -1), d1.reshape(B, S, -1), d2.reshape(B, S, -1))
        y = _outmlp(
            x.reshape(B * S, D), oa.reshape(B * S, -1), ob.reshape(B * S, -1),
            gates.reshape(B * S, -1),
            w_branch_a[l].astype(BF16), w_branch_b[l].astype(BF16), w_out[l].astype(BF16),
            norm_mlp_g[l][None, :], w_up[l].astype(BF16), w_down[l].astype(BF16),
            norm_final_g[None, :], final_norm=(l == depth - 1))
        x = y.reshape(B, S, D)
    return x
```

```python
import functools

import numpy as np
import jax
import jax.numpy as jnp
from jax import lax
from jax.experimental import pallas as pl
from jax.experimental.pallas import tpu as pltpu

F32 = jnp.float32
BF16 = jnp.bfloat16

HEAD_DIM = 64
FOX_HEADS = 8
FOX_W = FOX_HEADS * HEAD_DIM
DIL_GROUPS = ((128, 1), (512, 4), (2048, 16))
DIL_HEADS_PER_GROUP = 4
GROUP_W = DIL_HEADS_PER_GROUP * HEAD_DIM
N_GROUPS = len(DIL_GROUPS)
DIL_W = N_GROUPS * GROUP_W
ROPE_THETA = 500000.0
ROPE_DIM = HEAD_DIM // 4
ROPE_HALF = ROPE_DIM // 2
EPS = 1e-6
NEG_INF = -1e30
Q_SCALE = HEAD_DIM ** -0.5
LOG2E = 1.4426950408889634

DIL_BLK = 128
TM_IN = 512
TM_OUT = 512
TQ = 256
TK = 256
FF_CHUNK = 1024
VMEM_LIMIT = 56 * 1024 * 1024

_NT = (((1,), (1,)), ((), ()))


def _resident(shape):
    nd = len(shape)
    return pl.BlockSpec(shape, lambda *_: (0,) * nd, pipeline_mode=pl.Buffered(1))


def _aligned(x, m):
    return x if isinstance(x, int) else pl.multiple_of(x, m)


def _rms(x, g):
    ms = jnp.mean(x * x, axis=-1, keepdims=True)
    return x * lax.rsqrt(ms + EPS) * g


def _rope(x, cos, sin_lo, sin_hi):
    w = x.shape[1]
    up = pltpu.roll(x, w - ROPE_HALF, 1)
    dn = pltpu.roll(x, ROPE_HALF, 1)
    return x * cos + up * sin_lo + dn * sin_hi


def _split3(x):
    hi = x.astype(BF16)
    r1 = x - hi.astype(F32)
    mid = r1.astype(BF16)
    lo = (r1 - mid.astype(F32)).astype(BF16)
    return hi, mid, lo


def _inproj_kernel(x_ref, g_ref, wqk_ref, wt_ref, wf_ref, bf_ref,
                   cos_ref, slo_ref, shi_ref,
                   qk_ref, vt_ref, kx_ref, d0_ref, d1_ref, d2_ref, gate_ref,
                   perm_scr, carry_scr, wvt_ref):
    tm = x_ref.shape[1]
    d = x_ref.shape[2]

    @pl.when((pl.program_id(0) == 0) & (pl.program_id(1) == 0))
    def _transpose_v_weights():
        wv = wqk_ref[:, 2 * FOX_W:3 * FOX_W].astype(F32)
        wvt_ref[...] = wv.T.astype(BF16)

    @pl.when(pl.program_id(1) == 0)
    def _reset():
        carry_scr[...] = jnp.zeros_like(carry_scr)

    xt = x_ref[0]
    h = (xt * g_ref[...]).astype(BF16)
    inv = lax.rsqrt(jnp.mean(xt * xt, axis=-1, keepdims=True) + EPS)
    inv_row = jnp.broadcast_to(inv, (tm, 128)).T[0:1, :]
    proj = lambda ref, lo, hi: inv * jnp.dot(h, ref[:, lo:hi], preferred_element_type=F32)
    g_off = 3 * DIL_W
    lane = lax.broadcasted_iota(jnp.int32, (1, 128), 1)

    def forget_logits():
        z = proj(wf_ref, 0, 128) + bf_ref[...]
        log_f = jnp.where(lane < 3 * FOX_HEADS,
                          -(jnp.maximum(-z, 0.0) + jnp.log1p(jnp.exp(-jnp.abs(z)))), 0.0)
        return _split3(log_f)

    def forget_cumsum(parts):
        row = lax.broadcasted_iota(jnp.int32, (tm, tm), 0)
        col = lax.broadcasted_iota(jnp.int32, (tm, tm), 1)
        tri = jnp.where(col <= row, 1.0, 0.0).astype(BF16)
        cum = carry_scr[0:1, :]
        for part in parts:
            cum = cum + jnp.dot(tri, part, preferred_element_type=F32)
        carry_scr[0:1, :] = cum[tm - 1:tm, :]
        hi, mid, lo = _split3(-LOG2E * cum)
        kx_ref[0] = jnp.where(lane % 3 == 0, hi, jnp.where(lane % 3 == 1, mid, lo))

    def mixer_a():
        qk_ref[0, :, 0:FOX_W] = (proj(wqk_ref, 0, FOX_W) * (Q_SCALE * LOG2E)).astype(BF16)
        qk_ref[0, :, FOX_W:2 * FOX_W] = proj(wqk_ref, FOX_W, 2 * FOX_W).astype(BF16)
        vt_ref[0] = (lax.dot_general(wvt_ref[...], h, _NT, preferred_element_type=F32)
                     * inv_row).astype(BF16)

    outs = (d0_ref, d1_ref, d2_ref)

    def mixer_b(g):
        dil = DIL_GROUPS[g][1]
        cos, slo, shi = cos_ref[...], slo_ref[...], shi_ref[...]
        part = lambda t: proj(wt_ref, t * DIL_W + g * GROUP_W, t * DIL_W + (g + 1) * GROUP_W)
        q = _rope(part(0), cos, slo, shi) * (Q_SCALE * LOG2E)
        k = _rope(part(1), cos, slo, shi)
        v = part(2)
        n_res = max(dil, 4)
        for c, t in enumerate((q, k, v)):
            perm_scr[g, 2 * c] = t[:, 0:128]
            perm_scr[g, 2 * c + 1] = t[:, 128:256]
        for res in range(n_res):
            for c in range(6):
                outs[g][0, res, :, c * 128:(c + 1) * 128] = (
                    perm_scr[g, c, pl.ds(res, tm // n_res, stride=n_res), :].astype(BF16))

    def gates(c):
        r = proj(wt_ref, g_off + c * 512, g_off + (c + 1) * 512)
        gate_ref[0, :, c * 512:(c + 1) * 512] = jax.nn.sigmoid(r).astype(BF16)

    parts = forget_logits()
    mixer_b(0)
    gates(0)
    gates(1)
    forget_cumsum(parts)
    mixer_b(1)
    gates(2)
    gates(3)
    mixer_b(2)
    mixer_a()


def _inproj(x, g, w_qk, w_t, w_f, b_f, cos, slo, shi):
    B, S, D = x.shape
    tm = TM_IN
    grid = (B, S // tm)
    tok = lambda w: pl.BlockSpec((1, tm, w), lambda b, i: (b, i, 0))
    tab = pl.BlockSpec((tm, GROUP_W), lambda b, i: (i, 0))
    out_shape = (
        jax.ShapeDtypeStruct((B, S, 2 * FOX_W), BF16),
        jax.ShapeDtypeStruct((B, FOX_W, S), BF16),
        jax.ShapeDtypeStruct((B, S, 128), BF16),
        jax.ShapeDtypeStruct((B, 4, S // 4, 3 * GROUP_W), BF16),
        jax.ShapeDtypeStruct((B, 4, S // 4, 3 * GROUP_W), BF16),
        jax.ShapeDtypeStruct((B, 16, S // 16, 3 * GROUP_W), BF16),
        jax.ShapeDtypeStruct((B, S, 2 * D), BF16),
    )
    out_specs = (
        tok(2 * FOX_W),
        pl.BlockSpec((1, FOX_W, tm), lambda b, i: (b, 0, i)),
        tok(128),
        pl.BlockSpec((1, 4, tm // 4, 3 * GROUP_W), lambda b, i: (b, 0, i, 0)),
        pl.BlockSpec((1, 4, tm // 4, 3 * GROUP_W), lambda b, i: (b, 0, i, 0)),
        pl.BlockSpec((1, 16, tm // 16, 3 * GROUP_W), lambda b, i: (b, 0, i, 0)),
        tok(2 * D),
    )
    in_specs = [
        tok(D), _resident(g.shape), _resident(w_qk.shape), _resident(w_t.shape),
        _resident(w_f.shape), _resident(b_f.shape), tab, tab, tab,
    ]
    return pl.pallas_call(
        _inproj_kernel,
        grid=grid,
        in_specs=in_specs,
        out_specs=out_specs,
        out_shape=out_shape,
        scratch_shapes=[pltpu.VMEM((N_GROUPS, 6, tm, 128), F32), pltpu.VMEM((8, 128), F32),
                        pltpu.VMEM((FOX_W, D), BF16)],
        compiler_params=pltpu.CompilerParams(
            dimension_semantics=("arbitrary", "arbitrary"), vmem_limit_bytes=VMEM_LIMIT),
        name="inproj",
    )(x, g, w_qk, w_t, w_f, b_f, cos, slo, shi)


def _fox_kernel(q_ref, k_ref, kx_ref, vt_ref, o_ref):
    S = q_ref.shape[1]
    pw = 2 * HEAD_DIM
    hp = pl.program_id(1)
    lane = lax.broadcasted_iota(jnp.int32, (1, pw), 1)
    first = lane < HEAD_DIM
    zeros = jnp.zeros((TQ, pw), BF16)
    sel = [jnp.broadcast_to(
        jnp.where((lane >= 3 * (2 * hp + hh)) & (lane < 3 * (2 * hp + hh) + 3), 1.0, 0.0),
        (TQ, pw)).astype(BF16) for hh in range(2)]
    key = lax.broadcasted_iota(jnp.int32, (TK, 2 * TQ), 0)
    qry = lax.broadcasted_iota(jnp.int32, (TK, 2 * TQ), 1) % TQ
    causal = key <= qry

    def scores(qi):
        q2 = q_ref[0, qi * TQ:(qi + 1) * TQ, :]
        qmix = jnp.concatenate([
            jnp.concatenate([jnp.where(first, q2, zeros), sel[0]], axis=1),
            jnp.concatenate([jnp.where(first, zeros, q2), sel[1]], axis=1)], axis=0)
        n_k = (qi + 1) * TK
        kaug = jnp.concatenate([k_ref[0, 0:n_k, :], kx_ref[0, 0:n_k, :]], axis=1)
        s = lax.dot_general(kaug, qmix, _NT, preferred_element_type=F32)
        chunks = [s[c * TK:(c + 1) * TK] for c in range(qi)]
        chunks.append(jnp.where(causal, s[qi * TK:n_k], NEG_INF))
        m = jnp.max(chunks[0], axis=0, keepdims=True)
        for c in range(1, qi + 1):
            m = jnp.maximum(m, jnp.max(chunks[c], axis=0, keepdims=True))
        return chunks, m

    def outputs(qi, chunks, m):
        n_k = (qi + 1) * TK
        p = [jnp.exp2(s - m) for s in chunks]
        l = jnp.sum(p[0], axis=0, keepdims=True)
        for c in range(1, qi + 1):
            l = l + jnp.sum(p[c], axis=0, keepdims=True)
        pt = jnp.concatenate([t.astype(BF16) for t in p], axis=0)
        acc = jnp.dot(vt_ref[0, :, 0:n_k], pt, preferred_element_type=F32)
        ot = (acc / l).T
        o_ref[0, qi * TQ:(qi + 1) * TQ, :] = jnp.where(first, ot[0:TQ], ot[TQ:2 * TQ]).astype(BF16)

    n_q = S // TQ
    nxt = scores(0)
    for qi in range(n_q):
        cur = nxt
        if qi + 1 < n_q:
            nxt = scores(qi + 1)
        outputs(qi, *cur)


def _fox(qk, vt, kx):
    B, S, _ = qk.shape
    n_pairs = FOX_HEADS // 2
    pw = 2 * HEAD_DIM
    spec = lambda off: pl.BlockSpec((1, S, pw), lambda b, p: (b, 0, off + p))
    return pl.pallas_call(
        _fox_kernel,
        grid=(B, n_pairs),
        in_specs=[spec(0), spec(n_pairs),
                  pl.BlockSpec((1, S, 128), lambda b, p: (b, 0, 0)),
                  pl.BlockSpec((1, pw, S), lambda b, p: (b, p, 0))],
        out_specs=pl.BlockSpec((1, S, pw), lambda b, p: (b, 0, p)),
        out_shape=jax.ShapeDtypeStruct((B, S, FOX_W), BF16),
        compiler_params=pltpu.CompilerParams(
            dimension_semantics=("arbitrary", "arbitrary"), vmem_limit_bytes=VMEM_LIMIT),
        name="fox",
    )(qk, qk, kx, vt)


def _dil_kernel(d0_ref, d1_ref, d2_ref, o_ref, acc_scr, m_scr, l_scr, nat_scr):
    S = d0_ref.shape[1]
    nh = DIL_HEADS_PER_GROUP
    blk = DIL_BLK
    n_blocks = S // blk
    lane_head = lax.broadcasted_iota(jnp.int32, (1, GROUP_W), 1) // HEAD_DIM
    a2 = lax.broadcasted_iota(jnp.int32, (nh * blk, 2 * blk), 0) % blk
    k2 = lax.broadcasted_iota(jnp.int32, (nh * blk, 2 * blk), 1)
    bias_two = jnp.where((k2 >= a2) & (k2 <= a2 + blk), 0.0, NEG_INF)
    a1 =lax.broadcasted_iota(jnp.int32, (nh * blk, blk), 0) % blk
    k1 = lax.broadcasted_iota(jnp.int32, (nh * blk, blk), 1)
    bias_one = jnp.where(k1 <= a1, 0.0, NEG_INF)

    def assemble(x):
        out = x[(nh - 1) * blk:nh * blk]
        for h in range(nh - 2, -1, -1):
            out = jnp.where(lane_head == h, x[h * blk:(h + 1) * blk], out)
        return out

    def attend(qs, kcat, vcat, bias):
        zero = jnp.zeros_like(qs)
        qstack = jnp.concatenate(
            [jnp.where(lane_head == h, qs, zero) for h in range(nh)], axis=0)
        s = lax.dot_general(qstack, kcat, _NT, preferred_element_type=F32) + bias
        m = jnp.max(s, axis=1, keepdims=True)
        p = jnp.exp2(s - m)
        l = jnp.sum(p, axis=1, keepdims=True)
        pv = jnp.dot(p.astype(BF16), vcat, preferred_element_type=F32)
        return (assemble(pv), assemble(jnp.broadcast_to(m, pv.shape)),
                assemble(jnp.broadcast_to(l, pv.shape)))

    def put(res, pieces, first):
        acc_b, m_b, l_b = res
        for rows, idx in pieces:
            for c in range(GROUP_W // 128):
                lanes = slice(c * 128, (c + 1) * 128)
                if first:
                    acc_scr[c, idx, :] = acc_b[rows, lanes]
                    m_scr[c, idx, :] = m_b[rows, lanes]
                    l_scr[c, idx, :] = l_b[rows, lanes]
                else:
                    m_o = m_scr[c, idx, :]
                    m_n = jnp.maximum(m_o, m_b[rows, lanes])
                    e_o = jnp.exp2(m_o - m_n)
                    e_b = jnp.exp2(m_b[rows, lanes] - m_n)
                    acc_scr[c, idx, :] = acc_scr[c, idx, :] * e_o + acc_b[rows, lanes] * e_b
                    l_scr[c, idx, :] = l_scr[c, idx, :] * e_o + l_b[rows, lanes] * e_b
                    m_scr[c, idx, :] = m_n

    q_l, k_l, v_l = (slice(i * GROUP_W, (i + 1) * GROUP_W) for i in range(3))
    whole = slice(0, blk)
    n_res = 4
    slab = S // n_res
    per_slab = slab // blk

    def g1_slab(res, _):
        for pos in range(per_slab):
            r0 = _aligned(res * slab + pos * blk, blk)
            qs = d1_ref[0, pl.ds(r0, blk), q_l]
            if pos == 0:
                out = attend(qs, d1_ref[0, pl.ds(r0, blk), k_l], d1_ref[0, pl.ds(r0, blk), v_l], bias_one)
            else:
                rp = _aligned(r0 - blk, blk)
                out = attend(qs, d1_ref[0, pl.ds(rp, 2 * blk), k_l], d1_ref[0, pl.ds(rp, 2 * blk), v_l],
                             bias_two)
            put(out, [(whole, pl.ds(r0, blk))], first=True)
        return 0

    for res in range(n_res):
        g1_slab(res, 0)

    sub = blk // n_res
    qa = lax.broadcasted_iota(jnp.int32, (nh * blk, 2 * blk), 0) % blk
    kk = lax.broadcasted_iota(jnp.int32, (nh * blk, 2 * blk), 1)
    tq = blk + n_res * (qa % sub) + qa // sub
    tk = n_res * (kk % (2 * sub)) + kk // (2 * sub)
    bias0_two = jnp.where((tk <= tq) & (tq - tk <= blk), 0.0, NEG_INF)
    qa1 = lax.broadcasted_iota(jnp.int32, (nh * blk, blk), 0) % blk
    kk1 = lax.broadcasted_iota(jnp.int32, (nh * blk, blk), 1)
    bias0_one = jnp.where(n_res * (kk1 % sub) + kk1 // sub <= n_res * (qa1 % sub) + qa1 // sub,
                          0.0, NEG_INF)

    def g0_block(n, has_prev):
        rq = [_aligned(r * slab + n * sub, sub) for r in range(n_res)]
        qs = jnp.concatenate([d0_ref[0, pl.ds(rq[r], sub), q_l] for r in range(n_res)], axis=0)
        if has_prev:
            rk = [_aligned(r * slab + (n - 1) * sub, sub) for r in range(n_res)]
            kcat = jnp.concatenate([d0_ref[0, pl.ds(rk[r], 2 * sub), k_l] for r in range(n_res)], axis=0)
            vcat = jnp.concatenate([d0_ref[0, pl.ds(rk[r], 2 * sub), v_l] for r in range(n_res)], axis=0)
            out = attend(qs, kcat, vcat, bias0_two)
        else:
            kcat = jnp.concatenate([d0_ref[0, pl.ds(rq[r], sub), k_l] for r in range(n_res)], axis=0)
            vcat = jnp.concatenate([d0_ref[0, pl.ds(rq[r], sub), v_l] for r in range(n_res)], axis=0)
            out = attend(qs, kcat, vcat, bias0_one)
        put(out, [(slice(r * sub, (r + 1) * sub), pl.ds(rq[r], sub)) for r in range(n_res)],
            first=False)

    g0_block(0, False)

    def g0_rest(n, _):
        g0_block(n, True)
        return 0

    for n in range(1, n_blocks):
        g0_rest(n, 0)

    def g2_block(r16, _):
        r0 = _aligned(r16 * blk, blk)
        out = attend(d2_ref[0, pl.ds(r0, blk), q_l], d2_ref[0, pl.ds(r0, blk), k_l],
                     d2_ref[0, pl.ds(r0, blk), v_l], bias_one)
        put(out, [(whole, pl.ds((r16 % n_res) * slab + r16 // n_res, blk, stride=n_res))], first=False)
        return 0

    for r16 in range(n_blocks):
        g2_block(r16, 0)

    def finish(res, _):
        for pos in range(per_slab):
            r0 = _aligned(res * slab + pos * blk, blk)
            for c in range(GROUP_W // 128):
                nat_scr[c, pl.ds(n_res * pos * blk + res, blk, stride=n_res), :] = (
                    acc_scr[c, pl.ds(r0, blk), :] / l_scr[c, pl.ds(r0, blk), :])
        return 0

    for res in range(n_res):
        finish(res, 0)

    def emit(i, _):
        r0 = _aligned(i * 2 * blk, 2 * blk)
        for c in range(GROUP_W // 128):
            o_ref[0, pl.ds(r0, 2 * blk), c * 128:(c + 1) * 128] = nat_scr[c, pl.ds(r0, 2 * blk), :].astype(BF16)
        return 0

    for i in range(n_blocks // 2):
        emit(i, 0)


def _dilated(d0, d1, d2):
    B, S, W = d0.shape
    spec = pl.BlockSpec((1, S, W), lambda b: (b, 0, 0))
    return pl.pallas_call(
        _dil_kernel,
        grid=(B,),
        in_specs=[spec, spec, spec],
        out_specs=pl.BlockSpec((1, S, GROUP_W), lambda b: (b, 0, 0)),
        out_shape=jax.ShapeDtypeStruct((B, S, GROUP_W), BF16),
        scratch_shapes=[pltpu.VMEM((GROUP_W // 128, S, 128), F32)] * 4,
        compiler_params=pltpu.CompilerParams(
            dimension_semantics=("arbitrary",), vmem_limit_bytes=VMEM_LIMIT),
        name="dilated",
    )(d0, d1, d2)


def _outmlp_kernel(x_ref, oa_ref, ob_ref, gate_ref, wa_ref, wb_ref, wo_ref, g2_ref,
                   wu_ref, wdn_ref, g3_ref, o_ref, *, final_norm):
    d = x_ref.shape[1]
    ya = jnp.dot(oa_ref[...], wa_ref[...], preferred_element_type=F32)
    yb = jnp.dot(ob_ref[...], wb_ref[...], preferred_element_type=F32)
    mixed = gate_ref[:, 0:d].astype(F32) * ya + gate_ref[:, d:2 * d].astype(F32) * yb
    x1 = x_ref[...] + jnp.dot(mixed.astype(BF16), wo_ref[...], preferred_element_type=F32)
    h = (x1 * g2_ref[...]).astype(BF16)
    inv = lax.rsqrt(jnp.mean(x1 * x1, axis=-1, keepdims=True) + EPS)
    d_ff = wu_ref.shape[1]
    mlp = None
    for c in range(d_ff // FF_CHUNK):
        u = jnp.dot(h, wu_ref[:, c * FF_CHUNK:(c + 1) * FF_CHUNK], preferred_element_type=F32)
        a = jnp.square(jnp.maximum(u, 0.0)).astype(BF16)
        t = jnp.dot(a, wdn_ref[c * FF_CHUNK:(c + 1) * FF_CHUNK, :], preferred_element_type=F32)
        mlp = t if mlp is None else mlp + t
    x2 = x1 + (inv * inv) * mlp
    o_ref[...] = _rms(x2, g3_ref[...]) if final_norm else x2


def _outmlp(x, oa, ob, gates, w_a, w_b, w_o, g2, w_u, w_dn, g3, final_norm):
    N, D = x.shape
    tm = TM_OUT
    tok = lambda w: pl.BlockSpec((tm, w), lambda i: (i, 0))
    return pl.pallas_call(
        functools.partial(_outmlp_kernel, final_norm=final_norm),
        grid=(N // tm,),
        in_specs=[tok(D), tok(oa.shape[1]), tok(ob.shape[1]), tok(gates.shape[1]),
                  _resident(w_a.shape), _resident(w_b.shape), _resident(w_o.shape),
                  _resident(g2.shape), _resident(w_u.shape), _resident(w_dn.shape),
                  _resident(g3.shape)],
        out_specs=tok(D),
        out_shape=jax.ShapeDtypeStruct((N, D), F32),
        compiler_params=pltpu.CompilerParams(
            dimension_semantics=("arbitrary",), vmem_limit_bytes=VMEM_LIMIT),
        name="outmlp",
    )(x, oa, ob, gates, w_a, w_b, w_o, g2, w_u, w_dn, g3)


def _rope_tables(S):
    inv_freq = jnp.power(jnp.float32(ROPE_THETA),
                         -jnp.arange(ROPE_HALF, dtype=F32) * 2.0 / ROPE_DIM)
    ang = jnp.arange(S).astype(F32)[:, None] * inv_freq[None, :]
    cos, sin = jnp.cos(ang), jnp.sin(ang)
    rest = HEAD_DIM - ROPE_DIM
    one = jnp.ones((S, rest), F32)
    zero = lambda w: jnp.zeros((S, w), F32)
    cos_h = jnp.concatenate([cos, cos, one], axis=1)
    slo_h = jnp.concatenate([-sin, zero(ROPE_HALF + rest)], axis=1)
    shi_h = jnp.concatenate([zero(ROPE_HALF), sin, zero(rest)], axis=1)
    rep = lambda t: jnp.tile(t, (1, DIL_HEADS_PER_GROUP))
    return rep(cos_h), rep(slo_h), rep(shi_h)


def kernel(x, norm_attn_g, w_in, b_forget, w_branch_a, w_branch_b, w_out, norm_mlp_g, w_up,
           w_down, norm_final_g):
    B, S, D = x.shape
    depth = w_in.shape[0]
    assert S % (16 * DIL_BLK) == 0 and S // 16 == DIL_BLK, "sequence length fixed by the dilation groups"
    cos, slo, shi = _rope_tables(S)
    sizes = (FOX_W, FOX_W, FOX_W, FOX_HEADS, DIL_W, DIL_W, DIL_W, D, D)
    offs = np.concatenate([[0], np.cumsum(sizes)])
    col = lambda w, i: w[:, offs[i]:offs[i + 1]]
    for l in range(depth):
        w = w_in[l]
        w_qk = w[:, :offs[3]].astype(BF16)
        w_t = w[:, offs[4]:].astype(BF16)
        pad = 128 - 3 * FOX_HEADS
        w_f = jnp.pad(jnp.repeat(col(w, 3), 3, axis=1), ((0, 0), (0, pad))).astype(BF16)
        b_f = jnp.pad(jnp.repeat(b_forget[l].astype(F32), 3), (0, pad))[None, :]

        qk, vt, kx, d0, d1, d2, gates = _inproj(
            x, norm_attn_g[l][None, :], w_qk, w_t, w_f, b_f, cos, slo, shi)
        oa = _fox(qk, vt, kx)
        ob = _dilated(d0.reshape(B, S, -1), d1.reshape(B, S, -1), d2.reshape(B, S, -1))
        y = _outmlp(
            x.reshape(B * S, D), oa.reshape(B * S, -1), ob.reshape(B * S, -1),
            gates.reshape(B * S, -1),
            w_branch_a[l].astype(BF16), w_branch_b[l].astype(BF16), w_out[l].astype(BF16),
            norm_mlp_g[l][None, :], w_up[l].astype(BF16), w_down[l].astype(BF16),
            norm_final_g[None, :], final_norm=(l == depth - 1))
        x = y.reshape(B, S, D)
    return x
```

```python
import functools

import numpy as np
import jax
import jax.numpy as jnp
from jax import lax
from jax.experimental import pallas as pl
from jax.experimental.pallas import tpu as pltpu

F32 = jnp.float32
BF16 = jnp.bfloat16

HEAD_DIM = 64
FOX_HEADS = 8
FOX_W = FOX_HEADS * HEAD_DIM
DIL_GROUPS = ((128, 1), (512, 4), (2048, 16))
DIL_HEADS_PER_GROUP = 4
GROUP_W = DIL_HEADS_PER_GROUP * HEAD_DIM
N_GROUPS = len(DIL_GROUPS)
DIL_W = N_GROUPS * GROUP_W
ROPE_THETA = 500000.0
ROPE_DIM = HEAD_DIM // 4
ROPE_HALF = ROPE_DIM // 2
EPS = 1e-6
NEG_INF = -1e30
Q_SCALE = HEAD_DIM ** -0.5
LOG2E = 1.4426950408889634

DIL_BLK = 128
TM_IN = 512
TM_OUT = 512
TQ = 256
TK = 256
FF_CHUNK = 1024
VMEM_LIMIT = 56 * 1024 * 1024

_NT = (((1,), (1,)), ((), ()))


def _resident(shape):
    nd = len(shape)
    return pl.BlockSpec(shape, lambda *_: (0,) * nd, pipeline_mode=pl.Buffered(1))


def _aligned(x, m):
    return x if isinstance(x, int) else pl.multiple_of(x, m)


def _rms(x, g):
    ms = jnp.mean(x * x, axis=-1, keepdims=True)
    return x * lax.rsqrt(ms + EPS) * g


def _rope(x, cos, sin_lo, sin_hi):
    w = x.shape[1]
    up = pltpu.roll(x, w - ROPE_HALF, 1)
    dn = pltpu.roll(x, ROPE_HALF, 1)
    return x * cos + up * sin_lo + dn * sin_hi


def _split3(x):
    hi = x.astype(BF16)
    r1 = x - hi.astype(F32)
    mid = r1.astype(BF16)
    lo = (r1 - mid.astype(F32)).astype(BF16)
    return hi, mid, lo


def _inproj_kernel(x_ref, g_ref, wqk_ref, wvt_ref, wt_ref, wf_ref, bf_ref,
                   cos_ref, slo_ref, shi_ref,
                   qk_ref, vt_ref, kx_ref, d0_ref, d1_ref, d2_ref, gate_ref,
                   perm_scr, carry_scr):
    tm = x_ref.shape[1]
    d = x_ref.shape[2]

    @pl.when(pl.program_id(1) == 0)
    def _reset():
        carry_scr[...] = jnp.zeros_like(carry_scr)

    xt = x_ref[0]
    h = (xt * g_ref[...]).astype(BF16)
    inv = lax.rsqrt(jnp.mean(xt * xt, axis=-1, keepdims=True) + EPS)
    inv_row = jnp.broadcast_to(inv, (tm, 128)).T[0:1, :]
    proj = lambda ref, lo, hi: inv * lax.dot_general(h, ref[lo:hi, :], _NT,
                                                     preferred_element_type=F32)
    g_off = 3 * DIL_W
    lane = lax.broadcasted_iota(jnp.int32, (1, 128), 1)

    def forget_logits():
        z = proj(wf_ref, 0, 128) + bf_ref[...]
        log_f = jnp.where(lane < 3 * FOX_HEADS,
                          -(jnp.maximum(-z, 0.0) + jnp.log1p(jnp.exp(-jnp.abs(z)))), 0.0)
        return _split3(log_f)

    def forget_cumsum(parts):
        row = lax.broadcasted_iota(jnp.int32, (tm, tm), 0)
        col = lax.broadcasted_iota(jnp.int32, (tm, tm), 1)
        tri = jnp.where(col <= row, 1.0, 0.0).astype(BF16)
        cum = carry_scr[0:1, :]
        for part in parts:
            cum = cum + jnp.dot(tri, part, preferred_element_type=F32)
        carry_scr[0:1, :] = cum[tm - 1:tm, :]
        hi, mid, lo = _split3(-LOG2E * cum)
        kx_ref[0] = jnp.where(lane % 3 == 0, hi, jnp.where(lane % 3 == 1, mid, lo))

    def mixer_a():
        qk_ref[0, :, 0:FOX_W] = (proj(wqk_ref, 0, FOX_W) * (Q_SCALE * LOG2E)).astype(BF16)
        qk_ref[0, :, FOX_W:2 * FOX_W] = proj(wqk_ref, FOX_W, 2 * FOX_W).astype(BF16)
        vt_ref[0] = (lax.dot_general(wvt_ref[...], h, _NT, preferred_element_type=F32)
                     * inv_row).astype(BF16)

    outs = (d0_ref, d1_ref, d2_ref)

    def mixer_b(g):
        dil = DIL_GROUPS[g][1]
        cos, slo, shi = cos_ref[...], slo_ref[...], shi_ref[...]
        part = lambda t: proj(wt_ref, t * DIL_W + g * GROUP_W, t * DIL_W + (g + 1) * GROUP_W)
        q = _rope(part(0), cos, slo, shi) * (Q_SCALE * LOG2E)
        k = _rope(part(1), cos, slo, shi)
        v = part(2)
        n_res = max(dil, 4)
        for c, t in enumerate((q, k, v)):
            perm_scr[g, 2 * c] = t[:, 0:128]
            perm_scr[g, 2 * c + 1] = t[:, 128:256]
        for res in range(n_res):
            for c in range(6):
                outs[g][0, res, :, c * 128:(c + 1) * 128] = (
                    perm_scr[g, c, pl.ds(res, tm // n_res, stride=n_res), :].astype(BF16))

    def gates(c):
        r = proj(wt_ref, g_off + c * 512, g_off + (c + 1) * 512)
        gate_ref[0, :, c * 512:(c + 1) * 512] = jax.nn.sigmoid(r).astype(BF16)

    parts = forget_logits()
    mixer_b(0)
    gates(0)
    gates(1)
    forget_cumsum(parts)
    mixer_b(1)
    gates(2)
    gates(3)
    mixer_b(2)
    mixer_a()


def _inproj(x, g, w_qk, w_v, w_t, w_f, b_f, cos, slo, shi):
    B, S, D = x.shape
    tm = TM_IN
    grid = (B, S // tm)
    tok = lambda w: pl.BlockSpec((1, tm, w), lambda b, i: (b, i, 0))
    tab = pl.BlockSpec((tm, GROUP_W), lambda b, i: (i, 0))
    out_shape = (
        jax.ShapeDtypeStruct((B, S, 2 * FOX_W), BF16),
        jax.ShapeDtypeStruct((B, FOX_W, S), BF16),
        jax.ShapeDtypeStruct((B, S, 128), BF16),
        jax.ShapeDtypeStruct((B, 4, S // 4, 3 * GROUP_W), BF16),
        jax.ShapeDtypeStruct((B, 4, S // 4, 3 * GROUP_W), BF16),
        jax.ShapeDtypeStruct((B, 16, S // 16, 3 * GROUP_W), BF16),
        jax.ShapeDtypeStruct((B, S, 2 * D), BF16),
    )
    out_specs = (
        tok(2 * FOX_W),
        pl.BlockSpec((1, FOX_W, tm), lambda b, i: (b, 0, i)),
        tok(128),
        pl.BlockSpec((1, 4, tm // 4, 3 * GROUP_W), lambda b, i: (b, 0, i, 0)),
        pl.BlockSpec((1, 4, tm // 4, 3 * GROUP_W), lambda b, i: (b, 0, i, 0)),
        pl.BlockSpec((1, 16, tm // 16, 3 * GROUP_W), lambda b, i: (b, 0, i, 0)),
        tok(2 * D),
    )
    in_specs = [
        tok(D), _resident(g.shape), _resident(w_qk.shape), _resident(w_v.shape),
        _resident(w_t.shape), _resident(w_f.shape), _resident(b_f.shape), tab, tab, tab,
    ]
    return pl.pallas_call(
        _inproj_kernel,
        grid=grid,
        in_specs=in_specs,
        out_specs=out_specs,
        out_shape=out_shape,
        scratch_shapes=[pltpu.VMEM((N_GROUPS, 6, tm, 128), F32), pltpu.VMEM((8, 128), F32)],
        compiler_params=pltpu.CompilerParams(
            dimension_semantics=("arbitrary", "arbitrary"), vmem_limit_bytes=VMEM_LIMIT),
        name="inproj",
    )(x, g, w_qk, w_v, w_t, w_f, b_f, cos, slo, shi)


def _fox_kernel(q_ref, k_ref, kx_ref, vt_ref, o_ref):
    S = q_ref.shape[1]
    pw = 2 * HEAD_DIM
    hp = pl.program_id(1)
    lane = lax.broadcasted_iota(jnp.int32, (1, pw), 1)
    first = lane < HEAD_DIM
    zeros = jnp.zeros((TQ, pw), BF16)
    sel = [jnp.broadcast_to(
        jnp.where((lane >= 3 * (2 * hp + hh)) & (lane < 3 * (2 * hp + hh) + 3), 1.0, 0.0),
        (TQ, pw)).astype(BF16) for hh in range(2)]
    key = lax.broadcasted_iota(jnp.int32, (TK, 2 * TQ), 0)
    qry = lax.broadcasted_iota(jnp.int32, (TK, 2 * TQ), 1) % TQ
    causal = key <= qry

    def scores(qi):
        q2 = q_ref[0, qi * TQ:(qi + 1) * TQ, :]
        qmix = jnp.concatenate([
            jnp.concatenate([jnp.where(first, q2, zeros), sel[0]], axis=1),
            jnp.concatenate([jnp.where(first, zeros, q2), sel[1]], axis=1)], axis=0)
        n_k = (qi + 1) * TK
        kaug = jnp.concatenate([k_ref[0, 0:n_k, :], kx_ref[0, 0:n_k, :]], axis=1)
        s = lax.dot_general(kaug, qmix, _NT, preferred_element_type=F32)
        chunks = [s[c * TK:(c + 1) * TK] for c in range(qi)]
        chunks.append(jnp.where(causal, s[qi * TK:n_k], NEG_INF))
        m = jnp.max(chunks[0], axis=0, keepdims=True)
        for c in range(1, qi + 1):
            m = jnp.maximum(m, jnp.max(chunks[c], axis=0, keepdims=True))
        return chunks, m

    def outputs(qi, chunks, m):
        n_k = (qi + 1) * TK
        p = [jnp.exp2(s - m) for s in chunks]
        l = jnp.sum(p[0], axis=0, keepdims=True)
        for c in range(1, qi + 1):
            l = l + jnp.sum(p[c], axis=0, keepdims=True)
        pt = jnp.concatenate([t.astype(BF16) for t in p], axis=0)
        acc = jnp.dot(vt_ref[0, :, 0:n_k], pt, preferred_element_type=F32)
        ot = (acc / l).T
        o_ref[0, qi * TQ:(qi + 1) * TQ, :] = jnp.where(first, ot[0:TQ], ot[TQ:2 * TQ]).astype(BF16)

    n_q = S // TQ
    nxt = scores(0)
    for qi in range(n_q):
        cur = nxt
        if qi + 1 < n_q:
            nxt = scores(qi + 1)
        outputs(qi, *cur)


def _fox(qk, vt, kx):
    B, S, _ = qk.shape
    n_pairs = FOX_HEADS // 2
    pw = 2 * HEAD_DIM
    spec = lambda off: pl.BlockSpec((1, S, pw), lambda b, p: (b, 0, off + p))
    return pl.pallas_call(
        _fox_kernel,
        grid=(B, n_pairs),
        in_specs=[spec(0), spec(n_pairs),
                  pl.BlockSpec((1, S, 128), lambda b, p: (b, 0, 0)),
                  pl.BlockSpec((1, pw, S), lambda b, p: (b, p, 0))],
        out_specs=pl.BlockSpec((1, S, pw), lambda b, p: (b, 0, p)),
        out_shape=jax.ShapeDtypeStruct((B, S, FOX_W), BF16),
        compiler_params=pltpu.CompilerParams(
            dimension_semantics=("arbitrary", "arbitrary"), vmem_limit_bytes=VMEM_LIMIT),
        name="fox",
    )(qk, qk, kx, vt)


def _dil_kernel(d0_ref, d1_ref, d2_ref, o_ref, acc_scr, m_scr, l_scr, nat_scr):
    S = d0_ref.shape[1]
    nh = DIL_HEADS_PER_GROUP
    blk = DIL_BLK
    n_blocks = S // blk
    lane_head = lax.broadcasted_iota(jnp.int32, (1, GROUP_W), 1) // HEAD_DIM
    a2 = lax.broadcasted_iota(jnp.int32, (nh * blk, 2 * blk), 0) % blk
    k2 = lax.broadcasted_iota(jnp.int32, (nh * blk, 2 * blk), 1)
    bias_two = jnp.where((k2 >= a2) & (k2 <= a2 + blk), 0.0, NEG_INF)
    a1 =lax.broadcasted_iota(jnp.int32, (nh * blk, blk), 0) % blk
    k1 = lax.broadcasted_iota(jnp.int32, (nh * blk, blk), 1)
    bias_one = jnp.where(k1 <= a1, 0.0, NEG_INF)

    def assemble(x):
        out = x[(nh - 1) * blk:nh * blk]
        for h in range(nh - 2, -1, -1):
            out = jnp.where(lane_head == h, x[h * blk:(h + 1) * blk], out)
        return out

    def attend(qs, kcat, vcat, bias):
        zero = jnp.zeros_like(qs)
        qstack = jnp.concatenate(
            [jnp.where(lane_head == h, qs, zero) for h in range(nh)], axis=0)
        s = lax.dot_general(qstack, kcat, _NT, preferred_element_type=F32) + bias
        m = jnp.max(s, axis=1, keepdims=True)
        p = jnp.exp2(s - m)
        l = jnp.sum(p, axis=1, keepdims=True)
        pv = jnp.dot(p.astype(BF16), vcat, preferred_element_type=F32)
        return (assemble(pv), assemble(jnp.broadcast_to(m, pv.shape)),
                assemble(jnp.broadcast_to(l, pv.shape)))

    def put(res, pieces, first):
        acc_b, m_b, l_b = res
        for rows, idx in pieces:
            for c in range(GROUP_W // 128):
                lanes = slice(c * 128, (c + 1) * 128)
                if first:
                    acc_scr[c, idx, :] = acc_b[rows, lanes]
                    m_scr[c, idx, :] = m_b[rows, lanes]
                    l_scr[c, idx, :] = l_b[rows, lanes]
                else:
                    m_o = m_scr[c, idx, :]
                    m_n = jnp.maximum(m_o, m_b[rows, lanes])
                    e_o = jnp.exp2(m_o - m_n)
                    e_b = jnp.exp2(m_b[rows, lanes] - m_n)
                    acc_scr[c, idx, :] = acc_scr[c, idx, :] * e_o + acc_b[rows, lanes] * e_b
                    l_scr[c, idx, :] = l_scr[c, idx, :] * e_o + l_b[rows, lanes] * e_b
                    m_scr[c, idx, :] = m_n

    q_l, k_l, v_l = (slice(i * GROUP_W, (i + 1) * GROUP_W) for i in range(3))
    whole = slice(0, blk)
    n_res = 4
    slab = S // n_res
    per_slab = slab // blk

    def g1_slab(res, _):
        for pos in range(per_slab):
            r0 = _aligned(res * slab + pos * blk, blk)
            qs = d1_ref[0, pl.ds(r0, blk), q_l]
            if pos == 0:
                out = attend(qs, d1_ref[0, pl.ds(r0, blk), k_l], d1_ref[0, pl.ds(r0, blk), v_l], bias_one)
            else:
                rp = _aligned(r0 - blk, blk)
                out = attend(qs, d1_ref[0, pl.ds(rp, 2 * blk), k_l], d1_ref[0, pl.ds(rp, 2 * blk), v_l],
                             bias_two)
            put(out, [(whole, pl.ds(r0, blk))], first=True)
        return 0

    for res in range(n_res):
        g1_slab(res, 0)

    sub = blk // n_res
    qa = lax.broadcasted_iota(jnp.int32, (nh * blk, 2 * blk), 0) % blk
    kk = lax.broadcasted_iota(jnp.int32, (nh * blk, 2 * blk), 1)
    tq = blk + n_res * (qa % sub) + qa // sub
    tk = n_res * (kk % (2 * sub)) + kk // (2 * sub)
    bias0_two = jnp.where((tk <= tq) & (tq - tk <= blk), 0.0, NEG_INF)
    qa1 = lax.broadcasted_iota(jnp.int32, (nh * blk, blk), 0) % blk
    kk1 = lax.broadcasted_iota(jnp.int32, (nh * blk, blk), 1)
    bias0_one = jnp.where(n_res * (kk1 % sub) + kk1 // sub <= n_res * (qa1 % sub) + qa1 // sub,
                          0.0, NEG_INF)

    def g0_block(n, has_prev):
        rq = [_aligned(r * slab + n * sub, sub) for r in range(n_res)]
        qs = jnp.concatenate([d0_ref[0, pl.ds(rq[r], sub), q_l] for r in range(n_res)], axis=0)
        if has_prev:
            rk = [_aligned(r * slab + (n - 1) * sub, sub) for r in range(n_res)]
            kcat = jnp.concatenate([d0_ref[0, pl.ds(rk[r], 2 * sub), k_l] for r in range(n_res)], axis=0)
            vcat = jnp.concatenate([d0_ref[0, pl.ds(rk[r], 2 * sub), v_l] for r in range(n_res)], axis=0)
            out = attend(qs, kcat, vcat, bias0_two)
        else:
            kcat = jnp.concatenate([d0_ref[0, pl.ds(rq[r], sub), k_l] for r in range(n_res)], axis=0)
            vcat = jnp.concatenate([d0_ref[0, pl.ds(rq[r], sub), v_l] for r in range(n_res)], axis=0)
            out = attend(qs, kcat, vcat, bias0_one)
        put(out, [(slice(r * sub, (r + 1) * sub), pl.ds(rq[r], sub)) for r in range(n_res)],
            first=False)

    g0_block(0, False)

    def g0_rest(n, _):
        g0_block(n, True)
        return 0

    for n in range(1, n_blocks):
        g0_rest(n, 0)

    def g2_block(r16, _):
        r0 = _aligned(r16 * blk, blk)
        out = attend(d2_ref[0, pl.ds(r0, blk), q_l], d2_ref[0, pl.ds(r0, blk), k_l],
                     d2_ref[0, pl.ds(r0, blk), v_l], bias_one)
        put(out, [(whole, pl.ds((r16 % n_res) * slab + r16 // n_res, blk, stride=n_res))], first=False)
        return 0

    for r16 in range(n_blocks):
        g2_block(r16, 0)

    def finish(res, _):
        for pos in range(per_slab):
            r0 = _aligned(res * slab + pos * blk, blk)
            for c in range(GROUP_W // 128):
                nat_scr[c, pl.ds(n_res * pos * blk + res, blk, stride=n_res), :] = (
                    acc_scr[c, pl.ds(r0, blk), :] / l_scr[c, pl.ds(r0, blk), :])
        return 0

    for res in range(n_res):
        finish(res, 0)

    def emit(i, _):
        r0 = _aligned(i * 2 * blk, 2 * blk)
        for c in range(GROUP_W // 128):
            o_ref[0, pl.ds(r0, 2 * blk), c * 128:(c + 1) * 128] = nat_scr[c, pl.ds(r0, 2 * blk), :].astype(BF16)
        return 0

    for i in range(n_blocks // 2):
        emit(i, 0)


def _dilated(d0, d1, d2):
    B, S, W = d0.shape
    spec = pl.BlockSpec((1, S, W), lambda b: (b, 0, 0))
    return pl.pallas_call(
        _dil_kernel,
        grid=(B,),
        in_specs=[spec, spec, spec],
        out_specs=pl.BlockSpec((1, S, GROUP_W), lambda b: (b, 0, 0)),
        out_shape=jax.ShapeDtypeStruct((B, S, GROUP_W), BF16),
        scratch_shapes=[pltpu.VMEM((GROUP_W // 128, S, 128), F32)] * 4,
        compiler_params=pltpu.CompilerParams(
            dimension_semantics=("arbitrary",), vmem_limit_bytes=VMEM_LIMIT),
        name="dilated",
    )(d0, d1, d2)


def _outmlp_kernel(x_ref, oa_ref, ob_ref, gate_ref, wa_ref, wb_ref, wo_ref, g2_ref,
                   wu_ref, wdn_ref, g3_ref, o_ref, *, final_norm):
    d = x_ref.shape[1]
    ya = jnp.dot(oa_ref[...], wa_ref[...], preferred_element_type=F32)
    yb = jnp.dot(ob_ref[...], wb_ref[...], preferred_element_type=F32)
    mixed = gate_ref[:, 0:d].astype(F32) * ya + gate_ref[:, d:2 * d].astype(F32) * yb
    x1 = x_ref[...] + jnp.dot(mixed.astype(BF16), wo_ref[...], preferred_element_type=F32)
    h = (x1 * g2_ref[...]).astype(BF16)
    inv = lax.rsqrt(jnp.mean(x1 * x1, axis=-1, keepdims=True) + EPS)
    d_ff = wu_ref.shape[1]
    mlp = None
    for c in range(d_ff // FF_CHUNK):
        u = jnp.dot(h, wu_ref[:, c * FF_CHUNK:(c + 1) * FF_CHUNK], preferred_element_type=F32)
        a = jnp.square(jnp.maximum(u, 0.0)).astype(BF16)
        t = jnp.dot(a, wdn_ref[c * FF_CHUNK:(c + 1) * FF_CHUNK, :], preferred_element_type=F32)
        mlp = t if mlp is None else mlp + t
    x2 = x1 + (inv * inv) * mlp
    o_ref[...] = _rms(x2, g3_ref[...]) if final_norm else x2


def _outmlp(x, oa, ob, gates, w_a, w_b, w_o, g2, w_u, w_dn, g3, final_norm):
    N, D = x.shape
    tm = TM_OUT
    tok = lambda w: pl.BlockSpec((tm, w), lambda i: (i, 0))
    return pl.pallas_call(
        functools.partial(_outmlp_kernel, final_norm=final_norm),
        grid=(N // tm,),
        in_specs=[tok(D), tok(oa.shape[1]), tok(ob.shape[1]), tok(gates.shape[1]),
                  _resident(w_a.shape), _resident(w_b.shape), _resident(w_o.shape),
                  _resident(g2.shape), _resident(w_u.shape), _resident(w_dn.shape),
                  _resident(g3.shape)],
        out_specs=tok(D),
        out_shape=jax.ShapeDtypeStruct((N, D), F32),
        compiler_params=pltpu.CompilerParams(
            dimension_semantics=("arbitrary",), vmem_limit_bytes=VMEM_LIMIT),
        name="outmlp",
    )(x, oa, ob, gates, w_a, w_b, w_o, g2, w_u, w_dn, g3)


def _rope_tables(S):
    inv_freq = jnp.power(jnp.float32(ROPE_THETA),
                         -jnp.arange(ROPE_HALF, dtype=F32) * 2.0 / ROPE_DIM)
    ang = jnp.arange(S).astype(F32)[:, None] * inv_freq[None, :]
    cos, sin = jnp.cos(ang), jnp.sin(ang)
    rest = HEAD_DIM - ROPE_DIM
    one = jnp.ones((S, rest), F32)
    zero = lambda w: jnp.zeros((S, w), F32)
    cos_h = jnp.concatenate([cos, cos, one], axis=1)
    slo_h = jnp.concatenate([-sin, zero(ROPE_HALF + rest)], axis=1)
    shi_h = jnp.concatenate([zero(ROPE_HALF), sin, zero(rest)], axis=1)
    rep = lambda t: jnp.tile(t, (1, DIL_HEADS_PER_GROUP))
    return rep(cos_h), rep(slo_h), rep(shi_h)


def kernel(x, norm_attn_g, w_in, b_forget, w_branch_a, w_branch_b, w_out, norm_mlp_g, w_up,
           w_down, norm_final_g):
    B, S, D = x.shape
    depth = w_in.shape[0]
    assert S % (16 * DIL_BLK) == 0 and S // 16 == DIL_BLK, "sequence length fixed by the dilation groups"
    cos, slo, shi = _rope_tables(S)
    sizes = (FOX_W, FOX_W, FOX_W, FOX_HEADS, DIL_W, DIL_W, DIL_W, D, D)
    offs = np.concatenate([[0], np.cumsum(sizes)])
    for l in range(depth):
        wt = jnp.swapaxes(w_in[l], 0, 1)
        w_qk = wt[:offs[2]].astype(BF16)
        w_v = wt[offs[2]:offs[3]].astype(BF16)
        w_t = wt[offs[4]:].astype(BF16)
        pad = 128 - 3 * FOX_HEADS
        w_f = jnp.pad(jnp.repeat(wt[offs[3]:offs[4]], 3, axis=0), ((0, pad), (0, 0))).astype(BF16)
        b_f = jnp.pad(jnp.repeat(b_forget[l].astype(F32), 3), (0, pad))[None, :]

        qk, vt, kx, d0, d1, d2, gates = _inproj(
            x, norm_attn_g[l][None, :], w_qk, w_v, w_t, w_f, b_f, cos, slo, shi)
        oa = _fox(qk, vt, kx)
        ob = _dilated(d0.reshape(B, S, -1), d1.reshape(B, S, -1), d2.reshape(B, S, -1))
        y = _outmlp(
            x.reshape(B * S, D), oa.reshape(B * S, -1), ob.reshape(B * S, -1),
            gates.reshape(B * S, -1),
            w_branch_a[l].astype(BF16), w_branch_b[l].astype(BF16), w_out[l].astype(BF16),
            norm_mlp_g[l][None, :], w_up[l].astype(BF16), w_down[l].astype(BF16),
            norm_final_g[None, :], final_norm=(l == depth - 1))
        x = y.reshape(B, S, D)
    return x
```

```python
import functools

import numpy as np
import jax
import jax.numpy as jnp
from jax import lax
from jax.experimental import pallas as pl
from jax.experimental.pallas import tpu as pltpu

F32 = jnp.float32
BF16 = jnp.bfloat16

HEAD_DIM = 64
FOX_HEADS = 8
FOX_W = FOX_HEADS * HEAD_DIM
DIL_GROUPS = ((128, 1), (512, 4), (2048, 16))
DIL_HEADS_PER_GROUP = 4
GROUP_W = DIL_HEADS_PER_GROUP * HEAD_DIM
N_GROUPS = len(DIL_GROUPS)
DIL_W = N_GROUPS * GROUP_W
ROPE_THETA = 500000.0
ROPE_DIM = HEAD_DIM // 4
ROPE_HALF = ROPE_DIM // 2
EPS = 1e-6
NEG_INF = -1e30
Q_SCALE = HEAD_DIM ** -0.5
LOG2E = 1.4426950408889634

DIL_BLK = 128
TM_IN = 512
TM_OUT = 512
TQ = 256
TK = 256
FF_CHUNK = 1024
VMEM_LIMIT = 56 * 1024 * 1024

_NT = (((1,), (1,)), ((), ()))


def _resident(shape):
    nd = len(shape)
    return pl.BlockSpec(shape, lambda *_: (0,) * nd, pipeline_mode=pl.Buffered(1))


def _aligned(x, m):
    return x if isinstance(x, int) else pl.multiple_of(x, m)


def _rms(x, g):
    ms = jnp.mean(x * x, axis=-1, keepdims=True)
    return x * lax.rsqrt(ms + EPS) * g


def _qk_lane_dims():
    h4 = range(DIL_HEADS_PER_GROUP)
    return ([(h, 0, ROPE_HALF) for h in h4] + [(0, ROPE_DIM, 32)]
            + [(h, ROPE_HALF, ROPE_HALF) for h in h4] + [(1, ROPE_DIM, 32)]
            + [(0, 48, 16), (1, 48, 16), (2, ROPE_DIM, 48), (3, ROPE_DIM, 48)])


def _qk_lane_head(lane):
    head = jnp.zeros_like(lane)
    pos = 0
    for h, _, n in _qk_lane_dims():
        head = jnp.where((lane >= pos) & (lane < pos + n), h, head)
        pos += n
    return head


def _rope(x, cos, sin):
    x0 = x[:, 0:128]
    return x0 * cos + pltpu.roll(x0, 64, 1) * sin, x[:, 128:256]


def _split3(x):
    hi = x.astype(BF16)
    r1 = x - hi.astype(F32)
    mid = r1.astype(BF16)
    lo = (r1 - mid.astype(F32)).astype(BF16)
    return hi, mid, lo


def _inproj_kernel(x_ref, g_ref, w_ref, bf_ref, cos_ref, sin_ref,
                   qk_ref, vt_ref, kx_ref, d0_ref, d1_ref, d2_ref, gate_ref,
                   perm_scr, carry_scr):
    tm = x_ref.shape[1]
    d = x_ref.shape[2]

    @pl.when(pl.program_id(1) == 0)
    def _reset():
        carry_scr[...] = jnp.zeros_like(carry_scr)

    xt = x_ref[0]
    h = (xt * g_ref[...]).astype(BF16)
    inv = lax.rsqrt(jnp.mean(xt * xt, axis=-1, keepdims=True) + EPS)
    inv_row = jnp.broadcast_to(inv, (tm, 128)).T[0:1, :]
    proj = lambda lo, hi: inv * lax.dot_general(h, w_ref[lo:hi, :], _NT,
                                                preferred_element_type=F32)
    v_off = 2 * FOX_W
    f_off = v_off + FOX_W
    d_off = f_off + 128
    g_off = d_off + 3 * DIL_W
    lane = lax.broadcasted_iota(jnp.int32, (1, 128), 1)

    def forget_logits():
        z = proj(f_off, d_off) + bf_ref[...]
        log_f = jnp.where(lane < 3 * FOX_HEADS,
                          -(jnp.maximum(-z, 0.0) + jnp.log1p(jnp.exp(-jnp.abs(z)))), 0.0)
        return _split3(log_f)

    def forget_cumsum(parts):
        row = lax.broadcasted_iota(jnp.int32, (tm, tm), 0)
        col = lax.broadcasted_iota(jnp.int32, (tm, tm), 1)
        tri = jnp.where(col <= row, 1.0, 0.0).astype(BF16)
        cum = carry_scr[0:1, :]
        for part in parts:
            cum = cum + jnp.dot(tri, part, preferred_element_type=F32)
        carry_scr[0:1, :] = cum[tm - 1:tm, :]
        hi, mid, lo = _split3(-LOG2E * cum)
        kx_ref[0] = jnp.where(lane % 3 == 0, hi, jnp.where(lane % 3 == 1, mid, lo))

    def mixer_a():
        qk_ref[0, :, 0:FOX_W] = (proj(0, FOX_W) * (Q_SCALE * LOG2E)).astype(BF16)
        qk_ref[0, :, FOX_W:2 * FOX_W] = proj(FOX_W, 2 * FOX_W).astype(BF16)
        vt_ref[0] = (lax.dot_general(w_ref[v_off:f_off, :], h, _NT, preferred_element_type=F32)
                     * inv_row).astype(BF16)

    outs = (d0_ref, d1_ref, d2_ref)

    def mixer_b(g):
        dil = DIL_GROUPS[g][1]
        cos, sin = cos_ref[...], sin_ref[...]
        part = lambda t: proj(d_off + (3 * g + t) * GROUP_W, d_off + (3 * g + t + 1) * GROUP_W)
        q0, q1 = _rope(part(0), cos, sin)
        k0, k1 = _rope(part(1), cos, sin)
        v = part(2)
        n_res = max(dil, 4)
        halves = (q0 * (Q_SCALE * LOG2E), q1 * (Q_SCALE * LOG2E), k0, k1, v[:, 0:128], v[:, 128:256])
        for c, t in enumerate(halves):
            perm_scr[g, c] = t
        for res in range(n_res):
            for c in range(6):
                outs[g][0, res, :, c * 128:(c + 1) * 128] = (
                    perm_scr[g, c, pl.ds(res, tm // n_res, stride=n_res), :].astype(BF16))

    def gates(c):
        r = proj(g_off + c * 512, g_off + (c + 1) * 512)
        gate_ref[0, :, c * 512:(c + 1) * 512] = jax.nn.sigmoid(r).astype(BF16)

    parts = forget_logits()
    mixer_b(0)
    gates(0)
    gates(1)
    forget_cumsum(parts)
    mixer_b(1)
    gates(2)
    gates(3)
    mixer_b(2)
    mixer_a()


def _inproj(x, g, w, b_f, cos, sin):
    B, S, D = x.shape
    tm = TM_IN
    grid = (B, S // tm)
    tok = lambda w: pl.BlockSpec((1, tm, w), lambda b, i: (b, i, 0))
    tab = pl.BlockSpec((tm, 128), lambda b, i: (i, 0))
    out_shape = (
        jax.ShapeDtypeStruct((B, S, 2 * FOX_W), BF16),
        jax.ShapeDtypeStruct((B, FOX_W, S), BF16),
        jax.ShapeDtypeStruct((B, S, 128), BF16),
        jax.ShapeDtypeStruct((B, 4, S // 4, 3 * GROUP_W), BF16),
        jax.ShapeDtypeStruct((B, 4, S // 4, 3 * GROUP_W), BF16),
        jax.ShapeDtypeStruct((B, 16, S // 16, 3 * GROUP_W), BF16),
        jax.ShapeDtypeStruct((B, S, 2 * D), BF16),
    )
    out_specs = (
        tok(2 * FOX_W),
        pl.BlockSpec((1, FOX_W, tm), lambda b, i: (b, 0, i)),
        tok(128),
        pl.BlockSpec((1, 4, tm // 4, 3 * GROUP_W), lambda b, i: (b, 0, i, 0)),
        pl.BlockSpec((1, 4, tm // 4, 3 * GROUP_W), lambda b, i: (b, 0, i, 0)),
        pl.BlockSpec((1, 16, tm // 16, 3 * GROUP_W), lambda b, i: (b, 0, i, 0)),
        tok(2 * D),
    )
    in_specs = [tok(D), _resident(g.shape), _resident(w.shape), _resident(b_f.shape), tab, tab]
    return pl.pallas_call(
        _inproj_kernel,
        grid=grid,
        in_specs=in_specs,
        out_specs=out_specs,
        out_shape=out_shape,
        scratch_shapes=[pltpu.VMEM((N_GROUPS, 6, tm, 128), F32), pltpu.VMEM((8, 128), F32)],
        compiler_params=pltpu.CompilerParams(
            dimension_semantics=("arbitrary", "arbitrary"), vmem_limit_bytes=VMEM_LIMIT),
        name="inproj",
    )(x, g, w, b_f, cos, sin)


def _fox_kernel(q_ref, k_ref, kx_ref, vt_ref, o_ref):
    S = q_ref.shape[1]
    pw = 2 * HEAD_DIM
    hp = pl.program_id(1)
    lane = lax.broadcasted_iota(jnp.int32, (1, pw), 1)
    first = lane < HEAD_DIM
    zeros = jnp.zeros((TQ, pw), BF16)
    sel = [jnp.broadcast_to(
        jnp.where((lane >= 3 * (2 * hp + hh)) & (lane < 3 * (2 * hp + hh) + 3), 1.0, 0.0),
        (TQ, pw)).astype(BF16) for hh in range(2)]
    key = lax.broadcasted_iota(jnp.int32, (TK, 2 * TQ), 0)
    qry = lax.broadcasted_iota(jnp.int32, (TK, 2 * TQ), 1) % TQ
    causal = key <= qry

    def scores(qi):
        q2 = q_ref[0, qi * TQ:(qi + 1) * TQ, :]
        qmix = jnp.concatenate([
            jnp.concatenate([jnp.where(first, q2, zeros), sel[0]], axis=1),
            jnp.concatenate([jnp.where(first, zeros, q2), sel[1]], axis=1)], axis=0)
        n_k = (qi + 1) * TK
        kaug = jnp.concatenate([k_ref[0, 0:n_k, :], kx_ref[0, 0:n_k, :]], axis=1)
        s = lax.dot_general(kaug, qmix, _NT, preferred_element_type=F32)
        chunks = [s[c * TK:(c + 1) * TK] for c in range(qi)]
        chunks.append(jnp.where(causal, s[qi * TK:n_k], NEG_INF))
        m = jnp.max(chunks[0], axis=0, keepdims=True)
        for c in range(1, qi + 1):
            m = jnp.maximum(m, jnp.max(chunks[c], axis=0, keepdims=True))
        return chunks, m

    def outputs(qi, chunks, m):
        n_k = (qi + 1) * TK
        p = [jnp.exp2(s - m) for s in chunks]
        l = jnp.sum(p[0], axis=0, keepdims=True)
        for c in range(1, qi + 1):
            l = l + jnp.sum(p[c], axis=0, keepdims=True)
        pt = jnp.concatenate([t.astype(BF16) for t in p], axis=0)
        acc = jnp.dot(vt_ref[0, :, 0:n_k], pt, preferred_element_type=F32)
        ot = (acc / l).T
        o_ref[0, qi * TQ:(qi + 1) * TQ, :] = jnp.where(first, ot[0:TQ], ot[TQ:2 * TQ]).astype(BF16)

    n_q = S // TQ
    nxt = scores(0)
    for qi in range(n_q):
        cur = nxt
        if qi + 1 < n_q:
            nxt = scores(qi + 1)
        outputs(qi, *cur)


def _fox(qk, vt, kx):
    B, S, _ = qk.shape
    n_pairs = FOX_HEADS // 2
    pw = 2 * HEAD_DIM
    spec = lambda off: pl.BlockSpec((1, S, pw), lambda b, p: (b, 0, off + p))
    return pl.pallas_call(
        _fox_kernel,
        grid=(B, n_pairs),
        in_specs=[spec(0), spec(n_pairs),
                  pl.BlockSpec((1, S, 128), lambda b, p: (b, 0, 0)),
                  pl.BlockSpec((1, pw, S), lambda b, p: (b, p, 0))],
        out_specs=pl.BlockSpec((1, S, pw), lambda b, p: (b, 0, p)),
        out_shape=jax.ShapeDtypeStruct((B, S, FOX_W), BF16),
        compiler_params=pltpu.CompilerParams(
            dimension_semantics=("arbitrary", "arbitrary"), vmem_limit_bytes=VMEM_LIMIT),
        name="fox",
    )(qk, qk, kx, vt)


def _dil_kernel(d0_ref, d1_ref, d2_ref, o_ref, acc_scr, m_scr, l_scr, nat_scr):
    S = d0_ref.shape[1]
    nh = DIL_HEADS_PER_GROUP
    blk = DIL_BLK
    n_blocks = S // blk
    lane_head = lax.broadcasted_iota(jnp.int32, (1, GROUP_W), 1) // HEAD_DIM
    qk_head = _qk_lane_head(lax.broadcasted_iota(jnp.int32, (1, GROUP_W), 1))
    a2 = lax.broadcasted_iota(jnp.int32, (nh * blk, 2 * blk), 0) % blk
    k2 = lax.broadcasted_iota(jnp.int32, (nh * blk, 2 * blk), 1)
    bias_two = jnp.where((k2 >= a2) & (k2 <= a2 + blk), 0.0, NEG_INF)
    a1 =lax.broadcasted_iota(jnp.int32, (nh * blk, blk), 0) % blk
    k1 = lax.broadcasted_iota(jnp.int32, (nh * blk, blk), 1)
    bias_one = jnp.where(k1 <= a1, 0.0, NEG_INF)

    def assemble(x):
        out = x[(nh - 1) * blk:nh * blk]
        for h in range(nh - 2, -1, -1):
            out = jnp.where(lane_head == h, x[h * blk:(h + 1) * blk], out)
        return out

    def attend(qs, kcat, vcat, bias):
        zero = jnp.zeros_like(qs)
        qstack = jnp.concatenate(
            [jnp.where(qk_head == h, qs, zero) for h in range(nh)], axis=0)
        s = lax.dot_general(qstack, kcat, _NT, preferred_element_type=F32) + bias
        m = jnp.max(s, axis=1, keepdims=True)
        p = jnp.exp2(s - m)
        l = jnp.sum(p, axis=1, keepdims=True)
        pv = jnp.dot(p.astype(BF16), vcat, preferred_element_type=F32)
        return (assemble(pv), assemble(jnp.broadcast_to(m, pv.shape)),
                assemble(jnp.broadcast_to(l, pv.shape)))

    def put(res, pieces, first):
        acc_b, m_b, l_b = res
        for rows, idx in pieces:
            for c in range(GROUP_W // 128):
                lanes = slice(c * 128, (c + 1) * 128)
                if first:
                    acc_scr[c, idx, :] = acc_b[rows, lanes]
                    m_scr[c, idx, :] = m_b[rows, lanes]
                    l_scr[c, idx, :] = l_b[rows, lanes]
                else:
                    m_o = m_scr[c, idx, :]
                    m_n = jnp.maximum(m_o, m_b[rows, lanes])
                    e_o = jnp.exp2(m_o - m_n)
                    e_b = jnp.exp2(m_b[rows, lanes] - m_n)
                    acc_scr[c, idx, :] = acc_scr[c, idx, :] * e_o + acc_b[rows, lanes] * e_b
                    l_scr[c, idx, :] = l_scr[c, idx, :] * e_o + l_b[rows, lanes] * e_b
                    m_scr[c, idx, :] = m_n

    q_l, k_l, v_l = (slice(i * GROUP_W, (i + 1) * GROUP_W) for i in range(3))
    whole = slice(0, blk)
    n_res = 4
    slab = S // n_res
    per_slab = slab // blk

    def g1_slab(res, _):
        for pos in range(per_slab):
            r0 = _aligned(res * slab + pos * blk, blk)
            qs = d1_ref[0, pl.ds(r0, blk), q_l]
            if pos == 0:
                out = attend(qs, d1_ref[0, pl.ds(r0, blk), k_l], d1_ref[0, pl.ds(r0, blk), v_l], bias_one)
            else:
                rp = _aligned(r0 - blk, blk)
                out = attend(qs, d1_ref[0, pl.ds(rp, 2 * blk), k_l], d1_ref[0, pl.ds(rp, 2 * blk), v_l],
                             bias_two)
            put(out, [(whole, pl.ds(r0, blk))], first=True)
        return 0

    for res in range(n_res):
        g1_slab(res, 0)

    sub = blk // n_res
    qa = lax.broadcasted_iota(jnp.int32, (nh * blk, 2 * blk), 0) % blk
    kk = lax.broadcasted_iota(jnp.int32, (nh * blk, 2 * blk), 1)
    tq = blk + n_res * (qa % sub) + qa // sub
    tk = n_res * (kk % (2 * sub)) + kk // (2 * sub)
    bias0_two = jnp.where((tk <= tq) & (tq - tk <= blk), 0.0, NEG_INF)
    qa1 = lax.broadcasted_iota(jnp.int32, (nh * blk, blk), 0) % blk
    kk1 = lax.broadcasted_iota(jnp.int32, (nh * blk, blk), 1)
    bias0_one = jnp.where(n_res * (kk1 % sub) + kk1 // sub <= n_res * (qa1 % sub) + qa1 // sub,
                          0.0, NEG_INF)

    def g0_block(n, has_prev):
        rq = [_aligned(r * slab + n * sub, sub) for r in range(n_res)]
        qs = jnp.concatenate([d0_ref[0, pl.ds(rq[r], sub), q_l] for r in range(n_res)], axis=0)
        if has_prev:
            rk = [_aligned(r * slab + (n - 1) * sub, sub) for r in range(n_res)]
            kcat = jnp.concatenate([d0_ref[0, pl.ds(rk[r], 2 * sub), k_l] for r in range(n_res)], axis=0)
            vcat = jnp.concatenate([d0_ref[0, pl.ds(rk[r], 2 * sub), v_l] for r in range(n_res)], axis=0)
            out = attend(qs, kcat, vcat, bias0_two)
        else:
            kcat = jnp.concatenate([d0_ref[0, pl.ds(rq[r], sub), k_l] for r in range(n_res)], axis=0)
            vcat = jnp.concatenate([d0_ref[0, pl.ds(rq[r], sub), v_l] for r in range(n_res)], axis=0)
            out = attend(qs, kcat, vcat, bias0_one)
        put(out, [(slice(r * sub, (r + 1) * sub), pl.ds(rq[r], sub)) for r in range(n_res)],
            first=False)

    g0_block(0, False)

    def g0_rest(n, _):
        g0_block(n, True)
        return 0

    for n in range(1, n_blocks):
        g0_rest(n, 0)

    def g2_block(r16, _):
        r0 = _aligned(r16 * blk, blk)
        out = attend(d2_ref[0, pl.ds(r0, blk), q_l], d2_ref[0, pl.ds(r0, blk), k_l],
                     d2_ref[0, pl.ds(r0, blk), v_l], bias_one)
        put(out, [(whole, pl.ds((r16 % n_res) * slab + r16 // n_res, blk, stride=n_res))], first=False)
        return 0

    for r16 in range(n_blocks):
        g2_block(r16, 0)

    def finish(res, _):
        for pos in range(per_slab):
            r0 = _aligned(res * slab + pos * blk, blk)
            for c in range(GROUP_W // 128):
                nat_scr[c, pl.ds(n_res * pos * blk + res, blk, stride=n_res), :] = (
                    acc_scr[c, pl.ds(r0, blk), :] / l_scr[c, pl.ds(r0, blk), :])
        return 0

    for res in range(n_res):
        finish(res, 0)

    def emit(i, _):
        r0 = _aligned(i * 2 * blk, 2 * blk)
        for c in range(GROUP_W // 128):
            o_ref[0, pl.ds(r0, 2 * blk), c * 128:(c + 1) * 128] = nat_scr[c, pl.ds(r0, 2 * blk), :].astype(BF16)
        return 0

    for i in range(n_blocks // 2):
        emit(i, 0)


def _dilated(d0, d1, d2):
    B, S, W = d0.shape
    spec = pl.BlockSpec((1, S, W), lambda b: (b, 0, 0))
    return pl.pallas_call(
        _dil_kernel,
        grid=(B,),
        in_specs=[spec, spec, spec],
        out_specs=pl.BlockSpec((1, S, GROUP_W), lambda b: (b, 0, 0)),
        out_shape=jax.ShapeDtypeStruct((B, S, GROUP_W), BF16),
        scratch_shapes=[pltpu.VMEM((GROUP_W // 128, S, 128), F32)] * 4,
        compiler_params=pltpu.CompilerParams(
            dimension_semantics=("arbitrary",), vmem_limit_bytes=VMEM_LIMIT),
        name="dilated",
    )(d0, d1, d2)


def _outmlp_kernel(x_ref, oa_ref, ob_ref, gate_ref, wa_ref, wb_ref, wo_ref, g2_ref,
                   wu_ref, wdn_ref, g3_ref, o_ref, *, final_norm):
    d = x_ref.shape[1]
    ya = jnp.dot(oa_ref[...], wa_ref[...], preferred_element_type=F32)
    yb = jnp.dot(ob_ref[...], wb_ref[...], preferred_element_type=F32)
    mixed = gate_ref[:, 0:d].astype(F32) * ya + gate_ref[:, d:2 * d].astype(F32) * yb
    x1 = x_ref[...] + jnp.dot(mixed.astype(BF16), wo_ref[...], preferred_element_type=F32)
    h = (x1 * g2_ref[...]).astype(BF16)
    inv = lax.rsqrt(jnp.mean(x1 * x1, axis=-1, keepdims=True) + EPS)
    d_ff = wu_ref.shape[1]
    mlp = None
    for c in range(d_ff // FF_CHUNK):
        u = jnp.dot(h, wu_ref[:, c * FF_CHUNK:(c + 1) * FF_CHUNK], preferred_element_type=F32)
        a = jnp.square(jnp.maximum(u, 0.0)).astype(BF16)
        t = jnp.dot(a, wdn_ref[c * FF_CHUNK:(c + 1) * FF_CHUNK, :], preferred_element_type=F32)
        mlp = t if mlp is None else mlp + t
    x2 = x1 + (inv * inv) * mlp
    o_ref[...] = _rms(x2, g3_ref[...]) if final_norm else x2


def _outmlp(x, oa, ob, gates, w_a, w_b, w_o, g2, w_u, w_dn, g3, final_norm):
    N, D = x.shape
    tm = TM_OUT
    tok = lambda w: pl.BlockSpec((tm, w), lambda i: (i, 0))
    return pl.pallas_call(
        functools.partial(_outmlp_kernel, final_norm=final_norm),
        grid=(N // tm,),
        in_specs=[tok(D), tok(oa.shape[1]), tok(ob.shape[1]), tok(gates.shape[1]),
                  _resident(w_a.shape), _resident(w_b.shape), _resident(w_o.shape),
                  _resident(g2.shape), _resident(w_u.shape), _resident(w_dn.shape),
                  _resident(g3.shape)],
        out_specs=tok(D),
        out_shape=jax.ShapeDtypeStruct((N, D), F32),
        compiler_params=pltpu.CompilerParams(
            dimension_semantics=("arbitrary",), vmem_limit_bytes=VMEM_LIMIT),
        name="outmlp",
    )(x, oa, ob, gates, w_a, w_b, w_o, g2, w_u, w_dn, g3)


def _rope_tables(S):
    inv_freq = jnp.power(jnp.float32(ROPE_THETA),
                         -jnp.arange(ROPE_HALF, dtype=F32) * 2.0 / ROPE_DIM)
    ang = jnp.arange(S).astype(F32)[:, None] * inv_freq[None, :]
    cos4 = jnp.tile(jnp.cos(ang), (1, DIL_HEADS_PER_GROUP))
    sin4 = jnp.tile(jnp.sin(ang), (1, DIL_HEADS_PER_GROUP))
    one = jnp.ones((S, 32), F32)
    zero = jnp.zeros((S, 32), F32)
    cos_t = jnp.concatenate([cos4, one, cos4, one], axis=1)
    sin_t = jnp.concatenate([-sin4, zero, sin4, zero], axis=1)
    return cos_t, sin_t


def _regroup_rows(wt, offs):
    rows = lambda a, n: wt[a:a + n]
    pieces = [rows(0, offs[3])]
    pieces.append(jnp.repeat(rows(offs[3], FOX_HEADS), 3, axis=0))
    pieces.append(jnp.zeros((128 - 3 * FOX_HEADS, wt.shape[1]), wt.dtype))
    for g in range(N_GROUPS):
        for t in (4, 5):
            base = offs[t] + g * GROUP_W
            pieces += [rows(base + h * HEAD_DIM + d0, n) for h, d0, n in _qk_lane_dims()]
        pieces.append(rows(offs[6] + g * GROUP_W, GROUP_W))
    pieces.append(rows(offs[7], offs[9] - offs[7]))
    return jnp.concatenate(pieces, axis=0)


def kernel(x, norm_attn_g, w_in, b_forget, w_branch_a, w_branch_b, w_out, norm_mlp_g, w_up,
           w_down, norm_final_g):
    B, S, D = x.shape
    depth = w_in.shape[0]
    assert S % (16 * DIL_BLK) == 0 and S // 16 == DIL_BLK, "sequence length fixed by the dilation groups"
    cos, sin = _rope_tables(S)
    sizes = (FOX_W, FOX_W, FOX_W, FOX_HEADS, DIL_W, DIL_W, DIL_W, D, D)
    offs = [int(v) for v in np.concatenate([[0], np.cumsum(sizes)])]
    for l in range(depth):
        w = _regroup_rows(jnp.swapaxes(w_in[l], 0, 1), offs).astype(BF16)
        b_f = jnp.pad(jnp.repeat(b_forget[l].astype(F32), 3), (0, 128 - 3 * FOX_HEADS))[None, :]

        qk, vt, kx, d0, d1, d2, gates = _inproj(x, norm_attn_g[l][None, :], w, b_f, cos, sin)
        oa = _fox(qk, vt, kx)
        ob = _dilated(d0.reshape(B, S, -1), d1.reshape(B, S, -1), d2.reshape(B, S, -1))
        y = _outmlp(
            x.reshape(B * S, D), oa.reshape(B * S, -1), ob.reshape(B * S, -1),
            gates.reshape(B * S, -1),
            w_branch_a[l].astype(BF16), w_branch_b[l].astype(BF16), w_out[l].astype(BF16),
            norm_mlp_g[l][None, :], w_up[l].astype(BF16), w_down[l].astype(BF16),
            norm_final_g[None, :], final_norm=(l == depth - 1))
        x = y.reshape(B, S, D)
    return x
```

```python
import functools

import numpy as np
import jax
import jax.numpy as jnp
from jax import lax
from jax.experimental import pallas as pl
from jax.experimental.pallas import tpu as pltpu

F32 = jnp.float32
BF16 = jnp.bfloat16

HEAD_DIM = 64
FOX_HEADS = 8
FOX_W = FOX_HEADS * HEAD_DIM
DIL_GROUPS = ((128, 1), (512, 4), (2048, 16))
DIL_HEADS_PER_GROUP = 4
GROUP_W = DIL_HEADS_PER_GROUP * HEAD_DIM
N_GROUPS = len(DIL_GROUPS)
DIL_W = N_GROUPS * GROUP_W
ROPE_THETA = 500000.0
ROPE_DIM = HEAD_DIM // 4
ROPE_HALF = ROPE_DIM // 2
EPS = 1e-6
NEG_INF = -1e30
Q_SCALE = HEAD_DIM ** -0.5
LOG2E = 1.4426950408889634

DIL_BLK = 128
TM_IN = 512
TM_OUT = 512
TQ = 256
TK = 256
FF_CHUNK = 1024
VMEM_LIMIT = 56 * 1024 * 1024

_NT = (((1,), (1,)), ((), ()))


def _resident(shape):
    nd = len(shape)
    return pl.BlockSpec(shape, lambda *_: (0,) * nd, pipeline_mode=pl.Buffered(1))


def _aligned(x, m):
    return x if isinstance(x, int) else pl.multiple_of(x, m)


def _rms(x, g):
    ms = jnp.mean(x * x, axis=-1, keepdims=True)
    return x * lax.rsqrt(ms + EPS) * g


def _rope(x, cos, sin_lo, sin_hi):
    w = x.shape[1]
    up = pltpu.roll(x, w - ROPE_HALF, 1)
    dn = pltpu.roll(x, ROPE_HALF, 1)
    return x * cos + up * sin_lo + dn * sin_hi


def _split3(x):
    hi = x.astype(BF16)
    r1 = x - hi.astype(F32)
    mid = r1.astype(BF16)
    lo = (r1 - mid.astype(F32)).astype(BF16)
    return hi, mid, lo


def _inproj_kernel(x_ref, g_ref, wqk_ref, wvf_ref, wt_ref, bf_ref,
                   cos_ref, slo_ref, shi_ref,
                   qk_ref, vt_ref, kx_ref, d0_ref, d1_ref, d2_ref, gate_ref,
                   perm_scr, carry_scr):
    tm = x_ref.shape[1]
    d = x_ref.shape[2]

    @pl.when(pl.program_id(1) == 0)
    def _reset():
        carry_scr[...] = jnp.zeros_like(carry_scr)

    xt = x_ref[0]
    h = (xt * g_ref[...]).astype(BF16)
    inv = lax.rsqrt(jnp.mean(xt * xt, axis=-1, keepdims=True) + EPS)
    inv_row = jnp.broadcast_to(inv, (tm, 128)).T[0:1, :]
    proj = lambda ref, lo, hi: inv * lax.dot_general(h, ref[lo:hi, :], _NT,
                                                     preferred_element_type=F32)
    g_off = 3 * DIL_W
    lane = lax.broadcasted_iota(jnp.int32, (1, 128), 1)

    n_f = wvf_ref.shape[0] - FOX_W

    def v_and_forget_logits():
        vf = lax.dot_general(wvf_ref[...], h, _NT, preferred_element_type=F32) * inv_row
        vt_ref[0] = vf[0:FOX_W].astype(BF16)
        z = vf[FOX_W:FOX_W + n_f] + bf_ref[...]
        row = lax.broadcasted_iota(jnp.int32, (n_f, 1), 0)
        log_f = jnp.where(row < 3 * FOX_HEADS,
                          -(jnp.maximum(-z, 0.0) + jnp.log1p(jnp.exp(-jnp.abs(z)))), 0.0)
        return _split3(log_f)

    def forget_cumsum(parts):
        row = lax.broadcasted_iota(jnp.int32, (tm, tm), 0)
        col = lax.broadcasted_iota(jnp.int32, (tm, tm), 1)
        tri = jnp.where(row <= col, 1.0, 0.0).astype(BF16)
        cum = carry_scr[:, 0:1]
        for part in parts:
            cum = cum + jnp.dot(part, tri, preferred_element_type=F32)
        carry_scr[...] = jnp.broadcast_to(cum[:, tm - 1:tm], carry_scr.shape)
        nf = jnp.concatenate([-LOG2E * cum, jnp.zeros((128 - n_f, tm), F32)], axis=0).T
        hi, mid, lo = _split3(nf)
        kx_ref[0] = jnp.where(lane % 3 == 0, hi, jnp.where(lane % 3 == 1, mid, lo))

    def mixer_a_qk():
        qk_ref[0, :, 0:FOX_W] = (proj(wqk_ref, 0, FOX_W) * (Q_SCALE * LOG2E)).astype(BF16)
        qk_ref[0, :, FOX_W:2 * FOX_W] = proj(wqk_ref, FOX_W, 2 * FOX_W).astype(BF16)

    outs = (d0_ref, d1_ref, d2_ref)

    def mixer_b(g):
        dil = DIL_GROUPS[g][1]
        cos, slo, shi = cos_ref[...], slo_ref[...], shi_ref[...]
        part = lambda t: proj(wt_ref, t * DIL_W + g * GROUP_W, t * DIL_W + (g + 1) * GROUP_W)
        q = _rope(part(0), cos, slo, shi) * (Q_SCALE * LOG2E)
        k = _rope(part(1), cos, slo, shi)
        v = part(2)
        n_res = max(dil, 4)
        for c, t in enumerate((q, k, v)):
            perm_scr[g, 2 * c] = t[:, 0:128]
            perm_scr[g, 2 * c + 1] = t[:, 128:256]
        for res in range(n_res):
            for c in range(6):
                outs[g][0, res, :, c * 128:(c + 1) * 128] = (
                    perm_scr[g, c, pl.ds(res, tm // n_res, stride=n_res), :].astype(BF16))

    def gates(c):
        r = proj(wt_ref, g_off + c * 512, g_off + (c + 1) * 512)
        gate_ref[0, :, c * 512:(c + 1) * 512] = jax.nn.sigmoid(r).astype(BF16)

    parts = v_and_forget_logits()
    mixer_b(0)
    gates(0)
    gates(1)
    forget_cumsum(parts)
    mixer_b(1)
    gates(2)
    gates(3)
    mixer_b(2)
    mixer_a_qk()


def _inproj(x, g, w_qk, w_vf, w_t, b_f, cos, slo, shi):
    B, S, D = x.shape
    tm = TM_IN
    grid = (B, S // tm)
    tok = lambda w: pl.BlockSpec((1, tm, w), lambda b, i: (b, i, 0))
    tab = pl.BlockSpec((tm, GROUP_W), lambda b, i: (i, 0))
    out_shape = (
        jax.ShapeDtypeStruct((B, S, 2 * FOX_W), BF16),
        jax.ShapeDtypeStruct((B, FOX_W, S), BF16),
        jax.ShapeDtypeStruct((B, S, 128), BF16),
        jax.ShapeDtypeStruct((B, 4, S // 4, 3 * GROUP_W), BF16),
        jax.ShapeDtypeStruct((B, 4, S // 4, 3 * GROUP_W), BF16),
        jax.ShapeDtypeStruct((B, 16, S // 16, 3 * GROUP_W), BF16),
        jax.ShapeDtypeStruct((B, S, 2 * D), BF16),
    )
    out_specs = (
        tok(2 * FOX_W),
        pl.BlockSpec((1, FOX_W, tm), lambda b, i: (b, 0, i)),
        tok(128),
        pl.BlockSpec((1, 4, tm // 4, 3 * GROUP_W), lambda b, i: (b, 0, i, 0)),
        pl.BlockSpec((1, 4, tm // 4, 3 * GROUP_W), lambda b, i: (b, 0, i, 0)),
        pl.BlockSpec((1, 16, tm // 16, 3 * GROUP_W), lambda b, i: (b, 0, i, 0)),
        tok(2 * D),
    )
    in_specs = [
        tok(D), _resident(g.shape), _resident(w_qk.shape), _resident(w_vf.shape),
        _resident(w_t.shape), _resident(b_f.shape), tab, tab, tab,
    ]
    n_f = w_vf.shape[0] - FOX_W
    return pl.pallas_call(
        _inproj_kernel,
        grid=grid,
        in_specs=in_specs,
        out_specs=out_specs,
        out_shape=out_shape,
        scratch_shapes=[pltpu.VMEM((N_GROUPS, 6, tm, 128), F32), pltpu.VMEM((n_f, 128), F32)],
        compiler_params=pltpu.CompilerParams(
            dimension_semantics=("arbitrary", "arbitrary"), vmem_limit_bytes=VMEM_LIMIT),
        name="inproj",
    )(x, g, w_qk, w_vf, w_t, b_f, cos, slo, shi)


def _fox_kernel(q_ref, k_ref, kx_ref, vt_ref, o_ref):
    S = q_ref.shape[1]
    pw = 2 * HEAD_DIM
    hp = pl.program_id(1)
    lane = lax.broadcasted_iota(jnp.int32, (1, pw), 1)
    first = lane < HEAD_DIM
    zeros = jnp.zeros((TQ, pw), BF16)
    sel = [jnp.broadcast_to(
        jnp.where((lane >= 3 * (2 * hp + hh)) & (lane < 3 * (2 * hp + hh) + 3), 1.0, 0.0),
        (TQ, pw)).astype(BF16) for hh in range(2)]
    key = lax.broadcasted_iota(jnp.int32, (TK, 2 * TQ), 0)
    qry = lax.broadcasted_iota(jnp.int32, (TK, 2 * TQ), 1) % TQ
    causal = key <= qry

    def scores(qi):
        q2 = q_ref[0, qi * TQ:(qi + 1) * TQ, :]
        qmix = jnp.concatenate([
            jnp.concatenate([jnp.where(first, q2, zeros), sel[0]], axis=1),
            jnp.concatenate([jnp.where(first, zeros, q2), sel[1]], axis=1)], axis=0)
        n_k = (qi + 1) * TK
        kaug = jnp.concatenate([k_ref[0, 0:n_k, :], kx_ref[0, 0:n_k, :]], axis=1)
        s = lax.dot_general(kaug, qmix, _NT, preferred_element_type=F32)
        chunks = [s[c * TK:(c + 1) * TK] for c in range(qi)]
        chunks.append(jnp.where(causal, s[qi * TK:n_k], NEG_INF))
        m = jnp.max(chunks[0], axis=0, keepdims=True)
        for c in range(1, qi + 1):
            m = jnp.maximum(m, jnp.max(chunks[c], axis=0, keepdims=True))
        return chunks, m

    def outputs(qi, chunks, m):
        n_k = (qi + 1) * TK
        p = [jnp.exp2(s - m) for s in chunks]
        l = jnp.sum(p[0], axis=0, keepdims=True)
        for c in range(1, qi + 1):
            l = l + jnp.sum(p[c], axis=0, keepdims=True)
        pt = jnp.concatenate([t.astype(BF16) for t in p], axis=0)
        acc = jnp.dot(vt_ref[0, :, 0:n_k], pt, preferred_element_type=F32)
        ot = (acc / l).T
        o_ref[0, qi * TQ:(qi + 1) * TQ, :] = jnp.where(first, ot[0:TQ], ot[TQ:2 * TQ]).astype(BF16)

    n_q = S // TQ
    nxt = scores(0)
    for qi in range(n_q):
        cur = nxt
        if qi + 1 < n_q:
            nxt = scores(qi + 1)
        outputs(qi, *cur)


def _fox(qk, vt, kx):
    B, S, _ = qk.shape
    n_pairs = FOX_HEADS // 2
    pw = 2 * HEAD_DIM
    spec = lambda off: pl.BlockSpec((1, S, pw), lambda b, p: (b, 0, off + p))
    return pl.pallas_call(
        _fox_kernel,
        grid=(B, n_pairs),
        in_specs=[spec(0), spec(n_pairs),
                  pl.BlockSpec((1, S, 128), lambda b, p: (b, 0, 0)),
                  pl.BlockSpec((1, pw, S), lambda b, p: (b, p, 0))],
        out_specs=pl.BlockSpec((1, S, pw), lambda b, p: (b, 0, p)),
        out_shape=jax.ShapeDtypeStruct((B, S, FOX_W), BF16),
        compiler_params=pltpu.CompilerParams(
            dimension_semantics=("arbitrary", "arbitrary"), vmem_limit_bytes=VMEM_LIMIT),
        name="fox",
    )(qk, qk, kx, vt)


def _dil_kernel(d0_ref, d1_ref, d2_ref, o_ref, acc_scr, m_scr, l_scr, nat_scr):
    S = d0_ref.shape[1]
    nh = DIL_HEADS_PER_GROUP
    blk = DIL_BLK
    n_blocks = S // blk
    lane_head = lax.broadcasted_iota(jnp.int32, (1, GROUP_W), 1) // HEAD_DIM
    a2 = lax.broadcasted_iota(jnp.int32, (nh * blk, 2 * blk), 0) % blk
    k2 = lax.broadcasted_iota(jnp.int32, (nh * blk, 2 * blk), 1)
    bias_two = jnp.where((k2 >= a2) & (k2 <= a2 + blk), 0.0, NEG_INF)
    a1 =lax.broadcasted_iota(jnp.int32, (nh * blk, blk), 0) % blk
    k1 = lax.broadcasted_iota(jnp.int32, (nh * blk, blk), 1)
    bias_one = jnp.where(k1 <= a1, 0.0, NEG_INF)

    def assemble(x):
        out = x[(nh - 1) * blk:nh * blk]
        for h in range(nh - 2, -1, -1):
            out = jnp.where(lane_head == h, x[h * blk:(h + 1) * blk], out)
        return out

    def attend(qs, kcat, vcat, bias):
        zero = jnp.zeros_like(qs)
        qstack = jnp.concatenate(
            [jnp.where(lane_head == h, qs, zero) for h in range(nh)], axis=0)
        s = lax.dot_general(qstack, kcat, _NT, preferred_element_type=F32) + bias
        m = jnp.max(s, axis=1, keepdims=True)
        p = jnp.exp2(s - m)
        l = jnp.sum(p, axis=1, keepdims=True)
        pv = jnp.dot(p.astype(BF16), vcat, preferred_element_type=F32)
        return (assemble(pv), assemble(jnp.broadcast_to(m, pv.shape)),
                assemble(jnp.broadcast_to(l, pv.shape)))

    def put(res, pieces, first):
        acc_b, m_b, l_b = res
        for rows, idx in pieces:
            for c in range(GROUP_W // 128):
                lanes = slice(c * 128, (c + 1) * 128)
                if first:
                    acc_scr[c, idx, :] = acc_b[rows, lanes]
                    m_scr[c, idx, :] = m_b[rows, lanes]
                    l_scr[c, idx, :] = l_b[rows, lanes]
                else:
                    m_o = m_scr[c, idx, :]
                    m_n = jnp.maximum(m_o, m_b[rows, lanes])
                    e_o = jnp.exp2(m_o - m_n)
                    e_b = jnp.exp2(m_b[rows, lanes] - m_n)
                    acc_scr[c, idx, :] = acc_scr[c, idx, :] * e_o + acc_b[rows, lanes] * e_b
                    l_scr[c, idx, :] = l_scr[c, idx, :] * e_o + l_b[rows, lanes] * e_b
                    m_scr[c, idx, :] = m_n

    q_l, k_l, v_l = (slice(i * GROUP_W, (i + 1) * GROUP_W) for i in range(3))
    whole = slice(0, blk)
    n_res = 4
    slab = S // n_res
    per_slab = slab // blk

    def g1_slab(res, _):
        for pos in range(per_slab):
            r0 = _aligned(res * slab + pos * blk, blk)
            qs = d1_ref[0, pl.ds(r0, blk), q_l]
            if pos == 0:
                out = attend(qs, d1_ref[0, pl.ds(r0, blk), k_l], d1_ref[0, pl.ds(r0, blk), v_l], bias_one)
            else:
                rp = _aligned(r0 - blk, blk)
                out = attend(qs, d1_ref[0, pl.ds(rp, 2 * blk), k_l], d1_ref[0, pl.ds(rp, 2 * blk), v_l],
                             bias_two)
            put(out, [(whole, pl.ds(r0, blk))], first=True)
        return 0

    for res in range(n_res):
        g1_slab(res, 0)

    sub = blk // n_res
    qa = lax.broadcasted_iota(jnp.int32, (nh * blk, 2 * blk), 0) % blk
    kk = lax.broadcasted_iota(jnp.int32, (nh * blk, 2 * blk), 1)
    tq = blk + n_res * (qa % sub) + qa // sub
    tk = n_res * (kk % (2 * sub)) + kk // (2 * sub)
    bias0_two = jnp.where((tk <= tq) & (tq - tk <= blk), 0.0, NEG_INF)
    qa1 = lax.broadcasted_iota(jnp.int32, (nh * blk, blk), 0) % blk
    kk1 = lax.broadcasted_iota(jnp.int32, (nh * blk, blk), 1)
    bias0_one = jnp.where(n_res * (kk1 % sub) + kk1 // sub <= n_res * (qa1 % sub) + qa1 // sub,
                          0.0, NEG_INF)

    def g0_block(n, has_prev):
        rq = [_aligned(r * slab + n * sub, sub) for r in range(n_res)]
        qs = jnp.concatenate([d0_ref[0, pl.ds(rq[r], sub), q_l] for r in range(n_res)], axis=0)
        if has_prev:
            rk = [_aligned(r * slab + (n - 1) * sub, sub) for r in range(n_res)]
            kcat = jnp.concatenate([d0_ref[0, pl.ds(rk[r], 2 * sub), k_l] for r in range(n_res)], axis=0)
            vcat = jnp.concatenate([d0_ref[0, pl.ds(rk[r], 2 * sub), v_l] for r in range(n_res)], axis=0)
            out = attend(qs, kcat, vcat, bias0_two)
        else:
            kcat = jnp.concatenate([d0_ref[0, pl.ds(rq[r], sub), k_l] for r in range(n_res)], axis=0)
            vcat = jnp.concatenate([d0_ref[0, pl.ds(rq[r], sub), v_l] for r in range(n_res)], axis=0)
            out = attend(qs, kcat, vcat, bias0_one)
        put(out, [(slice(r * sub, (r + 1) * sub), pl.ds(rq[r], sub)) for r in range(n_res)],
            first=False)

    g0_block(0, False)

    def g0_rest(n, _):
        g0_block(n, True)
        return 0

    for n in range(1, n_blocks):
        g0_rest(n, 0)

    def g2_block(r16, _):
        r0 = _aligned(r16 * blk, blk)
        out = attend(d2_ref[0, pl.ds(r0, blk), q_l], d2_ref[0, pl.ds(r0, blk), k_l],
                     d2_ref[0, pl.ds(r0, blk), v_l], bias_one)
        put(out, [(whole, pl.ds((r16 % n_res) * slab + r16 // n_res, blk, stride=n_res))], first=False)
        return 0

    for r16 in range(n_blocks):
        g2_block(r16, 0)

    def finish(res, _):
        for pos in range(per_slab):
            r0 = _aligned(res * slab + pos * blk, blk)
            for c in range(GROUP_W // 128):
                nat_scr[c, pl.ds(n_res * pos * blk + res, blk, stride=n_res), :] = (
                    acc_scr[c, pl.ds(r0, blk), :] / l_scr[c, pl.ds(r0, blk), :])
        return 0

    for res in range(n_res):
        finish(res, 0)

    def emit(i, _):
        r0 = _aligned(i * 2 * blk, 2 * blk)
        for c in range(GROUP_W // 128):
            o_ref[0, pl.ds(r0, 2 * blk), c * 128:(c + 1) * 128] = nat_scr[c, pl.ds(r0, 2 * blk), :].astype(BF16)
        return 0

    for i in range(n_blocks // 2):
        emit(i, 0)


def _dilated(d0, d1, d2):
    B, S, W = d0.shape
    spec = pl.BlockSpec((1, S, W), lambda b: (b, 0, 0))
    return pl.pallas_call(
        _dil_kernel,
        grid=(B,),
        in_specs=[spec, spec, spec],
        out_specs=pl.BlockSpec((1, S, GROUP_W), lambda b: (b, 0, 0)),
        out_shape=jax.ShapeDtypeStruct((B, S, GROUP_W), BF16),
        scratch_shapes=[pltpu.VMEM((GROUP_W // 128, S, 128), F32)] * 4,
        compiler_params=pltpu.CompilerParams(
            dimension_semantics=("arbitrary",), vmem_limit_bytes=VMEM_LIMIT),
        name="dilated",
    )(d0, d1, d2)


def _outmlp_kernel(x_ref, oa_ref, ob_ref, gate_ref, wa_ref, wb_ref, wo_ref, g2_ref,
                   wu_ref, wdn_ref, g3_ref, o_ref, *, final_norm):
    d = x_ref.shape[1]
    ya = jnp.dot(oa_ref[...], wa_ref[...], preferred_element_type=F32)
    yb = jnp.dot(ob_ref[...], wb_ref[...], preferred_element_type=F32)
    mixed = gate_ref[:, 0:d].astype(F32) * ya + gate_ref[:, d:2 * d].astype(F32) * yb
    x1 = x_ref[...] + jnp.dot(mixed.astype(BF16), wo_ref[...], preferred_element_type=F32)
    h = (x1 * g2_ref[...]).astype(BF16)
    inv = lax.rsqrt(jnp.mean(x1 * x1, axis=-1, keepdims=True) + EPS)
    d_ff = wu_ref.shape[1]
    mlp = None
    for c in range(d_ff // FF_CHUNK):
        u = jnp.dot(h, wu_ref[:, c * FF_CHUNK:(c + 1) * FF_CHUNK], preferred_element_type=F32)
        a = jnp.square(jnp.maximum(u, 0.0)).astype(BF16)
        t = jnp.dot(a, wdn_ref[c * FF_CHUNK:(c + 1) * FF_CHUNK, :], preferred_element_type=F32)
        mlp = t if mlp is None else mlp + t
    x2 = x1 + (inv * inv) * mlp
    o_ref[...] = _rms(x2, g3_ref[...]) if final_norm else x2


def _outmlp(x, oa, ob, gates, w_a, w_b, w_o, g2, w_u, w_dn, g3, final_norm):
    N, D = x.shape
    tm = TM_OUT
    tok = lambda w: pl.BlockSpec((tm, w), lambda i: (i, 0))
    return pl.pallas_call(
        functools.partial(_outmlp_kernel, final_norm=final_norm),
        grid=(N // tm,),
        in_specs=[tok(D), tok(oa.shape[1]), tok(ob.shape[1]), tok(gates.shape[1]),
                  _resident(w_a.shape), _resident(w_b.shape), _resident(w_o.shape),
                  _resident(g2.shape), _resident(w_u.shape), _resident(w_dn.shape),
                  _resident(g3.shape)],
        out_specs=tok(D),
        out_shape=jax.ShapeDtypeStruct((N, D), F32),
        compiler_params=pltpu.CompilerParams(
            dimension_semantics=("arbitrary",), vmem_limit_bytes=VMEM_LIMIT),
        name="outmlp",
    )(x, oa, ob, gates, w_a, w_b, w_o, g2, w_u, w_dn, g3)


def _rope_tables(S):
    inv_freq = jnp.power(jnp.float32(ROPE_THETA),
                         -jnp.arange(ROPE_HALF, dtype=F32) * 2.0 / ROPE_DIM)
    ang = jnp.arange(S).astype(F32)[:, None] * inv_freq[None, :]
    cos, sin = jnp.cos(ang), jnp.sin(ang)
    rest = HEAD_DIM - ROPE_DIM
    one = jnp.ones((S, rest), F32)
    zero = lambda w: jnp.zeros((S, w), F32)
    cos_h = jnp.concatenate([cos, cos, one], axis=1)
    slo_h = jnp.concatenate([-sin, zero(ROPE_HALF + rest)], axis=1)
    shi_h = jnp.concatenate([zero(ROPE_HALF), sin, zero(rest)], axis=1)
    rep = lambda t: jnp.tile(t, (1, DIL_HEADS_PER_GROUP))
    return rep(cos_h), rep(slo_h), rep(shi_h)


def kernel(x, norm_attn_g, w_in, b_forget, w_branch_a, w_branch_b, w_out, norm_mlp_g, w_up,
           w_down, norm_final_g):
    B, S, D = x.shape
    depth = w_in.shape[0]
    assert S % (16 * DIL_BLK) == 0 and S // 16 == DIL_BLK, "sequence length fixed by the dilation groups"
    cos, slo, shi = _rope_tables(S)
    sizes = (FOX_W, FOX_W, FOX_W, FOX_HEADS, DIL_W, DIL_W, DIL_W, D, D)
    offs = np.concatenate([[0], np.cumsum(sizes)])
    for l in range(depth):
        wt = jnp.swapaxes(w_in[l], 0, 1)
        w_qk = wt[:offs[2]].astype(BF16)
        w_t = wt[offs[4]:].astype(BF16)
        pad = -(3 * FOX_HEADS) % 16
        w_vf = jnp.concatenate(
            [wt[offs[2]:offs[3]], jnp.repeat(wt[offs[3]:offs[4]], 3, axis=0),
             jnp.zeros((pad, D), wt.dtype)], axis=0).astype(BF16)
        b_f = jnp.pad(jnp.repeat(b_forget[l].astype(F32), 3), (0, pad))[:, None]

        qk, vt, kx, d0, d1, d2, gates = _inproj(
            x, norm_attn_g[l][None, :], w_qk, w_vf, w_t, b_f, cos, slo, shi)
        oa = _fox(qk, vt, kx)
        ob = _dilated(d0.reshape(B, S, -1), d1.reshape(B, S, -1), d2.reshape(B, S, -1))
        y = _outmlp(
            x.reshape(B * S, D), oa.reshape(B * S, -1), ob.reshape(B * S, -1),
            gates.reshape(B * S, -1),
            w_branch_a[l].astype(BF16), w_branch_b[l].astype(BF16), w_out[l].astype(BF16),
            norm_mlp_g[l][None, :], w_up[l].astype(BF16), w_down[l].astype(BF16),
            norm_final_g[None, :], final_norm=(l == depth - 1))
        x = y.reshape(B, S, D)
    return x
```

```python
import functools

import numpy as np
import jax
import jax.numpy as jnp
from jax import lax
from jax.experimental import pallas as pl
from jax.experimental.pallas import tpu as pltpu

F32 = jnp.float32
BF16 = jnp.bfloat16

HEAD_DIM = 64
FOX_HEADS = 8
FOX_W = FOX_HEADS * HEAD_DIM
DIL_GROUPS = ((128, 1), (512, 4), (2048, 16))
DIL_HEADS_PER_GROUP = 4
GROUP_W = DIL_HEADS_PER_GROUP * HEAD_DIM
N_GROUPS = len(DIL_GROUPS)
DIL_W = N_GROUPS * GROUP_W
ROPE_THETA = 500000.0
ROPE_DIM = HEAD_DIM // 4
ROPE_HALF = ROPE_DIM // 2
EPS = 1e-6
NEG_INF = -1e30
Q_SCALE = HEAD_DIM ** -0.5
LOG2E = 1.4426950408889634

DIL_BLK = 128
TM_IN = 512
TM_OUT = 512
TQ = 256
TK = 256
FF_CHUNK = 1024
VMEM_LIMIT = 56 * 1024 * 1024

_NT = (((1,), (1,)), ((), ()))


def _resident(shape):
    nd = len(shape)
    return pl.BlockSpec(shape, lambda *_: (0,) * nd, pipeline_mode=pl.Buffered(1))


def _aligned(x, m):
    return x if isinstance(x, int) else pl.multiple_of(x, m)


def _rms(x, g):
    ms = jnp.mean(x * x, axis=-1, keepdims=True)
    return x * lax.rsqrt(ms + EPS) * g


def _rope(x, cos, sin_lo, sin_hi):
    w = x.shape[1]
    up = pltpu.roll(x, w - ROPE_HALF, 1)
    dn = pltpu.roll(x, ROPE_HALF, 1)
    return x * cos + up * sin_lo + dn * sin_hi


def _split3(x):
    hi = x.astype(BF16)
    r1 = x - hi.astype(F32)
    mid = r1.astype(BF16)
    lo = (r1 - mid.astype(F32)).astype(BF16)
    return hi, mid, lo


def _inproj_kernel(x_ref, g_ref, wqk_ref, wvf_ref, wt_ref, bf_ref,
                   cos_ref, slo_ref, shi_ref,
                   qk_ref, vt_ref, kx_ref, d0_ref, d1_ref, d2_ref, gate_ref,
                   perm_scr, carry_scr):
    tm = x_ref.shape[1]
    d = x_ref.shape[2]

    @pl.when(pl.program_id(1) == 0)
    def _reset():
        carry_scr[...] = jnp.zeros_like(carry_scr)

    xt = x_ref[0]
    h = (xt * g_ref[...]).astype(BF16)
    inv = lax.rsqrt(jnp.mean(xt * xt, axis=-1, keepdims=True) + EPS)
    inv_row = jnp.broadcast_to(inv, (tm, 128)).T[0:1, :]
    proj = lambda ref, lo, hi: inv * lax.dot_general(h, ref[lo:hi, :], _NT,
                                                     preferred_element_type=F32)
    g_off = 3 * DIL_W
    lane = lax.broadcasted_iota(jnp.int32, (1, 128), 1)

    n_f = wvf_ref.shape[0] - FOX_W

    def v_and_forget_logits():
        vf = lax.dot_general(wvf_ref[...], h, _NT, preferred_element_type=F32) * inv_row
        vt_ref[0] = vf[0:FOX_W].astype(BF16)
        z = vf[FOX_W:FOX_W + n_f] + bf_ref[...]
        row = lax.broadcasted_iota(jnp.int32, (n_f, 1), 0)
        log_f = jnp.where(row < 3 * FOX_HEADS,
                          -(jnp.maximum(-z, 0.0) + jnp.log1p(jnp.exp(-jnp.abs(z)))), 0.0)
        return _split3(log_f)

    def forget_cumsum(parts):
        row = lax.broadcasted_iota(jnp.int32, (tm, tm), 0)
        col = lax.broadcasted_iota(jnp.int32, (tm, tm), 1)
        tri = jnp.where(row <= col, 1.0, 0.0).astype(BF16)
        cum = carry_scr[:, 0:1]
        for part in parts:
            cum = cum + jnp.dot(part, tri, preferred_element_type=F32)
        carry_scr[...] = jnp.broadcast_to(cum[:, tm - 1:tm], carry_scr.shape)
        nf = jnp.concatenate([-LOG2E * cum, jnp.zeros((128 - n_f, tm), F32)], axis=0).T
        hi, mid, lo = _split3(nf)
        kx_ref[0] = jnp.where(lane % 3 == 0, hi, jnp.where(lane % 3 == 1, mid, lo))

    def mixer_a_qk():
        qk_ref[0, :, 0:FOX_W] = (proj(wqk_ref, 0, FOX_W) * (Q_SCALE * LOG2E)).astype(BF16)
        qk_ref[0, :, FOX_W:2 * FOX_W] = proj(wqk_ref, FOX_W, 2 * FOX_W).astype(BF16)

    outs = (d0_ref, d1_ref, d2_ref)

    def mixer_b(g):
        dil = DIL_GROUPS[g][1]
        rows = pl.ds(pl.multiple_of(pl.program_id(1) * tm, tm), tm)
        cos, slo, shi = cos_ref[rows, :], slo_ref[rows, :], shi_ref[rows, :]
        part = lambda t: proj(wt_ref, t * DIL_W + g * GROUP_W, t * DIL_W + (g + 1) * GROUP_W)
        q, k, v = part(0), part(1), part(2)
        halves = []
        for c in range(GROUP_W // 128):
            halves.append(_rope(q[:, c * 128:(c + 1) * 128], cos, slo, shi) * (Q_SCALE * LOG2E))
        for c in range(GROUP_W // 128):
            halves.append(_rope(k[:, c * 128:(c + 1) * 128], cos, slo, shi))
        halves += [v[:, 0:128], v[:, 128:256]]
        n_res = max(dil, 4)
        for c, t in enumerate(halves):
            perm_scr[g, c] = t
        for res in range(n_res):
            for c in range(6):
                outs[g][0, res, :, c * 128:(c + 1) * 128] = (
                    perm_scr[g, c, pl.ds(res, tm // n_res, stride=n_res), :].astype(BF16))

    def gates(c):
        r = proj(wt_ref, g_off + c * 512, g_off + (c + 1) * 512)
        gate_ref[0, :, c * 512:(c + 1) * 512] = jax.nn.sigmoid(r).astype(BF16)

    parts = v_and_forget_logits()
    mixer_b(0)
    gates(0)
    gates(1)
    forget_cumsum(parts)
    mixer_b(1)
    gates(2)
    gates(3)
    mixer_b(2)
    mixer_a_qk()


def _inproj(x, g, w_qk, w_vf, w_t, b_f, cos, slo, shi):
    B, S, D = x.shape
    tm = TM_IN
    grid = (B, S // tm)
    tok = lambda w: pl.BlockSpec((1, tm, w), lambda b, i: (b, i, 0))
    tab = _resident(cos.shape)
    out_shape = (
        jax.ShapeDtypeStruct((B, S, 2 * FOX_W), BF16),
        jax.ShapeDtypeStruct((B, FOX_W, S), BF16),
        jax.ShapeDtypeStruct((B, S, 128), BF16),
        jax.ShapeDtypeStruct((B, 4, S // 4, 3 * GROUP_W), BF16),
        jax.ShapeDtypeStruct((B, 4, S // 4, 3 * GROUP_W), BF16),
        jax.ShapeDtypeStruct((B, 16, S // 16, 3 * GROUP_W), BF16),
        jax.ShapeDtypeStruct((B, S, 2 * D), BF16),
    )
    out_specs = (
        tok(2 * FOX_W),
        pl.BlockSpec((1, FOX_W, tm), lambda b, i: (b, 0, i)),
        tok(128),
        pl.BlockSpec((1, 4, tm // 4, 3 * GROUP_W), lambda b, i: (b, 0, i, 0)),
        pl.BlockSpec((1, 4, tm // 4, 3 * GROUP_W), lambda b, i: (b, 0, i, 0)),
        pl.BlockSpec((1, 16, tm // 16, 3 * GROUP_W), lambda b, i: (b, 0, i, 0)),
        tok(2 * D),
    )
    in_specs = [
        tok(D), _resident(g.shape), _resident(w_qk.shape), _resident(w_vf.shape),
        _resident(w_t.shape), _resident(b_f.shape), tab, tab, tab,
    ]
    n_f = w_vf.shape[0] - FOX_W
    return pl.pallas_call(
        _inproj_kernel,
        grid=grid,
        in_specs=in_specs,
        out_specs=out_specs,
        out_shape=out_shape,
        scratch_shapes=[pltpu.VMEM((N_GROUPS, 6, tm, 128), F32), pltpu.VMEM((n_f, 128), F32)],
        compiler_params=pltpu.CompilerParams(
            dimension_semantics=("arbitrary", "arbitrary"), vmem_limit_bytes=VMEM_LIMIT),
        name="inproj",
    )(x, g, w_qk, w_vf, w_t, b_f, cos, slo, shi)


def _fox_kernel(q_ref, k_ref, kx_ref, vt_ref, o_ref):
    S = q_ref.shape[1]
    pw = 2 * HEAD_DIM
    hp = pl.program_id(1)
    lane = lax.broadcasted_iota(jnp.int32, (1, pw), 1)
    first = lane < HEAD_DIM
    zeros = jnp.zeros((TQ, pw), BF16)
    sel = [jnp.broadcast_to(
        jnp.where((lane >= 3 * (2 * hp + hh)) & (lane < 3 * (2 * hp + hh) + 3), 1.0, 0.0),
        (TQ, pw)).astype(BF16) for hh in range(2)]
    key = lax.broadcasted_iota(jnp.int32, (TK, 2 * TQ), 0)
    qry = lax.broadcasted_iota(jnp.int32, (TK, 2 * TQ), 1) % TQ
    causal = key <= qry

    def scores(qi):
        q2 = q_ref[0, qi * TQ:(qi + 1) * TQ, :]
        qmix = jnp.concatenate([
            jnp.concatenate([jnp.where(first, q2, zeros), sel[0]], axis=1),
            jnp.concatenate([jnp.where(first, zeros, q2), sel[1]], axis=1)], axis=0)
        n_k = (qi + 1) * TK
        kaug = jnp.concatenate([k_ref[0, 0:n_k, :], kx_ref[0, 0:n_k, :]], axis=1)
        s = lax.dot_general(kaug, qmix, _NT, preferred_element_type=F32)
        chunks = [s[c * TK:(c + 1) * TK] for c in range(qi)]
        chunks.append(jnp.where(causal, s[qi * TK:n_k], NEG_INF))
        m = jnp.max(chunks[0], axis=0, keepdims=True)
        for c in range(1, qi + 1):
            m = jnp.maximum(m, jnp.max(chunks[c], axis=0, keepdims=True))
        return chunks, m

    def outputs(qi, chunks, m):
        n_k = (qi + 1) * TK
        p = [jnp.exp2(s - m) for s in chunks]
        l = jnp.sum(p[0], axis=0, keepdims=True)
        for c in range(1, qi + 1):
            l = l + jnp.sum(p[c], axis=0, keepdims=True)
        pt = jnp.concatenate([t.astype(BF16) for t in p], axis=0)
        acc = jnp.dot(vt_ref[0, :, 0:n_k], pt, preferred_element_type=F32)
        ot = (acc / l).T
        o_ref[0, qi * TQ:(qi + 1) * TQ, :] = jnp.where(first, ot[0:TQ], ot[TQ:2 * TQ]).astype(BF16)

    n_q = S // TQ
    nxt = scores(0)
    for qi in range(n_q):
        cur = nxt
        if qi + 1 < n_q:
            nxt = scores(qi + 1)
        outputs(qi, *cur)


def _fox(qk, vt, kx):
    B, S, _ = qk.shape
    n_pairs = FOX_HEADS // 2
    pw = 2 * HEAD_DIM
    spec = lambda off: pl.BlockSpec((1, S, pw), lambda b, p: (b, 0, off + p))
    return pl.pallas_call(
        _fox_kernel,
        grid=(B, n_pairs),
        in_specs=[spec(0), spec(n_pairs),
                  pl.BlockSpec((1, S, 128), lambda b, p: (b, 0, 0)),
                  pl.BlockSpec((1, pw, S), lambda b, p: (b, p, 0))],
        out_specs=pl.BlockSpec((1, S, pw), lambda b, p: (b, 0, p)),
        out_shape=jax.ShapeDtypeStruct((B, S, FOX_W), BF16),
        compiler_params=pltpu.CompilerParams(
            dimension_semantics=("arbitrary", "arbitrary"), vmem_limit_bytes=VMEM_LIMIT),
        name="fox",
    )(qk, qk, kx, vt)


def _dil_kernel(d0_ref, d1_ref, d2_ref, o_ref, acc_scr, m_scr, l_scr, nat_scr):
    S = d0_ref.shape[1]
    nh = DIL_HEADS_PER_GROUP
    blk = DIL_BLK
    n_blocks = S // blk
    lane_head = lax.broadcasted_iota(jnp.int32, (1, GROUP_W), 1) // HEAD_DIM
    a2 = lax.broadcasted_iota(jnp.int32, (nh * blk, 2 * blk), 0) % blk
    k2 = lax.broadcasted_iota(jnp.int32, (nh * blk, 2 * blk), 1)
    bias_two = jnp.where((k2 >= a2) & (k2 <= a2 + blk), 0.0, NEG_INF)
    a1 =lax.broadcasted_iota(jnp.int32, (nh * blk, blk), 0) % blk
    k1 = lax.broadcasted_iota(jnp.int32, (nh * blk, blk), 1)
    bias_one = jnp.where(k1 <= a1, 0.0, NEG_INF)

    def assemble(x):
        out = x[(nh - 1) * blk:nh * blk]
        for h in range(nh - 2, -1, -1):
            out = jnp.where(lane_head == h, x[h * blk:(h + 1) * blk], out)
        return out

    def attend(qs, kcat, vcat, bias):
        zero = jnp.zeros_like(qs)
        qstack = jnp.concatenate(
            [jnp.where(lane_head == h, qs, zero) for h in range(nh)], axis=0)
        s = lax.dot_general(qstack, kcat, _NT, preferred_element_type=F32) + bias
        m = jnp.max(s, axis=1, keepdims=True)
        p = jnp.exp2(s - m)
        l = jnp.sum(p, axis=1, keepdims=True)
        pv = jnp.dot(p.astype(BF16), vcat, preferred_element_type=F32)
        return (assemble(pv), assemble(jnp.broadcast_to(m, pv.shape)),
                assemble(jnp.broadcast_to(l, pv.shape)))

    def put(res, pieces, first):
        acc_b, m_b, l_b = res
        for rows, idx in pieces:
            for c in range(GROUP_W // 128):
                lanes = slice(c * 128, (c + 1) * 128)
                if first:
                    acc_scr[c, idx, :] = acc_b[rows, lanes]
                    m_scr[c, idx, :] = m_b[rows, lanes]
                    l_scr[c, idx, :] = l_b[rows, lanes]
                else:
                    m_o = m_scr[c, idx, :]
                    m_n = jnp.maximum(m_o, m_b[rows, lanes])
                    e_o = jnp.exp2(m_o - m_n)
                    e_b = jnp.exp2(m_b[rows, lanes] - m_n)
                    acc_scr[c, idx, :] = acc_scr[c, idx, :] * e_o + acc_b[rows, lanes] * e_b
                    l_scr[c, idx, :] = l_scr[c, idx, :] * e_o + l_b[rows, lanes] * e_b
                    m_scr[c, idx, :] = m_n

    q_l, k_l, v_l = (slice(i * GROUP_W, (i + 1) * GROUP_W) for i in range(3))
    whole = slice(0, blk)
    n_res = 4
    slab = S // n_res
    per_slab = slab // blk

    def g1_slab(res, _):
        for pos in range(per_slab):
            r0 = _aligned(res * slab + pos * blk, blk)
            qs = d1_ref[0, pl.ds(r0, blk), q_l]
            if pos == 0:
                out = attend(qs, d1_ref[0, pl.ds(r0, blk), k_l], d1_ref[0, pl.ds(r0, blk), v_l], bias_one)
            else:
                rp = _aligned(r0 - blk, blk)
                out = attend(qs, d1_ref[0, pl.ds(rp, 2 * blk), k_l], d1_ref[0, pl.ds(rp, 2 * blk), v_l],
                             bias_two)
            put(out, [(whole, pl.ds(r0, blk))], first=True)
        return 0

    for res in range(n_res):
        g1_slab(res, 0)

    sub = blk // n_res
    qa = lax.broadcasted_iota(jnp.int32, (nh * blk, 2 * blk), 0) % blk
    kk = lax.broadcasted_iota(jnp.int32, (nh * blk, 2 * blk), 1)
    tq = blk + n_res * (qa % sub) + qa // sub
    tk = n_res * (kk % (2 * sub)) + kk // (2 * sub)
    bias0_two = jnp.where((tk <= tq) & (tq - tk <= blk), 0.0, NEG_INF)
    qa1 = lax.broadcasted_iota(jnp.int32, (nh * blk, blk), 0) % blk
    kk1 = lax.broadcasted_iota(jnp.int32, (nh * blk, blk), 1)
    bias0_one = jnp.where(n_res * (kk1 % sub) + kk1 // sub <= n_res * (qa1 % sub) + qa1 // sub,
                          0.0, NEG_INF)

    def g0_block(n, has_prev):
        rq = [_aligned(r * slab + n * sub, sub) for r in range(n_res)]
        qs = jnp.concatenate([d0_ref[0, pl.ds(rq[r], sub), q_l] for r in range(n_res)], axis=0)
        if has_prev:
            rk = [_aligned(r * slab + (n - 1) * sub, sub) for r in range(n_res)]
            kcat = jnp.concatenate([d0_ref[0, pl.ds(rk[r], 2 * sub), k_l] for r in range(n_res)], axis=0)
            vcat = jnp.concatenate([d0_ref[0, pl.ds(rk[r], 2 * sub), v_l] for r in range(n_res)], axis=0)
            out = attend(qs, kcat, vcat, bias0_two)
        else:
            kcat = jnp.concatenate([d0_ref[0, pl.ds(rq[r], sub), k_l] for r in range(n_res)], axis=0)
            vcat = jnp.concatenate([d0_ref[0, pl.ds(rq[r], sub), v_l] for r in range(n_res)], axis=0)
            out = attend(qs, kcat, vcat, bias0_one)
        put(out, [(slice(r * sub, (r + 1) * sub), pl.ds(rq[r], sub)) for r in range(n_res)],
            first=False)

    g0_block(0, False)

    def g0_rest(n, _):
        g0_block(n, True)
        return 0

    for n in range(1, n_blocks):
        g0_rest(n, 0)

    def g2_block(r16, _):
        r0 = _aligned(r16 * blk, blk)
        out = attend(d2_ref[0, pl.ds(r0, blk), q_l], d2_ref[0, pl.ds(r0, blk), k_l],
                     d2_ref[0, pl.ds(r0, blk), v_l], bias_one)
        put(out, [(whole, pl.ds((r16 % n_res) * slab + r16 // n_res, blk, stride=n_res))], first=False)
        return 0

    for r16 in range(n_blocks):
        g2_block(r16, 0)

    def finish(res, _):
        for pos in range(per_slab):
            r0 = _aligned(res * slab + pos * blk, blk)
            for c in range(GROUP_W // 128):
                nat_scr[c, pl.ds(n_res * pos * blk + res, blk, stride=n_res), :] = (
                    acc_scr[c, pl.ds(r0, blk), :] / l_scr[c, pl.ds(r0, blk), :])
        return 0

    for res in range(n_res):
        finish(res, 0)

    def emit(i, _):
        r0 = _aligned(i * 2 * blk, 2 * blk)
        for c in range(GROUP_W // 128):
            o_ref[0, pl.ds(r0, 2 * blk), c * 128:(c + 1) * 128] = nat_scr[c, pl.ds(r0, 2 * blk), :].astype(BF16)
        return 0

    for i in range(n_blocks // 2):
        emit(i, 0)


def _dilated(d0, d1, d2):
    B, S, W = d0.shape
    spec = pl.BlockSpec((1, S, W), lambda b: (b, 0, 0))
    return pl.pallas_call(
        _dil_kernel,
        grid=(B,),
        in_specs=[spec, spec, spec],
        out_specs=pl.BlockSpec((1, S, GROUP_W), lambda b: (b, 0, 0)),
        out_shape=jax.ShapeDtypeStruct((B, S, GROUP_W), BF16),
        scratch_shapes=[pltpu.VMEM((GROUP_W // 128, S, 128), F32)] * 4,
        compiler_params=pltpu.CompilerParams(
            dimension_semantics=("arbitrary",), vmem_limit_bytes=VMEM_LIMIT),
        name="dilated",
    )(d0, d1, d2)


def _outmlp_kernel(x_ref, oa_ref, ob_ref, gate_ref, wa_ref, wb_ref, wo_ref, g2_ref,
                   wu_ref, wdn_ref, g3_ref, o_ref, *, final_norm):
    d = x_ref.shape[1]
    ya = jnp.dot(oa_ref[...], wa_ref[...], preferred_element_type=F32)
    yb = jnp.dot(ob_ref[...], wb_ref[...], preferred_element_type=F32)
    mixed = gate_ref[:, 0:d].astype(F32) * ya + gate_ref[:, d:2 * d].astype(F32) * yb
    x1 = x_ref[...] + jnp.dot(mixed.astype(BF16), wo_ref[...], preferred_element_type=F32)
    h = (x1 * g2_ref[...]).astype(BF16)
    inv = lax.rsqrt(jnp.mean(x1 * x1, axis=-1, keepdims=True) + EPS)
    d_ff = wu_ref.shape[1]
    mlp = None
    for c in range(d_ff // FF_CHUNK):
        u = jnp.dot(h, wu_ref[:, c * FF_CHUNK:(c + 1) * FF_CHUNK], preferred_element_type=F32)
        a = jnp.square(jnp.maximum(u, 0.0)).astype(BF16)
        t = jnp.dot(a, wdn_ref[c * FF_CHUNK:(c + 1) * FF_CHUNK, :], preferred_element_type=F32)
        mlp = t if mlp is None else mlp + t
    x2 = x1 + (inv * inv) * mlp
    o_ref[...] = _rms(x2, g3_ref[...]) if final_norm else x2


def _outmlp(x, oa, ob, gates, w_a, w_b, w_o, g2, w_u, w_dn, g3, final_norm):
    N, D = x.shape
    tm = TM_OUT
    tok = lambda w: pl.BlockSpec((tm, w), lambda i: (i, 0))
    return pl.pallas_call(
        functools.partial(_outmlp_kernel, final_norm=final_norm),
        grid=(N // tm,),
        in_specs=[tok(D), tok(oa.shape[1]), tok(ob.shape[1]), tok(gates.shape[1]),
                  _resident(w_a.shape), _resident(w_b.shape), _resident(w_o.shape),
                  _resident(g2.shape), _resident(w_u.shape), _resident(w_dn.shape),
                  _resident(g3.shape)],
        out_specs=tok(D),
        out_shape=jax.ShapeDtypeStruct((N, D), F32),
        compiler_params=pltpu.CompilerParams(
            dimension_semantics=("arbitrary",), vmem_limit_bytes=VMEM_LIMIT),
        name="outmlp",
    )(x, oa, ob, gates, w_a, w_b, w_o, g2, w_u, w_dn, g3)


def _rope_tables(S):
    inv_freq = jnp.power(jnp.float32(ROPE_THETA),
                         -jnp.arange(ROPE_HALF, dtype=F32) * 2.0 / ROPE_DIM)
    ang = jnp.arange(S).astype(F32)[:, None] * inv_freq[None, :]
    cos, sin = jnp.cos(ang), jnp.sin(ang)
    rest = HEAD_DIM - ROPE_DIM
    one = jnp.ones((S, rest), F32)
    zero = lambda w: jnp.zeros((S, w), F32)
    cos_h = jnp.concatenate([cos, cos, one], axis=1)
    slo_h = jnp.concatenate([-sin, zero(ROPE_HALF + rest)], axis=1)
    shi_h = jnp.concatenate([zero(ROPE_HALF), sin, zero(rest)], axis=1)
    rep = lambda t: jnp.tile(t, (1, 128 // HEAD_DIM))
    return rep(cos_h), rep(slo_h), rep(shi_h)


def kernel(x, norm_attn_g, w_in, b_forget, w_branch_a, w_branch_b, w_out, norm_mlp_g, w_up,
           w_down, norm_final_g):
    B, S, D = x.shape
    depth = w_in.shape[0]
    assert S % (16 * DIL_BLK) == 0 and S // 16 == DIL_BLK, "sequence length fixed by the dilation groups"
    cos, slo, shi = _rope_tables(S)
    sizes = (FOX_W, FOX_W, FOX_W, FOX_HEADS, DIL_W, DIL_W, DIL_W, D, D)
    offs = np.concatenate([[0], np.cumsum(sizes)])
    for l in range(depth):
        wt = jnp.swapaxes(w_in[l], 0, 1)
        w_qk = wt[:offs[2]].astype(BF16)
        w_t = wt[offs[4]:].astype(BF16)
        pad = -(3 * FOX_HEADS) % 16
        w_vf = jnp.concatenate(
            [wt[offs[2]:offs[3]], jnp.repeat(wt[offs[3]:offs[4]], 3, axis=0),
             jnp.zeros((pad, D), wt.dtype)], axis=0).astype(BF16)
        b_f = jnp.pad(jnp.repeat(b_forget[l].astype(F32), 3), (0, pad))[:, None]

        qk, vt, kx, d0, d1, d2, gates = _inproj(
            x, norm_attn_g[l][None, :], w_qk, w_vf, w_t, b_f, cos, slo, shi)
        oa = _fox(qk, vt, kx)
        ob = _dilated(d0.reshape(B, S, -1), d1.reshape(B, S, -1), d2.reshape(B, S, -1))
        y = _outmlp(
            x.reshape(B * S, D), oa.reshape(B * S, -1), ob.reshape(B * S, -1),
            gates.reshape(B * S, -1),
            w_branch_a[l].astype(BF16), w_branch_b[l].astype(BF16), w_out[l].astype(BF16),
            norm_mlp_g[l][None, :], w_up[l].astype(BF16), w_down[l].astype(BF16),
            norm_final_g[None, :], final_norm=(l == depth - 1))
        x = y.reshape(B, S, D)
    return x
```

```python
import functools

import numpy as np
import jax
import jax.numpy as jnp
from jax import lax
from jax.experimental import pallas as pl
from jax.experimental.pallas import tpu as pltpu

F32 = jnp.float32
BF16 = jnp.bfloat16

HEAD_DIM = 64
FOX_HEADS = 8
FOX_W = FOX_HEADS * HEAD_DIM
DIL_GROUPS = ((128, 1), (512, 4), (2048, 16))
DIL_HEADS_PER_GROUP = 4
GROUP_W = DIL_HEADS_PER_GROUP * HEAD_DIM
N_GROUPS = len(DIL_GROUPS)
DIL_W = N_GROUPS * GROUP_W
ROPE_THETA = 500000.0
ROPE_DIM = HEAD_DIM // 4
ROPE_HALF = ROPE_DIM // 2
EPS = 1e-6
NEG_INF = -1e30
Q_SCALE = HEAD_DIM ** -0.5
LOG2E = 1.4426950408889634

DIL_BLK = 128
TM_IN = 512
TM_OUT = 512
TQ = 256
TK = 256
FOX_PAIRS_PER_STEP = 2
FF_CHUNK = 1024
VMEM_LIMIT = 56 * 1024 * 1024

_NT = (((1,), (1,)), ((), ()))


def _resident(shape):
    nd = len(shape)
    return pl.BlockSpec(shape, lambda *_: (0,) * nd, pipeline_mode=pl.Buffered(1))


def _aligned(x, m):
    return x if isinstance(x, int) else pl.multiple_of(x, m)


def _rms(x, g):
    ms = jnp.mean(x * x, axis=-1, keepdims=True)
    return x * lax.rsqrt(ms + EPS) * g


def _rope(x, cos, sin_lo, sin_hi):
    w = x.shape[1]
    up = pltpu.roll(x, w - ROPE_HALF, 1)
    dn = pltpu.roll(x, ROPE_HALF, 1)
    return x * cos + up * sin_lo + dn * sin_hi


def _split3(x):
    hi = x.astype(BF16)
    r1 = x - hi.astype(F32)
    mid = r1.astype(BF16)
    lo = (r1 - mid.astype(F32)).astype(BF16)
    return hi, mid, lo


def _inproj_kernel(x_ref, g_ref, wqk_ref, wvf_ref, wt_ref, bf_ref,
                   cos_ref, slo_ref, shi_ref,
                   qk_ref, vt_ref, kx_ref, d0_ref, d1_ref, d2_ref, gate_ref,
                   perm_scr, carry_scr):
    tm = x_ref.shape[1]
    d = x_ref.shape[2]

    @pl.when(pl.program_id(1) == 0)
    def _reset():
        carry_scr[...] = jnp.zeros_like(carry_scr)

    xt = x_ref[0]
    h = (xt * g_ref[...]).astype(BF16)
    inv = lax.rsqrt(jnp.mean(xt * xt, axis=-1, keepdims=True) + EPS)
    inv_row = jnp.broadcast_to(inv, (tm, 128)).T[0:1, :]
    proj = lambda ref, lo, hi: inv * lax.dot_general(h, ref[lo:hi, :], _NT,
                                                     preferred_element_type=F32)
    g_off = 3 * DIL_W
    lane = lax.broadcasted_iota(jnp.int32, (1, 128), 1)

    n_f = wvf_ref.shape[0] - FOX_W

    def v_and_forget_logits():
        vf = lax.dot_general(wvf_ref[...], h, _NT, preferred_element_type=F32) * inv_row
        vt_ref[0] = vf[0:FOX_W].astype(BF16)
        z = vf[FOX_W:FOX_W + n_f] + bf_ref[...]
        row = lax.broadcasted_iota(jnp.int32, (n_f, 1), 0)
        log_f = jnp.where(row < 3 * FOX_HEADS,
                          -(jnp.maximum(-z, 0.0) + jnp.log1p(jnp.exp(-jnp.abs(z)))), 0.0)
        return _split3(log_f)

    def forget_cumsum(parts):
        row = lax.broadcasted_iota(jnp.int32, (tm, tm), 0)
        col = lax.broadcasted_iota(jnp.int32, (tm, tm), 1)
        tri = jnp.where(row <= col, 1.0, 0.0).astype(BF16)
        cum = carry_scr[:, 0:1]
        for part in parts:
            cum = cum + jnp.dot(part, tri, preferred_element_type=F32)
        carry_scr[...] = jnp.broadcast_to(cum[:, tm - 1:tm], carry_scr.shape)
        nf = jnp.concatenate([-LOG2E * cum, jnp.zeros((128 - n_f, tm), F32)], axis=0).T
        hi, mid, lo = _split3(nf)
        kx_ref[0] = jnp.where(lane % 3 == 0, hi, jnp.where(lane % 3 == 1, mid, lo))

    def mixer_a_qk():
        qk_ref[0, :, 0:FOX_W] = (proj(wqk_ref, 0, FOX_W) * (Q_SCALE * LOG2E)).astype(BF16)
        qk_ref[0, :, FOX_W:2 * FOX_W] = proj(wqk_ref, FOX_W, 2 * FOX_W).astype(BF16)

    outs = (d0_ref, d1_ref, d2_ref)

    def mixer_b(g):
        dil = DIL_GROUPS[g][1]
        rows = pl.ds(pl.multiple_of(pl.program_id(1) * tm, tm), tm)
        cos, slo, shi = cos_ref[rows, :], slo_ref[rows, :], shi_ref[rows, :]
        part = lambda t: proj(wt_ref, t * DIL_W + g * GROUP_W, t * DIL_W + (g + 1) * GROUP_W)
        q, k, v = part(0), part(1), part(2)
        halves = []
        for c in range(GROUP_W // 128):
            halves.append(_rope(q[:, c * 128:(c + 1) * 128], cos, slo, shi) * (Q_SCALE * LOG2E))
        for c in range(GROUP_W // 128):
            halves.append(_rope(k[:, c * 128:(c + 1) * 128], cos, slo, shi))
        halves += [v[:, 0:128], v[:, 128:256]]
        n_res = max(dil, 4)
        for c, t in enumerate(halves):
            perm_scr[g, c] = t
        for res in range(n_res):
            for c in range(6):
                outs[g][0, res, :, c * 128:(c + 1) * 128] = (
                    perm_scr[g, c, pl.ds(res, tm // n_res, stride=n_res), :].astype(BF16))

    def gates(c):
        r = proj(wt_ref, g_off + c * 512, g_off + (c + 1) * 512)
        gate_ref[0, :, c * 512:(c + 1) * 512] = jax.nn.sigmoid(r).astype(BF16)

    parts = v_and_forget_logits()
    mixer_b(0)
    gates(0)
    gates(1)
    forget_cumsum(parts)
    mixer_b(1)
    gates(2)
    gates(3)
    mixer_b(2)
    mixer_a_qk()


def _inproj(x, g, w_qk, w_vf, w_t, b_f, cos, slo, shi):
    B, S, D = x.shape
    tm = TM_IN
    grid = (B, S // tm)
    tok = lambda w: pl.BlockSpec((1, tm, w), lambda b, i: (b, i, 0))
    tab = _resident(cos.shape)
    out_shape = (
        jax.ShapeDtypeStruct((B, S, 2 * FOX_W), BF16),
        jax.ShapeDtypeStruct((B, FOX_W, S), BF16),
        jax.ShapeDtypeStruct((B, S, 128), BF16),
        jax.ShapeDtypeStruct((B, 4, S // 4, 3 * GROUP_W), BF16),
        jax.ShapeDtypeStruct((B, 4, S // 4, 3 * GROUP_W), BF16),
        jax.ShapeDtypeStruct((B, 16, S // 16, 3 * GROUP_W), BF16),
        jax.ShapeDtypeStruct((B, S, 2 * D), BF16),
    )
    out_specs = (
        tok(2 * FOX_W),
        pl.BlockSpec((1, FOX_W, tm), lambda b, i: (b, 0, i)),
        tok(128),
        pl.BlockSpec((1, 4, tm // 4, 3 * GROUP_W), lambda b, i: (b, 0, i, 0)),
        pl.BlockSpec((1, 4, tm // 4, 3 * GROUP_W), lambda b, i: (b, 0, i, 0)),
        pl.BlockSpec((1, 16, tm // 16, 3 * GROUP_W), lambda b, i: (b, 0, i, 0)),
        tok(2 * D),
    )
    in_specs = [
        tok(D), _resident(g.shape), _resident(w_qk.shape), _resident(w_vf.shape),
        _resident(w_t.shape), _resident(b_f.shape), tab, tab, tab,
    ]
    n_f = w_vf.shape[0] - FOX_W
    return pl.pallas_call(
        _inproj_kernel,
        grid=grid,
        in_specs=in_specs,
        out_specs=out_specs,
        out_shape=out_shape,
        scratch_shapes=[pltpu.VMEM((N_GROUPS, 6, tm, 128), F32), pltpu.VMEM((n_f, 128), F32)],
        compiler_params=pltpu.CompilerParams(
            dimension_semantics=("arbitrary", "arbitrary"), vmem_limit_bytes=VMEM_LIMIT),
        name="inproj",
    )(x, g, w_qk, w_vf, w_t, b_f, cos, slo, shi)


def _fox_kernel(q_ref, k_ref, kx_ref, vt_ref, o_ref):
    S = q_ref.shape[1]
    pw = 2 * HEAD_DIM
    pairs = q_ref.shape[2] // pw
    step = pl.program_id(1)
    lane = lax.broadcasted_iota(jnp.int32, (1, pw), 1)
    first = lane < HEAD_DIM
    zeros = jnp.zeros((TQ, pw), BF16)

    def sel(head):
        return jnp.broadcast_to(
            jnp.where((lane >= 3 * head) & (lane < 3 * head + 3), 1.0, 0.0), (TQ, pw)).astype(BF16)

    key = lax.broadcasted_iota(jnp.int32, (TK, 2 * TQ), 0)
    qry = lax.broadcasted_iota(jnp.int32, (TK, 2 * TQ), 1) % TQ
    causal = key <= qry

    def scores(pair, qi):
        lanes = slice(pair * pw, (pair + 1) * pw)
        head0 = 2 * (pairs * step + pair)
        q2 = q_ref[0, qi * TQ:(qi + 1) * TQ, lanes]
        qmix = jnp.concatenate([
            jnp.concatenate([jnp.where(first, q2, zeros), sel(head0)], axis=1),
            jnp.concatenate([jnp.where(first, zeros, q2), sel(head0 + 1)], axis=1)], axis=0)
        n_k = (qi + 1) * TK
        kaug = jnp.concatenate([k_ref[0, 0:n_k, lanes], kx_ref[0, 0:n_k, :]], axis=1)
        s = lax.dot_general(kaug, qmix, _NT, preferred_element_type=F32)
        chunks = [s[c * TK:(c + 1) * TK] for c in range(qi)]
        chunks.append(jnp.where(causal, s[qi * TK:n_k], NEG_INF))
        m = jnp.max(chunks[0], axis=0, keepdims=True)
        for c in range(1, qi + 1):
            m = jnp.maximum(m, jnp.max(chunks[c], axis=0, keepdims=True))
        return chunks, m

    def outputs(pair, qi, chunks, m):
        n_k = (qi + 1) * TK
        p = [jnp.exp2(s - m) for s in chunks]
        l = jnp.sum(p[0], axis=0, keepdims=True)
        for c in range(1, qi + 1):
            l = l + jnp.sum(p[c], axis=0, keepdims=True)
        pt = jnp.concatenate([t.astype(BF16) for t in p], axis=0)
        acc = jnp.dot(vt_ref[0, pair * pw:(pair + 1) * pw, 0:n_k], pt,
                      preferred_element_type=F32)
        ot = (acc / l).T
        o_ref[0, qi * TQ:(qi + 1) * TQ, pair * pw:(pair + 1) * pw] = (
            jnp.where(first, ot[0:TQ], ot[TQ:2 * TQ]).astype(BF16))

    items = [(pair, qi) for pair in range(pairs) for qi in range(S // TQ)]
    nxt = scores(*items[0])
    for i, item in enumerate(items):
        cur = nxt
        if i + 1 < len(items):
            nxt = scores(*items[i + 1])
        outputs(*item, *cur)


def _fox(qk, vt, kx):
    B, S, _ = qk.shape
    pairs = FOX_PAIRS_PER_STEP
    steps = FOX_HEADS // (2 * pairs)
    w = pairs * 2 * HEAD_DIM
    spec = lambda off: pl.BlockSpec((1, S, w), lambda b, p: (b, 0, off + p))
    return pl.pallas_call(
        _fox_kernel,
        grid=(B, steps),
        in_specs=[spec(0), spec(steps),
                  pl.BlockSpec((1, S, 128), lambda b, p: (b, 0, 0)),
                  pl.BlockSpec((1, w, S), lambda b, p: (b, p, 0))],
        out_specs=pl.BlockSpec((1, S, w), lambda b, p: (b, 0, p)),
        out_shape=jax.ShapeDtypeStruct((B, S, FOX_W), BF16),
        compiler_params=pltpu.CompilerParams(
            dimension_semantics=("arbitrary", "arbitrary"), vmem_limit_bytes=VMEM_LIMIT),
        name="fox",
    )(qk, qk, kx, vt)


def _dil_kernel(d0_ref, d1_ref, d2_ref, o_ref, acc_scr, m_scr, l_scr, nat_scr):
    S = d0_ref.shape[1]
    nh = DIL_HEADS_PER_GROUP
    blk = DIL_BLK
    n_blocks = S // blk
    lane_head = lax.broadcasted_iota(jnp.int32, (1, GROUP_W), 1) // HEAD_DIM
    a2 = lax.broadcasted_iota(jnp.int32, (nh * blk, 2 * blk), 0) % blk
    k2 = lax.broadcasted_iota(jnp.int32, (nh * blk, 2 * blk), 1)
    bias_two = jnp.where((k2 >= a2) & (k2 <= a2 + blk), 0.0, NEG_INF)
    a1 =lax.broadcasted_iota(jnp.int32, (nh * blk, blk), 0) % blk
    k1 = lax.broadcasted_iota(jnp.int32, (nh * blk, blk), 1)
    bias_one = jnp.where(k1 <= a1, 0.0, NEG_INF)

    def assemble(x):
        out = x[(nh - 1) * blk:nh * blk]
        for h in range(nh - 2, -1, -1):
            out = jnp.where(lane_head == h, x[h * blk:(h + 1) * blk], out)
        return out

    def attend(qs, kcat, vcat, bias):
        zero = jnp.zeros_like(qs)
        qstack = jnp.concatenate(
            [jnp.where(lane_head == h, qs, zero) for h in range(nh)], axis=0)
        s = lax.dot_general(qstack, kcat, _NT, preferred_element_type=F32) + bias
        m = jnp.max(s, axis=1, keepdims=True)
        p = jnp.exp2(s - m)
        l = jnp.sum(p, axis=1, keepdims=True)
        pv = jnp.dot(p.astype(BF16), vcat, preferred_element_type=F32)
        return (assemble(pv), assemble(jnp.broadcast_to(m, pv.shape)),
                assemble(jnp.broadcast_to(l, pv.shape)))

    def put(res, pieces, first):
        acc_b, m_b, l_b = res
        for rows, idx in pieces:
            for c in range(GROUP_W // 128):
                lanes = slice(c * 128, (c + 1) * 128)
                if first:
                    acc_scr[c, idx, :] = acc_b[rows, lanes]
                    m_scr[c, idx, :] = m_b[rows, lanes]
                    l_scr[c, idx, :] = l_b[rows, lanes]
                else:
                    m_o = m_scr[c, idx, :]
                    m_n = jnp.maximum(m_o, m_b[rows, lanes])
                    e_o = jnp.exp2(m_o - m_n)
                    e_b = jnp.exp2(m_b[rows, lanes] - m_n)
                    acc_scr[c, idx, :] = acc_scr[c, idx, :] * e_o + acc_b[rows, lanes] * e_b
                    l_scr[c, idx, :] = l_scr[c, idx, :] * e_o + l_b[rows, lanes] * e_b
                    m_scr[c, idx, :] = m_n

    q_l, k_l, v_l = (slice(i * GROUP_W, (i + 1) * GROUP_W) for i in range(3))
    whole = slice(0, blk)
    n_res = 4
    slab = S // n_res
    per_slab = slab // blk

    def g1_slab(res, _):
        for pos in range(per_slab):
            r0 = _aligned(res * slab + pos * blk, blk)
            qs = d1_ref[0, pl.ds(r0, blk), q_l]
            if pos == 0:
                out = attend(qs, d1_ref[0, pl.ds(r0, blk), k_l], d1_ref[0, pl.ds(r0, blk), v_l], bias_one)
            else:
                rp = _aligned(r0 - blk, blk)
                out = attend(qs, d1_ref[0, pl.ds(rp, 2 * blk), k_l], d1_ref[0, pl.ds(rp, 2 * blk), v_l],
                             bias_two)
            put(out, [(whole, pl.ds(r0, blk))], first=True)
        return 0

    for res in range(n_res):
        g1_slab(res, 0)

    sub = blk // n_res
    qa = lax.broadcasted_iota(jnp.int32, (nh * blk, 2 * blk), 0) % blk
    kk = lax.broadcasted_iota(jnp.int32, (nh * blk, 2 * blk), 1)
    tq = blk + n_res * (qa % sub) + qa // sub
    tk = n_res * (kk % (2 * sub)) + kk // (2 * sub)
    bias0_two = jnp.where((tk <= tq) & (tq - tk <= blk), 0.0, NEG_INF)
    qa1 = lax.broadcasted_iota(jnp.int32, (nh * blk, blk), 0) % blk
    kk1 = lax.broadcasted_iota(jnp.int32, (nh * blk, blk), 1)
    bias0_one = jnp.where(n_res * (kk1 % sub) + kk1 // sub <= n_res * (qa1 % sub) + qa1 // sub,
                          0.0, NEG_INF)

    def g0_block(n, has_prev):
        rq = [_aligned(r * slab + n * sub, sub) for r in range(n_res)]
        qs = jnp.concatenate([d0_ref[0, pl.ds(rq[r], sub), q_l] for r in range(n_res)], axis=0)
        if has_prev:
            rk = [_aligned(r * slab + (n - 1) * sub, sub) for r in range(n_res)]
            kcat = jnp.concatenate([d0_ref[0, pl.ds(rk[r], 2 * sub), k_l] for r in range(n_res)], axis=0)
            vcat = jnp.concatenate([d0_ref[0, pl.ds(rk[r], 2 * sub), v_l] for r in range(n_res)], axis=0)
            out = attend(qs, kcat, vcat, bias0_two)
        else:
            kcat = jnp.concatenate([d0_ref[0, pl.ds(rq[r], sub), k_l] for r in range(n_res)], axis=0)
            vcat = jnp.concatenate([d0_ref[0, pl.ds(rq[r], sub), v_l] for r in range(n_res)], axis=0)
            out = attend(qs, kcat, vcat, bias0_one)
        put(out, [(slice(r * sub, (r + 1) * sub), pl.ds(rq[r], sub)) for r in range(n_res)],
            first=False)

    g0_block(0, False)

    def g0_rest(n, _):
        g0_block(n, True)
        return 0

    for n in range(1, n_blocks):
        g0_rest(n, 0)

    def g2_block(r16, _):
        r0 = _aligned(r16 * blk, blk)
        out = attend(d2_ref[0, pl.ds(r0, blk), q_l], d2_ref[0, pl.ds(r0, blk), k_l],
                     d2_ref[0, pl.ds(r0, blk), v_l], bias_one)
        put(out, [(whole, pl.ds((r16 % n_res) * slab + r16 // n_res, blk, stride=n_res))], first=False)
        return 0

    for r16 in range(n_blocks):
        g2_block(r16, 0)

    def finish(res, _):
        for pos in range(per_slab):
            r0 = _aligned(res * slab + pos * blk, blk)
            for c in range(GROUP_W // 128):
                nat_scr[c, pl.ds(n_res * pos * blk + res, blk, stride=n_res), :] = (
                    acc_scr[c, pl.ds(r0, blk), :] / l_scr[c, pl.ds(r0, blk), :])
        return 0

    for res in range(n_res):
        finish(res, 0)

    def emit(i, _):
        r0 = _aligned(i * 2 * blk, 2 * blk)
        for c in range(GROUP_W // 128):
            o_ref[0, pl.ds(r0, 2 * blk), c * 128:(c + 1) * 128] = nat_scr[c, pl.ds(r0, 2 * blk), :].astype(BF16)
        return 0

    for i in range(n_blocks // 2):
        emit(i, 0)


def _dilated(d0, d1, d2):
    B, S, W = d0.shape
    spec = pl.BlockSpec((1, S, W), lambda b: (b, 0, 0))
    return pl.pallas_call(
        _dil_kernel,
        grid=(B,),
        in_specs=[spec, spec, spec],
        out_specs=pl.BlockSpec((1, S, GROUP_W), lambda b: (b, 0, 0)),
        out_shape=jax.ShapeDtypeStruct((B, S, GROUP_W), BF16),
        scratch_shapes=[pltpu.VMEM((GROUP_W // 128, S, 128), F32)] * 4,
        compiler_params=pltpu.CompilerParams(
            dimension_semantics=("arbitrary",), vmem_limit_bytes=VMEM_LIMIT),
        name="dilated",
    )(d0, d1, d2)


def _outmlp_kernel(x_ref, oa_ref, ob_ref, gate_ref, wa_ref, wb_ref, wo_ref, g2_ref,
                   wu_ref, wdn_ref, g3_ref, o_ref, *, final_norm):
    d = x_ref.shape[1]
    ya = jnp.dot(oa_ref[...], wa_ref[...], preferred_element_type=F32)
    yb = jnp.dot(ob_ref[...], wb_ref[...], preferred_element_type=F32)
    mixed = gate_ref[:, 0:d].astype(F32) * ya + gate_ref[:, d:2 * d].astype(F32) * yb
    x1 = x_ref[...] + jnp.dot(mixed.astype(BF16), wo_ref[...], preferred_element_type=F32)
    h = (x1 * g2_ref[...]).astype(BF16)
    inv = lax.rsqrt(jnp.mean(x1 * x1, axis=-1, keepdims=True) + EPS)
    d_ff = wu_ref.shape[1]
    mlp = None
    for c in range(d_ff // FF_CHUNK):
        u = jnp.dot(h, wu_ref[:, c * FF_CHUNK:(c + 1) * FF_CHUNK], preferred_element_type=F32)
        a = jnp.square(jnp.maximum(u, 0.0)).astype(BF16)
        t = jnp.dot(a, wdn_ref[c * FF_CHUNK:(c + 1) * FF_CHUNK, :], preferred_element_type=F32)
        mlp = t if mlp is None else mlp + t
    x2 = x1 + (inv * inv) * mlp
    o_ref[...] = _rms(x2, g3_ref[...]) if final_norm else x2


def _outmlp(x, oa, ob, gates, w_a, w_b, w_o, g2, w_u, w_dn, g3, final_norm):
    N, D = x.shape
    tm = TM_OUT
    tok = lambda w: pl.BlockSpec((tm, w), lambda i: (i, 0))
    return pl.pallas_call(
        functools.partial(_outmlp_kernel, final_norm=final_norm),
        grid=(N // tm,),
        in_specs=[tok(D), tok(oa.shape[1]), tok(ob.shape[1]), tok(gates.shape[1]),
                  _resident(w_a.shape), _resident(w_b.shape), _resident(w_o.shape),
                  _resident(g2.shape), _resident(w_u.shape), _resident(w_dn.shape),
                  _resident(g3.shape)],
        out_specs=tok(D),
        out_shape=jax.ShapeDtypeStruct((N, D), F32),
        compiler_params=pltpu.CompilerParams(
            dimension_semantics=("arbitrary",), vmem_limit_bytes=VMEM_LIMIT),
        name="outmlp",
    )(x, oa, ob, gates, w_a, w_b, w_o, g2, w_u, w_dn, g3)


def _rope_tables(S):
    inv_freq = jnp.power(jnp.float32(ROPE_THETA),
                         -jnp.arange(ROPE_HALF, dtype=F32) * 2.0 / ROPE_DIM)
    ang = jnp.arange(S).astype(F32)[:, None] * inv_freq[None, :]
    cos, sin = jnp.cos(ang), jnp.sin(ang)
    rest = HEAD_DIM - ROPE_DIM
    one = jnp.ones((S, rest), F32)
    zero = lambda w: jnp.zeros((S, w), F32)
    cos_h = jnp.concatenate([cos, cos, one], axis=1)
    slo_h = jnp.concatenate([-sin, zero(ROPE_HALF + rest)], axis=1)
    shi_h = jnp.concatenate([zero(ROPE_HALF), sin, zero(rest)], axis=1)
    rep = lambda t: jnp.tile(t, (1, 128 // HEAD_DIM))
    return rep(cos_h), rep(slo_h), rep(shi_h)


def kernel(x, norm_attn_g, w_in, b_forget, w_branch_a, w_branch_b, w_out, norm_mlp_g, w_up,
           w_down, norm_final_g):
    B, S, D = x.shape
    depth = w_in.shape[0]
    assert S % (16 * DIL_BLK) == 0 and S // 16 == DIL_BLK, "sequence length fixed by the dilation groups"
    cos, slo, shi = _rope_tables(S)
    sizes = (FOX_W, FOX_W, FOX_W, FOX_HEADS, DIL_W, DIL_W, DIL_W, D, D)
    offs = np.concatenate([[0], np.cumsum(sizes)])
    for l in range(depth):
        wt = jnp.swapaxes(w_in[l], 0, 1)
        w_qk = wt[:offs[2]].astype(BF16)
        mid = offs[4] + 3 * DIL_W
        w_t = jnp.concatenate([wt[offs[4]:mid], wt[mid:]], axis=0).astype(BF16)
        pad = -(3 * FOX_HEADS) % 16
        w_vf = jnp.concatenate(
            [wt[offs[2]:offs[3]], jnp.repeat(wt[offs[3]:offs[4]], 3, axis=0),
             jnp.zeros((pad, D), wt.dtype)], axis=0).astype(BF16)
        b_f = jnp.pad(jnp.repeat(b_forget[l].astype(F32), 3), (0, pad))[:, None]

        qk, vt, kx, d0, d1, d2, gates = _inproj(
            x, norm_attn_g[l][None, :], w_qk, w_vf, w_t, b_f, cos, slo, shi)
        oa = _fox(qk, vt, kx)
        ob = _dilated(d0.reshape(B, S, -1), d1.reshape(B, S, -1), d2.reshape(B, S, -1))
        y = _outmlp(
            x.reshape(B * S, D), oa.reshape(B * S, -1), ob.reshape(B * S, -1),
            gates.reshape(B * S, -1),
            w_branch_a[l].astype(BF16), w_branch_b[l].astype(BF16), w_out[l].astype(BF16),
            norm_mlp_g[l][None, :], w_up[l].astype(BF16), w_down[l].astype(BF16),
            norm_final_g[None, :], final_norm=(l == depth - 1))
        x = y.reshape(B, S, D)
    return x
```

```python
import functools

import numpy as np
import jax
import jax.numpy as jnp
from jax import lax
from jax.experimental import pallas as pl
from jax.experimental.pallas import tpu as pltpu

F32 = jnp.float32
BF16 = jnp.bfloat16

HEAD_DIM = 64
FOX_HEADS = 8
FOX_W = FOX_HEADS * HEAD_DIM
DIL_GROUPS = ((128, 1), (512, 4), (2048, 16))
DIL_HEADS_PER_GROUP = 4
GROUP_W = DIL_HEADS_PER_GROUP * HEAD_DIM
N_GROUPS = len(DIL_GROUPS)
DIL_W = N_GROUPS * GROUP_W
ROPE_THETA = 500000.0
ROPE_DIM = HEAD_DIM // 4
ROPE_HALF = ROPE_DIM // 2
EPS = 1e-6
NEG_INF = -1e30
Q_SCALE = HEAD_DIM ** -0.5
LOG2E = 1.4426950408889634

DIL_BLK = 128
TM_IN = 512
TM_OUT = 512
TQ = 256
TK = 256
FOX_PAIRS_PER_STEP = 2
FF_CHUNK = 1024
VMEM_LIMIT = 56 * 1024 * 1024

_NT = (((1,), (1,)), ((), ()))


def _resident(shape):
    nd = len(shape)
    return pl.BlockSpec(shape, lambda *_: (0,) * nd, pipeline_mode=pl.Buffered(1))


def _aligned(x, m):
    return x if isinstance(x, int) else pl.multiple_of(x, m)


def _rms(x, g):
    ms = jnp.mean(x * x, axis=-1, keepdims=True)
    return x * lax.rsqrt(ms + EPS) * g


def _rope(x, cos, sin_lo, sin_hi):
    w = x.shape[1]
    up = pltpu.roll(x, w - ROPE_HALF, 1)
    dn = pltpu.roll(x, ROPE_HALF, 1)
    return x * cos + up * sin_lo + dn * sin_hi


def _split3(x):
    hi = x.astype(BF16)
    r1 = x - hi.astype(F32)
    mid = r1.astype(BF16)
    lo = (r1 - mid.astype(F32)).astype(BF16)
    return hi, mid, lo


def _inproj_kernel(x_ref, g_ref, wqk_ref, wvf_ref, wt_ref, bf_ref,
                   cos_ref, slo_ref, shi_ref,
                   qk_ref, vt_ref, kx_ref, d0_ref, d1_ref, d2_ref, gate_ref,
                   perm_scr, carry_scr):
    tm = x_ref.shape[1]
    d = x_ref.shape[2]

    @pl.when(pl.program_id(1) == 0)
    def _reset():
        carry_scr[...] = jnp.zeros_like(carry_scr)

    xt = x_ref[0]
    h = (xt * g_ref[...]).astype(BF16)
    inv = lax.rsqrt(jnp.mean(xt * xt, axis=-1, keepdims=True) + EPS)
    inv_row = jnp.broadcast_to(inv, (tm, 128)).T[0:1, :]
    proj = lambda ref, lo, hi: inv * lax.dot_general(h, ref[lo:hi, :], _NT,
                                                     preferred_element_type=F32)
    g_off = 3 * DIL_W
    lane = lax.broadcasted_iota(jnp.int32, (1, 128), 1)

    n_f = wvf_ref.shape[0] - FOX_W

    def v_and_forget_logits():
        vf = lax.dot_general(wvf_ref[...], h, _NT, preferred_element_type=F32) * inv_row
        vt_ref[0] = vf[0:FOX_W].astype(BF16)
        z = vf[FOX_W:FOX_W + n_f] + bf_ref[...]
        row = lax.broadcasted_iota(jnp.int32, (n_f, 1), 0)
        log_f = jnp.where(row < 3 * FOX_HEADS,
                          -(jnp.maximum(-z, 0.0) + jnp.log1p(jnp.exp(-jnp.abs(z)))), 0.0)
        return _split3(log_f)

    def forget_cumsum(parts):
        row = lax.broadcasted_iota(jnp.int32, (tm, tm), 0)
        col = lax.broadcasted_iota(jnp.int32, (tm, tm), 1)
        tri = jnp.where(row <= col, 1.0, 0.0).astype(BF16)
        cum = carry_scr[:, 0:1]
        for part in parts:
            cum = cum + jnp.dot(part, tri, preferred_element_type=F32)
        carry_scr[...] = jnp.broadcast_to(cum[:, tm - 1:tm], carry_scr.shape)
        nf = jnp.concatenate([-LOG2E * cum, jnp.zeros((128 - n_f, tm), F32)], axis=0).T
        hi, mid, lo = _split3(nf)
        kx_ref[0] = jnp.where(lane % 3 == 0, hi, jnp.where(lane % 3 == 1, mid, lo))

    def mixer_a_qk():
        qk_ref[0, :, 0:FOX_W] = (proj(wqk_ref, 0, FOX_W) * (Q_SCALE * LOG2E)).astype(BF16)
        qk_ref[0, :, FOX_W:2 * FOX_W] = proj(wqk_ref, FOX_W, 2 * FOX_W).astype(BF16)

    outs = (d0_ref, d1_ref, d2_ref)

    def mixer_b(g):
        dil = DIL_GROUPS[g][1]
        rows = pl.ds(pl.multiple_of(pl.program_id(1) * tm, tm), tm)
        cos, slo, shi = cos_ref[rows, :], slo_ref[rows, :], shi_ref[rows, :]
        part = lambda t: proj(wt_ref, t * DIL_W + g * GROUP_W, t * DIL_W + (g + 1) * GROUP_W)
        q, k, v = part(0), part(1), part(2)
        halves = []
        for c in range(GROUP_W // 128):
            halves.append(_rope(q[:, c * 128:(c + 1) * 128], cos, slo, shi) * (Q_SCALE * LOG2E))
        for c in range(GROUP_W // 128):
            halves.append(_rope(k[:, c * 128:(c + 1) * 128], cos, slo, shi))
        halves += [v[:, 0:128], v[:, 128:256]]
        n_res = max(dil, 4)
        for c, t in enumerate(halves):
            perm_scr[g, c] = t
        for res in range(n_res):
            for c in range(6):
                outs[g][0, res, :, c * 128:(c + 1) * 128] = (
                    perm_scr[g, c, pl.ds(res, tm // n_res, stride=n_res), :].astype(BF16))

    def gates(c):
        r = proj(wt_ref, g_off + c * 512, g_off + (c + 1) * 512)
        gate_ref[0, :, c * 512:(c + 1) * 512] = jax.nn.sigmoid(r).astype(BF16)

    parts = v_and_forget_logits()
    mixer_b(0)
    gates(0)
    gates(1)
    forget_cumsum(parts)
    mixer_b(1)
    gates(2)
    gates(3)
    mixer_b(2)
    mixer_a_qk()


def _inproj(x, g, w_qk, w_vf, w_t, b_f, cos, slo, shi):
    B, S, D = x.shape
    tm = TM_IN
    grid = (B, S // tm)
    tok = lambda w: pl.BlockSpec((1, tm, w), lambda b, i: (b, i, 0))
    tab = _resident(cos.shape)
    out_shape = (
        jax.ShapeDtypeStruct((B, S, 2 * FOX_W), BF16),
        jax.ShapeDtypeStruct((B, FOX_W, S), BF16),
        jax.ShapeDtypeStruct((B, S, 128), BF16),
        jax.ShapeDtypeStruct((B, 4, S // 4, 3 * GROUP_W), BF16),
        jax.ShapeDtypeStruct((B, 4, S // 4, 3 * GROUP_W), BF16),
        jax.ShapeDtypeStruct((B, 16, S // 16, 3 * GROUP_W), BF16),
        jax.ShapeDtypeStruct((B, S, 2 * D), BF16),
    )
    out_specs = (
        tok(2 * FOX_W),
        pl.BlockSpec((1, FOX_W, tm), lambda b, i: (b, 0, i)),
        tok(128),
        pl.BlockSpec((1, 4, tm // 4, 3 * GROUP_W), lambda b, i: (b, 0, i, 0)),
        pl.BlockSpec((1, 4, tm // 4, 3 * GROUP_W), lambda b, i: (b, 0, i, 0)),
        pl.BlockSpec((1, 16, tm // 16, 3 * GROUP_W), lambda b, i: (b, 0, i, 0)),
        tok(2 * D),
    )
    in_specs = [
        tok(D), _resident(g.shape), _resident(w_qk.shape), _resident(w_vf.shape),
        _resident(w_t.shape), _resident(b_f.shape), tab, tab, tab,
    ]
    n_f = w_vf.shape[0] - FOX_W
    return pl.pallas_call(
        _inproj_kernel,
        grid=grid,
        in_specs=in_specs,
        out_specs=out_specs,
        out_shape=out_shape,
        scratch_shapes=[pltpu.VMEM((N_GROUPS, 6, tm, 128), F32), pltpu.VMEM((n_f, 128), F32)],
        compiler_params=pltpu.CompilerParams(
            dimension_semantics=("arbitrary", "arbitrary"), vmem_limit_bytes=VMEM_LIMIT),
        name="inproj",
    )(x, g, w_qk, w_vf, w_t, b_f, cos, slo, shi)


def _fox_kernel(q_ref, k_ref, kx_ref, vt_ref, o_ref):
    S = q_ref.shape[1]
    pw = 2 * HEAD_DIM
    pairs = q_ref.shape[2] // pw
    step = pl.program_id(1)
    lane = lax.broadcasted_iota(jnp.int32, (1, pw), 1)
    first = lane < HEAD_DIM
    zeros = jnp.zeros((TQ, pw), BF16)

    def sel(head):
        return jnp.broadcast_to(
            jnp.where((lane >= 3 * head) & (lane < 3 * head + 3), 1.0, 0.0), (TQ, pw)).astype(BF16)

    hq = TQ // 2
    key = lax.broadcasted_iota(jnp.int32, (hq, TQ), 0)
    qry = lax.broadcasted_iota(jnp.int32, (hq, TQ), 1) % hq
    causal = key <= qry

    def colmax(parts):
        m = jnp.max(parts[0], axis=0, keepdims=True)
        for t in parts[1:]:
            m = jnp.maximum(m, jnp.max(t, axis=0, keepdims=True))
        return m

    def scores(pair, qi):
        lanes = slice(pair * pw, (pair + 1) * pw)
        head0 = 2 * (pairs * step + pair)
        q2 = q_ref[0, qi * TQ:(qi + 1) * TQ, lanes]
        qa = jnp.concatenate([jnp.where(first, q2, zeros), sel(head0)], axis=1)
        qb = jnp.concatenate([jnp.where(first, zeros, q2), sel(head0 + 1)], axis=1)
        qmix = jnp.concatenate([qa[0:hq], qb[0:hq], qa[hq:TQ], qb[hq:TQ]], axis=0)
        n_k = (qi + 1) * TK
        n_lo = n_k - hq
        kaug = jnp.concatenate([k_ref[0, 0:n_k, lanes], kx_ref[0, 0:n_k, :]], axis=1)
        s = lax.dot_general(kaug[0:n_lo], qmix, _NT, preferred_element_type=F32)
        s_hi = lax.dot_general(kaug[n_lo:n_k], qmix[TQ:2 * TQ], _NT, preferred_element_type=F32)
        full = [s[c * TK:(c + 1) * TK] for c in range(qi)]
        diag = s[qi * TK:n_lo]
        lo = [t[:, 0:TQ] for t in full] + [jnp.where(causal, diag[:, 0:TQ], NEG_INF)]
        hi = ([t[:, TQ:2 * TQ] for t in full] + [diag[:, TQ:2 * TQ]]
              + [jnp.where(causal, s_hi, NEG_INF)])
        return (lo, colmax(lo)), (hi, colmax(hi))

    def outputs(pair, qi, lo, hi):
        for half, (chunks, m) in enumerate((lo, hi)):
            n = (qi + 1) * TK - (hq if half == 0 else 0)
            p = [jnp.exp2(s - m) for s in chunks]
            l = jnp.sum(p[0], axis=0, keepdims=True)
            for t in p[1:]:
                l = l + jnp.sum(t, axis=0, keepdims=True)
            pt = jnp.concatenate([t.astype(BF16) for t in p], axis=0)
            acc = jnp.dot(vt_ref[0, pair * pw:(pair + 1) * pw, 0:n], pt,
                          preferred_element_type=F32)
            ot = (acc / l).T
            r0 = qi * TQ + half * hq
            o_ref[0, r0:r0 + hq, pair * pw:(pair + 1) * pw] = (
                jnp.where(first, ot[0:hq], ot[hq:TQ]).astype(BF16))

    items = [(pair, qi) for pair in range(pairs) for qi in range(S // TQ)]
    nxt = scores(*items[0])
    for i, item in enumerate(items):
        cur = nxt
        if i + 1 < len(items):
            nxt = scores(*items[i + 1])
        outputs(*item, *cur)


def _fox(qk, vt, kx):
    B, S, _ = qk.shape
    pairs = FOX_PAIRS_PER_STEP
    steps = FOX_HEADS // (2 * pairs)
    w = pairs * 2 * HEAD_DIM
    spec = lambda off: pl.BlockSpec((1, S, w), lambda b, p: (b, 0, off + p))
    return pl.pallas_call(
        _fox_kernel,
        grid=(B, steps),
        in_specs=[spec(0), spec(steps),
                  pl.BlockSpec((1, S, 128), lambda b, p: (b, 0, 0)),
                  pl.BlockSpec((1, w, S), lambda b, p: (b, p, 0))],
        out_specs=pl.BlockSpec((1, S, w), lambda b, p: (b, 0, p)),
        out_shape=jax.ShapeDtypeStruct((B, S, FOX_W), BF16),
        compiler_params=pltpu.CompilerParams(
            dimension_semantics=("arbitrary", "arbitrary"), vmem_limit_bytes=VMEM_LIMIT),
        name="fox",
    )(qk, qk, kx, vt)


def _dil_kernel(d0_ref, d1_ref, d2_ref, o_ref, acc_scr, m_scr, l_scr, nat_scr):
    S = d0_ref.shape[1]
    nh = DIL_HEADS_PER_GROUP
    blk = DIL_BLK
    n_blocks = S // blk
    lane_head = lax.broadcasted_iota(jnp.int32, (1, GROUP_W), 1) // HEAD_DIM
    a2 = lax.broadcasted_iota(jnp.int32, (nh * blk, 2 * blk), 0) % blk
    k2 = lax.broadcasted_iota(jnp.int32, (nh * blk, 2 * blk), 1)
    bias_two = jnp.where((k2 >= a2) & (k2 <= a2 + blk), 0.0, NEG_INF)
    a1 =lax.broadcasted_iota(jnp.int32, (nh * blk, blk), 0) % blk
    k1 = lax.broadcasted_iota(jnp.int32, (nh * blk, blk), 1)
    bias_one = jnp.where(k1 <= a1, 0.0, NEG_INF)

    def assemble(x):
        out = x[(nh - 1) * blk:nh * blk]
        for h in range(nh - 2, -1, -1):
            out = jnp.where(lane_head == h, x[h * blk:(h + 1) * blk], out)
        return out

    def logits(load_q, load_k, bias):
        qs = load_q()
        zero = jnp.zeros_like(qs)
        qstack = jnp.concatenate(
            [jnp.where(lane_head == h, qs, zero) for h in range(nh)], axis=0)
        s = lax.dot_general(qstack, load_k(), _NT, preferred_element_type=F32) + bias
        return s, jnp.max(s, axis=1, keepdims=True)

    def weigh(s, m, load_v):
        p = jnp.exp2(s - m)
        l = jnp.sum(p, axis=1, keepdims=True)
        pv = jnp.dot(p.astype(BF16), load_v(), preferred_element_type=F32)
        return (assemble(pv), assemble(jnp.broadcast_to(m, pv.shape)),
                assemble(jnp.broadcast_to(l, pv.shape)))

    def put(res, pieces, first):
        acc_b, m_b, l_b = res
        for rows, idx in pieces:
            for c in range(GROUP_W // 128):
                lanes = slice(c * 128, (c + 1) * 128)
                if first:
                    acc_scr[c, idx, :] = acc_b[rows, lanes]
                    m_scr[c, idx, :] = m_b[rows, lanes]
                    l_scr[c, idx, :] = l_b[rows, lanes]
                else:
                    m_o = m_scr[c, idx, :]
                    m_n = jnp.maximum(m_o, m_b[rows, lanes])
                    e_o = jnp.exp2(m_o - m_n)
                    e_b = jnp.exp2(m_b[rows, lanes] - m_n)
                    acc_scr[c, idx, :] = acc_scr[c, idx, :] * e_o + acc_b[rows, lanes] * e_b
                    l_scr[c, idx, :] = l_scr[c, idx, :] * e_o + l_b[rows, lanes] * e_b
                    m_scr[c, idx, :] = m_n

    q_l, k_l, v_l = (slice(i * GROUP_W, (i + 1) * GROUP_W) for i in range(3))
    whole = slice(0, blk)
    n_res = 4
    slab = S // n_res
    per_slab = slab // blk

    items = []
    rows_of = lambda ref, r0, n, lanes: (lambda: ref[0, pl.ds(r0, n), lanes])

    for res in range(n_res):
        for pos in range(per_slab):
            r0 = res * slab + pos * blk
            k0, nk, bias = (r0, blk, bias_one) if pos == 0 else (r0 - blk, 2 * blk, bias_two)
            items.append((rows_of(d1_ref, r0, blk, q_l), rows_of(d1_ref, k0, nk, k_l), bias,
                          rows_of(d1_ref, k0, nk, v_l), [(whole, pl.ds(r0, blk))], True))

    sub = blk // n_res
    qa = lax.broadcasted_iota(jnp.int32, (nh * blk, 2 * blk), 0) % blk
    kk = lax.broadcasted_iota(jnp.int32, (nh * blk, 2 * blk), 1)
    tq = blk + n_res * (qa % sub) + qa // sub
    tk = n_res * (kk % (2 * sub)) + kk // (2 * sub)
    bias0_two = jnp.where((tk <= tq) & (tq - tk <= blk), 0.0, NEG_INF)
    qa1 = lax.broadcasted_iota(jnp.int32, (nh * blk, blk), 0) % blk
    kk1 = lax.broadcasted_iota(jnp.int32, (nh * blk, blk), 1)
    bias0_one = jnp.where(n_res * (kk1 % sub) + kk1 // sub <= n_res * (qa1 % sub) + qa1 // sub,
                          0.0, NEG_INF)

    def slabs_of(ref, start, n, lanes):
        return lambda: jnp.concatenate(
            [ref[0, pl.ds(r * slab + start, n), lanes] for r in range(n_res)], axis=0)

    for n in range(n_blocks):
        k0, nk, bias = (0, sub, bias0_one) if n == 0 else ((n - 1) * sub, 2 * sub, bias0_two)
        dst = [(slice(r * sub, (r + 1) * sub), pl.ds(r * slab + n * sub, sub)) for r in range(n_res)]
        items.append((slabs_of(d0_ref, n * sub, sub, q_l), slabs_of(d0_ref, k0, nk, k_l), bias,
                      slabs_of(d0_ref, k0, nk, v_l), dst, False))

    for r16 in range(n_blocks):
        r0 = r16 * blk
        dst = [(whole, pl.ds((r16 % n_res) * slab + r16 // n_res, blk, stride=n_res))]
        items.append((rows_of(d2_ref, r0, blk, q_l), rows_of(d2_ref, r0, blk, k_l), bias_one,
                      rows_of(d2_ref, r0, blk, v_l), dst, False))

    nxt = logits(*items[0][0:3])
    for i, (_, _, _, load_v, dst, first_write) in enumerate(items):
        cur = nxt
        if i + 1 < len(items):
            nxt = logits(*items[i + 1][0:3])
        put(weigh(*cur, load_v), dst, first_write)

    def finish(res, _):
        for pos in range(per_slab):
            r0 = _aligned(res * slab + pos * blk, blk)
            for c in range(GROUP_W // 128):
                nat_scr[c, pl.ds(n_res * pos * blk + res, blk, stride=n_res), :] = (
                    acc_scr[c, pl.ds(r0, blk), :] / l_scr[c, pl.ds(r0, blk), :])
        return 0

    for res in range(n_res):
        finish(res, 0)

    def emit(i, _):
        r0 = _aligned(i * 2 * blk, 2 * blk)
        for c in range(GROUP_W // 128):
            o_ref[0, pl.ds(r0, 2 * blk), c * 128:(c + 1) * 128] = nat_scr[c, pl.ds(r0, 2 * blk), :].astype(BF16)
        return 0

    for i in range(n_blocks // 2):
        emit(i, 0)


def _dilated(d0, d1, d2):
    B, S, W = d0.shape
    spec = pl.BlockSpec((1, S, W), lambda b: (b, 0, 0))
    return pl.pallas_call(
        _dil_kernel,
        grid=(B,),
        in_specs=[spec, spec, spec],
        out_specs=pl.BlockSpec((1, S, GROUP_W), lambda b: (b, 0, 0)),
        out_shape=jax.ShapeDtypeStruct((B, S, GROUP_W), BF16),
        scratch_shapes=[pltpu.VMEM((GROUP_W // 128, S, 128), F32)] * 4,
        compiler_params=pltpu.CompilerParams(
            dimension_semantics=("arbitrary",), vmem_limit_bytes=VMEM_LIMIT),
        name="dilated",
    )(d0, d1, d2)


def _outmlp_kernel(x_ref, oa_ref, ob_ref, gate_ref, wa_ref, wb_ref, wo_ref, g2_ref,
                   wu_ref, wdn_ref, g3_ref, o_ref, *, final_norm):
    d = x_ref.shape[1]
    ya = jnp.dot(oa_ref[...], wa_ref[...], preferred_element_type=F32)
    yb = jnp.dot(ob_ref[...], wb_ref[...], preferred_element_type=F32)
    mixed = gate_ref[:, 0:d].astype(F32) * ya + gate_ref[:, d:2 * d].astype(F32) * yb
    x1 = x_ref[...] + jnp.dot(mixed.astype(BF16), wo_ref[...], preferred_element_type=F32)
    h = (x1 * g2_ref[...]).astype(BF16)
    inv = lax.rsqrt(jnp.mean(x1 * x1, axis=-1, keepdims=True) + EPS)
    d_ff = wu_ref.shape[1]
    mlp = None
    for c in range(d_ff // FF_CHUNK):
        u = jnp.dot(h, wu_ref[:, c * FF_CHUNK:(c + 1) * FF_CHUNK], preferred_element_type=F32)
        a = jnp.square(jnp.maximum(u, 0.0)).astype(BF16)
        t = jnp.dot(a, wdn_ref[c * FF_CHUNK:(c + 1) * FF_CHUNK, :], preferred_element_type=F32)
        mlp = t if mlp is None else mlp + t
    x2 = x1 + (inv * inv) * mlp
    o_ref[...] = _rms(x2, g3_ref[...]) if final_norm else x2


def _outmlp(x, oa, ob, gates, w_a, w_b, w_o, g2, w_u, w_dn, g3, final_norm):
    N, D = x.shape
    tm = TM_OUT
    tok = lambda w: pl.BlockSpec((tm, w), lambda i: (i, 0))
    return pl.pallas_call(
        functools.partial(_outmlp_kernel, final_norm=final_norm),
        grid=(N // tm,),
        in_specs=[tok(D), tok(oa.shape[1]), tok(ob.shape[1]), tok(gates.shape[1]),
                  _resident(w_a.shape), _resident(w_b.shape), _resident(w_o.shape),
                  _resident(g2.shape), _resident(w_u.shape), _resident(w_dn.shape),
                  _resident(g3.shape)],
        out_specs=tok(D),
        out_shape=jax.ShapeDtypeStruct((N, D), F32),
        compiler_params=pltpu.CompilerParams(
            dimension_semantics=("arbitrary",), vmem_limit_bytes=VMEM_LIMIT),
        name="outmlp",
    )(x, oa, ob, gates, w_a, w_b, w_o, g2, w_u, w_dn, g3)


def _rope_tables(S):
    inv_freq = jnp.power(jnp.float32(ROPE_THETA),
                         -jnp.arange(ROPE_HALF, dtype=F32) * 2.0 / ROPE_DIM)
    ang = jnp.arange(S).astype(F32)[:, None] * inv_freq[None, :]
    cos, sin = jnp.cos(ang), jnp.sin(ang)
    rest = HEAD_DIM - ROPE_DIM
    one = jnp.ones((S, rest), F32)
    zero = lambda w: jnp.zeros((S, w), F32)
    cos_h = jnp.concatenate([cos, cos, one], axis=1)
    slo_h = jnp.concatenate([-sin, zero(ROPE_HALF + rest)], axis=1)
    shi_h = jnp.concatenate([zero(ROPE_HALF), sin, zero(rest)], axis=1)
    rep = lambda t: jnp.tile(t, (1, 128 // HEAD_DIM))
    return rep(cos_h), rep(slo_h), rep(shi_h)


def kernel(x, norm_attn_g, w_in, b_forget, w_branch_a, w_branch_b, w_out, norm_mlp_g, w_up,
           w_down, norm_final_g):
    B, S, D = x.shape
    depth = w_in.shape[0]
    assert S % (16 * DIL_BLK) == 0 and S // 16 == DIL_BLK, "sequence length fixed by the dilation groups"
    cos, slo, shi = _rope_tables(S)
    sizes = (FOX_W, FOX_W, FOX_W, FOX_HEADS, DIL_W, DIL_W, DIL_W, D, D)
    offs = np.concatenate([[0], np.cumsum(sizes)])
    for l in range(depth):
        wt = jnp.swapaxes(w_in[l], 0, 1)
        w_qk = wt[:offs[2]].astype(BF16)
        mid = offs[4] + 3 * DIL_W
        w_t = jnp.concatenate([wt[offs[4]:mid], wt[mid:]], axis=0).astype(BF16)
        pad = -(3 * FOX_HEADS) % 16
        w_vf = jnp.concatenate(
            [wt[offs[2]:offs[3]], jnp.repeat(wt[offs[3]:offs[4]], 3, axis=0),
             jnp.zeros((pad, D), wt.dtype)], axis=0).astype(BF16)
        b_f = jnp.pad(jnp.repeat(b_forget[l].astype(F32), 3), (0, pad))[:, None]

        qk, vt, kx, d0, d1, d2, gates = _inproj(
            x, norm_attn_g[l][None, :], w_qk, w_vf, w_t, b_f, cos, slo, shi)
        oa = _fox(qk, vt, kx)
        ob = _dilated(d0.reshape(B, S, -1), d1.reshape(B, S, -1), d2.reshape(B, S, -1))
        y = _outmlp(
            x.reshape(B * S, D), oa.reshape(B * S, -1), ob.reshape(B * S, -1),
            gates.reshape(B * S, -1),
            w_branch_a[l].astype(BF16), w_branch_b[l].astype(BF16), w_out[l].astype(BF16),
            norm_mlp_g[l][None, :], w_up[l].astype(BF16), w_down[l].astype(BF16),
            norm_final_g[None, :], final_norm=(l == depth - 1))
        x = y.reshape(B, S, D)
    return x
```

```python
import functools

import numpy as np
import jax
import jax.numpy as jnp
from jax import lax
from jax.experimental import pallas as pl
from jax.experimental.pallas import tpu as pltpu

F32 = jnp.float32
BF16 = jnp.bfloat16

HEAD_DIM = 64
FOX_HEADS = 8
FOX_W = FOX_HEADS * HEAD_DIM
DIL_GROUPS = ((128, 1), (512, 4), (2048, 16))
DIL_HEADS_PER_GROUP = 4
GROUP_W = DIL_HEADS_PER_GROUP * HEAD_DIM
N_GROUPS = len(DIL_GROUPS)
DIL_W = N_GROUPS * GROUP_W
ROPE_THETA = 500000.0
ROPE_DIM = HEAD_DIM // 4
ROPE_HALF = ROPE_DIM // 2
EPS = 1e-6
NEG_INF = -1e30
Q_SCALE = HEAD_DIM ** -0.5
LOG2E = 1.4426950408889634

LANES = 128
DIL_BLK = 128
TM_IN = 512
TM_OUT = 512
TQ = 256
TK = 256
FOX_PAIRS_PER_STEP = 2
FF_CHUNK = 1024
VMEM_LIMIT = 56 * 1024 * 1024

_NT = (((1,), (1,)), ((), ()))


def _resident(shape):
    nd = len(shape)
    return pl.BlockSpec(shape, lambda *_: (0,) * nd, pipeline_mode=pl.Buffered(1))


def _aligned(x, m):
    return x if isinstance(x, int) else pl.multiple_of(x, m)


def _rms(x, g):
    ms = jnp.mean(x * x, axis=-1, keepdims=True)
    return x * lax.rsqrt(ms + EPS) * g


def _rope(x, cos, sin_lo, sin_hi):
    w = x.shape[1]
    up = pltpu.roll(x, w - ROPE_HALF, 1)
    dn = pltpu.roll(x, ROPE_HALF, 1)
    return x * cos + up * sin_lo + dn * sin_hi


def _split3(x):
    hi = x.astype(BF16)
    r1 = x - hi.astype(F32)
    mid = r1.astype(BF16)
    lo = (r1 - mid.astype(F32)).astype(BF16)
    return hi, mid, lo


def _inproj_kernel(x_ref, g_ref, wqk_ref, wvf_ref, bf_ref,
                   cos_ref, slo_ref, shi_ref,
                   qk_ref, vt_ref, kx_ref, d0_ref, d1_ref, d2_ref, gate_ref,
                   perm_scr, carry_scr, wt_ref):
    tm = x_ref.shape[1]
    d = x_ref.shape[2]

    @pl.when((pl.program_id(0) == 0) & (pl.program_id(1) == 0))
    def _realign_tail_weights():
        src0 = 3 * FOX_W
        chunk = 256
        n_chunks = wt_ref.shape[0] // chunk

        def move(j, _):
            src = pl.multiple_of(src0 + j * chunk, 16)
            rows = wqk_ref[pl.ds(src, chunk + 16), :].astype(F32)
            dst = pl.multiple_of(j * chunk, 16)
            wt_ref[pl.ds(dst, chunk), :] = rows[FOX_HEADS:FOX_HEADS + chunk].astype(BF16)
            return 0

        lax.fori_loop(0, n_chunks - 1, move, 0)
        src = src0 + (n_chunks - 1) * chunk
        rows = jnp.concatenate([wqk_ref[src:src + chunk, :].astype(F32),
                                wqk_ref[src + chunk:src + chunk + FOX_HEADS, :].astype(F32)], axis=0)
        wt_ref[(n_chunks - 1) * chunk:n_chunks * chunk, :] = (
            rows[FOX_HEADS:FOX_HEADS + chunk].astype(BF16))

    @pl.when(pl.program_id(1) == 0)
    def _reset():
        carry_scr[...] = jnp.zeros_like(carry_scr)

    xt = x_ref[0]
    h = (xt * g_ref[...]).astype(BF16)
    inv = lax.rsqrt(jnp.mean(xt * xt, axis=-1, keepdims=True) + EPS)
    inv_row = jnp.broadcast_to(inv, (tm, LANES)).T[0:1, :]
    proj = lambda ref, lo, hi: inv * lax.dot_general(h, ref[lo:hi, :], _NT,
                                                     preferred_element_type=F32)
    g_off = 3 * DIL_W
    lane = lax.broadcasted_iota(jnp.int32, (1, LANES), 1)

    n_f =wvf_ref.shape[0] - FOX_W

    def v_and_forget_logits():
        vf = lax.dot_general(wvf_ref[...], h, _NT, preferred_element_type=F32) * inv_row
        vt_ref[0] = vf[0:FOX_W].astype(BF16)
        z = vf[FOX_W:FOX_W + n_f] + bf_ref[...]
        row = lax.broadcasted_iota(jnp.int32, (n_f, 1), 0)
        log_f = jnp.where(row < 3 * FOX_HEADS,
                          -(jnp.maximum(-z, 0.0) + jnp.log1p(jnp.exp(-jnp.abs(z)))), 0.0)
        return _split3(log_f)

    def forget_cumsum(parts):
        row = lax.broadcasted_iota(jnp.int32, (tm, tm), 0)
        col = lax.broadcasted_iota(jnp.int32, (tm, tm), 1)
        tri = jnp.where(row <= col, 1.0, 0.0).astype(BF16)
        cum = carry_scr[:, 0:1]
        for part in parts:
            cum = cum + jnp.dot(part, tri, preferred_element_type=F32)
        carry_scr[...] = jnp.broadcast_to(cum[:, tm - 1:tm], carry_scr.shape)
        nf = jnp.concatenate([-LOG2E * cum, jnp.zeros((LANES - n_f, tm), F32)], axis=0).T
        hi, mid, lo = _split3(nf)
        kx_ref[0] = jnp.where(lane % 3 == 0, hi, jnp.where(lane % 3 == 1, mid, lo))

    def mixer_a_qk():
        qk_ref[0, :, 0:FOX_W] = (proj(wqk_ref, 0, FOX_W) * (Q_SCALE * LOG2E)).astype(BF16)
        qk_ref[0, :, FOX_W:2 * FOX_W] = proj(wqk_ref, FOX_W, 2 * FOX_W).astype(BF16)

    outs = (d0_ref, d1_ref, d2_ref)

    def mixer_b(g):
        dil = DIL_GROUPS[g][1]
        rows = pl.ds(pl.multiple_of(pl.program_id(1) * tm, tm), tm)
        cos, slo, shi = cos_ref[rows, :], slo_ref[rows, :], shi_ref[rows, :]
        part = lambda t: proj(wt_ref, t * DIL_W + g * GROUP_W, t * DIL_W + (g + 1) * GROUP_W)
        q, k, v = part(0), part(1), part(2)
        halves = []
        for c in range(GROUP_W // LANES):
            halves.append(_rope(q[:, c * LANES:(c + 1) * LANES], cos, slo, shi) * (Q_SCALE * LOG2E))
        for c in range(GROUP_W // LANES):
            halves.append(_rope(k[:, c * LANES:(c + 1) * LANES], cos, slo, shi))
        halves += [v[:, c * LANES:(c + 1) * LANES] for c in range(GROUP_W // LANES)]
        n_res = max(dil, 4)
        for c, t in enumerate(halves):
            perm_scr[g, c] = t
        for res in range(n_res):
            for c in range(len(halves)):
                outs[g][0, res, :, c * LANES:(c + 1) * LANES] = (
                    perm_scr[g, c, pl.ds(res, tm // n_res, stride=n_res), :].astype(BF16))

    def gates(c):
        r = proj(wt_ref, g_off + c * 512, g_off + (c + 1) * 512)
        gate_ref[0, :, c * 512:(c + 1) * 512] = jax.nn.sigmoid(r).astype(BF16)

    parts = v_and_forget_logits()
    mixer_b(0)
    gates(0)
    gates(1)
    forget_cumsum(parts)
    mixer_b(1)
    gates(2)
    gates(3)
    mixer_b(2)
    mixer_a_qk()


def _inproj(x, g, w_all, w_vf, b_f, cos, slo, shi):
    B, S, D = x.shape
    n_tail = 3 * DIL_W + 2 * D
    tm = TM_IN
    grid = (B, S // tm)
    tok = lambda w: pl.BlockSpec((1, tm, w), lambda b, i: (b, i, 0))
    tab = _resident(cos.shape)
    out_shape = (
        jax.ShapeDtypeStruct((B, S, 2 * FOX_W), BF16),
        jax.ShapeDtypeStruct((B, FOX_W, S), BF16),
        jax.ShapeDtypeStruct((B, S, LANES), BF16),
        jax.ShapeDtypeStruct((B, 4, S // 4, 3 * GROUP_W), BF16),
        jax.ShapeDtypeStruct((B, 4, S // 4, 3 * GROUP_W), BF16),
        jax.ShapeDtypeStruct((B, 16, S // 16, 3 * GROUP_W), BF16),
        jax.ShapeDtypeStruct((B, S, 2 * D), BF16),
    )
    out_specs = (
        tok(2 * FOX_W),
        pl.BlockSpec((1, FOX_W, tm), lambda b, i: (b, 0, i)),
        tok(LANES),
        pl.BlockSpec((1, 4, tm // 4, 3 * GROUP_W), lambda b, i: (b, 0, i, 0)),
        pl.BlockSpec((1, 4, tm // 4, 3 * GROUP_W), lambda b, i: (b, 0, i, 0)),
        pl.BlockSpec((1, 16, tm // 16, 3 * GROUP_W), lambda b, i: (b, 0, i, 0)),
        tok(2 * D),
    )
    in_specs = [
        tok(D), _resident(g.shape), _resident(w_all.shape), _resident(w_vf.shape),
        _resident(b_f.shape), tab, tab, tab,
    ]
    n_f = w_vf.shape[0] - FOX_W
    return pl.pallas_call(
        _inproj_kernel,
        grid=grid,
        in_specs=in_specs,
        out_specs=out_specs,
        out_shape=out_shape,
        scratch_shapes=[pltpu.VMEM((N_GROUPS, 3 * GROUP_W // LANES, tm, LANES), F32),
                        pltpu.VMEM((n_f, LANES), F32), pltpu.VMEM((n_tail, D), BF16)],
        compiler_params=pltpu.CompilerParams(
            dimension_semantics=("arbitrary", "arbitrary"), vmem_limit_bytes=VMEM_LIMIT),
        name="inproj",
    )(x, g, w_all, w_vf, b_f, cos, slo, shi)


def _fox_kernel(q_ref, k_ref, kx_ref, vt_ref, o_ref):
    S = q_ref.shape[1]
    pw = 2 * HEAD_DIM
    pairs = q_ref.shape[2] // pw
    step = pl.program_id(1)
    lane = lax.broadcasted_iota(jnp.int32, (1, pw), 1)
    first = lane < HEAD_DIM
    zeros = jnp.zeros((TQ, pw), BF16)

    def sel(head):
        return jnp.broadcast_to(
            jnp.where((lane >= 3 * head) & (lane < 3 * head + 3), 1.0, 0.0), (TQ, pw)).astype(BF16)

    hq = TQ // 2
    key = lax.broadcasted_iota(jnp.int32, (hq, TQ), 0)
    qry = lax.broadcasted_iota(jnp.int32, (hq, TQ), 1) % hq
    causal = key <= qry

    def colmax(parts):
        m = jnp.max(parts[0], axis=0, keepdims=True)
        for t in parts[1:]:
            m = jnp.maximum(m, jnp.max(t, axis=0, keepdims=True))
        return m

    def scores(pair, qi):
        lanes = slice(pair * pw, (pair + 1) * pw)
        head0 = 2 * (pairs * step + pair)
        q2 = q_ref[0, qi * TQ:(qi + 1) * TQ, lanes]
        qa = jnp.concatenate([jnp.where(first, q2, zeros), sel(head0)], axis=1)
        qb = jnp.concatenate([jnp.where(first, zeros, q2), sel(head0 + 1)], axis=1)
        qmix = jnp.concatenate([qa[0:hq], qb[0:hq], qa[hq:TQ], qb[hq:TQ]], axis=0)
        n_k = (qi + 1) * TK
        n_lo = n_k - hq
        kaug = jnp.concatenate([k_ref[0, 0:n_k, lanes], kx_ref[0, 0:n_k, :]], axis=1)
        s = lax.dot_general(kaug[0:n_lo], qmix, _NT, preferred_element_type=F32)
        s_hi = lax.dot_general(kaug[n_lo:n_k], qmix[TQ:2 * TQ], _NT, preferred_element_type=F32)
        full = [s[c * TK:(c + 1) * TK] for c in range(qi)]
        diag = s[qi * TK:n_lo]
        lo = [t[:, 0:TQ] for t in full] + [jnp.where(causal, diag[:, 0:TQ], NEG_INF)]
        hi = ([t[:, TQ:2 * TQ] for t in full] + [diag[:, TQ:2 * TQ]]
              + [jnp.where(causal, s_hi, NEG_INF)])
        return (lo, colmax(lo)), (hi, colmax(hi))

    def outputs(pair, qi, lo, hi):
        for half, (chunks, m) in enumerate((lo, hi)):
            n = (qi + 1) * TK - (hq if half == 0 else 0)
            p = [jnp.exp2(s - m) for s in chunks]
            l = jnp.sum(p[0], axis=0, keepdims=True)
            for t in p[1:]:
                l = l + jnp.sum(t, axis=0, keepdims=True)
            pt = jnp.concatenate([t.astype(BF16) for t in p], axis=0)
            acc = jnp.dot(vt_ref[0, pair * pw:(pair + 1) * pw, 0:n], pt,
                          preferred_element_type=F32)
            ot = (acc / l).T
            r0 = qi * TQ + half * hq
            o_ref[0, r0:r0 + hq, pair * pw:(pair + 1) * pw] = (
                jnp.where(first, ot[0:hq], ot[hq:TQ]).astype(BF16))

    items = [(pair, qi) for pair in range(pairs) for qi in range(S // TQ)]
    nxt = scores(*items[0])
    for i, item in enumerate(items):
        cur = nxt
        if i + 1 < len(items):
            nxt = scores(*items[i + 1])
        outputs(*item, *cur)


def _fox(qk, vt, kx):
    B, S, _ = qk.shape
    pairs = FOX_PAIRS_PER_STEP
    steps = FOX_HEADS // (2 * pairs)
    w = pairs * 2 * HEAD_DIM
    spec = lambda off: pl.BlockSpec((1, S, w), lambda b, p: (b, 0, off + p))
    return pl.pallas_call(
        _fox_kernel,
        grid=(B, steps),
        in_specs=[spec(0), spec(steps),
                  pl.BlockSpec((1, S, LANES), lambda b, p: (b, 0, 0)),
                  pl.BlockSpec((1, w, S), lambda b, p: (b, p, 0))],
        out_specs=pl.BlockSpec((1, S, w), lambda b, p: (b, 0, p)),
        out_shape=jax.ShapeDtypeStruct((B, S, FOX_W), BF16),
        compiler_params=pltpu.CompilerParams(
            dimension_semantics=("arbitrary", "arbitrary"), vmem_limit_bytes=VMEM_LIMIT),
        name="fox",
    )(qk, qk, kx, vt)


def _dil_kernel(d0_ref, d1_ref, d2_ref, o_ref, acc_scr, m_scr, l_scr, nat_scr):
    S = d0_ref.shape[1]
    nh = DIL_HEADS_PER_GROUP
    blk = DIL_BLK
    n_blocks = S // blk
    lane_head = lax.broadcasted_iota(jnp.int32, (1, GROUP_W), 1) // HEAD_DIM
    a2 = lax.broadcasted_iota(jnp.int32, (nh * blk, 2 * blk), 0) % blk
    k2 = lax.broadcasted_iota(jnp.int32, (nh * blk, 2 * blk), 1)
    bias_two = jnp.where((k2 >= a2) & (k2 <= a2 + blk), 0.0, NEG_INF)
    a1 =lax.broadcasted_iota(jnp.int32, (nh * blk, blk), 0) % blk
    k1 = lax.broadcasted_iota(jnp.int32, (nh * blk, blk), 1)
    bias_one = jnp.where(k1 <= a1, 0.0, NEG_INF)

    def assemble(x):
        out = x[(nh - 1) * blk:nh * blk]
        for h in range(nh - 2, -1, -1):
            out = jnp.where(lane_head == h, x[h * blk:(h + 1) * blk], out)
        return out

    def logits(load_q, load_k, bias):
        qs = load_q()
        zero = jnp.zeros_like(qs)
        qstack = jnp.concatenate(
            [jnp.where(lane_head == h, qs, zero) for h in range(nh)], axis=0)
        s = lax.dot_general(qstack, load_k(), _NT, preferred_element_type=F32) + bias
        return s, jnp.max(s, axis=1, keepdims=True)

    def weigh(s, m, load_v):
        p = jnp.exp2(s - m)
        l = jnp.sum(p, axis=1, keepdims=True)
        pv = jnp.dot(p.astype(BF16), load_v(), preferred_element_type=F32)
        return (assemble(pv), assemble(jnp.broadcast_to(m, pv.shape)),
                assemble(jnp.broadcast_to(l, pv.shape)))

    def put(res, pieces, first):
        acc_b, m_b, l_b = res
        for rows, idx in pieces:
            for c in range(GROUP_W // LANES):
                lanes = slice(c * LANES, (c + 1) * LANES)
                if first:
                    acc_scr[c, idx, :] = acc_b[rows, lanes]
                    m_scr[c, idx, :] = m_b[rows, lanes]
                    l_scr[c, idx, :] = l_b[rows, lanes]
                else:
                    m_o = m_scr[c, idx, :]
                    m_n = jnp.maximum(m_o, m_b[rows, lanes])
                    e_o = jnp.exp2(m_o - m_n)
                    e_b = jnp.exp2(m_b[rows, lanes] - m_n)
                    acc_scr[c, idx, :] = acc_scr[c, idx, :] * e_o + acc_b[rows, lanes] * e_b
                    l_scr[c, idx, :] = l_scr[c, idx, :] * e_o + l_b[rows, lanes] * e_b
                    m_scr[c, idx, :] = m_n

    q_l, k_l, v_l = (slice(i * GROUP_W, (i + 1) * GROUP_W) for i in range(3))
    whole = slice(0, blk)
    n_res = 4
    slab = S // n_res
    per_slab = slab // blk

    items = []
    rows_of = lambda ref, r0, n, lanes: (lambda: ref[0, pl.ds(r0, n), lanes])

    for res in range(n_res):
        for pos in range(per_slab):
            r0 = res * slab + pos * blk
            k0, nk, bias = (r0, blk, bias_one) if pos == 0 else (r0 - blk, 2 * blk, bias_two)
            items.append((rows_of(d1_ref, r0, blk, q_l), rows_of(d1_ref, k0, nk, k_l), bias,
                          rows_of(d1_ref, k0, nk, v_l), [(whole, pl.ds(r0, blk))], True))

    sub = blk // n_res
    qa = lax.broadcasted_iota(jnp.int32, (nh * blk, 2 * blk), 0) % blk
    kk = lax.broadcasted_iota(jnp.int32, (nh * blk, 2 * blk), 1)
    tq = blk + n_res * (qa % sub) + qa // sub
    tk = n_res * (kk % (2 * sub)) + kk // (2 * sub)
    bias0_two = jnp.where((tk <= tq) & (tq - tk <= blk), 0.0, NEG_INF)
    qa1 = lax.broadcasted_iota(jnp.int32, (nh * blk, blk), 0) % blk
    kk1 = lax.broadcasted_iota(jnp.int32, (nh * blk, blk), 1)
    bias0_one = jnp.where(n_res * (kk1 % sub) + kk1 // sub <= n_res * (qa1 % sub) + qa1 // sub,
                          0.0, NEG_INF)

    def slabs_of(ref, start, n, lanes):
        return lambda: jnp.concatenate(
            [ref[0, pl.ds(r * slab + start, n), lanes] for r in range(n_res)], axis=0)

    for n in range(n_blocks):
        k0, nk, bias = (0, sub, bias0_one) if n == 0 else ((n - 1) * sub, 2 * sub, bias0_two)
        dst = [(slice(r * sub, (r + 1) * sub), pl.ds(r * slab + n * sub, sub)) for r in range(n_res)]
        items.append((slabs_of(d0_ref, n * sub, sub, q_l), slabs_of(d0_ref, k0, nk, k_l), bias,
                      slabs_of(d0_ref, k0, nk, v_l), dst, False))

    for r16 in range(n_blocks):
        r0 = r16 * blk
        dst = [(whole, pl.ds((r16 % n_res) * slab + r16 // n_res, blk, stride=n_res))]
        items.append((rows_of(d2_ref, r0, blk, q_l), rows_of(d2_ref, r0, blk, k_l), bias_one,
                      rows_of(d2_ref, r0, blk, v_l), dst, False))

    nxt = logits(*items[0][0:3])
    for i, (_, _, _, load_v, dst, first_write) in enumerate(items):
        cur = nxt
        if i + 1 < len(items):
            nxt = logits(*items[i + 1][0:3])
        put(weigh(*cur, load_v), dst, first_write)

    def finish(res, _):
        for pos in range(per_slab):
            r0 = _aligned(res * slab + pos * blk, blk)
            for c in range(GROUP_W // LANES):
                nat_scr[c, pl.ds(n_res * pos * blk + res, blk, stride=n_res), :] = (
                    acc_scr[c, pl.ds(r0, blk), :] / l_scr[c, pl.ds(r0, blk), :])
        return 0

    for res in range(n_res):
        finish(res, 0)

    def emit(i, _):
        r0 = _aligned(i * 2 * blk, 2 * blk)
        for c in range(GROUP_W // LANES):
            o_ref[0, pl.ds(r0, 2 * blk), c * LANES:(c + 1) * LANES] = nat_scr[c, pl.ds(r0, 2 * blk), :].astype(BF16)
        return 0

    for i in range(n_blocks // 2):
        emit(i, 0)


def _dilated(d0, d1, d2):
    B, S, W = d0.shape
    spec = pl.BlockSpec((1, S, W), lambda b: (b, 0, 0))
    return pl.pallas_call(
        _dil_kernel,
        grid=(B,),
        in_specs=[spec, spec, spec],
        out_specs=pl.BlockSpec((1, S, GROUP_W), lambda b: (b, 0, 0)),
        out_shape=jax.ShapeDtypeStruct((B, S, GROUP_W), BF16),
        scratch_shapes=[pltpu.VMEM((GROUP_W // LANES, S, LANES), F32)] * 4,
        compiler_params=pltpu.CompilerParams(
            dimension_semantics=("arbitrary",), vmem_limit_bytes=VMEM_LIMIT),
        name="dilated",
    )(d0, d1, d2)


def _outmlp_kernel(x_ref, oa_ref, ob_ref, gate_ref, wa_ref, wb_ref, wo_ref, g2_ref,
                   wu_ref, wdn_ref, g3_ref, o_ref, *, final_norm):
    d = x_ref.shape[1]
    ya = jnp.dot(oa_ref[...], wa_ref[...], preferred_element_type=F32)
    yb = jnp.dot(ob_ref[...], wb_ref[...], preferred_element_type=F32)
    mixed = gate_ref[:, 0:d].astype(F32) * ya + gate_ref[:, d:2 * d].astype(F32) * yb
    x1 = x_ref[...] + jnp.dot(mixed.astype(BF16), wo_ref[...], preferred_element_type=F32)
    h = (x1 * g2_ref[...]).astype(BF16)
    inv = lax.rsqrt(jnp.mean(x1 * x1, axis=-1, keepdims=True) + EPS)
    d_ff = wu_ref.shape[1]
    mlp = None
    for c in range(d_ff // FF_CHUNK):
        u = jnp.dot(h, wu_ref[:, c * FF_CHUNK:(c + 1) * FF_CHUNK], preferred_element_type=F32)
        a = jnp.square(jnp.maximum(u, 0.0)).astype(BF16)
        t = jnp.dot(a, wdn_ref[c * FF_CHUNK:(c + 1) * FF_CHUNK, :], preferred_element_type=F32)
        mlp = t if mlp is None else mlp + t
    x2 = x1 + (inv * inv) * mlp
    o_ref[...] = _rms(x2, g3_ref[...]) if final_norm else x2


def _outmlp(x, oa, ob, gates, w_a, w_b, w_o, g2, w_u, w_dn, g3, final_norm):
    N, D = x.shape
    tm = TM_OUT
    tok = lambda w: pl.BlockSpec((tm, w), lambda i: (i, 0))
    return pl.pallas_call(
        functools.partial(_outmlp_kernel, final_norm=final_norm),
        grid=(N // tm,),
        in_specs=[tok(D), tok(oa.shape[1]), tok(ob.shape[1]), tok(gates.shape[1]),
                  _resident(w_a.shape), _resident(w_b.shape), _resident(w_o.shape),
                  _resident(g2.shape), _resident(w_u.shape), _resident(w_dn.shape),
                  _resident(g3.shape)],
        out_specs=tok(D),
        out_shape=jax.ShapeDtypeStruct((N, D), F32),
        compiler_params=pltpu.CompilerParams(
            dimension_semantics=("arbitrary",), vmem_limit_bytes=VMEM_LIMIT),
        name="outmlp",
    )(x, oa, ob, gates, w_a, w_b, w_o, g2, w_u, w_dn, g3)


def _rope_tables(S):
    inv_freq = jnp.power(jnp.float32(ROPE_THETA),
                         -jnp.arange(ROPE_HALF, dtype=F32) * 2.0 / ROPE_DIM)
    ang = jnp.arange(S).astype(F32)[:, None] * inv_freq[None, :]
    cos, sin = jnp.cos(ang), jnp.sin(ang)
    rest = HEAD_DIM - ROPE_DIM
    one = jnp.ones((S, rest), F32)
    zero = lambda w: jnp.zeros((S, w), F32)
    cos_h = jnp.concatenate([cos, cos, one], axis=1)
    slo_h = jnp.concatenate([-sin, zero(ROPE_HALF + rest)], axis=1)
    shi_h = jnp.concatenate([zero(ROPE_HALF), sin, zero(rest)], axis=1)
    rep = lambda t: jnp.tile(t, (1, 128 // HEAD_DIM))
    return rep(cos_h), rep(slo_h), rep(shi_h)


def kernel(x, norm_attn_g, w_in, b_forget, w_branch_a, w_branch_b, w_out, norm_mlp_g, w_up,
           w_down, norm_final_g):
    B, S, D = x.shape
    depth = w_in.shape[0]
    assert S % (16 * DIL_BLK) == 0 and S // 16 == DIL_BLK, "sequence length fixed by the dilation groups"
    cos, slo, shi = _rope_tables(S)
    sizes = (FOX_W, FOX_W, FOX_W, FOX_HEADS, DIL_W, DIL_W, DIL_W, D, D)
    offs = np.concatenate([[0], np.cumsum(sizes)])
    for l in range(depth):
        wt = jnp.swapaxes(w_in[l], 0, 1)
        w_all = wt.astype(BF16)
        pad = -(3 * FOX_HEADS) % 16
        w_vf = jnp.concatenate(
            [wt[offs[2]:offs[3]], jnp.repeat(wt[offs[3]:offs[4]], 3, axis=0),
             jnp.zeros((pad, D), wt.dtype)], axis=0).astype(BF16)
        b_f = jnp.pad(jnp.repeat(b_forget[l].astype(F32), 3), (0, pad))[:, None]

        qk, vt, kx, d0, d1, d2, gates = _inproj(
            x, norm_attn_g[l][None, :], w_all, w_vf, b_f, cos, slo, shi)
        oa = _fox(qk, vt, kx)
        ob = _dilated(d0.reshape(B, S, -1), d1.reshape(B, S, -1), d2.reshape(B, S, -1))
        y = _outmlp(
            x.reshape(B * S, D), oa.reshape(B * S, -1), ob.reshape(B * S, -1),
            gates.reshape(B * S, -1),
            w_branch_a[l].astype(BF16), w_branch_b[l].astype(BF16), w_out[l].astype(BF16),
            norm_mlp_g[l][None, :], w_up[l].astype(BF16), w_down[l].astype(BF16),
            norm_final_g[None, :], final_norm=(l == depth - 1))
        x = y.reshape(B, S, D)
    return x
```

```python
import functools

import numpy as np
import jax
import jax.numpy as jnp
from jax import lax
from jax.experimental import pallas as pl
from jax.experimental.pallas import tpu as pltpu

F32 = jnp.float32
BF16 = jnp.bfloat16

HEAD_DIM = 64
FOX_HEADS = 8
FOX_W = FOX_HEADS * HEAD_DIM
DIL_GROUPS = ((128, 1), (512, 4), (2048, 16))
DIL_HEADS_PER_GROUP = 4
GROUP_W = DIL_HEADS_PER_GROUP * HEAD_DIM
N_GROUPS = len(DIL_GROUPS)
DIL_W = N_GROUPS * GROUP_W
ROPE_THETA = 500000.0
ROPE_DIM = HEAD_DIM // 4
ROPE_HALF = ROPE_DIM // 2
EPS = 1e-6
NEG_INF = -1e30
Q_SCALE = HEAD_DIM ** -0.5
LOG2E = 1.4426950408889634

LANES = 128
DIL_BLK = 128
TM_IN = 512
TM_OUT = 512
TQ = 256
TK = 256
FOX_PAIRS_PER_STEP = 2
FF_CHUNK = 1024
VMEM_LIMIT = 56 * 1024 * 1024

_NT = (((1,), (1,)), ((), ()))


def _resident(shape):
    nd = len(shape)
    return pl.BlockSpec(shape, lambda *_: (0,) * nd, pipeline_mode=pl.Buffered(1))


def _aligned(x, m):
    return x if isinstance(x, int) else pl.multiple_of(x, m)


def _rms(x, g):
    ms = jnp.mean(x * x, axis=-1, keepdims=True)
    return x * lax.rsqrt(ms + EPS) * g


def _rope(x, cos, sin_lo, sin_hi):
    w = x.shape[1]
    up = pltpu.roll(x, w - ROPE_HALF, 1)
    dn = pltpu.roll(x, ROPE_HALF, 1)
    return x * cos + up * sin_lo + dn * sin_hi


def _split3(x):
    hi = x.astype(BF16)
    r1 = x - hi.astype(F32)
    mid = r1.astype(BF16)
    lo = (r1 - mid.astype(F32)).astype(BF16)
    return hi, mid, lo


def _inproj_kernel(x_ref, g_ref, wqk_ref, wvf_ref, bf_ref,
                   cos_ref, slo_ref, shi_ref,
                   qk_ref, vt_ref, kx_ref, d0_ref, d1_ref, d2_ref, gate_ref,
                   perm_scr, carry_scr, wt_ref):
    tm = x_ref.shape[1]
    d = x_ref.shape[2]

    @pl.when((pl.program_id(0) == 0) & (pl.program_id(1) == 0))
    def _realign_tail_weights():
        src0 = 3 * FOX_W
        chunk = 256
        n_chunks = wt_ref.shape[0] // chunk

        def move(j, _):
            src = pl.multiple_of(src0 + j * chunk, 16)
            rows = wqk_ref[pl.ds(src, chunk + 16), :].astype(F32)
            dst = pl.multiple_of(j * chunk, 16)
            wt_ref[pl.ds(dst, chunk), :] = rows[FOX_HEADS:FOX_HEADS + chunk].astype(BF16)
            return 0

        lax.fori_loop(0, n_chunks - 1, move, 0)
        src = src0 + (n_chunks - 1) * chunk
        rows = jnp.concatenate([wqk_ref[src:src + chunk, :].astype(F32),
                                wqk_ref[src + chunk:src + chunk + FOX_HEADS, :].astype(F32)], axis=0)
        wt_ref[(n_chunks - 1) * chunk:n_chunks * chunk, :] = (
            rows[FOX_HEADS:FOX_HEADS + chunk].astype(BF16))

    @pl.when(pl.program_id(1) == 0)
    def _reset():
        carry_scr[...] = jnp.zeros_like(carry_scr)

    xt = x_ref[0]
    h = (xt * g_ref[...]).astype(BF16)
    inv = lax.rsqrt(jnp.mean(xt * xt, axis=-1, keepdims=True) + EPS)
    inv_row = jnp.broadcast_to(inv, (tm, LANES)).T[0:1, :]
    proj = lambda ref, lo, hi: inv * lax.dot_general(h, ref[lo:hi, :], _NT,
                                                     preferred_element_type=F32)
    g_off = 3 * DIL_W
    lane = lax.broadcasted_iota(jnp.int32, (1, LANES), 1)

    n_f =wvf_ref.shape[0] - FOX_W

    def v_and_forget_logits():
        vf = lax.dot_general(wvf_ref[...], h, _NT, preferred_element_type=F32) * inv_row
        vt_ref[0] = vf[0:FOX_W].astype(BF16)
        z = vf[FOX_W:FOX_W + n_f] + bf_ref[...]
        row = lax.broadcasted_iota(jnp.int32, (n_f, 1), 0)
        log_f = jnp.where(row < 3 * FOX_HEADS,
                          -(jnp.maximum(-z, 0.0) + jnp.log1p(jnp.exp(-jnp.abs(z)))), 0.0)
        return _split3(log_f)

    def forget_cumsum(parts):
        row = lax.broadcasted_iota(jnp.int32, (tm, tm), 0)
        col = lax.broadcasted_iota(jnp.int32, (tm, tm), 1)
        tri = jnp.where(row <= col, 1.0, 0.0).astype(BF16)
        cum = carry_scr[:, 0:1]
        for part in parts:
            cum = cum + jnp.dot(part, tri, preferred_element_type=F32)
        carry_scr[...] = jnp.broadcast_to(cum[:, tm - 1:tm], carry_scr.shape)
        nf = jnp.concatenate([-LOG2E * cum, jnp.zeros((LANES - n_f, tm), F32)], axis=0).T
        hi, mid, lo = _split3(nf)
        kx_ref[0] = jnp.where(lane % 3 == 0, hi, jnp.where(lane % 3 == 1, mid, lo))

    def mixer_a_qk():
        qk_ref[0, :, 0:FOX_W] = (proj(wqk_ref, 0, FOX_W) * (Q_SCALE * LOG2E)).astype(BF16)
        qk_ref[0, :, FOX_W:2 * FOX_W] = proj(wqk_ref, FOX_W, 2 * FOX_W).astype(BF16)

    outs = (d0_ref, d1_ref, d2_ref)

    def mixer_b(g):
        dil = DIL_GROUPS[g][1]
        rows = pl.ds(pl.multiple_of(pl.program_id(1) * tm, tm), tm)
        cos, slo, shi = cos_ref[rows, :], slo_ref[rows, :], shi_ref[rows, :]
        part = lambda t: proj(wt_ref, t * DIL_W + g * GROUP_W, t * DIL_W + (g + 1) * GROUP_W)
        q, k, v = part(0), part(1), part(2)
        halves = []
        for c in range(GROUP_W // LANES):
            halves.append(_rope(q[:, c * LANES:(c + 1) * LANES], cos, slo, shi) * (Q_SCALE * LOG2E))
        for c in range(GROUP_W // LANES):
            halves.append(_rope(k[:, c * LANES:(c + 1) * LANES], cos, slo, shi))
        halves += [v[:, c * LANES:(c + 1) * LANES] for c in range(GROUP_W // LANES)]
        n_res = max(dil, 4)
        for c, t in enumerate(halves):
            perm_scr[g, c] = t
        for res in range(n_res):
            for c in range(len(halves)):
                outs[g][0, res, :, c * LANES:(c + 1) * LANES] = (
                    perm_scr[g, c, pl.ds(res, tm // n_res, stride=n_res), :].astype(BF16))

    def gates(c):
        r = proj(wt_ref, g_off + c * 512, g_off + (c + 1) * 512)
        gate_ref[0, :, c * 512:(c + 1) * 512] = jax.nn.sigmoid(r).astype(BF16)

    parts = v_and_forget_logits()
    mixer_b(0)
    gates(0)
    gates(1)
    forget_cumsum(parts)
    mixer_b(1)
    gates(2)
    gates(3)
    mixer_b(2)
    mixer_a_qk()


def _inproj(x, g, w_all, w_vf, b_f, cos, slo, shi):
    B, S, D = x.shape
    n_tail = 3 * DIL_W + 2 * D
    tm = TM_IN
    grid = (B, S // tm)
    tok = lambda w: pl.BlockSpec((1, tm, w), lambda b, i: (b, i, 0))
    tab = _resident(cos.shape)
    out_shape = (
        jax.ShapeDtypeStruct((B, S, 2 * FOX_W), BF16),
        jax.ShapeDtypeStruct((B, FOX_W, S), BF16),
        jax.ShapeDtypeStruct((B, S, LANES), BF16),
        jax.ShapeDtypeStruct((B, 4, S // 4, 3 * GROUP_W), BF16),
        jax.ShapeDtypeStruct((B, 4, S // 4, 3 * GROUP_W), BF16),
        jax.ShapeDtypeStruct((B, 16, S // 16, 3 * GROUP_W), BF16),
        jax.ShapeDtypeStruct((B, S, 2 * D), BF16),
    )
    out_specs = (
        tok(2 * FOX_W),
        pl.BlockSpec((1, FOX_W, tm), lambda b, i: (b, 0, i)),
        tok(LANES),
        pl.BlockSpec((1, 4, tm // 4, 3 * GROUP_W), lambda b, i: (b, 0, i, 0)),
        pl.BlockSpec((1, 4, tm // 4, 3 * GROUP_W), lambda b, i: (b, 0, i, 0)),
        pl.BlockSpec((1, 16, tm // 16, 3 * GROUP_W), lambda b, i: (b, 0, i, 0)),
        tok(2 * D),
    )
    in_specs = [
        tok(D), _resident(g.shape), _resident(w_all.shape), _resident(w_vf.shape),
        _resident(b_f.shape), tab, tab, tab,
    ]
    n_f = w_vf.shape[0] - FOX_W
    return pl.pallas_call(
        _inproj_kernel,
        grid=grid,
        in_specs=in_specs,
        out_specs=out_specs,
        out_shape=out_shape,
        scratch_shapes=[pltpu.VMEM((N_GROUPS, 3 * GROUP_W // LANES, tm, LANES), F32),
                        pltpu.VMEM((n_f, LANES), F32), pltpu.VMEM((n_tail, D), BF16)],
        compiler_params=pltpu.CompilerParams(
            dimension_semantics=("arbitrary", "arbitrary"), vmem_limit_bytes=VMEM_LIMIT),
        name="inproj",
    )(x, g, w_all, w_vf, b_f, cos, slo, shi)


def _fox_kernel(q_ref, k_ref, kx_ref, vt_ref, o_ref):
    S = q_ref.shape[1]
    pw = 2 * HEAD_DIM
    pairs = q_ref.shape[2] // pw
    step = pl.program_id(1)
    lane = lax.broadcasted_iota(jnp.int32, (1, pw), 1)
    first = lane < HEAD_DIM
    zeros = jnp.zeros((TQ, pw), BF16)

    def sel(head):
        return jnp.broadcast_to(
            jnp.where((lane >= 3 * head) & (lane < 3 * head + 3), 1.0, 0.0), (TQ, pw)).astype(BF16)

    hq = TQ // 2
    key = lax.broadcasted_iota(jnp.int32, (hq, TQ), 0)
    qry = lax.broadcasted_iota(jnp.int32, (hq, TQ), 1) % hq
    causal = key <= qry

    def colmax(parts):
        m = jnp.max(parts[0], axis=0, keepdims=True)
        for t in parts[1:]:
            m = jnp.maximum(m, jnp.max(t, axis=0, keepdims=True))
        return m

    def scores(pair, qi):
        lanes = slice(pair * pw, (pair + 1) * pw)
        head0 = 2 * (pairs * step + pair)
        q2 = q_ref[0, qi * TQ:(qi + 1) * TQ, lanes]
        qa = jnp.concatenate([jnp.where(first, q2, zeros), sel(head0)], axis=1)
        qb = jnp.concatenate([jnp.where(first, zeros, q2), sel(head0 + 1)], axis=1)
        qmix = jnp.concatenate([qa[0:hq], qb[0:hq], qa[hq:TQ], qb[hq:TQ]], axis=0)
        n_k = (qi + 1) * TK
        n_lo = n_k - hq
        kaug = jnp.concatenate([k_ref[0, 0:n_k, lanes], kx_ref[0, 0:n_k, :]], axis=1)
        s = lax.dot_general(kaug[0:n_lo], qmix, _NT, preferred_element_type=F32)
        s_hi = lax.dot_general(kaug[n_lo:n_k], qmix[TQ:2 * TQ], _NT, preferred_element_type=F32)
        full = [s[c * TK:(c + 1) * TK] for c in range(qi)]
        diag = s[qi * TK:n_lo]
        lo = [t[:, 0:TQ] for t in full] + [jnp.where(causal, diag[:, 0:TQ], NEG_INF)]
        hi = ([t[:, TQ:2 * TQ] for t in full] + [diag[:, TQ:2 * TQ]]
              + [jnp.where(causal, s_hi, NEG_INF)])
        return (lo, colmax(lo)), (hi, colmax(hi))

    def outputs(pair, qi, lo, hi):
        for half, (chunks, m) in enumerate((lo, hi)):
            n = (qi + 1) * TK - (hq if half == 0 else 0)
            p = [jnp.exp2(s - m) for s in chunks]
            l = jnp.sum(p[0], axis=0, keepdims=True)
            for t in p[1:]:
                l = l + jnp.sum(t, axis=0, keepdims=True)
            pt = jnp.concatenate([t.astype(BF16) for t in p], axis=0)
            acc = jnp.dot(vt_ref[0, pair * pw:(pair + 1) * pw, 0:n], pt,
                          preferred_element_type=F32)
            ot = (acc / l).T
            r0 = qi * TQ + half * hq
            o_ref[0, r0:r0 + hq, pair * pw:(pair + 1) * pw] = (
                jnp.where(first, ot[0:hq], ot[hq:TQ]).astype(BF16))

    items = [(pair, qi) for pair in range(pairs) for qi in range(S // TQ)]
    nxt = scores(*items[0])
    for i, item in enumerate(items):
        cur = nxt
        if i + 1 < len(items):
            nxt = scores(*items[i + 1])
        outputs(*item, *cur)


def _fox(qk, vt, kx):
    B, S, _ = qk.shape
    pairs = FOX_PAIRS_PER_STEP
    steps = FOX_HEADS // (2 * pairs)
    w = pairs * 2 * HEAD_DIM
    spec = lambda off: pl.BlockSpec((1, S, w), lambda b, p: (b, 0, off + p))
    return pl.pallas_call(
        _fox_kernel,
        grid=(B, steps),
        in_specs=[spec(0), spec(steps),
                  pl.BlockSpec((1, S, LANES), lambda b, p: (b, 0, 0)),
                  pl.BlockSpec((1, w, S), lambda b, p: (b, p, 0))],
        out_specs=pl.BlockSpec((1, S, w), lambda b, p: (b, 0, p)),
        out_shape=jax.ShapeDtypeStruct((B, S, FOX_W), BF16),
        compiler_params=pltpu.CompilerParams(
            dimension_semantics=("arbitrary", "arbitrary"), vmem_limit_bytes=VMEM_LIMIT),
        name="fox",
    )(qk, qk, kx, vt)


def _dil_kernel(d0_ref, d1_ref, d2_ref, o_ref, acc_scr, m_scr, l_scr, nat_scr):
    S = d0_ref.shape[1]
    nh = DIL_HEADS_PER_GROUP
    blk = DIL_BLK
    n_blocks = S // blk
    lane_head = lax.broadcasted_iota(jnp.int32, (1, GROUP_W), 1) // HEAD_DIM
    a2 = lax.broadcasted_iota(jnp.int32, (nh * blk, 2 * blk), 0) % blk
    k2 = lax.broadcasted_iota(jnp.int32, (nh * blk, 2 * blk), 1)
    bias_two = jnp.where((k2 >= a2) & (k2 <= a2 + blk), 0.0, NEG_INF)
    a1 =lax.broadcasted_iota(jnp.int32, (nh * blk, blk), 0) % blk
    k1 = lax.broadcasted_iota(jnp.int32, (nh * blk, blk), 1)
    bias_one = jnp.where(k1 <= a1, 0.0, NEG_INF)

    def assemble(x):
        out = x[(nh - 1) * blk:nh * blk]
        for h in range(nh - 2, -1, -1):
            out = jnp.where(lane_head == h, x[h * blk:(h + 1) * blk], out)
        return out

    def logits(load_q, load_k, bias):
        qs = load_q()
        zero = jnp.zeros_like(qs)
        qstack = jnp.concatenate(
            [jnp.where(lane_head == h, qs, zero) for h in range(nh)], axis=0)
        s = lax.dot_general(qstack, load_k(), _NT, preferred_element_type=F32) + bias
        return s, jnp.max(s, axis=1, keepdims=True)

    def weigh(s, m, load_v):
        p = jnp.exp2(s - m)
        l = jnp.sum(p, axis=1, keepdims=True)
        pv = jnp.dot(p.astype(BF16), load_v(), preferred_element_type=F32)
        return (assemble(pv), assemble(jnp.broadcast_to(m, pv.shape)),
                assemble(jnp.broadcast_to(l, pv.shape)))

    def put(res, pieces, first):
        acc_b, m_b, l_b = res
        for rows, idx in pieces:
            for c in range(GROUP_W // LANES):
                lanes = slice(c * LANES, (c + 1) * LANES)
                if first:
                    acc_scr[c, idx, :] = acc_b[rows, lanes]
                    m_scr[c, idx, :] = m_b[rows, lanes]
                    l_scr[c, idx, :] = l_b[rows, lanes]
                else:
                    m_o = m_scr[c, idx, :]
                    m_n = jnp.maximum(m_o, m_b[rows, lanes])
                    e_o = jnp.exp2(m_o - m_n)
                    e_b = jnp.exp2(m_b[rows, lanes] - m_n)
                    acc_scr[c, idx, :] = acc_scr[c, idx, :] * e_o + acc_b[rows, lanes] * e_b
                    l_scr[c, idx, :] = l_scr[c, idx, :] * e_o + l_b[rows, lanes] * e_b
                    m_scr[c, idx, :] = m_n

    q_l, k_l, v_l = (slice(i * GROUP_W, (i + 1) * GROUP_W) for i in range(3))
    whole = slice(0, blk)
    n_res = 4
    slab = S // n_res
    per_slab = slab // blk

    items = []
    rows_of = lambda ref, r0, n, lanes: (lambda: ref[0, pl.ds(r0, n), lanes])

    for res in range(n_res):
        for pos in range(per_slab):
            r0 = res * slab + pos * blk
            k0, nk, bias = (r0, blk, bias_one) if pos == 0 else (r0 - blk, 2 * blk, bias_two)
            items.append((rows_of(d1_ref, r0, blk, q_l), rows_of(d1_ref, k0, nk, k_l), bias,
                          rows_of(d1_ref, k0, nk, v_l), [(whole, pl.ds(r0, blk))], True))

    sub = blk // n_res
    qa = lax.broadcasted_iota(jnp.int32, (nh * blk, 2 * blk), 0) % blk
    kk = lax.broadcasted_iota(jnp.int32, (nh * blk, 2 * blk), 1)
    tq = blk + n_res * (qa % sub) + qa // sub
    tk = n_res * (kk % (2 * sub)) + kk // (2 * sub)
    bias0_two = jnp.where((tk <= tq) & (tq - tk <= blk), 0.0, NEG_INF)
    qa1 = lax.broadcasted_iota(jnp.int32, (nh * blk, blk), 0) % blk
    kk1 = lax.broadcasted_iota(jnp.int32, (nh * blk, blk), 1)
    bias0_one = jnp.where(n_res * (kk1 % sub) + kk1 // sub <= n_res * (qa1 % sub) + qa1 // sub,
                          0.0, NEG_INF)

    def slabs_of(ref, start, n, lanes):
        return lambda: jnp.concatenate(
            [ref[0, pl.ds(r * slab + start, n), lanes] for r in range(n_res)], axis=0)

    for n in range(n_blocks):
        k0, nk, bias = (0, sub, bias0_one) if n == 0 else ((n - 1) * sub, 2 * sub, bias0_two)
        dst = [(slice(r * sub, (r + 1) * sub), pl.ds(r * slab + n * sub, sub)) for r in range(n_res)]
        items.append((slabs_of(d0_ref, n * sub, sub, q_l), slabs_of(d0_ref, k0, nk, k_l), bias,
                      slabs_of(d0_ref, k0, nk, v_l), dst, False))

    for r16 in range(n_blocks):
        r0 = r16 * blk
        dst = [(whole, pl.ds((r16 % n_res) * slab + r16 // n_res, blk, stride=n_res))]
        items.append((rows_of(d2_ref, r0, blk, q_l), rows_of(d2_ref, r0, blk, k_l), bias_one,
                      rows_of(d2_ref, r0, blk, v_l), dst, False))

    nxt = logits(*items[0][0:3])
    for i, (_, _, _, load_v, dst, first_write) in enumerate(items):
        cur = nxt
        if i + 1 < len(items):
            nxt = logits(*items[i + 1][0:3])
        put(weigh(*cur, load_v), dst, first_write)

    def finish(res, _):
        for pos in range(per_slab):
            r0 = _aligned(res * slab + pos * blk, blk)
            for c in range(GROUP_W // LANES):
                nat_scr[c, pl.ds(n_res * pos * blk + res, blk, stride=n_res), :] = (
                    acc_scr[c, pl.ds(r0, blk), :] / l_scr[c, pl.ds(r0, blk), :])
        return 0

    for res in range(n_res):
        finish(res, 0)

    def emit(i, _):
        r0 = _aligned(i * 2 * blk, 2 * blk)
        for c in range(GROUP_W // LANES):
            o_ref[0, pl.ds(r0, 2 * blk), c * LANES:(c + 1) * LANES] = nat_scr[c, pl.ds(r0, 2 * blk), :].astype(BF16)
        return 0

    for i in range(n_blocks // 2):
        emit(i, 0)


def _dilated(d0, d1, d2):
    B, S, W = d0.shape
    spec = pl.BlockSpec((1, S, W), lambda b: (b, 0, 0))
    return pl.pallas_call(
        _dil_kernel,
        grid=(B,),
        in_specs=[spec, spec, spec],
        out_specs=pl.BlockSpec((1, S, GROUP_W), lambda b: (b, 0, 0)),
        out_shape=jax.ShapeDtypeStruct((B, S, GROUP_W), BF16),
        scratch_shapes=[pltpu.VMEM((GROUP_W // LANES, S, LANES), F32)] * 4,
        compiler_params=pltpu.CompilerParams(
            dimension_semantics=("arbitrary",), vmem_limit_bytes=VMEM_LIMIT),
        name="dilated",
    )(d0, d1, d2)


STAGE_ROWS = 512


def _stage_weights(pairs, stage, sem):
    cols = stage.shape[2]
    chunks = []
    for src, dst in pairs:
        for r0 in range(0, src.shape[0], STAGE_ROWS):
            n = min(STAGE_ROWS, src.shape[0] - r0)
            for c0 in range(0, src.shape[1], cols):
                chunks.append((src, dst, r0, n, c0))

    def copy(k):
        src, _, r0, n, c0 = chunks[k]
        return pltpu.make_async_copy(src.at[r0:r0 + n, c0:c0 + cols], stage.at[k % 2, 0:n],
                                     sem.at[k % 2])

    copy(0).start()
    for k, (_, dst, r0, n, c0) in enumerate(chunks):
        if k + 1 < len(chunks):
            copy(k + 1).start()
        copy(k).wait()
        dst[r0:r0 + n, c0:c0 + cols] = stage[k % 2, 0:n].astype(BF16)


def _outmlp_kernel(x_ref, oa_ref, ob_ref, gate_ref, wa_hbm, wb_hbm, wo_hbm, g2_ref,
                   wu_hbm, wdn_hbm, g3_ref, o_ref,
                   wa_ref, wb_ref, wo_ref, wu_ref, wdn_ref, stage, sem, *, final_norm):
    d = x_ref.shape[1]

    @pl.when(pl.program_id(0) == 0)
    def _load_weights():
        _stage_weights([(wa_hbm, wa_ref), (wb_hbm, wb_ref), (wo_hbm, wo_ref),
                        (wu_hbm, wu_ref), (wdn_hbm, wdn_ref)], stage, sem)

    ya = jnp.dot(oa_ref[...], wa_ref[...], preferred_element_type=F32)
    yb = jnp.dot(ob_ref[...], wb_ref[...], preferred_element_type=F32)
    mixed = gate_ref[:, 0:d].astype(F32) * ya + gate_ref[:, d:2 * d].astype(F32) * yb
    x1 = x_ref[...] + jnp.dot(mixed.astype(BF16), wo_ref[...], preferred_element_type=F32)
    h = (x1 * g2_ref[...]).astype(BF16)
    inv = lax.rsqrt(jnp.mean(x1 * x1, axis=-1, keepdims=True) + EPS)
    d_ff = wu_ref.shape[1]
    mlp = None
    for c in range(d_ff // FF_CHUNK):
        u = jnp.dot(h, wu_ref[:, c * FF_CHUNK:(c + 1) * FF_CHUNK], preferred_element_type=F32)
        a = jnp.square(jnp.maximum(u, 0.0)).astype(BF16)
        t = jnp.dot(a, wdn_ref[c * FF_CHUNK:(c + 1) * FF_CHUNK, :], preferred_element_type=F32)
        mlp = t if mlp is None else mlp + t
    x2 = x1 + (inv * inv) * mlp
    o_ref[...] = _rms(x2, g3_ref[...]) if final_norm else x2


def _outmlp(x, oa, ob, gates, w_a, w_b, w_o, g2, w_u, w_dn, g3, final_norm):
    N, D = x.shape
    tm = TM_OUT
    tok = lambda w: pl.BlockSpec((tm, w), lambda i: (i, 0))
    hbm = pl.BlockSpec(memory_space=pl.ANY)
    weights = (w_a, w_b, w_o, w_u, w_dn)
    return pl.pallas_call(
        functools.partial(_outmlp_kernel, final_norm=final_norm),
        grid=(N // tm,),
        in_specs=[tok(D), tok(oa.shape[1]), tok(ob.shape[1]), tok(gates.shape[1]),
                  hbm, hbm, hbm, _resident(g2.shape), hbm, hbm, _resident(g3.shape)],
        out_specs=tok(D),
        out_shape=jax.ShapeDtypeStruct((N, D), F32),
        scratch_shapes=[pltpu.VMEM(w.shape, BF16) for w in weights]
        + [pltpu.VMEM((2, STAGE_ROWS, D), F32), pltpu.SemaphoreType.DMA((2,))],
        compiler_params=pltpu.CompilerParams(
            dimension_semantics=("arbitrary",), vmem_limit_bytes=VMEM_LIMIT),
        name="outmlp",
    )(x, oa, ob, gates, w_a, w_b, w_o, g2, w_u, w_dn, g3)


def _rope_tables(S):
    inv_freq = jnp.power(jnp.float32(ROPE_THETA),
                         -jnp.arange(ROPE_HALF, dtype=F32) * 2.0 / ROPE_DIM)
    ang = jnp.arange(S).astype(F32)[:, None] * inv_freq[None, :]
    cos, sin = jnp.cos(ang), jnp.sin(ang)
    rest = HEAD_DIM - ROPE_DIM
    one = jnp.ones((S, rest), F32)
    zero = lambda w: jnp.zeros((S, w), F32)
    cos_h = jnp.concatenate([cos, cos, one], axis=1)
    slo_h = jnp.concatenate([-sin, zero(ROPE_HALF + rest)], axis=1)
    shi_h = jnp.concatenate([zero(ROPE_HALF), sin, zero(rest)], axis=1)
    rep = lambda t: jnp.tile(t, (1, 128 // HEAD_DIM))
    return rep(cos_h), rep(slo_h), rep(shi_h)


def kernel(x, norm_attn_g, w_in, b_forget, w_branch_a, w_branch_b, w_out, norm_mlp_g, w_up,
           w_down, norm_final_g):
    B, S, D = x.shape
    depth = w_in.shape[0]
    assert S % (16 * DIL_BLK) == 0 and S // 16 == DIL_BLK, "sequence length fixed by the dilation groups"
    cos, slo, shi = _rope_tables(S)
    sizes = (FOX_W, FOX_W, FOX_W, FOX_HEADS, DIL_W, DIL_W, DIL_W, D, D)
    offs = np.concatenate([[0], np.cumsum(sizes)])
    for l in range(depth):
        wt = jnp.swapaxes(w_in[l], 0, 1)
        w_all = wt.astype(BF16)
        pad = -(3 * FOX_HEADS) % 16
        w_vf = jnp.concatenate(
            [wt[offs[2]:offs[3]], jnp.repeat(wt[offs[3]:offs[4]], 3, axis=0),
             jnp.zeros((pad, D), wt.dtype)], axis=0).astype(BF16)
        b_f = jnp.pad(jnp.repeat(b_forget[l].astype(F32), 3), (0, pad))[:, None]

        qk, vt, kx, d0, d1, d2, gates = _inproj(
            x, norm_attn_g[l][None, :], w_all, w_vf, b_f, cos, slo, shi)
        oa = _fox(qk, vt, kx)
        ob = _dilated(d0.reshape(B, S, -1), d1.reshape(B, S, -1), d2.reshape(B, S, -1))
        y = _outmlp(
            x.reshape(B * S, D), oa.reshape(B * S, -1), ob.reshape(B * S, -1),
            gates.reshape(B * S, -1),
            w_branch_a[l], w_branch_b[l], w_out[l], norm_mlp_g[l][None, :], w_up[l], w_down[l],
            norm_final_g[None, :], final_norm=(l == depth - 1))
        x = y.reshape(B, S, D)
    return x
```

```python
import functools

import numpy as np
import jax
import jax.numpy as jnp
from jax import lax
from jax.experimental import pallas as pl
from jax.experimental.pallas import tpu as pltpu

F32 = jnp.float32
BF16 = jnp.bfloat16

HEAD_DIM = 64
FOX_HEADS = 8
FOX_W = FOX_HEADS * HEAD_DIM
DIL_GROUPS = ((128, 1), (512, 4), (2048, 16))
DIL_HEADS_PER_GROUP = 4
GROUP_W = DIL_HEADS_PER_GROUP * HEAD_DIM
N_GROUPS = len(DIL_GROUPS)
DIL_W = N_GROUPS * GROUP_W
ROPE_THETA = 500000.0
ROPE_DIM = HEAD_DIM // 4
ROPE_HALF = ROPE_DIM // 2
EPS = 1e-6
NEG_INF = -1e30
Q_SCALE = HEAD_DIM ** -0.5
LOG2E = 1.4426950408889634

LANES = 128
DIL_BLK = 128
TM_IN = 512
TM_OUT = 512
TQ = 256
TK = 256
FOX_PAIRS_PER_STEP = 2
FF_CHUNK = 1024
VMEM_LIMIT = 56 * 1024 * 1024

_NT = (((1,), (1,)), ((), ()))


def _resident(shape):
    nd = len(shape)
    return pl.BlockSpec(shape, lambda *_: (0,) * nd, pipeline_mode=pl.Buffered(1))


def _aligned(x, m):
    return x if isinstance(x, int) else pl.multiple_of(x, m)


def _rms(x, g):
    ms = jnp.mean(x * x, axis=-1, keepdims=True)
    return x * lax.rsqrt(ms + EPS) * g


def _rope(x, cos, sin_lo, sin_hi):
    w = x.shape[1]
    up = pltpu.roll(x, w - ROPE_HALF, 1)
    dn = pltpu.roll(x, ROPE_HALF, 1)
    return x * cos + up * sin_lo + dn * sin_hi


def _split3(x):
    hi = x.astype(BF16)
    r1 = x - hi.astype(F32)
    mid = r1.astype(BF16)
    lo = (r1 - mid.astype(F32)).astype(BF16)
    return hi, mid, lo


def _inproj_kernel(x_ref, g_ref, w_hbm, wvf_ref, bf_ref,
                   cos_ref, slo_ref, shi_ref,
                   qk_ref, vt_ref, kx_ref, d0_ref, d1_ref, d2_ref, gate_ref,
                   perm_scr, carry_scr, wqk_ref, wt_ref, stage, sem):
    tm = x_ref.shape[1]
    d = x_ref.shape[2]

    @pl.when((pl.program_id(0) == 0) & (pl.program_id(1) == 0))
    def _load_weights():
        tail0 = 3 * FOX_W + FOX_HEADS
        _stage_weights([(w_hbm.at[0:2 * FOX_W], wqk_ref),
                        (w_hbm.at[tail0:tail0 + wt_ref.shape[0]], wt_ref)], stage, sem)

    @pl.when(pl.program_id(1) == 0)
    def _reset():
        carry_scr[...] = jnp.zeros_like(carry_scr)

    xt = x_ref[0]
    h = (xt * g_ref[...]).astype(BF16)
    inv = lax.rsqrt(jnp.mean(xt * xt, axis=-1, keepdims=True) + EPS)
    inv_row = jnp.broadcast_to(inv, (tm, LANES)).T[0:1, :]
    proj = lambda ref, lo, hi: inv * lax.dot_general(h, ref[lo:hi, :], _NT,
                                                     preferred_element_type=F32)
    g_off = 3 * DIL_W
    lane = lax.broadcasted_iota(jnp.int32, (1, LANES), 1)

    n_f =wvf_ref.shape[0] - FOX_W

    def v_and_forget_logits():
        vf = lax.dot_general(wvf_ref[...], h, _NT, preferred_element_type=F32) * inv_row
        vt_ref[0] = vf[0:FOX_W].astype(BF16)
        z = vf[FOX_W:FOX_W + n_f] + bf_ref[...]
        row = lax.broadcasted_iota(jnp.int32, (n_f, 1), 0)
        log_f = jnp.where(row < 3 * FOX_HEADS,
                          -(jnp.maximum(-z, 0.0) + jnp.log1p(jnp.exp(-jnp.abs(z)))), 0.0)
        return _split3(log_f)

    def forget_cumsum(parts):
        row = lax.broadcasted_iota(jnp.int32, (tm, tm), 0)
        col = lax.broadcasted_iota(jnp.int32, (tm, tm), 1)
        tri = jnp.where(row <= col, 1.0, 0.0).astype(BF16)
        cum = carry_scr[:, 0:1]
        for part in parts:
            cum = cum + jnp.dot(part, tri, preferred_element_type=F32)
        carry_scr[...] = jnp.broadcast_to(cum[:, tm - 1:tm], carry_scr.shape)
        nf = jnp.concatenate([-LOG2E * cum, jnp.zeros((LANES - n_f, tm), F32)], axis=0).T
        hi, mid, lo = _split3(nf)
        kx_ref[0] = jnp.where(lane % 3 == 0, hi, jnp.where(lane % 3 == 1, mid, lo))

    def mixer_a_qk():
        qk_ref[0, :, 0:FOX_W] = (proj(wqk_ref, 0, FOX_W) * (Q_SCALE * LOG2E)).astype(BF16)
        qk_ref[0, :, FOX_W:2 * FOX_W] = proj(wqk_ref, FOX_W, 2 * FOX_W).astype(BF16)

    outs = (d0_ref, d1_ref, d2_ref)

    def mixer_b(g):
        dil = DIL_GROUPS[g][1]
        rows = pl.ds(pl.multiple_of(pl.program_id(1) * tm, tm), tm)
        cos, slo, shi = cos_ref[rows, :], slo_ref[rows, :], shi_ref[rows, :]
        part = lambda t: proj(wt_ref, t * DIL_W + g * GROUP_W, t * DIL_W + (g + 1) * GROUP_W)
        q, k, v = part(0), part(1), part(2)
        halves = []
        for c in range(GROUP_W // LANES):
            halves.append(_rope(q[:, c * LANES:(c + 1) * LANES], cos, slo, shi) * (Q_SCALE * LOG2E))
        for c in range(GROUP_W // LANES):
            halves.append(_rope(k[:, c * LANES:(c + 1) * LANES], cos, slo, shi))
        halves += [v[:, c * LANES:(c + 1) * LANES] for c in range(GROUP_W // LANES)]
        n_res = max(dil, 4)
        for c, t in enumerate(halves):
            perm_scr[g, c] = t
        for res in range(n_res):
            for c in range(len(halves)):
                outs[g][0, res, :, c * LANES:(c + 1) * LANES] = (
                    perm_scr[g, c, pl.ds(res, tm // n_res, stride=n_res), :].astype(BF16))

    def gates(c):
        r = proj(wt_ref, g_off + c * 512, g_off + (c + 1) * 512)
        gate_ref[0, :, c * 512:(c + 1) * 512] = jax.nn.sigmoid(r).astype(BF16)

    parts = v_and_forget_logits()
    mixer_b(0)
    gates(0)
    gates(1)
    forget_cumsum(parts)
    mixer_b(1)
    gates(2)
    gates(3)
    mixer_b(2)
    mixer_a_qk()


def _inproj(x, g, w_all, w_vf, b_f, cos, slo, shi):
    B, S, D = x.shape
    n_tail = 3 * DIL_W + 2 * D
    tm = TM_IN
    grid = (B, S // tm)
    tok = lambda w: pl.BlockSpec((1, tm, w), lambda b, i: (b, i, 0))
    tab = _resident(cos.shape)
    out_shape = (
        jax.ShapeDtypeStruct((B, S, 2 * FOX_W), BF16),
        jax.ShapeDtypeStruct((B, FOX_W, S), BF16),
        jax.ShapeDtypeStruct((B, S, LANES), BF16),
        jax.ShapeDtypeStruct((B, 4, S // 4, 3 * GROUP_W), BF16),
        jax.ShapeDtypeStruct((B, 4, S // 4, 3 * GROUP_W), BF16),
        jax.ShapeDtypeStruct((B, 16, S // 16, 3 * GROUP_W), BF16),
        jax.ShapeDtypeStruct((B, S, 2 * D), BF16),
    )
    out_specs = (
        tok(2 * FOX_W),
        pl.BlockSpec((1, FOX_W, tm), lambda b, i: (b, 0, i)),
        tok(LANES),
        pl.BlockSpec((1, 4, tm // 4, 3 * GROUP_W), lambda b, i: (b, 0, i, 0)),
        pl.BlockSpec((1, 4, tm // 4, 3 * GROUP_W), lambda b, i: (b, 0, i, 0)),
        pl.BlockSpec((1, 16, tm // 16, 3 * GROUP_W), lambda b, i: (b, 0, i, 0)),
        tok(2 * D),
    )
    in_specs = [
        tok(D), _resident(g.shape), pl.BlockSpec(memory_space=pl.ANY), _resident(w_vf.shape),
        _resident(b_f.shape), tab, tab, tab,
    ]
    n_f = w_vf.shape[0] - FOX_W
    return pl.pallas_call(
        _inproj_kernel,
        grid=grid,
        in_specs=in_specs,
        out_specs=out_specs,
        out_shape=out_shape,
        scratch_shapes=[pltpu.VMEM((N_GROUPS, 3 * GROUP_W // LANES, tm, LANES), F32),
                        pltpu.VMEM((n_f, LANES), F32),
                        pltpu.VMEM((2 * FOX_W, D), BF16), pltpu.VMEM((n_tail, D), BF16),
                        pltpu.VMEM((2, STAGE_ROWS, D), F32), pltpu.SemaphoreType.DMA((2,))],
        compiler_params=pltpu.CompilerParams(
            dimension_semantics=("arbitrary", "arbitrary"), vmem_limit_bytes=VMEM_LIMIT),
        name="inproj",
    )(x, g, w_all, w_vf, b_f, cos, slo, shi)


def _fox_kernel(q_ref, k_ref, kx_ref, vt_ref, o_ref):
    S = q_ref.shape[1]
    pw = 2 * HEAD_DIM
    pairs = q_ref.shape[2] // pw
    step = pl.program_id(1)
    lane = lax.broadcasted_iota(jnp.int32, (1, pw), 1)
    first = lane < HEAD_DIM
    zeros = jnp.zeros((TQ, pw), BF16)

    def sel(head):
        return jnp.broadcast_to(
            jnp.where((lane >= 3 * head) & (lane < 3 * head + 3), 1.0, 0.0), (TQ, pw)).astype(BF16)

    hq = TQ // 2
    key = lax.broadcasted_iota(jnp.int32, (hq, TQ), 0)
    qry = lax.broadcasted_iota(jnp.int32, (hq, TQ), 1) % hq
    causal = key <= qry

    def colmax(parts):
        m = jnp.max(parts[0], axis=0, keepdims=True)
        for t in parts[1:]:
            m = jnp.maximum(m, jnp.max(t, axis=0, keepdims=True))
        return m

    def scores(pair, qi):
        lanes = slice(pair * pw, (pair + 1) * pw)
        head0 = 2 * (pairs * step + pair)
        q2 = q_ref[0, qi * TQ:(qi + 1) * TQ, lanes]
        qa = jnp.concatenate([jnp.where(first, q2, zeros), sel(head0)], axis=1)
        qb = jnp.concatenate([jnp.where(first, zeros, q2), sel(head0 + 1)], axis=1)
        qmix = jnp.concatenate([qa[0:hq], qb[0:hq], qa[hq:TQ], qb[hq:TQ]], axis=0)
        n_k = (qi + 1) * TK
        n_lo = n_k - hq
        kaug = jnp.concatenate([k_ref[0, 0:n_k, lanes], kx_ref[0, 0:n_k, :]], axis=1)
        s = lax.dot_general(kaug[0:n_lo], qmix, _NT, preferred_element_type=F32)
        s_hi = lax.dot_general(kaug[n_lo:n_k], qmix[TQ:2 * TQ], _NT, preferred_element_type=F32)
        full = [s[c * TK:(c + 1) * TK] for c in range(qi)]
        diag = s[qi * TK:n_lo]
        lo = [t[:, 0:TQ] for t in full] + [jnp.where(causal, diag[:, 0:TQ], NEG_INF)]
        hi = ([t[:, TQ:2 * TQ] for t in full] + [diag[:, TQ:2 * TQ]]
              + [jnp.where(causal, s_hi, NEG_INF)])
        return (lo, colmax(lo)), (hi, colmax(hi))

    def outputs(pair, qi, lo, hi):
        for half, (chunks, m) in enumerate((lo, hi)):
            n = (qi + 1) * TK - (hq if half == 0 else 0)
            p = [jnp.exp2(s - m) for s in chunks]
            l = jnp.sum(p[0], axis=0, keepdims=True)
            for t in p[1:]:
                l = l + jnp.sum(t, axis=0, keepdims=True)
            pt = jnp.concatenate([t.astype(BF16) for t in p], axis=0)
            acc = jnp.dot(vt_ref[0, pair * pw:(pair + 1) * pw, 0:n], pt,
                          preferred_element_type=F32)
            ot = (acc / l).T
            r0 = qi * TQ + half * hq
            o_ref[0, r0:r0 + hq, pair * pw:(pair + 1) * pw] = (
                jnp.where(first, ot[0:hq], ot[hq:TQ]).astype(BF16))

    items = [(pair, qi) for pair in range(pairs) for qi in range(S // TQ)]
    nxt = scores(*items[0])
    for i, item in enumerate(items):
        cur = nxt
        if i + 1 < len(items):
            nxt = scores(*items[i + 1])
        outputs(*item, *cur)


def _fox(qk, vt, kx):
    B, S, _ = qk.shape
    pairs = FOX_PAIRS_PER_STEP
    steps = FOX_HEADS // (2 * pairs)
    w = pairs * 2 * HEAD_DIM
    spec = lambda off: pl.BlockSpec((1, S, w), lambda b, p: (b, 0, off + p))
    return pl.pallas_call(
        _fox_kernel,
        grid=(B, steps),
        in_specs=[spec(0), spec(steps),
                  pl.BlockSpec((1, S, LANES), lambda b, p: (b, 0, 0)),
                  pl.BlockSpec((1, w, S), lambda b, p: (b, p, 0))],
        out_specs=pl.BlockSpec((1, S, w), lambda b, p: (b, 0, p)),
        out_shape=jax.ShapeDtypeStruct((B, S, FOX_W), BF16),
        compiler_params=pltpu.CompilerParams(
            dimension_semantics=("arbitrary", "arbitrary"), vmem_limit_bytes=VMEM_LIMIT),
        name="fox",
    )(qk, qk, kx, vt)


def _dil_kernel(d0_ref, d1_ref, d2_ref, o_ref, acc_scr, m_scr, l_scr, nat_scr):
    S = d0_ref.shape[1]
    nh = DIL_HEADS_PER_GROUP
    blk = DIL_BLK
    n_blocks = S // blk
    lane_head = lax.broadcasted_iota(jnp.int32, (1, GROUP_W), 1) // HEAD_DIM
    a2 = lax.broadcasted_iota(jnp.int32, (nh * blk, 2 * blk), 0) % blk
    k2 = lax.broadcasted_iota(jnp.int32, (nh * blk, 2 * blk), 1)
    bias_two = jnp.where((k2 >= a2) & (k2 <= a2 + blk), 0.0, NEG_INF)
    a1 =lax.broadcasted_iota(jnp.int32, (nh * blk, blk), 0) % blk
    k1 = lax.broadcasted_iota(jnp.int32, (nh * blk, blk), 1)
    bias_one = jnp.where(k1 <= a1, 0.0, NEG_INF)

    def assemble(x):
        out = x[(nh - 1) * blk:nh * blk]
        for h in range(nh - 2, -1, -1):
            out = jnp.where(lane_head == h, x[h * blk:(h + 1) * blk], out)
        return out

    def logits(load_q, load_k, bias):
        qs = load_q()
        zero = jnp.zeros_like(qs)
        qstack = jnp.concatenate(
            [jnp.where(lane_head == h, qs, zero) for h in range(nh)], axis=0)
        s = lax.dot_general(qstack, load_k(), _NT, preferred_element_type=F32) + bias
        return s, jnp.max(s, axis=1, keepdims=True)

    def weigh(s, m, load_v):
        p = jnp.exp2(s - m)
        l = jnp.sum(p, axis=1, keepdims=True)
        pv = jnp.dot(p.astype(BF16), load_v(), preferred_element_type=F32)
        return (assemble(pv), assemble(jnp.broadcast_to(m, pv.shape)),
                assemble(jnp.broadcast_to(l, pv.shape)))

    def put(res, pieces, first):
        acc_b, m_b, l_b = res
        for rows, idx in pieces:
            for c in range(GROUP_W // LANES):
                lanes = slice(c * LANES, (c + 1) * LANES)
                if first:
                    acc_scr[c, idx, :] = acc_b[rows, lanes]
                    m_scr[c, idx, :] = m_b[rows, lanes]
                    l_scr[c, idx, :] = l_b[rows, lanes]
                else:
                    m_o = m_scr[c, idx, :]
                    m_n = jnp.maximum(m_o, m_b[rows, lanes])
                    e_o = jnp.exp2(m_o - m_n)
                    e_b = jnp.exp2(m_b[rows, lanes] - m_n)
                    acc_scr[c, idx, :] = acc_scr[c, idx, :] * e_o + acc_b[rows, lanes] * e_b
                    l_scr[c, idx, :] = l_scr[c, idx, :] * e_o + l_b[rows, lanes] * e_b
                    m_scr[c, idx, :] = m_n

    q_l, k_l, v_l = (slice(i * GROUP_W, (i + 1) * GROUP_W) for i in range(3))
    whole = slice(0, blk)
    n_res = 4
    slab = S // n_res
    per_slab = slab // blk

    items = []
    rows_of = lambda ref, r0, n, lanes: (lambda: ref[0, pl.ds(r0, n), lanes])

    for res in range(n_res):
        for pos in range(per_slab):
            r0 = res * slab + pos * blk
            k0, nk, bias = (r0, blk, bias_one) if pos == 0 else (r0 - blk, 2 * blk, bias_two)
            items.append((rows_of(d1_ref, r0, blk, q_l), rows_of(d1_ref, k0, nk, k_l), bias,
                          rows_of(d1_ref, k0, nk, v_l), [(whole, pl.ds(r0, blk))], True))

    sub = blk // n_res
    qa = lax.broadcasted_iota(jnp.int32, (nh * blk, 2 * blk), 0) % blk
    kk = lax.broadcasted_iota(jnp.int32, (nh * blk, 2 * blk), 1)
    tq = blk + n_res * (qa % sub) + qa // sub
    tk = n_res * (kk % (2 * sub)) + kk // (2 * sub)
    bias0_two = jnp.where((tk <= tq) & (tq - tk <= blk), 0.0, NEG_INF)
    qa1 = lax.broadcasted_iota(jnp.int32, (nh * blk, blk), 0) % blk
    kk1 = lax.broadcasted_iota(jnp.int32, (nh * blk, blk), 1)
    bias0_one = jnp.where(n_res * (kk1 % sub) + kk1 // sub <= n_res * (qa1 % sub) + qa1 // sub,
                          0.0, NEG_INF)

    def slabs_of(ref, start, n, lanes):
        return lambda: jnp.concatenate(
            [ref[0, pl.ds(r * slab + start, n), lanes] for r in range(n_res)], axis=0)

    for n in range(n_blocks):
        k0, nk, bias = (0, sub, bias0_one) if n == 0 else ((n - 1) * sub, 2 * sub, bias0_two)
        dst = [(slice(r * sub, (r + 1) * sub), pl.ds(r * slab + n * sub, sub)) for r in range(n_res)]
        items.append((slabs_of(d0_ref, n * sub, sub, q_l), slabs_of(d0_ref, k0, nk, k_l), bias,
                      slabs_of(d0_ref, k0, nk, v_l), dst, False))

    for r16 in range(n_blocks):
        r0 = r16 * blk
        dst = [(whole, pl.ds((r16 % n_res) * slab + r16 // n_res, blk, stride=n_res))]
        items.append((rows_of(d2_ref, r0, blk, q_l), rows_of(d2_ref, r0, blk, k_l), bias_one,
                      rows_of(d2_ref, r0, blk, v_l), dst, False))

    nxt = logits(*items[0][0:3])
    for i, (_, _, _, load_v, dst, first_write) in enumerate(items):
        cur = nxt
        if i + 1 < len(items):
            nxt = logits(*items[i + 1][0:3])
        put(weigh(*cur, load_v), dst, first_write)

    def finish(res, _):
        for pos in range(per_slab):
            r0 = _aligned(res * slab + pos * blk, blk)
            for c in range(GROUP_W // LANES):
                nat_scr[c, pl.ds(n_res * pos * blk + res, blk, stride=n_res), :] = (
                    acc_scr[c, pl.ds(r0, blk), :] / l_scr[c, pl.ds(r0, blk), :])
        return 0

    for res in range(n_res):
        finish(res, 0)

    def emit(i, _):
        r0 = _aligned(i * 2 * blk, 2 * blk)
        for c in range(GROUP_W // LANES):
            o_ref[0, pl.ds(r0, 2 * blk), c * LANES:(c + 1) * LANES] = nat_scr[c, pl.ds(r0, 2 * blk), :].astype(BF16)
        return 0

    for i in range(n_blocks // 2):
        emit(i, 0)


def _dilated(d0, d1, d2):
    B, S, W = d0.shape
    spec = pl.BlockSpec((1, S, W), lambda b: (b, 0, 0))
    return pl.pallas_call(
        _dil_kernel,
        grid=(B,),
        in_specs=[spec, spec, spec],
        out_specs=pl.BlockSpec((1, S, GROUP_W), lambda b: (b, 0, 0)),
        out_shape=jax.ShapeDtypeStruct((B, S, GROUP_W), BF16),
        scratch_shapes=[pltpu.VMEM((GROUP_W // LANES, S, LANES), F32)] * 4,
        compiler_params=pltpu.CompilerParams(
            dimension_semantics=("arbitrary",), vmem_limit_bytes=VMEM_LIMIT),
        name="dilated",
    )(d0, d1, d2)


STAGE_ROWS = 512


def _stage_weights(pairs, stage, sem):
    cols = stage.shape[2]
    chunks = []
    for src, dst in pairs:
        for r0 in range(0, src.shape[0], STAGE_ROWS):
            n = min(STAGE_ROWS, src.shape[0] - r0)
            for c0 in range(0, src.shape[1], cols):
                chunks.append((src, dst, r0, n, c0))

    def copy(k):
        src, _, r0, n, c0 = chunks[k]
        return pltpu.make_async_copy(src.at[r0:r0 + n, c0:c0 + cols], stage.at[k % 2, 0:n],
                                     sem.at[k % 2])

    copy(0).start()
    for k, (_, dst, r0, n, c0) in enumerate(chunks):
        if k + 1 < len(chunks):
            copy(k + 1).start()
        copy(k).wait()
        dst[r0:r0 + n, c0:c0 + cols] = stage[k % 2, 0:n].astype(BF16)


def _outmlp_kernel(x_ref, oa_ref, ob_ref, gate_ref, wa_hbm, wb_hbm, wo_hbm, g2_ref,
                   wu_hbm, wdn_hbm, g3_ref, o_ref,
                   wa_ref, wb_ref, wo_ref, wu_ref, wdn_ref, stage, sem, *, final_norm):
    d = x_ref.shape[1]

    @pl.when(pl.program_id(0) == 0)
    def _load_weights():
        _stage_weights([(wa_hbm, wa_ref), (wb_hbm, wb_ref), (wo_hbm, wo_ref),
                        (wu_hbm, wu_ref), (wdn_hbm, wdn_ref)], stage, sem)

    ya = jnp.dot(oa_ref[...], wa_ref[...], preferred_element_type=F32)
    yb = jnp.dot(ob_ref[...], wb_ref[...], preferred_element_type=F32)
    mixed = gate_ref[:, 0:d].astype(F32) * ya + gate_ref[:, d:2 * d].astype(F32) * yb
    x1 = x_ref[...] + jnp.dot(mixed.astype(BF16), wo_ref[...], preferred_element_type=F32)
    h = (x1 * g2_ref[...]).astype(BF16)
    inv = lax.rsqrt(jnp.mean(x1 * x1, axis=-1, keepdims=True) + EPS)
    d_ff = wu_ref.shape[1]
    mlp = None
    for c in range(d_ff // FF_CHUNK):
        u = jnp.dot(h, wu_ref[:, c * FF_CHUNK:(c + 1) * FF_CHUNK], preferred_element_type=F32)
        a = jnp.square(jnp.maximum(u, 0.0)).astype(BF16)
        t = jnp.dot(a, wdn_ref[c * FF_CHUNK:(c + 1) * FF_CHUNK, :], preferred_element_type=F32)
        mlp = t if mlp is None else mlp + t
    x2 = x1 + (inv * inv) * mlp
    o_ref[...] = _rms(x2, g3_ref[...]) if final_norm else x2


def _outmlp(x, oa, ob, gates, w_a, w_b, w_o, g2, w_u, w_dn, g3, final_norm):
    N, D = x.shape
    tm = TM_OUT
    tok = lambda w: pl.BlockSpec((tm, w), lambda i: (i, 0))
    hbm = pl.BlockSpec(memory_space=pl.ANY)
    weights = (w_a, w_b, w_o, w_u, w_dn)
    return pl.pallas_call(
        functools.partial(_outmlp_kernel, final_norm=final_norm),
        grid=(N // tm,),
        in_specs=[tok(D), tok(oa.shape[1]), tok(ob.shape[1]), tok(gates.shape[1]),
                  hbm, hbm, hbm, _resident(g2.shape), hbm, hbm, _resident(g3.shape)],
        out_specs=tok(D),
        out_shape=jax.ShapeDtypeStruct((N, D), F32),
        scratch_shapes=[pltpu.VMEM(w.shape, BF16) for w in weights]
        + [pltpu.VMEM((2, STAGE_ROWS, D), F32), pltpu.SemaphoreType.DMA((2,))],
        compiler_params=pltpu.CompilerParams(
            dimension_semantics=("arbitrary",), vmem_limit_bytes=VMEM_LIMIT),
        name="outmlp",
    )(x, oa, ob, gates, w_a, w_b, w_o, g2, w_u, w_dn, g3)


def _rope_tables(S):
    inv_freq = jnp.power(jnp.float32(ROPE_THETA),
                         -jnp.arange(ROPE_HALF, dtype=F32) * 2.0 / ROPE_DIM)
    ang = jnp.arange(S).astype(F32)[:, None] * inv_freq[None, :]
    cos, sin = jnp.cos(ang), jnp.sin(ang)
    rest = HEAD_DIM - ROPE_DIM
    one = jnp.ones((S, rest), F32)
    zero = lambda w: jnp.zeros((S, w), F32)
    cos_h = jnp.concatenate([cos, cos, one], axis=1)
    slo_h = jnp.concatenate([-sin, zero(ROPE_HALF + rest)], axis=1)
    shi_h = jnp.concatenate([zero(ROPE_HALF), sin, zero(rest)], axis=1)
    rep = lambda t: jnp.tile(t, (1, 128 // HEAD_DIM))
    return rep(cos_h), rep(slo_h), rep(shi_h)


def kernel(x, norm_attn_g, w_in, b_forget, w_branch_a, w_branch_b, w_out, norm_mlp_g, w_up,
           w_down, norm_final_g):
    B, S, D = x.shape
    depth = w_in.shape[0]
    assert S % (16 * DIL_BLK) == 0 and S // 16 == DIL_BLK, "sequence length fixed by the dilation groups"
    cos, slo, shi = _rope_tables(S)
    sizes = (FOX_W, FOX_W, FOX_W, FOX_HEADS, DIL_W, DIL_W, DIL_W, D, D)
    offs = np.concatenate([[0], np.cumsum(sizes)])
    for l in range(depth):
        wt = jnp.swapaxes(w_in[l], 0, 1)
        w_all = wt.astype(F32)
        pad = -(3 * FOX_HEADS) % 16
        w_vf = jnp.concatenate(
            [wt[offs[2]:offs[3]], jnp.repeat(wt[offs[3]:offs[4]], 3, axis=0),
             jnp.zeros((pad, D), wt.dtype)], axis=0).astype(BF16)
        b_f = jnp.pad(jnp.repeat(b_forget[l].astype(F32), 3), (0, pad))[:, None]

        qk, vt, kx, d0, d1, d2, gates = _inproj(
            x, norm_attn_g[l][None, :], w_all, w_vf, b_f, cos, slo, shi)
        oa = _fox(qk, vt, kx)
        ob = _dilated(d0.reshape(B, S, -1), d1.reshape(B, S, -1), d2.reshape(B, S, -1))
        y = _outmlp(
            x.reshape(B * S, D), oa.reshape(B * S, -1), ob.reshape(B * S, -1),
            gates.reshape(B * S, -1),
            w_branch_a[l], w_branch_b[l], w_out[l], norm_mlp_g[l][None, :], w_up[l], w_down[l],
            norm_final_g[None, :], final_norm=(l == depth - 1))
        x = y.reshape(B, S, D)
    return x
```

```python
import functools

import numpy as np
import jax
import jax.numpy as jnp
from jax import lax
from jax.experimental import pallas as pl
from jax.experimental.pallas import tpu as pltpu

F32 = jnp.float32
BF16 = jnp.bfloat16

HEAD_DIM = 64
FOX_HEADS = 8
FOX_W = FOX_HEADS * HEAD_DIM
DIL_GROUPS = ((128, 1), (512, 4), (2048, 16))
DIL_HEADS_PER_GROUP = 4
GROUP_W = DIL_HEADS_PER_GROUP * HEAD_DIM
N_GROUPS = len(DIL_GROUPS)
DIL_W = N_GROUPS * GROUP_W
ROPE_THETA = 500000.0
ROPE_DIM = HEAD_DIM // 4
ROPE_HALF = ROPE_DIM // 2
EPS = 1e-6
NEG_INF = -1e30
Q_SCALE = HEAD_DIM ** -0.5
LOG2E = 1.4426950408889634

LANES = 128
DIL_BLK = 128
TM_IN = 512
TM_OUT = 512
TQ = 256
TK = 256
FOX_PAIRS_PER_STEP = 2
FF_CHUNK = 1024
VMEM_LIMIT = 56 * 1024 * 1024

_NT = (((1,), (1,)), ((), ()))


def _resident(shape):
    nd = len(shape)
    return pl.BlockSpec(shape, lambda *_: (0,) * nd, pipeline_mode=pl.Buffered(1))


def _aligned(x, m):
    return x if isinstance(x, int) else pl.multiple_of(x, m)


def _rms(x, g):
    ms = jnp.mean(x * x, axis=-1, keepdims=True)
    return x * lax.rsqrt(ms + EPS) * g


def _rope(x, cos, sin_lo, sin_hi):
    w = x.shape[1]
    up = pltpu.roll(x, w - ROPE_HALF, 1)
    dn = pltpu.roll(x, ROPE_HALF, 1)
    return x * cos + up * sin_lo + dn * sin_hi


def _split3(x):
    hi = x.astype(BF16)
    r1 = x - hi.astype(F32)
    mid = r1.astype(BF16)
    lo = (r1 - mid.astype(F32)).astype(BF16)
    return hi, mid, lo


def _inproj_kernel(x_ref, g_ref, w_hbm, wvf_ref, bf_ref,
                   cos_ref, slo_ref, shi_ref,
                   qk_ref, vt_ref, kx_ref, d0_ref, d1_ref, d2_ref, gate_ref,
                   perm_scr, carry_scr, wqk_ref, wt_ref, stage, sem):
    tm = x_ref.shape[1]
    d = x_ref.shape[2]

    @pl.when((pl.program_id(0) == 0) & (pl.program_id(1) == 0))
    def _load_weights():
        tail0 = 3 * FOX_W + FOX_HEADS
        _stage_weights([(w_hbm.at[0:2 * FOX_W], wqk_ref),
                        (w_hbm.at[tail0:tail0 + wt_ref.shape[0]], wt_ref)], stage, sem)

    @pl.when(pl.program_id(1) == 0)
    def _reset():
        carry_scr[...] = jnp.zeros_like(carry_scr)

    xt = x_ref[0]
    h = (xt * g_ref[...]).astype(BF16)
    inv = lax.rsqrt(jnp.mean(xt * xt, axis=-1, keepdims=True) + EPS)
    inv_row = jnp.broadcast_to(inv, (tm, LANES)).T[0:1, :]
    proj = lambda ref, lo, hi: inv * lax.dot_general(h, ref[lo:hi, :], _NT,
                                                     preferred_element_type=F32)
    g_off = 3 * DIL_W
    lane = lax.broadcasted_iota(jnp.int32, (1, LANES), 1)

    n_f = wvf_ref.shape[0] - FOX_W

    def v_and_forget_logits():
        vf = lax.dot_general(wvf_ref[...], h, _NT, preferred_element_type=F32) * inv_row
        vt_ref[0] = vf[0:FOX_W].astype(BF16)
        z = vf[FOX_W:FOX_W + n_f] + bf_ref[...]
        row = lax.broadcasted_iota(jnp.int32, (n_f, 1), 0)
        log_f = jnp.where(row < 3 * FOX_HEADS,
                          -(jnp.maximum(-z, 0.0) + jnp.log1p(jnp.exp(-jnp.abs(z)))), 0.0)
        return _split3(log_f)

    def forget_cumsum(parts):
        row = lax.broadcasted_iota(jnp.int32, (tm, tm), 0)
        col = lax.broadcasted_iota(jnp.int32, (tm, tm), 1)
        tri = jnp.where(row <= col, 1.0, 0.0).astype(BF16)
        cum = carry_scr[:, 0:1]
        for part in parts:
            cum = cum + jnp.dot(part, tri, preferred_element_type=F32)
        carry_scr[...] = jnp.broadcast_to(cum[:, tm - 1:tm], carry_scr.shape)
        nf = jnp.concatenate([-LOG2E * cum, jnp.zeros((LANES - n_f, tm), F32)], axis=0).T
        hi, mid, lo = _split3(nf)
        kx_ref[0] = jnp.where(lane % 3 == 0, hi, jnp.where(lane % 3 == 1, mid, lo))

    def mixer_a_qk():
        qk_ref[0, :, 0:FOX_W] = (proj(wqk_ref, 0, FOX_W) * (Q_SCALE * LOG2E)).astype(BF16)
        qk_ref[0, :, FOX_W:2 * FOX_W] = proj(wqk_ref, FOX_W, 2 * FOX_W).astype(BF16)

    outs = (d0_ref, d1_ref, d2_ref)

    def mixer_b(g):
        dil = DIL_GROUPS[g][1]
        rows = pl.ds(pl.multiple_of(pl.program_id(1) * tm, tm), tm)
        cos, slo, shi = cos_ref[rows, :], slo_ref[rows, :], shi_ref[rows, :]
        part = lambda t: proj(wt_ref, t * DIL_W + g * GROUP_W, t * DIL_W + (g + 1) * GROUP_W)
        q, k, v = part(0), part(1), part(2)
        halves = []
        for c in range(GROUP_W // LANES):
            halves.append(_rope(q[:, c * LANES:(c + 1) * LANES], cos, slo, shi) * (Q_SCALE * LOG2E))
        for c in range(GROUP_W // LANES):
            halves.append(_rope(k[:, c * LANES:(c + 1) * LANES], cos, slo, shi))
        halves += [v[:, c * LANES:(c + 1) * LANES] for c in range(GROUP_W // LANES)]
        n_res = 4
        for c, t in enumerate(halves):
            perm_scr[g, c] = t
        for res in range(n_res):
            for c in range(len(halves)):
                rows_c = perm_scr[g, c, pl.ds(res, tm // n_res, stride=n_res), :]
                if dil <= n_res:
                    outs[g][0, res, :, c * LANES:(c + 1) * LANES] = rows_c.astype(BF16)
                else:
                    outs[g][0, c, res] = rows_c

    def gates(c):
        r = proj(wt_ref, g_off + c * 512, g_off + (c + 1) * 512)
        gate_ref[0, :, c * 512:(c + 1) * 512] = jax.nn.sigmoid(r).astype(BF16)

    parts = v_and_forget_logits()
    mixer_b(0)
    gates(0)
    gates(1)
    forget_cumsum(parts)
    mixer_b(1)
    gates(2)
    gates(3)
    mixer_b(2)
    mixer_a_qk()


def _inproj(x, g, w_all, w_vf, b_f, cos, slo, shi):
    B, S, D = x.shape
    n_tail = 3 * DIL_W + 2 * D
    tm = TM_IN
    grid = (B, S // tm)
    tok = lambda w: pl.BlockSpec((1, tm, w), lambda b, i: (b, i, 0))
    tab = _resident(cos.shape)
    out_shape = (
        jax.ShapeDtypeStruct((B, S, 2 * FOX_W), BF16),
        jax.ShapeDtypeStruct((B, FOX_W, S), BF16),
        jax.ShapeDtypeStruct((B, S, LANES), BF16),
        jax.ShapeDtypeStruct((B, 4, S // 4, 3 * GROUP_W), BF16),
        jax.ShapeDtypeStruct((B, 4, S // 4, 3 * GROUP_W), BF16),
        jax.ShapeDtypeStruct((B, 3 * GROUP_W // LANES, 4, S // 4, LANES), F32),
        jax.ShapeDtypeStruct((B, S, 2 * D), BF16),
    )
    out_specs = (
        tok(2 * FOX_W),
        pl.BlockSpec((1, FOX_W, tm), lambda b, i: (b, 0, i)),
        tok(LANES),
        pl.BlockSpec((1, 4, tm // 4, 3 * GROUP_W), lambda b, i: (b, 0, i, 0)),
        pl.BlockSpec((1, 4, tm // 4, 3 * GROUP_W), lambda b, i: (b, 0, i, 0)),
        pl.BlockSpec((1, 3 * GROUP_W // LANES, 4, tm // 4, LANES), lambda b, i: (b, 0, 0, i, 0)),
        tok(2 * D),
    )
    in_specs = [
        tok(D), _resident(g.shape), pl.BlockSpec(memory_space=pl.ANY), _resident(w_vf.shape),
        _resident(b_f.shape), tab, tab, tab,
    ]
    n_f = w_vf.shape[0] - FOX_W
    return pl.pallas_call(
        _inproj_kernel,
        grid=grid,
        in_specs=in_specs,
        out_specs=out_specs,
        out_shape=out_shape,
        scratch_shapes=[pltpu.VMEM((N_GROUPS, 3 * GROUP_W // LANES, tm, LANES), F32),
                        pltpu.VMEM((n_f, LANES), F32),
                        pltpu.VMEM((2 * FOX_W, D), BF16), pltpu.VMEM((n_tail, D), BF16),
                        pltpu.VMEM((2, STAGE_ROWS, D), F32), pltpu.SemaphoreType.DMA((2,))],
        compiler_params=pltpu.CompilerParams(
            dimension_semantics=("arbitrary", "arbitrary"), vmem_limit_bytes=VMEM_LIMIT),
        name="inproj",
    )(x, g, w_all, w_vf, b_f, cos, slo, shi)


def _fox_kernel(q_ref, k_ref, kx_ref, vt_ref, o_ref):
    S = q_ref.shape[1]
    pw = 2 * HEAD_DIM
    pairs = q_ref.shape[2] // pw
    step = pl.program_id(1)
    lane = lax.broadcasted_iota(jnp.int32, (1, pw), 1)
    first = lane < HEAD_DIM
    zeros = jnp.zeros((TQ, pw), BF16)

    def sel(head):
        return jnp.broadcast_to(
            jnp.where((lane >= 3 * head) & (lane < 3 * head + 3), 1.0, 0.0), (TQ, pw)).astype(BF16)

    hq = TQ // 2
    key = lax.broadcasted_iota(jnp.int32, (hq, TQ), 0)
    qry = lax.broadcasted_iota(jnp.int32, (hq, TQ), 1) % hq
    causal = key <= qry

    def colmax(parts):
        m = jnp.max(parts[0], axis=0, keepdims=True)
        for t in parts[1:]:
            m = jnp.maximum(m, jnp.max(t, axis=0, keepdims=True))
        return m

    def scores(pair, qi):
        lanes = slice(pair * pw, (pair + 1) * pw)
        head0 = 2 * (pairs * step + pair)
        q2 = q_ref[0, qi * TQ:(qi + 1) * TQ, lanes]
        qa = jnp.concatenate([jnp.where(first, q2, zeros), sel(head0)], axis=1)
        qb = jnp.concatenate([jnp.where(first, zeros, q2), sel(head0 + 1)], axis=1)
        qmix = jnp.concatenate([qa[0:hq], qb[0:hq], qa[hq:TQ], qb[hq:TQ]], axis=0)
        n_k = (qi + 1) * TK
        n_lo = n_k - hq
        kaug = jnp.concatenate([k_ref[0, 0:n_k, lanes], kx_ref[0, 0:n_k, :]], axis=1)
        s = lax.dot_general(kaug[0:n_lo], qmix, _NT, preferred_element_type=F32)
        s_hi = lax.dot_general(kaug[n_lo:n_k], qmix[TQ:2 * TQ], _NT, preferred_element_type=F32)
        full = [s[c * TK:(c + 1) * TK] for c in range(qi)]
        diag = s[qi * TK:n_lo]
        lo = [t[:, 0:TQ] for t in full] + [jnp.where(causal, diag[:, 0:TQ], NEG_INF)]
        hi = ([t[:, TQ:2 * TQ] for t in full] + [diag[:, TQ:2 * TQ]]
              + [jnp.where(causal, s_hi, NEG_INF)])
        return (lo, colmax(lo)), (hi, colmax(hi))

    def outputs(pair, qi, lo, hi):
        for half, (chunks, m) in enumerate((lo, hi)):
            n = (qi + 1) * TK - (hq if half == 0 else 0)
            p = [jnp.exp2(s - m) for s in chunks]
            l = jnp.sum(p[0], axis=0, keepdims=True)
            for t in p[1:]:
                l = l + jnp.sum(t, axis=0, keepdims=True)
            pt = jnp.concatenate([t.astype(BF16) for t in p], axis=0)
            acc = jnp.dot(vt_ref[0, pair * pw:(pair + 1) * pw, 0:n], pt,
                          preferred_element_type=F32)
            ot = (acc / l).T
            r0 = qi * TQ + half * hq
            o_ref[0, r0:r0 + hq, pair * pw:(pair + 1) * pw] = (
                jnp.where(first, ot[0:hq], ot[hq:TQ]).astype(BF16))

    items = [(pair, qi) for pair in range(pairs) for qi in range(S // TQ)]
    nxt = scores(*items[0])
    for i, item in enumerate(items):
        cur = nxt
        if i + 1 < len(items):
            nxt = scores(*items[i + 1])
        outputs(*item, *cur)


def _fox(qk, vt, kx):
    B, S, _ = qk.shape
    pairs = FOX_PAIRS_PER_STEP
    steps = FOX_HEADS // (2 * pairs)
    w = pairs * 2 * HEAD_DIM
    spec = lambda off: pl.BlockSpec((1, S, w), lambda b, p: (b, 0, off + p))
    return pl.pallas_call(
        _fox_kernel,
        grid=(B, steps),
        in_specs=[spec(0), spec(steps),
                  pl.BlockSpec((1, S, LANES), lambda b, p: (b, 0, 0)),
                  pl.BlockSpec((1, w, S), lambda b, p: (b, p, 0))],
        out_specs=pl.BlockSpec((1, S, w), lambda b, p: (b, 0, p)),
        out_shape=jax.ShapeDtypeStruct((B, S, FOX_W), BF16),
        compiler_params=pltpu.CompilerParams(
            dimension_semantics=("arbitrary", "arbitrary"), vmem_limit_bytes=VMEM_LIMIT),
        name="fox",
    )(qk, qk, kx, vt)


def _dil_kernel(d0_ref, d1_ref, d2_ref, o_ref, acc_scr, m_scr, l_scr, nat_scr):
    S = d0_ref.shape[1]
    nh = DIL_HEADS_PER_GROUP
    blk = DIL_BLK
    n_blocks = S // blk
    lane_head = lax.broadcasted_iota(jnp.int32, (1, GROUP_W), 1) // HEAD_DIM
    a2 = lax.broadcasted_iota(jnp.int32, (nh * blk, 2 * blk), 0) % blk
    k2 = lax.broadcasted_iota(jnp.int32, (nh * blk, 2 * blk), 1)
    bias_two = jnp.where((k2 >= a2) & (k2 <= a2 + blk), 0.0, NEG_INF)
    a1 = lax.broadcasted_iota(jnp.int32, (nh * blk, blk), 0) % blk
    k1 = lax.broadcasted_iota(jnp.int32, (nh * blk, blk), 1)
    bias_one = jnp.where(k1 <= a1, 0.0, NEG_INF)

    def assemble(x):
        out = x[(nh - 1) * blk:nh * blk]
        for h in range(nh - 2, -1, -1):
            out = jnp.where(lane_head == h, x[h * blk:(h + 1) * blk], out)
        return out

    def logits(load_q, load_k, bias):
        qs = load_q()
        zero = jnp.zeros_like(qs)
        qstack = jnp.concatenate(
            [jnp.where(lane_head == h, qs, zero) for h in range(nh)], axis=0)
        s = lax.dot_general(qstack, load_k(), _NT, preferred_element_type=F32) + bias
        return s, jnp.max(s, axis=1, keepdims=True)

    def weigh(s, m, load_v):
        p = jnp.exp2(s - m)
        l = jnp.sum(p, axis=1, keepdims=True)
        pv = jnp.dot(p.astype(BF16), load_v(), preferred_element_type=F32)
        return (assemble(pv), assemble(jnp.broadcast_to(m, pv.shape)),
                assemble(jnp.broadcast_to(l, pv.shape)))

    def put(res, pieces, first):
        acc_b, m_b, l_b = res
        for rows, idx in pieces:
            for c in range(GROUP_W // LANES):
                lanes = slice(c * LANES, (c + 1) * LANES)
                if first:
                    acc_scr[c, idx, :] = acc_b[rows, lanes]
                    m_scr[c, idx, :] = m_b[rows, lanes]
                    l_scr[c, idx, :] = l_b[rows, lanes]
                else:
                    m_o = m_scr[c, idx, :]
                    m_n = jnp.maximum(m_o, m_b[rows, lanes])
                    e_o = jnp.exp2(m_o - m_n)
                    e_b = jnp.exp2(m_b[rows, lanes] - m_n)
                    acc_scr[c, idx, :] = acc_scr[c, idx, :] * e_o + acc_b[rows, lanes] * e_b
                    l_scr[c, idx, :] = l_scr[c, idx, :] * e_o + l_b[rows, lanes] * e_b
                    m_scr[c, idx, :] = m_n

    q_l, k_l, v_l = (slice(i * GROUP_W, (i + 1) * GROUP_W) for i in range(3))
    whole = slice(0, blk)
    n_res = 4
    slab = S // n_res
    per_slab = slab // blk

    items = []
    rows_of = lambda ref, r0, n, lanes: (lambda: ref[0, pl.ds(r0, n), lanes])

    for res in range(n_res):
        for pos in range(per_slab):
            r0 = res * slab + pos * blk
            k0, nk, bias = (r0, blk, bias_one) if pos == 0 else (r0 - blk, 2 * blk, bias_two)
            items.append((rows_of(d1_ref, r0, blk, q_l), rows_of(d1_ref, k0, nk, k_l), bias,
                          rows_of(d1_ref, k0, nk, v_l), [(whole, pl.ds(r0, blk))], True))

    sub = blk // n_res
    qa = lax.broadcasted_iota(jnp.int32, (nh * blk, 2 * blk), 0) % blk
    kk = lax.broadcasted_iota(jnp.int32, (nh * blk, 2 * blk), 1)
    tq = blk + n_res * (qa % sub) + qa // sub
    tk = n_res * (kk % (2 * sub)) + kk // (2 * sub)
    bias0_two = jnp.where((tk <= tq) & (tq - tk <= blk), 0.0, NEG_INF)
    qa1 = lax.broadcasted_iota(jnp.int32, (nh * blk, blk), 0) % blk
    kk1 = lax.broadcasted_iota(jnp.int32, (nh * blk, blk), 1)
    bias0_one = jnp.where(n_res * (kk1 % sub) + kk1 // sub <= n_res * (qa1 % sub) + qa1 // sub,
                          0.0, NEG_INF)

    def slabs_of(ref, start, n, lanes):
        return lambda: jnp.concatenate(
            [ref[0, pl.ds(r * slab + start, n), lanes] for r in range(n_res)], axis=0)

    for n in range(n_blocks):
        k0, nk, bias = (0, sub, bias0_one) if n == 0 else ((n - 1) * sub, 2 * sub, bias0_two)
        dst = [(slice(r * sub, (r + 1) * sub), pl.ds(r * slab + n * sub, sub)) for r in range(n_res)]
        items.append((slabs_of(d0_ref, n * sub, sub, q_l), slabs_of(d0_ref, k0, nk, k_l), bias,
                      slabs_of(d0_ref, k0, nk, v_l), dst, False))

    def gather_of(start, chunks):
        return lambda: jnp.concatenate(
            [d2_ref[0, c, pl.ds(start, blk, stride=n_res), :].astype(BF16) for c in chunks], axis=1)

    per_part = GROUP_W // LANES
    for r16 in range(n_blocks):
        start = (r16 % n_res) * slab + r16 // n_res
        dst = [(whole, pl.ds(start, blk, stride=n_res))]
        items.append((gather_of(start, range(0, per_part)),
                      gather_of(start, range(per_part, 2 * per_part)), bias_one,
                      gather_of(start, range(2 * per_part, 3 * per_part)), dst, False))

    nxt = logits(*items[0][0:3])
    for i, (_, _, _, load_v, dst, first_write) in enumerate(items):
        cur = nxt
        if i + 1 < len(items):
            nxt = logits(*items[i + 1][0:3])
        put(weigh(*cur, load_v), dst, first_write)

    def finish(res, _):
        for pos in range(per_slab):
            r0 = _aligned(res * slab + pos * blk, blk)
            for c in range(GROUP_W // LANES):
                nat_scr[c, pl.ds(n_res * pos * blk + res, blk, stride=n_res), :] = (
                    acc_scr[c, pl.ds(r0, blk), :] / l_scr[c, pl.ds(r0, blk), :])
        return 0

    for res in range(n_res):
        finish(res, 0)

    def emit(i, _):
        r0 = _aligned(i * 2 * blk, 2 * blk)
        for c in range(GROUP_W // LANES):
            o_ref[0, pl.ds(r0, 2 * blk), c * LANES:(c + 1) * LANES] = nat_scr[c, pl.ds(r0, 2 * blk), :].astype(BF16)
        return 0

    for i in range(n_blocks // 2):
        emit(i, 0)


def _dilated(d0, d1, d2):
    B, S, W = d0.shape
    spec = pl.BlockSpec((1, S, W), lambda b: (b, 0, 0))
    spec2 = pl.BlockSpec((1,) + d2.shape[1:], lambda b: (b, 0, 0, 0))
    return pl.pallas_call(
        _dil_kernel,
        grid=(B,),
        in_specs=[spec, spec, spec2],
        out_specs=pl.BlockSpec((1, S, GROUP_W), lambda b: (b, 0, 0)),
        out_shape=jax.ShapeDtypeStruct((B, S, GROUP_W), BF16),
        scratch_shapes=[pltpu.VMEM((GROUP_W // LANES, S, LANES), F32)] * 4,
        compiler_params=pltpu.CompilerParams(
            dimension_semantics=("arbitrary",), vmem_limit_bytes=VMEM_LIMIT),
        name="dilated",
    )(d0, d1, d2)


STAGE_ROWS = 512


def _stage_weights(pairs, stage, sem):
    cols = stage.shape[2]
    chunks = []
    for src, dst in pairs:
        for r0 in range(0, src.shape[0], STAGE_ROWS):
            n = min(STAGE_ROWS, src.shape[0] - r0)
            for c0 in range(0, src.shape[1], cols):
                chunks.append((src, dst, r0, n, c0))

    def copy(k):
        src, _, r0, n, c0 = chunks[k]
        return pltpu.make_async_copy(src.at[r0:r0 + n, c0:c0 + cols], stage.at[k % 2, 0:n],
                                     sem.at[k % 2])

    copy(0).start()
    for k, (_, dst, r0, n, c0) in enumerate(chunks):
        if k + 1 < len(chunks):
            copy(k + 1).start()
        copy(k).wait()
        dst[r0:r0 + n, c0:c0 + cols] = stage[k % 2, 0:n].astype(BF16)


def _outmlp_kernel(x_ref, oa_ref, ob_ref, gate_ref, wa_hbm, wb_hbm, wo_hbm, g2_ref,
                   wu_hbm, wdn_hbm, g3_ref, o_ref,
                   wa_ref, wb_ref, wo_ref, wu_ref, wdn_ref, stage, sem, *, final_norm):
    d = x_ref.shape[1]

    @pl.when(pl.program_id(0) == 0)
    def _load_weights():
        _stage_weights([(wa_hbm, wa_ref), (wb_hbm, wb_ref), (wo_hbm, wo_ref),
                        (wu_hbm, wu_ref), (wdn_hbm, wdn_ref)], stage, sem)

    ya = jnp.dot(oa_ref[...], wa_ref[...], preferred_element_type=F32)
    yb = jnp.dot(ob_ref[...], wb_ref[...], preferred_element_type=F32)
    mixed = gate_ref[:, 0:d].astype(F32) * ya + gate_ref[:, d:2 * d].astype(F32) * yb
    x1 = x_ref[...] + jnp.dot(mixed.astype(BF16), wo_ref[...], preferred_element_type=F32)
    h = (x1 * g2_ref[...]).astype(BF16)
    inv = lax.rsqrt(jnp.mean(x1 * x1, axis=-1, keepdims=True) + EPS)
    d_ff = wu_ref.shape[1]
    mlp = None
    for c in range(d_ff // FF_CHUNK):
        u = jnp.dot(h, wu_ref[:, c * FF_CHUNK:(c + 1) * FF_CHUNK], preferred_element_type=F32)
        a = jnp.square(jnp.maximum(u, 0.0)).astype(BF16)
        t = jnp.dot(a, wdn_ref[c * FF_CHUNK:(c + 1) * FF_CHUNK, :], preferred_element_type=F32)
        mlp = t if mlp is None else mlp + t
    x2 = x1 + (inv * inv) * mlp
    o_ref[...] = _rms(x2, g3_ref[...]) if final_norm else x2


def _outmlp(x, oa, ob, gates, w_a, w_b, w_o, g2, w_u, w_dn, g3, final_norm):
    N, D = x.shape
    tm = TM_OUT
    tok = lambda w: pl.BlockSpec((tm, w), lambda i: (i, 0))
    hbm = pl.BlockSpec(memory_space=pl.ANY)
    weights = (w_a, w_b, w_o, w_u, w_dn)
    return pl.pallas_call(
        functools.partial(_outmlp_kernel, final_norm=final_norm),
        grid=(N // tm,),
        in_specs=[tok(D), tok(oa.shape[1]), tok(ob.shape[1]), tok(gates.shape[1]),
                  hbm, hbm, hbm, _resident(g2.shape), hbm, hbm, _resident(g3.shape)],
        out_specs=tok(D),
        out_shape=jax.ShapeDtypeStruct((N, D), F32),
        scratch_shapes=[pltpu.VMEM(w.shape, BF16) for w in weights]
        + [pltpu.VMEM((2, STAGE_ROWS, D), F32), pltpu.SemaphoreType.DMA((2,))],
        compiler_params=pltpu.CompilerParams(
            dimension_semantics=("arbitrary",), vmem_limit_bytes=VMEM_LIMIT),
        name="outmlp",
    )(x, oa, ob, gates, w_a, w_b, w_o, g2, w_u, w_dn, g3)


def _rope_tables(S):
    inv_freq = jnp.power(jnp.float32(ROPE_THETA),
                         -jnp.arange(ROPE_HALF, dtype=F32) * 2.0 / ROPE_DIM)
    ang = jnp.arange(S).astype(F32)[:, None] * inv_freq[None, :]
    cos, sin = jnp.cos(ang), jnp.sin(ang)
    rest = HEAD_DIM - ROPE_DIM
    one = jnp.ones((S, rest), F32)
    zero = lambda w: jnp.zeros((S, w), F32)
    cos_h = jnp.concatenate([cos, cos, one], axis=1)
    slo_h = jnp.concatenate([-sin, zero(ROPE_HALF + rest)], axis=1)
    shi_h = jnp.concatenate([zero(ROPE_HALF), sin, zero(rest)], axis=1)
    rep = lambda t: jnp.tile(t, (1, 128 // HEAD_DIM))
    return rep(cos_h), rep(slo_h), rep(shi_h)


def kernel(x, norm_attn_g, w_in, b_forget, w_branch_a, w_branch_b, w_out, norm_mlp_g, w_up,
           w_down, norm_final_g):
    B, S, D = x.shape
    depth = w_in.shape[0]
    assert S % (16 * DIL_BLK) == 0 and S // 16 == DIL_BLK, "sequence length fixed by the dilation groups"
    cos, slo, shi = _rope_tables(S)
    sizes = (FOX_W, FOX_W, FOX_W, FOX_HEADS, DIL_W, DIL_W, DIL_W, D, D)
    offs = np.concatenate([[0], np.cumsum(sizes)])
    for l in range(depth):
        wt = jnp.swapaxes(w_in[l], 0, 1)
        w_all = wt.astype(F32)
        pad = -(3 * FOX_HEADS) % 16
        w_vf = jnp.concatenate(
            [wt[offs[2]:offs[3]], jnp.repeat(wt[offs[3]:offs[4]], 3, axis=0),
             jnp.zeros((pad, D), wt.dtype)], axis=0).astype(BF16)
        b_f = jnp.pad(jnp.repeat(b_forget[l].astype(F32), 3), (0, pad))[:, None]

        qk, vt, kx, d0, d1, d2, gates = _inproj(
            x, norm_attn_g[l][None, :], w_all, w_vf, b_f, cos, slo, shi)
        oa = _fox(qk, vt, kx)
        ob = _dilated(d0.reshape(B, S, -1), d1.reshape(B, S, -1),
                      d2.reshape(B, d2.shape[1], S, LANES))
        y = _outmlp(
            x.reshape(B * S, D), oa.reshape(B * S, -1), ob.reshape(B * S, -1),
            gates.reshape(B * S, -1),
            w_branch_a[l], w_branch_b[l], w_out[l], norm_mlp_g[l][None, :], w_up[l], w_down[l],
            norm_final_g[None, :], final_norm=(l == depth - 1))
        x = y.reshape(B, S, D)
    return x
```

```python
import functools

import numpy as np
import jax
import jax.numpy as jnp
from jax import lax
from jax.experimental import pallas as pl
from jax.experimental.pallas import tpu as pltpu

F32 = jnp.float32
BF16 = jnp.bfloat16

HEAD_DIM = 64
FOX_HEADS = 8
FOX_W = FOX_HEADS * HEAD_DIM
DIL_GROUPS = ((128, 1), (512, 4), (2048, 16))
DIL_HEADS_PER_GROUP = 4
GROUP_W = DIL_HEADS_PER_GROUP * HEAD_DIM
N_GROUPS = len(DIL_GROUPS)
DIL_W = N_GROUPS * GROUP_W
ROPE_THETA = 500000.0
ROPE_DIM = HEAD_DIM // 4
ROPE_HALF = ROPE_DIM // 2
EPS = 1e-6
NEG_INF = -1e30
Q_SCALE = HEAD_DIM ** -0.5
LOG2E = 1.4426950408889634

LANES = 128
DIL_BLK = 128
TM_IN = 512
TM_OUT = 512
TQ = 256
TK = 256
FOX_PAIRS_PER_STEP = 2
FF_CHUNK = 1024
VMEM_LIMIT = 56 * 1024 * 1024

_NT = (((1,), (1,)), ((), ()))


def _resident(shape):
    nd = len(shape)
    return pl.BlockSpec(shape, lambda *_: (0,) * nd, pipeline_mode=pl.Buffered(1))


def _aligned(x, m):
    return x if isinstance(x, int) else pl.multiple_of(x, m)


def _rms(x, g):
    ms = jnp.mean(x * x, axis=-1, keepdims=True)
    return x * lax.rsqrt(ms + EPS) * g


def _rope(x, cos, sin_lo, sin_hi):
    w = x.shape[1]
    up = pltpu.roll(x, w - ROPE_HALF, 1)
    dn = pltpu.roll(x, ROPE_HALF, 1)
    return x * cos + up * sin_lo + dn * sin_hi


def _split3(x):
    hi = x.astype(BF16)
    r1 = x - hi.astype(F32)
    mid = r1.astype(BF16)
    lo = (r1 - mid.astype(F32)).astype(BF16)
    return hi, mid, lo


def _inproj_kernel(x_ref, g_ref, w_hbm, wvf_ref, bf_ref,
                   cos_ref, slo_ref, shi_ref,
                   qk_ref, vt_ref, kx_ref, d0_ref, d1_ref, d2_ref, gate_ref,
                   perm_scr, carry_scr, wqk_ref, wt_ref, stage, sem):
    tm = x_ref.shape[1]
    d = x_ref.shape[2]

    @pl.when((pl.program_id(0) == 0) & (pl.program_id(1) == 0))
    def _load_weights():
        tail0 = 3 * FOX_W + FOX_HEADS
        _stage_weights([(w_hbm.at[0:2 * FOX_W], wqk_ref),
                        (w_hbm.at[tail0:tail0 + wt_ref.shape[0]], wt_ref)], stage, sem)

    @pl.when(pl.program_id(1) == 0)
    def _reset():
        carry_scr[...] = jnp.zeros_like(carry_scr)

    xt = x_ref[0]
    xg = xt * g_ref[...]
    h = xg.astype(BF16)
    inv = lax.rsqrt(jnp.mean(xt * xt, axis=-1, keepdims=True) + EPS)
    inv_b = jnp.broadcast_to(inv, (tm, LANES))
    inv_row = inv_b.T[0:1, :]
    proj = lambda ref, lo, hi: inv * lax.dot_general(h, ref[lo:hi, :], _NT,
                                                     preferred_element_type=F32)

    n_res = 4
    sub_t = tm // n_res

    def by_residue(ref, start=0):
        return jnp.concatenate(
            [ref[pl.ds(start + res, sub_t, stride=n_res), :] for res in range(n_res)], axis=0)

    for c in range(d // LANES):
        perm_scr[c] = xg[:, c * LANES:(c + 1) * LANES]
    perm_scr[d // LANES] = inv_b
    h4 = jnp.concatenate([by_residue(perm_scr.at[c]) for c in range(d // LANES)],
                         axis=1).astype(BF16)
    inv4 = by_residue(perm_scr.at[d // LANES])[:, 0:1]
    proj4 = lambda lo, hi: inv4 * lax.dot_general(h4, wt_ref[lo:hi, :], _NT,
                                                  preferred_element_type=F32)
    g_off = 3 * DIL_W
    lane = lax.broadcasted_iota(jnp.int32, (1, LANES), 1)

    n_f = wvf_ref.shape[0] - FOX_W

    def v_and_forget_logits():
        vf = lax.dot_general(wvf_ref[...], h, _NT, preferred_element_type=F32) * inv_row
        vt_ref[0] = vf[0:FOX_W].astype(BF16)
        z = vf[FOX_W:FOX_W + n_f] + bf_ref[...]
        row = lax.broadcasted_iota(jnp.int32, (n_f, 1), 0)
        log_f = jnp.where(row < 3 * FOX_HEADS,
                          -(jnp.maximum(-z, 0.0) + jnp.log1p(jnp.exp(-jnp.abs(z)))), 0.0)
        return _split3(log_f)

    def forget_cumsum(parts):
        row = lax.broadcasted_iota(jnp.int32, (tm, tm), 0)
        col = lax.broadcasted_iota(jnp.int32, (tm, tm), 1)
        tri = jnp.where(row <= col, 1.0, 0.0).astype(BF16)
        cum = carry_scr[:, 0:1]
        for part in parts:
            cum = cum + jnp.dot(part, tri, preferred_element_type=F32)
        carry_scr[...] = jnp.broadcast_to(cum[:, tm - 1:tm], carry_scr.shape)
        nf = jnp.concatenate([-LOG2E * cum, jnp.zeros((LANES - n_f, tm), F32)], axis=0).T
        hi, mid, lo = _split3(nf)
        kx_ref[0] = jnp.where(lane % 3 == 0, hi, jnp.where(lane % 3 == 1, mid, lo))

    def mixer_a_qk():
        qk_ref[0, :, 0:FOX_W] = (proj(wqk_ref, 0, FOX_W) * (Q_SCALE * LOG2E)).astype(BF16)
        qk_ref[0, :, FOX_W:2 * FOX_W] = proj(wqk_ref, FOX_W, 2 * FOX_W).astype(BF16)

    outs = (d0_ref, d1_ref, d2_ref)

    def mixer_b(g):
        dil = DIL_GROUPS[g][1]
        t0 = pl.multiple_of(pl.program_id(1) * tm, tm)
        cos, slo, shi = (by_residue(t, t0) for t in (cos_ref, slo_ref, shi_ref))
        part = lambda t: proj4(t * DIL_W + g * GROUP_W, t * DIL_W + (g + 1) * GROUP_W)
        q, k, v = part(0), part(1), part(2)
        halves = []
        for c in range(GROUP_W // LANES):
            halves.append(_rope(q[:, c * LANES:(c + 1) * LANES], cos, slo, shi) * (Q_SCALE * LOG2E))
        for c in range(GROUP_W // LANES):
            halves.append(_rope(k[:, c * LANES:(c + 1) * LANES], cos, slo, shi))
        halves += [v[:, c * LANES:(c + 1) * LANES] for c in range(GROUP_W // LANES)]
        for res in range(n_res):
            for c, t in enumerate(halves):
                rows_c = t[res * sub_t:(res + 1) * sub_t]
                if dil <= n_res:
                    outs[g][0, res, :, c * LANES:(c + 1) * LANES] = rows_c.astype(BF16)
                else:
                    outs[g][0, c, res] = rows_c

    def gates(c):
        r = proj(wt_ref, g_off + c * 512, g_off + (c + 1) * 512)
        gate_ref[0, :, c * 512:(c + 1) * 512] = jax.nn.sigmoid(r).astype(BF16)

    parts = v_and_forget_logits()
    mixer_b(0)
    gates(0)
    gates(1)
    forget_cumsum(parts)
    mixer_b(1)
    gates(2)
    gates(3)
    mixer_b(2)
    mixer_a_qk()


def _inproj(x, g, w_all, w_vf, b_f, cos, slo, shi):
    B, S, D = x.shape
    n_tail = 3 * DIL_W + 2 * D
    tm = TM_IN
    grid = (B, S // tm)
    tok = lambda w: pl.BlockSpec((1, tm, w), lambda b, i: (b, i, 0))
    tab = _resident(cos.shape)
    out_shape = (
        jax.ShapeDtypeStruct((B, S, 2 * FOX_W), BF16),
        jax.ShapeDtypeStruct((B, FOX_W, S), BF16),
        jax.ShapeDtypeStruct((B, S, LANES), BF16),
        jax.ShapeDtypeStruct((B, 4, S // 4, 3 * GROUP_W), BF16),
        jax.ShapeDtypeStruct((B, 4, S // 4, 3 * GROUP_W), BF16),
        jax.ShapeDtypeStruct((B, 3 * GROUP_W // LANES, 4, S // 4, LANES), F32),
        jax.ShapeDtypeStruct((B, S, 2 * D), BF16),
    )
    out_specs = (
        tok(2 * FOX_W),
        pl.BlockSpec((1, FOX_W, tm), lambda b, i: (b, 0, i)),
        tok(LANES),
        pl.BlockSpec((1, 4, tm // 4, 3 * GROUP_W), lambda b, i: (b, 0, i, 0)),
        pl.BlockSpec((1, 4, tm // 4, 3 * GROUP_W), lambda b, i: (b, 0, i, 0)),
        pl.BlockSpec((1, 3 * GROUP_W // LANES, 4, tm // 4, LANES), lambda b, i: (b, 0, 0, i, 0)),
        tok(2 * D),
    )
    in_specs = [
        tok(D), _resident(g.shape), pl.BlockSpec(memory_space=pl.ANY), _resident(w_vf.shape),
        _resident(b_f.shape), tab, tab, tab,
    ]
    n_f = w_vf.shape[0] - FOX_W
    return pl.pallas_call(
        _inproj_kernel,
        grid=grid,
        in_specs=in_specs,
        out_specs=out_specs,
        out_shape=out_shape,
        scratch_shapes=[pltpu.VMEM((D // LANES + 1, tm, LANES), F32),
                        pltpu.VMEM((n_f, LANES), F32),
                        pltpu.VMEM((2 * FOX_W, D), BF16), pltpu.VMEM((n_tail, D), BF16),
                        pltpu.VMEM((2, STAGE_ROWS, D), F32), pltpu.SemaphoreType.DMA((2,))],
        compiler_params=pltpu.CompilerParams(
            dimension_semantics=("arbitrary", "arbitrary"), vmem_limit_bytes=VMEM_LIMIT),
        name="inproj",
    )(x, g, w_all, w_vf, b_f, cos, slo, shi)


def _fox_kernel(q_ref, k_ref, kx_ref, vt_ref, o_ref):
    S = q_ref.shape[1]
    pw = 2 * HEAD_DIM
    pairs = q_ref.shape[2] // pw
    step = pl.program_id(1)
    lane = lax.broadcasted_iota(jnp.int32, (1, pw), 1)
    first = lane < HEAD_DIM
    zeros = jnp.zeros((TQ, pw), BF16)

    def sel(head):
        return jnp.broadcast_to(
            jnp.where((lane >= 3 * head) & (lane < 3 * head + 3), 1.0, 0.0), (TQ, pw)).astype(BF16)

    hq = TQ // 2
    key = lax.broadcasted_iota(jnp.int32, (hq, TQ), 0)
    qry = lax.broadcasted_iota(jnp.int32, (hq, TQ), 1) % hq
    causal = key <= qry

    def colmax(parts):
        m = jnp.max(parts[0], axis=0, keepdims=True)
        for t in parts[1:]:
            m = jnp.maximum(m, jnp.max(t, axis=0, keepdims=True))
        return m

    def scores(pair, qi):
        lanes = slice(pair * pw, (pair + 1) * pw)
        head0 = 2 * (pairs * step + pair)
        q2 = q_ref[0, qi * TQ:(qi + 1) * TQ, lanes]
        qa = jnp.concatenate([jnp.where(first, q2, zeros), sel(head0)], axis=1)
        qb = jnp.concatenate([jnp.where(first, zeros, q2), sel(head0 + 1)], axis=1)
        qmix = jnp.concatenate([qa[0:hq], qb[0:hq], qa[hq:TQ], qb[hq:TQ]], axis=0)
        n_k = (qi + 1) * TK
        n_lo = n_k - hq
        kaug = jnp.concatenate([k_ref[0, 0:n_k, lanes], kx_ref[0, 0:n_k, :]], axis=1)
        s = lax.dot_general(kaug[0:n_lo], qmix, _NT, preferred_element_type=F32)
        s_hi = lax.dot_general(kaug[n_lo:n_k], qmix[TQ:2 * TQ], _NT, preferred_element_type=F32)
        full = [s[c * TK:(c + 1) * TK] for c in range(qi)]
        diag = s[qi * TK:n_lo]
        lo = [t[:, 0:TQ] for t in full] + [jnp.where(causal, diag[:, 0:TQ], NEG_INF)]
        hi = ([t[:, TQ:2 * TQ] for t in full] + [diag[:, TQ:2 * TQ]]
              + [jnp.where(causal, s_hi, NEG_INF)])
        return (lo, colmax(lo)), (hi, colmax(hi))

    def outputs(pair, qi, lo, hi):
        for half, (chunks, m) in enumerate((lo, hi)):
            n = (qi + 1) * TK - (hq if half == 0 else 0)
            p = [jnp.exp2(s - m) for s in chunks]
            l = jnp.sum(p[0], axis=0, keepdims=True)
            for t in p[1:]:
                l = l + jnp.sum(t, axis=0, keepdims=True)
            pt = jnp.concatenate([t.astype(BF16) for t in p], axis=0)
            acc = jnp.dot(vt_ref[0, pair * pw:(pair + 1) * pw, 0:n], pt,
                          preferred_element_type=F32)
            ot = (acc / l).T
            r0 = qi * TQ + half * hq
            o_ref[0, r0:r0 + hq, pair * pw:(pair + 1) * pw] = (
                jnp.where(first, ot[0:hq], ot[hq:TQ]).astype(BF16))

    items = [(pair, qi) for pair in range(pairs) for qi in range(S // TQ)]
    nxt = scores(*items[0])
    for i, item in enumerate(items):
        cur = nxt
        if i + 1 < len(items):
            nxt = scores(*items[i + 1])
        outputs(*item, *cur)


def _fox(qk, vt, kx):
    B, S, _ = qk.shape
    pairs = FOX_PAIRS_PER_STEP
    steps = FOX_HEADS // (2 * pairs)
    w = pairs * 2 * HEAD_DIM
    spec = lambda off: pl.BlockSpec((1, S, w), lambda b, p: (b, 0, off + p))
    return pl.pallas_call(
        _fox_kernel,
        grid=(B, steps),
        in_specs=[spec(0), spec(steps),
                  pl.BlockSpec((1, S, LANES), lambda b, p: (b, 0, 0)),
                  pl.BlockSpec((1, w, S), lambda b, p: (b, p, 0))],
        out_specs=pl.BlockSpec((1, S, w), lambda b, p: (b, 0, p)),
        out_shape=jax.ShapeDtypeStruct((B, S, FOX_W), BF16),
        compiler_params=pltpu.CompilerParams(
            dimension_semantics=("arbitrary", "arbitrary"), vmem_limit_bytes=VMEM_LIMIT),
        name="fox",
    )(qk, qk, kx, vt)


def _dil_kernel(d0_ref, d1_ref, d2_ref, o_ref, acc_scr, m_scr, l_scr, nat_scr):
    S = d0_ref.shape[1]
    nh = DIL_HEADS_PER_GROUP
    blk = DIL_BLK
    n_blocks = S // blk
    lane_head = lax.broadcasted_iota(jnp.int32, (1, GROUP_W), 1) // HEAD_DIM
    a2 = lax.broadcasted_iota(jnp.int32, (nh * blk, 2 * blk), 0) % blk
    k2 = lax.broadcasted_iota(jnp.int32, (nh * blk, 2 * blk), 1)
    bias_two = jnp.where((k2 >= a2) & (k2 <= a2 + blk), 0.0, NEG_INF)
    a1 = lax.broadcasted_iota(jnp.int32, (nh * blk, blk), 0) % blk
    k1 = lax.broadcasted_iota(jnp.int32, (nh * blk, blk), 1)
    bias_one = jnp.where(k1 <= a1, 0.0, NEG_INF)

    def assemble(x):
        out = x[(nh - 1) * blk:nh * blk]
        for h in range(nh - 2, -1, -1):
            out = jnp.where(lane_head == h, x[h * blk:(h + 1) * blk], out)
        return out

    def logits(load_q, load_k, bias):
        qs = load_q()
        zero = jnp.zeros_like(qs)
        qstack = jnp.concatenate(
            [jnp.where(lane_head == h, qs, zero) for h in range(nh)], axis=0)
        s = lax.dot_general(qstack, load_k(), _NT, preferred_element_type=F32) + bias
        return s, jnp.max(s, axis=1, keepdims=True)

    def weigh(s, m, load_v):
        p = jnp.exp2(s - m)
        l = jnp.sum(p, axis=1, keepdims=True)
        pv = jnp.dot(p.astype(BF16), load_v(), preferred_element_type=F32)
        return (assemble(pv), assemble(jnp.broadcast_to(m, pv.shape)),
                assemble(jnp.broadcast_to(l, pv.shape)))

    def put(res, pieces, first):
        acc_b, m_b, l_b = res
        for rows, idx in pieces:
            for c in range(GROUP_W // LANES):
                lanes = slice(c * LANES, (c + 1) * LANES)
                if first:
                    acc_scr[c, idx, :] = acc_b[rows, lanes]
                    m_scr[c, idx, :] = m_b[rows, lanes]
                    l_scr[c, idx, :] = l_b[rows, lanes]
                else:
                    m_o = m_scr[c, idx, :]
                    m_n = jnp.maximum(m_o, m_b[rows, lanes])
                    e_o = jnp.exp2(m_o - m_n)
                    e_b = jnp.exp2(m_b[rows, lanes] - m_n)
                    acc_scr[c, idx, :] = acc_scr[c, idx, :] * e_o + acc_b[rows, lanes] * e_b
                    l_scr[c, idx, :] = l_scr[c, idx, :] * e_o + l_b[rows, lanes] * e_b
                    m_scr[c, idx, :] = m_n

    q_l, k_l, v_l = (slice(i * GROUP_W, (i + 1) * GROUP_W) for i in range(3))
    whole = slice(0, blk)
    n_res = 4
    slab = S // n_res
    per_slab = slab // blk

    items = []
    rows_of = lambda ref, r0, n, lanes: (lambda: ref[0, pl.ds(r0, n), lanes])

    for res in range(n_res):
        for pos in range(per_slab):
            r0 = res * slab + pos * blk
            k0, nk, bias = (r0, blk, bias_one) if pos == 0 else (r0 - blk, 2 * blk, bias_two)
            items.append((rows_of(d1_ref, r0, blk, q_l), rows_of(d1_ref, k0, nk, k_l), bias,
                          rows_of(d1_ref, k0, nk, v_l), [(whole, pl.ds(r0, blk))], True))

    sub = blk // n_res
    qa = lax.broadcasted_iota(jnp.int32, (nh * blk, 2 * blk), 0) % blk
    kk = lax.broadcasted_iota(jnp.int32, (nh * blk, 2 * blk), 1)
    tq = blk + n_res * (qa % sub) + qa // sub
    tk = n_res * (kk % (2 * sub)) + kk // (2 * sub)
    bias0_two = jnp.where((tk <= tq) & (tq - tk <= blk), 0.0, NEG_INF)
    qa1 = lax.broadcasted_iota(jnp.int32, (nh * blk, blk), 0) % blk
    kk1 = lax.broadcasted_iota(jnp.int32, (nh * blk, blk), 1)
    bias0_one = jnp.where(n_res * (kk1 % sub) + kk1 // sub <= n_res * (qa1 % sub) + qa1 // sub,
                          0.0, NEG_INF)

    def slabs_of(ref, start, n, lanes):
        return lambda: jnp.concatenate(
            [ref[0, pl.ds(r * slab + start, n), lanes] for r in range(n_res)], axis=0)

    for n in range(n_blocks):
        k0, nk, bias = (0, sub, bias0_one) if n == 0 else ((n - 1) * sub, 2 * sub, bias0_two)
        dst = [(slice(r * sub, (r + 1) * sub), pl.ds(r * slab + n * sub, sub)) for r in range(n_res)]
        items.append((slabs_of(d0_ref, n * sub, sub, q_l), slabs_of(d0_ref, k0, nk, k_l), bias,
                      slabs_of(d0_ref, k0, nk, v_l), dst, False))

    def gather_of(start, chunks):
        return lambda: jnp.concatenate(
            [d2_ref[0, c, pl.ds(start, blk, stride=n_res), :].astype(BF16) for c in chunks], axis=1)

    per_part = GROUP_W // LANES
    for r16 in range(n_blocks):
        start = (r16 % n_res) * slab + r16 // n_res
        dst = [(whole, pl.ds(start, blk, stride=n_res))]
        items.append((gather_of(start, range(0, per_part)),
                      gather_of(start, range(per_part, 2 * per_part)), bias_one,
                      gather_of(start, range(2 * per_part, 3 * per_part)), dst, False))

    nxt = logits(*items[0][0:3])
    for i, (_, _, _, load_v, dst, first_write) in enumerate(items):
        cur = nxt
        if i + 1 < len(items):
            nxt = logits(*items[i + 1][0:3])
        put(weigh(*cur, load_v), dst, first_write)

    def finish(res, _):
        for pos in range(per_slab):
            r0 = _aligned(res * slab + pos * blk, blk)
            for c in range(GROUP_W // LANES):
                nat_scr[c, pl.ds(n_res * pos * blk + res, blk, stride=n_res), :] = (
                    acc_scr[c, pl.ds(r0, blk), :] / l_scr[c, pl.ds(r0, blk), :])
        return 0

    for res in range(n_res):
        finish(res, 0)

    def emit(i, _):
        r0 = _aligned(i * 2 * blk, 2 * blk)
        for c in range(GROUP_W // LANES):
            o_ref[0, pl.ds(r0, 2 * blk), c * LANES:(c + 1) * LANES] = nat_scr[c, pl.ds(r0, 2 * blk), :].astype(BF16)
        return 0

    for i in range(n_blocks // 2):
        emit(i, 0)


def _dilated(d0, d1, d2):
    B, S, W = d0.shape
    spec = pl.BlockSpec((1, S, W), lambda b: (b, 0, 0))
    spec2 = pl.BlockSpec((1,) + d2.shape[1:], lambda b: (b, 0, 0, 0))
    return pl.pallas_call(
        _dil_kernel,
        grid=(B,),
        in_specs=[spec, spec, spec2],
        out_specs=pl.BlockSpec((1, S, GROUP_W), lambda b: (b, 0, 0)),
        out_shape=jax.ShapeDtypeStruct((B, S, GROUP_W), BF16),
        scratch_shapes=[pltpu.VMEM((GROUP_W // LANES, S, LANES), F32)] * 4,
        compiler_params=pltpu.CompilerParams(
            dimension_semantics=("arbitrary",), vmem_limit_bytes=VMEM_LIMIT),
        name="dilated",
    )(d0, d1, d2)


STAGE_ROWS = 512


def _stage_weights(pairs, stage, sem):
    cols = stage.shape[2]
    chunks = []
    for src, dst in pairs:
        for r0 in range(0, src.shape[0], STAGE_ROWS):
            n = min(STAGE_ROWS, src.shape[0] - r0)
            for c0 in range(0, src.shape[1], cols):
                chunks.append((src, dst, r0, n, c0))

    def copy(k):
        src, _, r0, n, c0 = chunks[k]
        return pltpu.make_async_copy(src.at[r0:r0 + n, c0:c0 + cols], stage.at[k % 2, 0:n],
                                     sem.at[k % 2])

    copy(0).start()
    for k, (_, dst, r0, n, c0) in enumerate(chunks):
        if k + 1 < len(chunks):
            copy(k + 1).start()
        copy(k).wait()
        dst[r0:r0 + n, c0:c0 + cols] = stage[k % 2, 0:n].astype(BF16)


def _outmlp_kernel(x_ref, oa_ref, ob_ref, gate_ref, wa_hbm, wb_hbm, wo_hbm, g2_ref,
                   wu_hbm, wdn_hbm, g3_ref, o_ref,
                   wa_ref, wb_ref, wo_ref, wu_ref, wdn_ref, stage, sem, *, final_norm):
    d = x_ref.shape[1]

    @pl.when(pl.program_id(0) == 0)
    def _load_weights():
        _stage_weights([(wa_hbm, wa_ref), (wb_hbm, wb_ref), (wo_hbm, wo_ref),
                        (wu_hbm, wu_ref), (wdn_hbm, wdn_ref)], stage, sem)

    ya = jnp.dot(oa_ref[...], wa_ref[...], preferred_element_type=F32)
    yb = jnp.dot(ob_ref[...], wb_ref[...], preferred_element_type=F32)
    mixed = gate_ref[:, 0:d].astype(F32) * ya + gate_ref[:, d:2 * d].astype(F32) * yb
    x1 = x_ref[...] + jnp.dot(mixed.astype(BF16), wo_ref[...], preferred_element_type=F32)
    h = (x1 * g2_ref[...]).astype(BF16)
    inv = lax.rsqrt(jnp.mean(x1 * x1, axis=-1, keepdims=True) + EPS)
    d_ff = wu_ref.shape[1]
    mlp = None
    for c in range(d_ff // FF_CHUNK):
        u = jnp.dot(h, wu_ref[:, c * FF_CHUNK:(c + 1) * FF_CHUNK], preferred_element_type=F32)
        a = jnp.square(jnp.maximum(u, 0.0)).astype(BF16)
        t = jnp.dot(a, wdn_ref[c * FF_CHUNK:(c + 1) * FF_CHUNK, :], preferred_element_type=F32)
        mlp = t if mlp is None else mlp + t
    x2 = x1 + (inv * inv) * mlp
    o_ref[...] = _rms(x2, g3_ref[...]) if final_norm else x2


def _outmlp(x, oa, ob, gates, w_a, w_b, w_o, g2, w_u, w_dn, g3, final_norm):
    N, D = x.shape
    tm = TM_OUT
    tok = lambda w: pl.BlockSpec((tm, w), lambda i: (i, 0))
    hbm = pl.BlockSpec(memory_space=pl.ANY)
    weights = (w_a, w_b, w_o, w_u, w_dn)
    return pl.pallas_call(
        functools.partial(_outmlp_kernel, final_norm=final_norm),
        grid=(N // tm,),
        in_specs=[tok(D), tok(oa.shape[1]), tok(ob.shape[1]), tok(gates.shape[1]),
                  hbm, hbm, hbm, _resident(g2.shape), hbm, hbm, _resident(g3.shape)],
        out_specs=tok(D),
        out_shape=jax.ShapeDtypeStruct((N, D), F32),
        scratch_shapes=[pltpu.VMEM(w.shape, BF16) for w in weights]
        + [pltpu.VMEM((2, STAGE_ROWS, D), F32), pltpu.SemaphoreType.DMA((2,))],
        compiler_params=pltpu.CompilerParams(
            dimension_semantics=("arbitrary",), vmem_limit_bytes=VMEM_LIMIT),
        name="outmlp",
    )(x, oa, ob, gates, w_a, w_b, w_o, g2, w_u, w_dn, g3)


def _rope_tables(S):
    inv_freq = jnp.power(jnp.float32(ROPE_THETA),
                         -jnp.arange(ROPE_HALF, dtype=F32) * 2.0 / ROPE_DIM)
    ang = jnp.arange(S).astype(F32)[:, None] * inv_freq[None, :]
    cos, sin = jnp.cos(ang), jnp.sin(ang)
    rest = HEAD_DIM - ROPE_DIM
    one = jnp.ones((S, rest), F32)
    zero = lambda w: jnp.zeros((S, w), F32)
    cos_h = jnp.concatenate([cos, cos, one], axis=1)
    slo_h = jnp.concatenate([-sin, zero(ROPE_HALF + rest)], axis=1)
    shi_h = jnp.concatenate([zero(ROPE_HALF), sin, zero(rest)], axis=1)
    rep = lambda t: jnp.tile(t, (1, 128 // HEAD_DIM))
    return rep(cos_h), rep(slo_h), rep(shi_h)


def kernel(x, norm_attn_g, w_in, b_forget, w_branch_a, w_branch_b, w_out, norm_mlp_g, w_up,
           w_down, norm_final_g):
    B, S, D = x.shape
    depth = w_in.shape[0]
    assert S % (16 * DIL_BLK) == 0 and S // 16 == DIL_BLK, "sequence length fixed by the dilation groups"
    cos, slo, shi = _rope_tables(S)
    sizes = (FOX_W, FOX_W, FOX_W, FOX_HEADS, DIL_W, DIL_W, DIL_W, D, D)
    offs = np.concatenate([[0], np.cumsum(sizes)])
    for l in range(depth):
        wt = jnp.swapaxes(w_in[l], 0, 1)
        w_all = wt.astype(F32)
        pad = -(3 * FOX_HEADS) % 16
        w_vf = jnp.concatenate(
            [wt[offs[2]:offs[3]], jnp.repeat(wt[offs[3]:offs[4]], 3, axis=0),
             jnp.zeros((pad, D), wt.dtype)], axis=0).astype(BF16)
        b_f = jnp.pad(jnp.repeat(b_forget[l].astype(F32), 3), (0, pad))[:, None]

        qk, vt, kx, d0, d1, d2, gates = _inproj(
            x, norm_attn_g[l][None, :], w_all, w_vf, b_f, cos, slo, shi)
        oa = _fox(qk, vt, kx)
        ob = _dilated(d0.reshape(B, S, -1), d1.reshape(B, S, -1),
                      d2.reshape(B, d2.shape[1], S, LANES))
        y = _outmlp(
            x.reshape(B * S, D), oa.reshape(B * S, -1), ob.reshape(B * S, -1),
            gates.reshape(B * S, -1),
            w_branch_a[l], w_branch_b[l], w_out[l], norm_mlp_g[l][None, :], w_up[l], w_down[l],
            norm_final_g[None, :], final_norm=(l == depth - 1))
        x = y.reshape(B, S, D)
    return x
```

```python
import functools

import numpy as np
import jax
import jax.numpy as jnp
from jax import lax
from jax.experimental import pallas as pl
from jax.experimental.pallas import tpu as pltpu

F32 = jnp.float32
BF16 = jnp.bfloat16

HEAD_DIM = 64
FOX_HEADS = 8
FOX_W = FOX_HEADS * HEAD_DIM
DIL_GROUPS = ((128, 1), (512, 4), (2048, 16))
DIL_HEADS_PER_GROUP = 4
GROUP_W = DIL_HEADS_PER_GROUP * HEAD_DIM
N_GROUPS = len(DIL_GROUPS)
DIL_W = N_GROUPS * GROUP_W
ROPE_THETA = 500000.0
ROPE_DIM = HEAD_DIM // 4
ROPE_HALF = ROPE_DIM // 2
EPS = 1e-6
NEG_INF = -1e30
Q_SCALE = HEAD_DIM ** -0.5
LOG2E = 1.4426950408889634

LANES = 128
DIL_BLK = 128
TM_IN = 512
TM_OUT = 512
TQ = 256
TK = 256
FOX_PAIRS_PER_STEP = 2
FF_CHUNK = 1024
STAGE_ROWS = 512
VMEM_LIMIT = 56 * 1024 * 1024

_NT = (((1,), (1,)), ((), ()))


def _resident(shape):
    nd = len(shape)
    return pl.BlockSpec(shape, lambda *_: (0,) * nd, pipeline_mode=pl.Buffered(1))


def _aligned(x, m):
    return x if isinstance(x, int) else pl.multiple_of(x, m)


def _rms(x, g):
    ms = jnp.mean(x * x, axis=-1, keepdims=True)
    return x * lax.rsqrt(ms + EPS) * g


def _rope(x, cos, sin_lo, sin_hi):
    w = x.shape[1]
    up = pltpu.roll(x, w - ROPE_HALF, 1)
    dn = pltpu.roll(x, ROPE_HALF, 1)
    return x * cos + up * sin_lo + dn * sin_hi


def _split3(x):
    hi = x.astype(BF16)
    r1 = x - hi.astype(F32)
    mid = r1.astype(BF16)
    lo = (r1 - mid.astype(F32)).astype(BF16)
    return hi, mid, lo


def _inproj_kernel(x_ref, g_ref, w_hbm, wvf_ref, bf_ref,
                   cos_ref, slo_ref, shi_ref,
                   qk_ref, vt_ref, kx_ref, d0_ref, d1_ref, d2_ref, gate_ref,
                   perm_scr, carry_scr, wqk_ref, wt_ref, stage, sem):
    tm = x_ref.shape[1]
    d = x_ref.shape[2]

    @pl.when((pl.program_id(0) == 0) & (pl.program_id(1) == 0))
    def _load_weights():
        tail0 = 3 * FOX_W + FOX_HEADS
        _stage_weights([(w_hbm.at[0:2 * FOX_W], wqk_ref),
                        (w_hbm.at[tail0:tail0 + wt_ref.shape[0]], wt_ref)], stage, sem)

    @pl.when(pl.program_id(1) == 0)
    def _reset():
        carry_scr[...] = jnp.zeros_like(carry_scr)

    xt = x_ref[0]
    xg = xt * g_ref[...]
    h = xg.astype(BF16)
    inv = lax.rsqrt(jnp.mean(xt * xt, axis=-1, keepdims=True) + EPS)
    inv_b = jnp.broadcast_to(inv, (tm, LANES))
    inv_row = inv_b.T[0:1, :]
    proj = lambda ref, lo, hi: inv * lax.dot_general(h, ref[lo:hi, :], _NT,
                                                     preferred_element_type=F32)

    n_res = 4
    sub_t = tm // n_res

    def by_residue(ref, start=0):
        return jnp.concatenate(
            [ref[pl.ds(start + res, sub_t, stride=n_res), :] for res in range(n_res)], axis=0)

    for c in range(d // LANES):
        perm_scr[c] = xg[:, c * LANES:(c + 1) * LANES]
    perm_scr[d // LANES] = inv_b
    h4 = jnp.concatenate([by_residue(perm_scr.at[c]) for c in range(d // LANES)],
                         axis=1).astype(BF16)
    inv4 = by_residue(perm_scr.at[d // LANES])[:, 0:1]
    proj4 = lambda lo, hi: inv4 * lax.dot_general(h4, wt_ref[lo:hi, :], _NT,
                                                  preferred_element_type=F32)
    g_off = 3 * DIL_W
    lane = lax.broadcasted_iota(jnp.int32, (1, LANES), 1)

    n_f = wvf_ref.shape[0] - FOX_W

    def v_and_forget_logits():
        vf = lax.dot_general(wvf_ref[...], h, _NT, preferred_element_type=F32) * inv_row
        vt_ref[0] = vf[0:FOX_W].astype(BF16)
        z = vf[FOX_W:FOX_W + n_f] + bf_ref[...]
        row = lax.broadcasted_iota(jnp.int32, (n_f, 1), 0)
        log_f = jnp.where(row < 3 * FOX_HEADS,
                          -(jnp.maximum(-z, 0.0) + jnp.log1p(jnp.exp(-jnp.abs(z)))), 0.0)
        return _split3(log_f)

    def forget_cumsum(parts):
        row = lax.broadcasted_iota(jnp.int32, (tm, tm), 0)
        col = lax.broadcasted_iota(jnp.int32, (tm, tm), 1)
        tri = jnp.where(row <= col, 1.0, 0.0).astype(BF16)
        cum = carry_scr[:, 0:1]
        for part in parts:
            cum = cum + jnp.dot(part, tri, preferred_element_type=F32)
        carry_scr[...] = jnp.broadcast_to(cum[:, tm - 1:tm], carry_scr.shape)
        nf = jnp.concatenate([-LOG2E * cum, jnp.zeros((LANES - n_f, tm), F32)], axis=0).T
        hi, mid, lo = _split3(nf)
        kx_ref[0] = jnp.where(lane % 3 == 0, hi, jnp.where(lane % 3 == 1, mid, lo))

    def mixer_a_qk():
        qk_ref[0, :, 0:FOX_W] = (proj(wqk_ref, 0, FOX_W) * (Q_SCALE * LOG2E)).astype(BF16)
        qk_ref[0, :, FOX_W:2 * FOX_W] = proj(wqk_ref, FOX_W, 2 * FOX_W).astype(BF16)

    outs = (d0_ref, d1_ref, d2_ref)

    def mixer_b(g):
        dil = DIL_GROUPS[g][1]
        t0 = pl.multiple_of(pl.program_id(1) * tm, tm)
        cos, slo, shi = (by_residue(t, t0) for t in (cos_ref, slo_ref, shi_ref))
        part = lambda t: proj4(t * DIL_W + g * GROUP_W, t * DIL_W + (g + 1) * GROUP_W)
        q, k, v = part(0), part(1), part(2)
        halves = []
        for c in range(GROUP_W // LANES):
            halves.append(_rope(q[:, c * LANES:(c + 1) * LANES], cos, slo, shi) * (Q_SCALE * LOG2E))
        for c in range(GROUP_W // LANES):
            halves.append(_rope(k[:, c * LANES:(c + 1) * LANES], cos, slo, shi))
        halves += [v[:, c * LANES:(c + 1) * LANES] for c in range(GROUP_W // LANES)]
        for res in range(n_res):
            for c, t in enumerate(halves):
                rows_c = t[res * sub_t:(res + 1) * sub_t]
                if dil <= n_res:
                    outs[g][0, res, :, c * LANES:(c + 1) * LANES] = rows_c.astype(BF16)
                else:
                    outs[g][0, c, res] = rows_c

    def gates(c):
        r = proj(wt_ref, g_off + c * 512, g_off + (c + 1) * 512)
        gate_ref[0, :, c * 512:(c + 1) * 512] = jax.nn.sigmoid(r).astype(BF16)

    parts = v_and_forget_logits()
    mixer_b(0)
    gates(0)
    gates(1)
    forget_cumsum(parts)
    mixer_b(1)
    gates(2)
    gates(3)
    mixer_b(2)
    mixer_a_qk()


def _inproj(x, g, w_all, w_vf, b_f, cos, slo, shi):
    B, S, D = x.shape
    n_tail = 3 * DIL_W + 2 * D
    tm = TM_IN
    grid = (B, S // tm)
    tok = lambda w: pl.BlockSpec((1, tm, w), lambda b, i: (b, i, 0))
    tab = _resident(cos.shape)
    out_shape = (
        jax.ShapeDtypeStruct((B, S, 2 * FOX_W), BF16),
        jax.ShapeDtypeStruct((B, FOX_W, S), BF16),
        jax.ShapeDtypeStruct((B, S, LANES), BF16),
        jax.ShapeDtypeStruct((B, 4, S // 4, 3 * GROUP_W), BF16),
        jax.ShapeDtypeStruct((B, 4, S // 4, 3 * GROUP_W), BF16),
        jax.ShapeDtypeStruct((B, 3 * GROUP_W // LANES, 4, S // 4, LANES), F32),
        jax.ShapeDtypeStruct((B, S, 2 * D), BF16),
    )
    out_specs = (
        tok(2 * FOX_W),
        pl.BlockSpec((1, FOX_W, tm), lambda b, i: (b, 0, i)),
        tok(LANES),
        pl.BlockSpec((1, 4, tm // 4, 3 * GROUP_W), lambda b, i: (b, 0, i, 0)),
        pl.BlockSpec((1, 4, tm // 4, 3 * GROUP_W), lambda b, i: (b, 0, i, 0)),
        pl.BlockSpec((1, 3 * GROUP_W // LANES, 4, tm // 4, LANES), lambda b, i: (b, 0, 0, i, 0)),
        tok(2 * D),
    )
    in_specs = [
        tok(D), _resident(g.shape), pl.BlockSpec(memory_space=pl.ANY), _resident(w_vf.shape),
        _resident(b_f.shape), tab, tab, tab,
    ]
    n_f = w_vf.shape[0] - FOX_W
    return pl.pallas_call(
        _inproj_kernel,
        grid=grid,
        in_specs=in_specs,
        out_specs=out_specs,
        out_shape=out_shape,
        scratch_shapes=[pltpu.VMEM((D // LANES + 1, tm, LANES), F32),
                        pltpu.VMEM((n_f, LANES), F32),
                        pltpu.VMEM((2 * FOX_W, D), BF16), pltpu.VMEM((n_tail, D), BF16),
                        pltpu.VMEM((2, STAGE_ROWS, D), F32), pltpu.SemaphoreType.DMA((2,))],
        compiler_params=pltpu.CompilerParams(
            dimension_semantics=("arbitrary", "arbitrary"), vmem_limit_bytes=VMEM_LIMIT),
        name="inproj",
    )(x, g, w_all, w_vf, b_f, cos, slo, shi)


def _fox_kernel(q_ref, k_ref, kx_ref, vt_ref, o_ref):
    S = q_ref.shape[1]
    pw = 2 * HEAD_DIM
    pairs = q_ref.shape[2] // pw
    step = pl.program_id(1)
    lane = lax.broadcasted_iota(jnp.int32, (1, pw), 1)
    first = lane < HEAD_DIM
    zeros = jnp.zeros((TQ, pw), BF16)

    def sel(head):
        return jnp.broadcast_to(
            jnp.where((lane >= 3 * head) & (lane < 3 * head + 3), 1.0, 0.0), (TQ, pw)).astype(BF16)

    hq = TQ // 2
    key = lax.broadcasted_iota(jnp.int32, (hq, TQ), 0)
    qry = lax.broadcasted_iota(jnp.int32, (hq, TQ), 1) % hq
    causal = key <= qry

    def colmax(parts):
        m = jnp.max(parts[0], axis=0, keepdims=True)
        for t in parts[1:]:
            m = jnp.maximum(m, jnp.max(t, axis=0, keepdims=True))
        return m

    def scores(pair, qi):
        lanes = slice(pair * pw, (pair + 1) * pw)
        head0 = 2 * (pairs * step + pair)
        q2 = q_ref[0, qi * TQ:(qi + 1) * TQ, lanes]
        qa = jnp.concatenate([jnp.where(first, q2, zeros), sel(head0)], axis=1)
        qb = jnp.concatenate([jnp.where(first, zeros, q2), sel(head0 + 1)], axis=1)
        qmix = jnp.concatenate([qa[0:hq], qb[0:hq], qa[hq:TQ], qb[hq:TQ]], axis=0)
        n_k = (qi + 1) * TK
        n_lo = n_k - hq
        kaug = jnp.concatenate([k_ref[0, 0:n_k, lanes], kx_ref[0, 0:n_k, :]], axis=1)
        s = lax.dot_general(kaug[0:n_lo], qmix, _NT, preferred_element_type=F32)
        s_hi = lax.dot_general(kaug[n_lo:n_k], qmix[TQ:2 * TQ], _NT, preferred_element_type=F32)
        full = [s[c * TK:(c + 1) * TK] for c in range(qi)]
        diag = s[qi * TK:n_lo]
        lo = [t[:, 0:TQ] for t in full] + [jnp.where(causal, diag[:, 0:TQ], NEG_INF)]
        hi = ([t[:, TQ:2 * TQ] for t in full] + [diag[:, TQ:2 * TQ]]
              + [jnp.where(causal, s_hi, NEG_INF)])
        return (lo, colmax(lo)), (hi, colmax(hi))

    def outputs(pair, qi, lo, hi):
        for half, (chunks, m) in enumerate((lo, hi)):
            n = (qi + 1) * TK - (hq if half == 0 else 0)
            p = [jnp.exp2(s - m) for s in chunks]
            l = jnp.sum(p[0], axis=0, keepdims=True)
            for t in p[1:]:
                l = l + jnp.sum(t, axis=0, keepdims=True)
            pt = jnp.concatenate([t.astype(BF16) for t in p], axis=0)
            acc = jnp.dot(vt_ref[0, pair * pw:(pair + 1) * pw, 0:n], pt,
                          preferred_element_type=F32)
            ot = (acc / l).T
            r0 = qi * TQ + half * hq
            o_ref[0, r0:r0 + hq, pair * pw:(pair + 1) * pw] = (
                jnp.where(first, ot[0:hq], ot[hq:TQ]).astype(BF16))

    items = [(pair, qi) for pair in range(pairs) for qi in range(S // TQ)]
    nxt = scores(*items[0])
    for i, item in enumerate(items):
        cur = nxt
        if i + 1 < len(items):
            nxt = scores(*items[i + 1])
        outputs(*item, *cur)


def _fox(qk, vt, kx):
    B, S, _ = qk.shape
    pairs = FOX_PAIRS_PER_STEP
    steps = FOX_HEADS // (2 * pairs)
    w = pairs * 2 * HEAD_DIM
    spec = lambda off: pl.BlockSpec((1, S, w), lambda b, p: (b, 0, off + p))
    return pl.pallas_call(
        _fox_kernel,
        grid=(B, steps),
        in_specs=[spec(0), spec(steps),
                  pl.BlockSpec((1, S, LANES), lambda b, p: (b, 0, 0)),
                  pl.BlockSpec((1, w, S), lambda b, p: (b, p, 0))],
        out_specs=pl.BlockSpec((1, S, w), lambda b, p: (b, 0, p)),
        out_shape=jax.ShapeDtypeStruct((B, S, FOX_W), BF16),
        compiler_params=pltpu.CompilerParams(
            dimension_semantics=("arbitrary", "arbitrary"), vmem_limit_bytes=VMEM_LIMIT),
        name="fox",
    )(qk, qk, kx, vt)


def _dil_kernel(d0_ref, d1_ref, d2_ref, o_ref, acc_scr, m_scr, l_scr, nat_scr):
    S = d0_ref.shape[1]
    nh = DIL_HEADS_PER_GROUP
    blk = DIL_BLK
    n_blocks = S // blk
    lane_head = lax.broadcasted_iota(jnp.int32, (1, GROUP_W), 1) // HEAD_DIM
    a2 = lax.broadcasted_iota(jnp.int32, (nh * blk, 2 * blk), 0) % blk
    k2 = lax.broadcasted_iota(jnp.int32, (nh * blk, 2 * blk), 1)
    bias_two = jnp.where((k2 >= a2) & (k2 <= a2 + blk), 0.0, NEG_INF)
    a1 = lax.broadcasted_iota(jnp.int32, (nh * blk, blk), 0) % blk
    k1 = lax.broadcasted_iota(jnp.int32, (nh * blk, blk), 1)
    bias_one = jnp.where(k1 <= a1, 0.0, NEG_INF)

    def assemble(x):
        out = x[(nh - 1) * blk:nh * blk]
        for h in range(nh - 2, -1, -1):
            out = jnp.where(lane_head == h, x[h * blk:(h + 1) * blk], out)
        return out

    def logits(load_q, load_k, bias):
        qs = load_q()
        zero = jnp.zeros_like(qs)
        qstack = jnp.concatenate(
            [jnp.where(lane_head == h, qs, zero) for h in range(nh)], axis=0)
        s = lax.dot_general(qstack, load_k(), _NT, preferred_element_type=F32) + bias
        return s, jnp.max(s, axis=1, keepdims=True)

    def weigh(s, m, load_v):
        p = jnp.exp2(s - m)
        l = jnp.sum(p, axis=1, keepdims=True)
        pv = jnp.dot(p.astype(BF16), load_v(), preferred_element_type=F32)
        return (assemble(pv), assemble(jnp.broadcast_to(m, pv.shape)),
                assemble(jnp.broadcast_to(l, pv.shape)))

    def put(res, pieces, first):
        acc_b, m_b, l_b = res
        for rows, idx in pieces:
            for c in range(GROUP_W // LANES):
                lanes = slice(c * LANES, (c + 1) * LANES)
                if first:
                    acc_scr[c, idx, :] = acc_b[rows, lanes]
                    m_scr[c, idx, :] = m_b[rows, lanes]
                    l_scr[c, idx, :] = l_b[rows, lanes]
                else:
                    m_o = m_scr[c, idx, :]
                    m_n = jnp.maximum(m_o, m_b[rows, lanes])
                    e_o = jnp.exp2(m_o - m_n)
                    e_b = jnp.exp2(m_b[rows, lanes] - m_n)
                    acc_scr[c, idx, :] = acc_scr[c, idx, :] * e_o + acc_b[rows, lanes] * e_b
                    l_scr[c, idx, :] = l_scr[c, idx, :] * e_o + l_b[rows, lanes] * e_b
                    m_scr[c, idx, :] = m_n

    q_l, k_l, v_l = (slice(i * GROUP_W, (i + 1) * GROUP_W) for i in range(3))
    whole = slice(0, blk)
    n_res = 4
    slab = S // n_res
    per_slab = slab // blk

    items = []
    rows_of = lambda ref, r0, n, lanes: (lambda: ref[0, pl.ds(r0, n), lanes])

    for res in range(n_res):
        for pos in range(per_slab):
            r0 = res * slab + pos * blk
            k0, nk, bias = (r0, blk, bias_one) if pos == 0 else (r0 - blk, 2 * blk, bias_two)
            items.append((rows_of(d1_ref, r0, blk, q_l), rows_of(d1_ref, k0, nk, k_l), bias,
                          rows_of(d1_ref, k0, nk, v_l), [(whole, pl.ds(r0, blk))], True))

    sub = blk // n_res
    qa = lax.broadcasted_iota(jnp.int32, (nh * blk, 2 * blk), 0) % blk
    kk = lax.broadcasted_iota(jnp.int32, (nh * blk, 2 * blk), 1)
    tq = blk + n_res * (qa % sub) + qa // sub
    tk = n_res * (kk % (2 * sub)) + kk // (2 * sub)
    bias0_two = jnp.where((tk <= tq) & (tq - tk <= blk), 0.0, NEG_INF)
    qa1 = lax.broadcasted_iota(jnp.int32, (nh * blk, blk), 0) % blk
    kk1 = lax.broadcasted_iota(jnp.int32, (nh * blk, blk), 1)
    bias0_one = jnp.where(n_res * (kk1 % sub) + kk1 // sub <= n_res * (qa1 % sub) + qa1 // sub,
                          0.0, NEG_INF)

    def slabs_of(ref, start, n, lanes):
        return lambda: jnp.concatenate(
            [ref[0, pl.ds(r * slab + start, n), lanes] for r in range(n_res)], axis=0)

    for n in range(n_blocks):
        k0, nk, bias = (0, sub, bias0_one) if n == 0 else ((n - 1) * sub, 2 * sub, bias0_two)
        dst = [(slice(r * sub, (r + 1) * sub), pl.ds(r * slab + n * sub, sub)) for r in range(n_res)]
        items.append((slabs_of(d0_ref, n * sub, sub, q_l), slabs_of(d0_ref, k0, nk, k_l), bias,
                      slabs_of(d0_ref, k0, nk, v_l), dst, False))

    def gather_of(start, chunks):
        return lambda: jnp.concatenate(
            [d2_ref[0, c, pl.ds(start, blk, stride=n_res), :].astype(BF16) for c in chunks], axis=1)

    per_part = GROUP_W // LANES
    for r16 in range(n_blocks):
        start = (r16 % n_res) * slab + r16 // n_res
        dst = [(whole, pl.ds(start, blk, stride=n_res))]
        items.append((gather_of(start, range(0, per_part)),
                      gather_of(start, range(per_part, 2 * per_part)), bias_one,
                      gather_of(start, range(2 * per_part, 3 * per_part)), dst, False))

    nxt = logits(*items[0][0:3])
    for i, (_, _, _, load_v, dst, first_write) in enumerate(items):
        cur = nxt
        if i + 1 < len(items):
            nxt = logits(*items[i + 1][0:3])
        put(weigh(*cur, load_v), dst, first_write)

    def finish(res, _):
        for pos in range(per_slab):
            r0 = _aligned(res * slab + pos * blk, blk)
            for c in range(GROUP_W // LANES):
                nat_scr[c, pl.ds(n_res * pos * blk + res, blk, stride=n_res), :] = (
                    acc_scr[c, pl.ds(r0, blk), :] / l_scr[c, pl.ds(r0, blk), :])
        return 0

    for res in range(n_res):
        finish(res, 0)

    def emit(i, _):
        r0 = _aligned(i * 2 * blk, 2 * blk)
        for c in range(GROUP_W // LANES):
            o_ref[0, pl.ds(r0, 2 * blk), c * LANES:(c + 1) * LANES] = nat_scr[c, pl.ds(r0, 2 * blk), :].astype(BF16)
        return 0

    for i in range(n_blocks // 2):
        emit(i, 0)


def _dilated(d0, d1, d2):
    B, S, W = d0.shape
    spec = pl.BlockSpec((1, S, W), lambda b: (b, 0, 0))
    spec2 = pl.BlockSpec((1,) + d2.shape[1:], lambda b: (b, 0, 0, 0))
    return pl.pallas_call(
        _dil_kernel,
        grid=(B,),
        in_specs=[spec, spec, spec2],
        out_specs=pl.BlockSpec((1, S, GROUP_W), lambda b: (b, 0, 0)),
        out_shape=jax.ShapeDtypeStruct((B, S, GROUP_W), BF16),
        scratch_shapes=[pltpu.VMEM((GROUP_W // LANES, S, LANES), F32)] * 4,
        compiler_params=pltpu.CompilerParams(
            dimension_semantics=("arbitrary",), vmem_limit_bytes=VMEM_LIMIT),
        name="dilated",
    )(d0, d1, d2)


def _stage_weights(pairs, stage, sem):
    cols = stage.shape[2]
    chunks = []
    for src, dst in pairs:
        for r0 in range(0, src.shape[0], STAGE_ROWS):
            n = min(STAGE_ROWS, src.shape[0] - r0)
            for c0 in range(0, src.shape[1], cols):
                chunks.append((src, dst, r0, n, c0))

    def copy(k):
        src, _, r0, n, c0 = chunks[k]
        return pltpu.make_async_copy(src.at[r0:r0 + n, c0:c0 + cols], stage.at[k % 2, 0:n],
                                     sem.at[k % 2])

    copy(0).start()
    for k, (_, dst, r0, n, c0) in enumerate(chunks):
        if k + 1 < len(chunks):
            copy(k + 1).start()
        copy(k).wait()
        dst[r0:r0 + n, c0:c0 + cols] = stage[k % 2, 0:n].astype(BF16)


def _outmlp_kernel(x_ref, oa_ref, ob_ref, gate_ref, wa_hbm, wb_hbm, wo_hbm, g2_ref,
                   wu_hbm, wdn_hbm, g3_ref, o_ref,
                   wa_ref, wb_ref, wo_ref, wu_ref, wdn_ref, stage, sem, *, final_norm):
    d = x_ref.shape[1]

    @pl.when(pl.program_id(0) == 0)
    def _load_weights():
        _stage_weights([(wa_hbm, wa_ref), (wb_hbm, wb_ref), (wo_hbm, wo_ref),
                        (wu_hbm, wu_ref), (wdn_hbm, wdn_ref)], stage, sem)

    ya = jnp.dot(oa_ref[...], wa_ref[...], preferred_element_type=F32)
    yb = jnp.dot(ob_ref[...], wb_ref[...], preferred_element_type=F32)
    mixed = gate_ref[:, 0:d].astype(F32) * ya + gate_ref[:, d:2 * d].astype(F32) * yb
    x1 = x_ref[...] + jnp.dot(mixed.astype(BF16), wo_ref[...], preferred_element_type=F32)
    h = (x1 * g2_ref[...]).astype(BF16)
    inv = lax.rsqrt(jnp.mean(x1 * x1, axis=-1, keepdims=True) + EPS)
    d_ff = wu_ref.shape[1]
    mlp = None
    for c in range(d_ff // FF_CHUNK):
        u = jnp.dot(h, wu_ref[:, c * FF_CHUNK:(c + 1) * FF_CHUNK], preferred_element_type=F32)
        a = jnp.square(jnp.maximum(u, 0.0)).astype(BF16)
        t = jnp.dot(a, wdn_ref[c * FF_CHUNK:(c + 1) * FF_CHUNK, :], preferred_element_type=F32)
        mlp = t if mlp is None else mlp + t
    x2 = x1 + (inv * inv) * mlp
    o_ref[...] = _rms(x2, g3_ref[...]) if final_norm else x2


def _outmlp(x, oa, ob, gates, w_a, w_b, w_o, g2, w_u, w_dn, g3, final_norm):
    N, D = x.shape
    tm = TM_OUT
    tok = lambda w: pl.BlockSpec((tm, w), lambda i: (i, 0))
    hbm = pl.BlockSpec(memory_space=pl.ANY)
    weights = (w_a, w_b, w_o, w_u, w_dn)
    return pl.pallas_call(
        functools.partial(_outmlp_kernel, final_norm=final_norm),
        grid=(N // tm,),
        in_specs=[tok(D), tok(oa.shape[1]), tok(ob.shape[1]), tok(gates.shape[1]),
                  hbm, hbm, hbm, _resident(g2.shape), hbm, hbm, _resident(g3.shape)],
        out_specs=tok(D),
        out_shape=jax.ShapeDtypeStruct((N, D), F32),
        scratch_shapes=[pltpu.VMEM(w.shape, BF16) for w in weights]
        + [pltpu.VMEM((2, STAGE_ROWS, D), F32), pltpu.SemaphoreType.DMA((2,))],
        compiler_params=pltpu.CompilerParams(
            dimension_semantics=("arbitrary",), vmem_limit_bytes=VMEM_LIMIT),
        name="outmlp",
    )(x, oa, ob, gates, w_a, w_b, w_o, g2, w_u, w_dn, g3)


def _rope_tables(S):
    inv_freq = jnp.power(jnp.float32(ROPE_THETA),
                         -jnp.arange(ROPE_HALF, dtype=F32) * 2.0 / ROPE_DIM)
    ang = jnp.arange(S).astype(F32)[:, None] * inv_freq[None, :]
    cos, sin = jnp.cos(ang), jnp.sin(ang)
    rest = HEAD_DIM - ROPE_DIM
    one = jnp.ones((S, rest), F32)
    zero = lambda w: jnp.zeros((S, w), F32)
    cos_h = jnp.concatenate([cos, cos, one], axis=1)
    slo_h = jnp.concatenate([-sin, zero(ROPE_HALF + rest)], axis=1)
    shi_h = jnp.concatenate([zero(ROPE_HALF), sin, zero(rest)], axis=1)
    rep = lambda t: jnp.tile(t, (1, 128 // HEAD_DIM))
    return rep(cos_h), rep(slo_h), rep(shi_h)


def kernel(x, norm_attn_g, w_in, b_forget, w_branch_a, w_branch_b, w_out, norm_mlp_g, w_up,
           w_down, norm_final_g):
    B, S, D = x.shape
    depth = w_in.shape[0]
    assert S % (16 * DIL_BLK) == 0 and S // 16 == DIL_BLK, "sequence length fixed by the dilation groups"
    cos, slo, shi = _rope_tables(S)
    sizes = (FOX_W, FOX_W, FOX_W, FOX_HEADS, DIL_W, DIL_W, DIL_W, D, D)
    offs = np.concatenate([[0], np.cumsum(sizes)])
    for l in range(depth):
        wt = jnp.swapaxes(w_in[l], 0, 1)
        w_all = wt.astype(F32)
        pad = -(3 * FOX_HEADS) % 16
        w_vf = jnp.concatenate(
            [wt[offs[2]:offs[3]], jnp.repeat(wt[offs[3]:offs[4]], 3, axis=0),
             jnp.zeros((pad, D), wt.dtype)], axis=0).astype(BF16)
        b_f = jnp.pad(jnp.repeat(b_forget[l].astype(F32), 3), (0, pad))[:, None]

        qk, vt, kx, d0, d1, d2, gates = _inproj(
            x, norm_attn_g[l][None, :], w_all, w_vf, b_f, cos, slo, shi)
        oa = _fox(qk, vt, kx)
        ob = _dilated(d0.reshape(B, S, -1), d1.reshape(B, S, -1),
                      d2.reshape(B, d2.shape[1], S, LANES))
        y = _outmlp(
            x.reshape(B * S, D), oa.reshape(B * S, -1), ob.reshape(B * S, -1),
            gates.reshape(B * S, -1),
            w_branch_a[l], w_branch_b[l], w_out[l], norm_mlp_g[l][None, :], w_up[l], w_down[l],
            norm_final_g[None, :], final_norm=(l == depth - 1))
        x = y.reshape(B, S, D)
    return x
```

```python
import functools

import numpy as np
import jax
import jax.numpy as jnp
from jax import lax
from jax.experimental import pallas as pl
from jax.experimental.pallas import tpu as pltpu

F32 = jnp.float32
BF16 = jnp.bfloat16

HEAD_DIM = 64
FOX_HEADS = 8
FOX_W = FOX_HEADS * HEAD_DIM
DIL_GROUPS = ((128, 1), (512, 4), (2048, 16))
DIL_HEADS_PER_GROUP = 4
GROUP_W = DIL_HEADS_PER_GROUP * HEAD_DIM
N_GROUPS = len(DIL_GROUPS)
DIL_W = N_GROUPS * GROUP_W
ROPE_THETA = 500000.0
ROPE_DIM = HEAD_DIM // 4
ROPE_HALF = ROPE_DIM // 2
EPS = 1e-6
NEG_INF = -1e30
Q_SCALE = HEAD_DIM ** -0.5
LOG2E = 1.4426950408889634

LANES = 128
DIL_BLK = 128
TM_IN = 512
TM_OUT = 512
TQ = 256
TK = 256
FOX_PAIRS_PER_STEP = 2
FF_CHUNK = 1024
STAGE_ROWS = 512
VMEM_LIMIT = 56 * 1024 * 1024

_NT = (((1,), (1,)), ((), ()))


def _resident(shape):
    nd = len(shape)
    return pl.BlockSpec(shape, lambda *_: (0,) * nd, pipeline_mode=pl.Buffered(1))


def _aligned(x, m):
    return x if isinstance(x, int) else pl.multiple_of(x, m)


def _rms(x, g):
    ms = jnp.mean(x * x, axis=-1, keepdims=True)
    return x * lax.rsqrt(ms + EPS) * g


def _rope(x, cos, sin_lo, sin_hi):
    w = x.shape[1]
    up = pltpu.roll(x, w - ROPE_HALF, 1)
    dn = pltpu.roll(x, ROPE_HALF, 1)
    return x * cos + up * sin_lo + dn * sin_hi


def _split3(x):
    hi = x.astype(BF16)
    r1 = x - hi.astype(F32)
    mid = r1.astype(BF16)
    lo = (r1 - mid.astype(F32)).astype(BF16)
    return hi, mid, lo


def _inproj_kernel(x_ref, g_ref, w_hbm, wvf_ref, bf_ref,
                   cos_ref, slo_ref, shi_ref,
                   qk_ref, vt_ref, kx_ref, d0_ref, d1_ref, d2_ref, gate_ref,
                   perm_scr, carry_scr, wqk_ref, wt_ref, stage, sem):
    tm = x_ref.shape[1]
    d = x_ref.shape[2]

    @pl.when((pl.program_id(0) == 0) & (pl.program_id(1) == 0))
    def _load_weights():
        tail0 = 3 * FOX_W + FOX_HEADS
        _stage_weights([(w_hbm.at[0:2 * FOX_W], wqk_ref),
                        (w_hbm.at[tail0:tail0 + wt_ref.shape[0]], wt_ref)], stage, sem)

    @pl.when(pl.program_id(1) == 0)
    def _reset():
        carry_scr[...] = jnp.zeros_like(carry_scr)

    xt = x_ref[0]
    xg = xt * g_ref[...]
    h = xg.astype(BF16)
    inv = lax.rsqrt(jnp.mean(xt * xt, axis=-1, keepdims=True) + EPS)
    inv_b = jnp.broadcast_to(inv, (tm, LANES))
    inv_row = inv_b.T[0:1, :]
    proj = lambda ref, lo, hi: inv * lax.dot_general(h, ref[lo:hi, :], _NT,
                                                     preferred_element_type=F32)

    n_res = 4
    sub_t = tm // n_res

    def by_residue(ref, start=0):
        return jnp.concatenate(
            [ref[pl.ds(start + res, sub_t, stride=n_res), :] for res in range(n_res)], axis=0)

    for c in range(d // LANES):
        perm_scr[c] = xg[:, c * LANES:(c + 1) * LANES]
    perm_scr[d // LANES] = inv_b
    h4 = jnp.concatenate([by_residue(perm_scr.at[c]) for c in range(d // LANES)],
                         axis=1).astype(BF16)
    inv4 = by_residue(perm_scr.at[d // LANES])[:, 0:1]
    proj4 = lambda lo, hi: inv4 * lax.dot_general(h4, wt_ref[lo:hi, :], _NT,
                                                  preferred_element_type=F32)
    g_off = 3 * DIL_W
    lane = lax.broadcasted_iota(jnp.int32, (1, LANES), 1)

    n_f = wvf_ref.shape[0] - FOX_W

    def v_and_forget_logits():
        vf = lax.dot_general(wvf_ref[...], h, _NT, preferred_element_type=F32) * inv_row
        vt_ref[0] = vf[0:FOX_W].astype(BF16)
        z = vf[FOX_W:FOX_W + n_f] + bf_ref[...]
        row = lax.broadcasted_iota(jnp.int32, (n_f, 1), 0)
        log_f = jnp.where(row < 3 * FOX_HEADS,
                          -(jnp.maximum(-z, 0.0) + jnp.log1p(jnp.exp(-jnp.abs(z)))), 0.0)
        return _split3(log_f)

    def forget_cumsum(parts):
        row = lax.broadcasted_iota(jnp.int32, (tm, tm), 0)
        col = lax.broadcasted_iota(jnp.int32, (tm, tm), 1)
        tri = jnp.where(row <= col, 1.0, 0.0).astype(BF16)
        cum = carry_scr[:, 0:1]
        for part in parts:
            cum = cum + jnp.dot(part, tri, preferred_element_type=F32)
        carry_scr[...] = jnp.broadcast_to(cum[:, tm - 1:tm], carry_scr.shape)
        nf = jnp.concatenate([-LOG2E * cum, jnp.zeros((LANES - n_f, tm), F32)], axis=0).T
        hi, mid, lo = _split3(nf)
        kx_ref[0] = jnp.where(lane % 3 == 0, hi, jnp.where(lane % 3 == 1, mid, lo))

    def mixer_a_qk():
        qk_ref[0, :, 0:FOX_W] = (proj(wqk_ref, 0, FOX_W) * (Q_SCALE * LOG2E)).astype(BF16)
        qk_ref[0, :, FOX_W:2 * FOX_W] = proj(wqk_ref, FOX_W, 2 * FOX_W).astype(BF16)

    outs = (d0_ref, d1_ref, d2_ref)

    def mixer_b(g):
        dil = DIL_GROUPS[g][1]
        t0 = pl.multiple_of(pl.program_id(1) * tm, tm)
        cos, slo, shi = (by_residue(t, t0) for t in (cos_ref, slo_ref, shi_ref))
        part = lambda t: proj4(t * DIL_W + g * GROUP_W, t * DIL_W + (g + 1) * GROUP_W)
        q, k, v = part(0), part(1), part(2)
        halves = []
        for c in range(GROUP_W // LANES):
            halves.append(_rope(q[:, c * LANES:(c + 1) * LANES], cos, slo, shi) * (Q_SCALE * LOG2E))
        for c in range(GROUP_W // LANES):
            halves.append(_rope(k[:, c * LANES:(c + 1) * LANES], cos, slo, shi))
        halves += [v[:, c * LANES:(c + 1) * LANES] for c in range(GROUP_W // LANES)]
        for res in range(n_res):
            for c, t in enumerate(halves):
                rows_c = t[res * sub_t:(res + 1) * sub_t]
                if dil <= n_res:
                    outs[g][0, res, :, c * LANES:(c + 1) * LANES] = rows_c.astype(BF16)
                else:
                    outs[g][0, c, res] = rows_c

    def gates(c):
        r = proj(wt_ref, g_off + c * 512, g_off + (c + 1) * 512)
        gate_ref[0, :, c * 512:(c + 1) * 512] = jax.nn.sigmoid(r).astype(BF16)

    parts = v_and_forget_logits()
    mixer_b(0)
    gates(0)
    gates(1)
    forget_cumsum(parts)
    mixer_b(1)
    gates(2)
    gates(3)
    mixer_b(2)
    mixer_a_qk()


def _inproj(x, g, w_all, w_vf, b_f, cos, slo, shi):
    B, S, D = x.shape
    n_tail = 3 * DIL_W + 2 * D
    tm = TM_IN
    grid = (B, S // tm)
    tok = lambda w: pl.BlockSpec((1, tm, w), lambda b, i: (b, i, 0))
    tab = _resident(cos.shape)
    out_shape = (
        jax.ShapeDtypeStruct((B, S, 2 * FOX_W), BF16),
        jax.ShapeDtypeStruct((B, FOX_W, S), BF16),
        jax.ShapeDtypeStruct((B, S, LANES), BF16),
        jax.ShapeDtypeStruct((B, 4, S // 4, 3 * GROUP_W), BF16),
        jax.ShapeDtypeStruct((B, 4, S // 4, 3 * GROUP_W), BF16),
        jax.ShapeDtypeStruct((B, 3 * GROUP_W // LANES, 4, S // 4, LANES), F32),
        jax.ShapeDtypeStruct((B, S, 2 * D), BF16),
    )
    out_specs = (
        tok(2 * FOX_W),
        pl.BlockSpec((1, FOX_W, tm), lambda b, i: (b, 0, i)),
        tok(LANES),
        pl.BlockSpec((1, 4, tm // 4, 3 * GROUP_W), lambda b, i: (b, 0, i, 0)),
        pl.BlockSpec((1, 4, tm // 4, 3 * GROUP_W), lambda b, i: (b, 0, i, 0)),
        pl.BlockSpec((1, 3 * GROUP_W // LANES, 4, tm // 4, LANES), lambda b, i: (b, 0, 0, i, 0)),
        tok(2 * D),
    )
    in_specs = [
        tok(D), _resident(g.shape), pl.BlockSpec(memory_space=pl.ANY), _resident(w_vf.shape),
        _resident(b_f.shape), tab, tab, tab,
    ]
    n_f = w_vf.shape[0] - FOX_W
    return pl.pallas_call(
        _inproj_kernel,
        grid=grid,
        in_specs=in_specs,
        out_specs=out_specs,
        out_shape=out_shape,
        scratch_shapes=[pltpu.VMEM((D // LANES + 1, tm, LANES), F32),
                        pltpu.VMEM((n_f, LANES), F32),
                        pltpu.VMEM((2 * FOX_W, D), BF16), pltpu.VMEM((n_tail, D), BF16),
                        pltpu.VMEM((2, STAGE_ROWS, D), F32), pltpu.SemaphoreType.DMA((2,))],
        compiler_params=pltpu.CompilerParams(
            dimension_semantics=("arbitrary", "arbitrary"), vmem_limit_bytes=VMEM_LIMIT),
        name="inproj",
    )(x, g, w_all, w_vf, b_f, cos, slo, shi)


def _fox_kernel(q_ref, k_ref, kx_ref, vt_ref, o_ref):
    S = q_ref.shape[1]
    pw = 2 * HEAD_DIM
    pairs = q_ref.shape[2] // pw
    step = pl.program_id(1)
    lane = lax.broadcasted_iota(jnp.int32, (1, pw), 1)
    first = lane < HEAD_DIM
    zeros = jnp.zeros((TQ, pw), BF16)

    def sel(head):
        return jnp.broadcast_to(
            jnp.where((lane >= 3 * head) & (lane < 3 * head + 3), 1.0, 0.0), (TQ, pw)).astype(BF16)

    hq = TQ // 2
    key = lax.broadcasted_iota(jnp.int32, (hq, TQ), 0)
    qry = lax.broadcasted_iota(jnp.int32, (hq, TQ), 1) % hq
    causal = key <= qry

    def colmax(parts):
        m = jnp.max(parts[0], axis=0, keepdims=True)
        for t in parts[1:]:
            m = jnp.maximum(m, jnp.max(t, axis=0, keepdims=True))
        return m

    def scores(pair, qi):
        lanes = slice(pair * pw, (pair + 1) * pw)
        head0 = 2 * (pairs * step + pair)
        q2 = q_ref[0, qi * TQ:(qi + 1) * TQ, lanes]
        qa = jnp.concatenate([jnp.where(first, q2, zeros), sel(head0)], axis=1)
        qb = jnp.concatenate([jnp.where(first, zeros, q2), sel(head0 + 1)], axis=1)
        qmix = jnp.concatenate([qa[0:hq], qb[0:hq], qa[hq:TQ], qb[hq:TQ]], axis=0)
        n_k = (qi + 1) * TK
        n_lo = n_k - hq
        kaug = jnp.concatenate([k_ref[0, 0:n_k, lanes], kx_ref[0, 0:n_k, :]], axis=1)
        s = lax.dot_general(kaug[0:n_lo], qmix, _NT, preferred_element_type=F32)
        s_hi = lax.dot_general(kaug[n_lo:n_k], qmix[TQ:2 * TQ], _NT, preferred_element_type=F32)
        full = [s[c * TK:(c + 1) * TK] for c in range(qi)]
        diag = s[qi * TK:n_lo]
        lo = [t[:, 0:TQ] for t in full] + [jnp.where(causal, diag[:, 0:TQ], NEG_INF)]
        hi = ([t[:, TQ:2 * TQ] for t in full] + [diag[:, TQ:2 * TQ]]
              + [jnp.where(causal, s_hi, NEG_INF)])
        return (lo, colmax(lo)), (hi, colmax(hi))

    def outputs(pair, qi, lo, hi):
        for half, (chunks, m) in enumerate((lo, hi)):
            n = (qi + 1) * TK - (hq if half == 0 else 0)
            p = [jnp.exp2(s - m) for s in chunks]
            l = jnp.sum(p[0], axis=0, keepdims=True)
            for t in p[1:]:
                l = l + jnp.sum(t, axis=0, keepdims=True)
            pt = jnp.concatenate([t.astype(BF16) for t in p], axis=0)
            acc = jnp.dot(vt_ref[0, pair * pw:(pair + 1) * pw, 0:n], pt,
                          preferred_element_type=F32)
            ot = (acc / l).T
            r0 = qi * TQ + half * hq
            o_ref[0, r0:r0 + hq, pair * pw:(pair + 1) * pw] = (
                jnp.where(first, ot[0:hq], ot[hq:TQ]).astype(BF16))

    items = [(pair, qi) for pair in range(pairs) for qi in range(S // TQ)]
    nxt = scores(*items[0])
    for i, item in enumerate(items):
        cur = nxt
        if i + 1 < len(items):
            nxt = scores(*items[i + 1])
        outputs(*item, *cur)


def _fox(qk, vt, kx):
    B, S, _ = qk.shape
    pairs = FOX_PAIRS_PER_STEP
    steps = FOX_HEADS // (2 * pairs)
    w = pairs * 2 * HEAD_DIM
    spec = lambda off: pl.BlockSpec((1, S, w), lambda b, p: (b, 0, off + p))
    return pl.pallas_call(
        _fox_kernel,
        grid=(B, steps),
        in_specs=[spec(0), spec(steps),
                  pl.BlockSpec((1, S, LANES), lambda b, p: (b, 0, 0)),
                  pl.BlockSpec((1, w, S), lambda b, p: (b, p, 0))],
        out_specs=pl.BlockSpec((1, S, w), lambda b, p: (b, 0, p)),
        out_shape=jax.ShapeDtypeStruct((B, S, FOX_W), BF16),
        compiler_params=pltpu.CompilerParams(
            dimension_semantics=("arbitrary", "arbitrary"), vmem_limit_bytes=VMEM_LIMIT),
        name="fox",
    )(qk, qk, kx, vt)


def _dil_kernel(d0_ref, d1_ref, d2_ref, o_ref, acc_scr, m_scr, l_scr, nat_scr):
    S = d0_ref.shape[1]
    nh = DIL_HEADS_PER_GROUP
    blk = DIL_BLK
    n_blocks = S // blk
    lane_head = lax.broadcasted_iota(jnp.int32, (1, GROUP_W), 1) // HEAD_DIM
    a2 = lax.broadcasted_iota(jnp.int32, (nh * blk, 2 * blk), 0) % blk
    k2 = lax.broadcasted_iota(jnp.int32, (nh * blk, 2 * blk), 1)
    bias_two = jnp.where((k2 >= a2) & (k2 <= a2 + blk), 0.0, NEG_INF)
    a1 = lax.broadcasted_iota(jnp.int32, (nh * blk, blk), 0) % blk
    k1 = lax.broadcasted_iota(jnp.int32, (nh * blk, blk), 1)
    bias_one = jnp.where(k1 <= a1, 0.0, NEG_INF)

    def assemble(x):
        out = x[(nh - 1) * blk:nh * blk]
        for h in range(nh - 2, -1, -1):
            out = jnp.where(lane_head == h, x[h * blk:(h + 1) * blk], out)
        return out

    def logits(load_q, load_k, bias):
        qs = load_q()
        zero = jnp.zeros_like(qs)
        qstack = jnp.concatenate(
            [jnp.where(lane_head == h, qs, zero) for h in range(nh)], axis=0)
        s = lax.dot_general(qstack, load_k(), _NT, preferred_element_type=F32) + bias
        return s, jnp.max(s, axis=1, keepdims=True)

    def weigh(s, m, load_v):
        p = jnp.exp2(s - m)
        l = jnp.sum(p, axis=1, keepdims=True)
        pv = jnp.dot(p.astype(BF16), load_v(), preferred_element_type=F32)
        return (assemble(pv), assemble(jnp.broadcast_to(m, pv.shape)),
                assemble(jnp.broadcast_to(l, pv.shape)))

    def put(res, pieces, first):
        acc_b, m_b, l_b = res
        for rows, idx in pieces:
            for c in range(GROUP_W // LANES):
                lanes = slice(c * LANES, (c + 1) * LANES)
                if first:
                    acc_scr[c, idx, :] = acc_b[rows, lanes]
                    m_scr[c, idx, :] = m_b[rows, lanes]
                    l_scr[c, idx, :] = l_b[rows, lanes]
                else:
                    m_o = m_scr[c, idx, :]
                    m_n = jnp.maximum(m_o, m_b[rows, lanes])
                    e_o = jnp.exp2(m_o - m_n)
                    e_b = jnp.exp2(m_b[rows, lanes] - m_n)
                    acc_scr[c, idx, :] = acc_scr[c, idx, :] * e_o + acc_b[rows, lanes] * e_b
                    l_scr[c, idx, :] = l_scr[c, idx, :] * e_o + l_b[rows, lanes] * e_b
                    m_scr[c, idx, :] = m_n

    q_l, k_l, v_l = (slice(i * GROUP_W, (i + 1) * GROUP_W) for i in range(3))
    whole = slice(0, blk)
    n_res = 4
    slab = S // n_res
    per_slab = slab // blk

    items = []
    rows_of = lambda ref, r0, n, lanes: (lambda: ref[0, pl.ds(r0, n), lanes])

    for res in range(n_res):
        for pos in range(per_slab):
            r0 = res * slab + pos * blk
            k0, nk, bias = (r0, blk, bias_one) if pos == 0 else (r0 - blk, 2 * blk, bias_two)
            items.append((rows_of(d1_ref, r0, blk, q_l), rows_of(d1_ref, k0, nk, k_l), bias,
                          rows_of(d1_ref, k0, nk, v_l), [(whole, pl.ds(r0, blk))], True))

    sub = blk // n_res
    qa = lax.broadcasted_iota(jnp.int32, (nh * blk, 2 * blk), 0) % blk
    kk = lax.broadcasted_iota(jnp.int32, (nh * blk, 2 * blk), 1)
    tq = blk + n_res * (qa % sub) + qa // sub
    tk = n_res * (kk % (2 * sub)) + kk // (2 * sub)
    bias0_two = jnp.where((tk <= tq) & (tq - tk <= blk), 0.0, NEG_INF)
    qa1 = lax.broadcasted_iota(jnp.int32, (nh * blk, blk), 0) % blk
    kk1 = lax.broadcasted_iota(jnp.int32, (nh * blk, blk), 1)
    bias0_one = jnp.where(n_res * (kk1 % sub) + kk1 // sub <= n_res * (qa1 % sub) + qa1 // sub,
                          0.0, NEG_INF)

    def slabs_of(ref, start, n, lanes):
        return lambda: jnp.concatenate(
            [ref[0, pl.ds(r * slab + start, n), lanes] for r in range(n_res)], axis=0)

    for n in range(n_blocks):
        k0, nk, bias = (0, sub, bias0_one) if n == 0 else ((n - 1) * sub, 2 * sub, bias0_two)
        dst = [(slice(r * sub, (r + 1) * sub), pl.ds(r * slab + n * sub, sub)) for r in range(n_res)]
        items.append((slabs_of(d0_ref, n * sub, sub, q_l), slabs_of(d0_ref, k0, nk, k_l), bias,
                      slabs_of(d0_ref, k0, nk, v_l), dst, False))

    def gather_of(start, chunks):
        return lambda: jnp.concatenate(
            [d2_ref[0, c, pl.ds(start, blk, stride=n_res), :].astype(BF16) for c in chunks], axis=1)

    per_part = GROUP_W // LANES
    for r16 in range(n_blocks):
        start = (r16 % n_res) * slab + r16 // n_res
        dst = [(whole, pl.ds(start, blk, stride=n_res))]
        items.append((gather_of(start, range(0, per_part)),
                      gather_of(start, range(per_part, 2 * per_part)), bias_one,
                      gather_of(start, range(2 * per_part, 3 * per_part)), dst, False))

    nxt = logits(*items[0][0:3])
    for i, (_, _, _, load_v, dst, first_write) in enumerate(items):
        cur = nxt
        if i + 1 < len(items):
            nxt = logits(*items[i + 1][0:3])
        put(weigh(*cur, load_v), dst, first_write)

    def finish(res, _):
        for pos in range(per_slab):
            r0 = _aligned(res * slab + pos * blk, blk)
            for c in range(GROUP_W // LANES):
                nat_scr[c, pl.ds(n_res * pos * blk + res, blk, stride=n_res), :] = (
                    acc_scr[c, pl.ds(r0, blk), :] / l_scr[c, pl.ds(r0, blk), :])
        return 0

    for res in range(n_res):
        finish(res, 0)

    def emit(i, _):
        r0 = _aligned(i * 2 * blk, 2 * blk)
        for c in range(GROUP_W // LANES):
            o_ref[0, pl.ds(r0, 2 * blk), c * LANES:(c + 1) * LANES] = nat_scr[c, pl.ds(r0, 2 * blk), :].astype(BF16)
        return 0

    for i in range(n_blocks // 2):
        emit(i, 0)


def _dilated(d0, d1, d2):
    B, S, W = d0.shape
    spec = pl.BlockSpec((1, S, W), lambda b: (b, 0, 0))
    spec2 = pl.BlockSpec((1,) + d2.shape[1:], lambda b: (b, 0, 0, 0))
    return pl.pallas_call(
        _dil_kernel,
        grid=(B,),
        in_specs=[spec, spec, spec2],
        out_specs=pl.BlockSpec((1, S, GROUP_W), lambda b: (b, 0, 0)),
        out_shape=jax.ShapeDtypeStruct((B, S, GROUP_W), BF16),
        scratch_shapes=[pltpu.VMEM((GROUP_W // LANES, S, LANES), F32)] * 4,
        compiler_params=pltpu.CompilerParams(
            dimension_semantics=("arbitrary",), vmem_limit_bytes=VMEM_LIMIT),
        name="dilated",
    )(d0, d1, d2)


def _stage_weights(pairs, stage, sem):
    cols = stage.shape[2]
    chunks = []
    for src, dst in pairs:
        for r0 in range(0, src.shape[0], STAGE_ROWS):
            n = min(STAGE_ROWS, src.shape[0] - r0)
            for c0 in range(0, src.shape[1], cols):
                chunks.append((src, dst, r0, n, c0))

    def copy(k):
        src, _, r0, n, c0 = chunks[k]
        return pltpu.make_async_copy(src.at[r0:r0 + n, c0:c0 + cols], stage.at[k % 2, 0:n],
                                     sem.at[k % 2])

    copy(0).start()
    for k, (_, dst, r0, n, c0) in enumerate(chunks):
        if k + 1 < len(chunks):
            copy(k + 1).start()
        copy(k).wait()
        dst[r0:r0 + n, c0:c0 + cols] = stage[k % 2, 0:n].astype(BF16)


class _LazyStager:
    def __init__(self, stage, sem):
        self.stage, self.sem = stage, sem
        self.chunks, self.started, self.done = [], 0, 0

    def queue(self, src, dst, r0, n, c0):
        self.chunks.append((src, dst, r0, n, c0))
        return len(self.chunks)

    def _copy(self, k):
        src, _, r0, n, c0 = self.chunks[k]
        cols = self.stage.shape[2]
        return pltpu.make_async_copy(src.at[r0:r0 + n, c0:c0 + cols],
                                     self.stage.at[k % 2, 0:n], self.sem.at[k % 2])

    def _prefetch(self):
        while self.started < min(self.done + 2, len(self.chunks)):
            self._copy(self.started).start()
            self.started += 1

    def need(self, count):
        cols = self.stage.shape[2]
        while self.done < count:
            self._prefetch()
            k = self.done
            _, dst, r0, n, c0 = self.chunks[k]
            self._copy(k).wait()
            dst[r0:r0 + n, c0:c0 + cols] = self.stage[k % 2, 0:n].astype(BF16)
            self.done += 1
            self._prefetch()


def _outmlp_kernel(x_ref, oa_ref, ob_ref, gate_ref, wa_hbm, wb_hbm, wo_hbm, g2_ref,
                   wu_hbm, wdn_hbm, g3_ref, o_ref,
                   wa_ref, wb_ref, wo_ref, wu_ref, wdn_ref, stage, sem, *, final_norm):
    d = x_ref.shape[1]
    d_ff = wu_ref.shape[1]
    n_ff = d_ff // FF_CHUNK

    def body(need):
        need("branches")
        ya = jnp.dot(oa_ref[...], wa_ref[...], preferred_element_type=F32)
        yb = jnp.dot(ob_ref[...], wb_ref[...], preferred_element_type=F32)
        mixed = gate_ref[:, 0:d].astype(F32) * ya + gate_ref[:, d:2 * d].astype(F32) * yb
        need("out")
        x1 = x_ref[...] + jnp.dot(mixed.astype(BF16), wo_ref[...], preferred_element_type=F32)
        h = (x1 * g2_ref[...]).astype(BF16)
        inv = lax.rsqrt(jnp.mean(x1 * x1, axis=-1, keepdims=True) + EPS)
        mlp = None
        for c in range(n_ff):
            need(("up", c))
            u = jnp.dot(h, wu_ref[:, c * FF_CHUNK:(c + 1) * FF_CHUNK], preferred_element_type=F32)
            a = jnp.square(jnp.maximum(u, 0.0)).astype(BF16)
            need(("down", c))
            t = jnp.dot(a, wdn_ref[c * FF_CHUNK:(c + 1) * FF_CHUNK, :], preferred_element_type=F32)
            mlp = t if mlp is None else mlp + t
        x2 = x1 + (inv * inv) * mlp
        o_ref[...] = _rms(x2, g3_ref[...]) if final_norm else x2

    @pl.when(pl.program_id(0) == 0)
    def _first_step():
        stager = _LazyStager(stage, sem)
        ready = {}

        def queue_rows(src, dst, rows, c0=0):
            for r0 in range(rows.start, rows.stop, STAGE_ROWS):
                last = stager.queue(src, dst, r0, min(STAGE_ROWS, rows.stop - r0), c0)
            return last

        queue_rows(wa_hbm, wa_ref, range(0, wa_ref.shape[0]))
        ready["branches"] = queue_rows(wb_hbm, wb_ref, range(0, wb_ref.shape[0]))
        ready["out"] = queue_rows(wo_hbm, wo_ref, range(0, wo_ref.shape[0]))
        for c in range(n_ff):
            ready[("up", c)] = queue_rows(wu_hbm, wu_ref, range(0, wu_ref.shape[0]), c * FF_CHUNK)
            ready[("down", c)] = queue_rows(wdn_hbm, wdn_ref,
                                            range(c * FF_CHUNK, (c + 1) * FF_CHUNK))
        body(lambda tag: stager.need(ready[tag]))

    @pl.when(pl.program_id(0) != 0)
    def _other_steps():
        body(lambda tag: None)


def _outmlp(x, oa, ob, gates, w_a, w_b, w_o, g2, w_u, w_dn, g3, final_norm):
    N, D = x.shape
    tm = TM_OUT
    tok = lambda w: pl.BlockSpec((tm, w), lambda i: (i, 0))
    hbm = pl.BlockSpec(memory_space=pl.ANY)
    weights = (w_a, w_b, w_o, w_u, w_dn)
    return pl.pallas_call(
        functools.partial(_outmlp_kernel, final_norm=final_norm),
        grid=(N // tm,),
        in_specs=[tok(D), tok(oa.shape[1]), tok(ob.shape[1]), tok(gates.shape[1]),
                  hbm, hbm, hbm, _resident(g2.shape), hbm, hbm, _resident(g3.shape)],
        out_specs=tok(D),
        out_shape=jax.ShapeDtypeStruct((N, D), F32),
        scratch_shapes=[pltpu.VMEM(w.shape, BF16) for w in weights]
        + [pltpu.VMEM((2, STAGE_ROWS, D), F32), pltpu.SemaphoreType.DMA((2,))],
        compiler_params=pltpu.CompilerParams(
            dimension_semantics=("arbitrary",), vmem_limit_bytes=VMEM_LIMIT),
        name="outmlp",
    )(x, oa, ob, gates, w_a, w_b, w_o, g2, w_u, w_dn, g3)


def _rope_tables(S):
    inv_freq = jnp.power(jnp.float32(ROPE_THETA),
                         -jnp.arange(ROPE_HALF, dtype=F32) * 2.0 / ROPE_DIM)
    ang = jnp.arange(S).astype(F32)[:, None] * inv_freq[None, :]
    cos, sin = jnp.cos(ang), jnp.sin(ang)
    rest = HEAD_DIM - ROPE_DIM
    one = jnp.ones((S, rest), F32)
    zero = lambda w: jnp.zeros((S, w), F32)
    cos_h = jnp.concatenate([cos, cos, one], axis=1)
    slo_h = jnp.concatenate([-sin, zero(ROPE_HALF + rest)], axis=1)
    shi_h = jnp.concatenate([zero(ROPE_HALF), sin, zero(rest)], axis=1)
    rep = lambda t: jnp.tile(t, (1, 128 // HEAD_DIM))
    return rep(cos_h), rep(slo_h), rep(shi_h)


def kernel(x, norm_attn_g, w_in, b_forget, w_branch_a, w_branch_b, w_out, norm_mlp_g, w_up,
           w_down, norm_final_g):
    B, S, D = x.shape
    depth = w_in.shape[0]
    assert S % (16 * DIL_BLK) == 0 and S // 16 == DIL_BLK, "sequence length fixed by the dilation groups"
    cos, slo, shi = _rope_tables(S)
    sizes = (FOX_W, FOX_W, FOX_W, FOX_HEADS, DIL_W, DIL_W, DIL_W, D, D)
    offs = np.concatenate([[0], np.cumsum(sizes)])
    for l in range(depth):
        wt = jnp.swapaxes(w_in[l], 0, 1)
        w_all = wt.astype(F32)
        pad = -(3 * FOX_HEADS) % 16
        w_vf = jnp.concatenate(
            [wt[offs[2]:offs[3]], jnp.repeat(wt[offs[3]:offs[4]], 3, axis=0),
             jnp.zeros((pad, D), wt.dtype)], axis=0).astype(BF16)
        b_f = jnp.pad(jnp.repeat(b_forget[l].astype(F32), 3), (0, pad))[:, None]

        qk, vt, kx, d0, d1, d2, gates = _inproj(
            x, norm_attn_g[l][None, :], w_all, w_vf, b_f, cos, slo, shi)
        oa = _fox(qk, vt, kx)
        ob = _dilated(d0.reshape(B, S, -1), d1.reshape(B, S, -1),
                      d2.reshape(B, d2.shape[1], S, LANES))
        y = _outmlp(
            x.reshape(B * S, D), oa.reshape(B * S, -1), ob.reshape(B * S, -1),
            gates.reshape(B * S, -1),
            w_branch_a[l], w_branch_b[l], w_out[l], norm_mlp_g[l][None, :], w_up[l], w_down[l],
            norm_final_g[None, :], final_norm=(l == depth - 1))
        x = y.reshape(B, S, D)
    return x
```

```python
import functools

import numpy as np
import jax
import jax.numpy as jnp
from jax import lax
from jax.experimental import pallas as pl
from jax.experimental.pallas import tpu as pltpu

F32 = jnp.float32
BF16 = jnp.bfloat16

HEAD_DIM = 64
FOX_HEADS = 8
FOX_W = FOX_HEADS * HEAD_DIM
DIL_GROUPS = ((128, 1), (512, 4), (2048, 16))
DIL_HEADS_PER_GROUP = 4
GROUP_W = DIL_HEADS_PER_GROUP * HEAD_DIM
N_GROUPS = len(DIL_GROUPS)
DIL_W = N_GROUPS * GROUP_W
ROPE_THETA = 500000.0
ROPE_DIM = HEAD_DIM // 4
ROPE_HALF = ROPE_DIM // 2
EPS = 1e-6
NEG_INF = -1e30
Q_SCALE = HEAD_DIM ** -0.5
LOG2E = 1.4426950408889634

LANES = 128
DIL_BLK = 128
TM_IN = 512
TM_OUT = 512
TQ = 256
TK = 256
FOX_PAIRS_PER_STEP = 2
FF_CHUNK = 1024
STAGE_ROWS = 512
OUT_STAGE_SLOTS = 4
VMEM_LIMIT = 56 * 1024 * 1024

_NT = (((1,), (1,)), ((), ()))


def _resident(shape):
    nd = len(shape)
    return pl.BlockSpec(shape, lambda *_: (0,) * nd, pipeline_mode=pl.Buffered(1))


def _aligned(x, m):
    return x if isinstance(x, int) else pl.multiple_of(x, m)


def _rms(x, g):
    ms = jnp.mean(x * x, axis=-1, keepdims=True)
    return x * lax.rsqrt(ms + EPS) * g


def _rope(x, cos, sin_lo, sin_hi):
    w = x.shape[1]
    up = pltpu.roll(x, w - ROPE_HALF, 1)
    dn = pltpu.roll(x, ROPE_HALF, 1)
    return x * cos + up * sin_lo + dn * sin_hi


def _split3(x):
    hi = x.astype(BF16)
    r1 = x - hi.astype(F32)
    mid = r1.astype(BF16)
    lo = (r1 - mid.astype(F32)).astype(BF16)
    return hi, mid, lo


def _inproj_kernel(x_ref, g_ref, w_hbm, wvf_ref, bf_ref,
                   cos_ref, slo_ref, shi_ref,
                   qk_ref, vt_ref, kx_ref, d0_ref, d1_ref, d2_ref, gate_ref,
                   perm_scr, carry_scr, wqk_ref, wt_ref, stage, sem):
    tm = x_ref.shape[1]
    d = x_ref.shape[2]

    @pl.when((pl.program_id(0) == 0) & (pl.program_id(1) == 0))
    def _load_weights():
        tail0 = 3 * FOX_W + FOX_HEADS
        _stage_weights([(w_hbm.at[0:2 * FOX_W], wqk_ref),
                        (w_hbm.at[tail0:tail0 + wt_ref.shape[0]], wt_ref)], stage, sem)

    @pl.when(pl.program_id(1) == 0)
    def _reset():
        carry_scr[...] = jnp.zeros_like(carry_scr)

    xt = x_ref[0]
    xg = xt * g_ref[...]
    h = xg.astype(BF16)
    inv = lax.rsqrt(jnp.mean(xt * xt, axis=-1, keepdims=True) + EPS)
    inv_b = jnp.broadcast_to(inv, (tm, LANES))
    inv_row = inv_b.T[0:1, :]
    proj = lambda ref, lo, hi: inv * lax.dot_general(h, ref[lo:hi, :], _NT,
                                                     preferred_element_type=F32)

    n_res = 4
    sub_t = tm // n_res

    def by_residue(ref, start=0):
        return jnp.concatenate(
            [ref[pl.ds(start + res, sub_t, stride=n_res), :] for res in range(n_res)], axis=0)

    for c in range(d // LANES):
        perm_scr[c] = xg[:, c * LANES:(c + 1) * LANES]
    perm_scr[d // LANES] = inv_b
    h4 = jnp.concatenate([by_residue(perm_scr.at[c]) for c in range(d // LANES)],
                         axis=1).astype(BF16)
    inv4 = by_residue(perm_scr.at[d // LANES])[:, 0:1]
    proj4 = lambda lo, hi: inv4 * lax.dot_general(h4, wt_ref[lo:hi, :], _NT,
                                                  preferred_element_type=F32)
    g_off = 3 * DIL_W
    lane = lax.broadcasted_iota(jnp.int32, (1, LANES), 1)

    n_f = wvf_ref.shape[0] - FOX_W

    def v_and_forget_logits():
        vf = lax.dot_general(wvf_ref[...], h, _NT, preferred_element_type=F32) * inv_row
        vt_ref[0] = vf[0:FOX_W].astype(BF16)
        z = vf[FOX_W:FOX_W + n_f] + bf_ref[...]
        row = lax.broadcasted_iota(jnp.int32, (n_f, 1), 0)
        log_f = jnp.where(row < 3 * FOX_HEADS,
                          -(jnp.maximum(-z, 0.0) + jnp.log1p(jnp.exp(-jnp.abs(z)))), 0.0)
        return _split3(log_f)

    def forget_cumsum(parts):
        row = lax.broadcasted_iota(jnp.int32, (tm, tm), 0)
        col = lax.broadcasted_iota(jnp.int32, (tm, tm), 1)
        tri = jnp.where(row <= col, 1.0, 0.0).astype(BF16)
        cum = carry_scr[:, 0:1]
        for part in parts:
            cum = cum + jnp.dot(part, tri, preferred_element_type=F32)
        carry_scr[...] = jnp.broadcast_to(cum[:, tm - 1:tm], carry_scr.shape)
        nf = jnp.concatenate([-LOG2E * cum, jnp.zeros((LANES - n_f, tm), F32)], axis=0).T
        hi, mid, lo = _split3(nf)
        kx_ref[0] = jnp.where(lane % 3 == 0, hi, jnp.where(lane % 3 == 1, mid, lo))

    def mixer_a_qk():
        qk_ref[0, :, 0:FOX_W] = (proj(wqk_ref, 0, FOX_W) * (Q_SCALE * LOG2E)).astype(BF16)
        qk_ref[0, :, FOX_W:2 * FOX_W] = proj(wqk_ref, FOX_W, 2 * FOX_W).astype(BF16)

    outs = (d0_ref, d1_ref, d2_ref)

    def mixer_b(g):
        dil = DIL_GROUPS[g][1]
        t0 = pl.multiple_of(pl.program_id(1) * tm, tm)
        cos, slo, shi = (by_residue(t, t0) for t in (cos_ref, slo_ref, shi_ref))
        part = lambda t: proj4(t * DIL_W + g * GROUP_W, t * DIL_W + (g + 1) * GROUP_W)
        q, k, v = part(0), part(1), part(2)
        halves = []
        for c in range(GROUP_W // LANES):
            halves.append(_rope(q[:, c * LANES:(c + 1) * LANES], cos, slo, shi) * (Q_SCALE * LOG2E))
        for c in range(GROUP_W // LANES):
            halves.append(_rope(k[:, c * LANES:(c + 1) * LANES], cos, slo, shi))
        halves += [v[:, c * LANES:(c + 1) * LANES] for c in range(GROUP_W // LANES)]
        for res in range(n_res):
            for c, t in enumerate(halves):
                rows_c = t[res * sub_t:(res + 1) * sub_t]
                if dil <= n_res:
                    outs[g][0, res, :, c * LANES:(c + 1) * LANES] = rows_c.astype(BF16)
                else:
                    outs[g][0, c, res] = rows_c

    def gates(c):
        r = proj(wt_ref, g_off + c * 512, g_off + (c + 1) * 512)
        gate_ref[0, :, c * 512:(c + 1) * 512] = jax.nn.sigmoid(r).astype(BF16)

    parts = v_and_forget_logits()
    mixer_b(0)
    gates(0)
    gates(1)
    forget_cumsum(parts)
    mixer_b(1)
    gates(2)
    gates(3)
    mixer_b(2)
    mixer_a_qk()


def _inproj(x, g, w_all, w_vf, b_f, cos, slo, shi):
    B, S, D = x.shape
    n_tail = 3 * DIL_W + 2 * D
    tm = TM_IN
    grid = (B, S // tm)
    tok = lambda w: pl.BlockSpec((1, tm, w), lambda b, i: (b, i, 0))
    tab = _resident(cos.shape)
    out_shape = (
        jax.ShapeDtypeStruct((B, S, 2 * FOX_W), BF16),
        jax.ShapeDtypeStruct((B, FOX_W, S), BF16),
        jax.ShapeDtypeStruct((B, S, LANES), BF16),
        jax.ShapeDtypeStruct((B, 4, S // 4, 3 * GROUP_W), BF16),
        jax.ShapeDtypeStruct((B, 4, S // 4, 3 * GROUP_W), BF16),
        jax.ShapeDtypeStruct((B, 3 * GROUP_W // LANES, 4, S // 4, LANES), F32),
        jax.ShapeDtypeStruct((B, S, 2 * D), BF16),
    )
    out_specs = (
        tok(2 * FOX_W),
        pl.BlockSpec((1, FOX_W, tm), lambda b, i: (b, 0, i)),
        tok(LANES),
        pl.BlockSpec((1, 4, tm // 4, 3 * GROUP_W), lambda b, i: (b, 0, i, 0)),
        pl.BlockSpec((1, 4, tm // 4, 3 * GROUP_W), lambda b, i: (b, 0, i, 0)),
        pl.BlockSpec((1, 3 * GROUP_W // LANES, 4, tm // 4, LANES), lambda b, i: (b, 0, 0, i, 0)),
        tok(2 * D),
    )
    in_specs = [
        tok(D), _resident(g.shape), pl.BlockSpec(memory_space=pl.ANY), _resident(w_vf.shape),
        _resident(b_f.shape), tab, tab, tab,
    ]
    n_f = w_vf.shape[0] - FOX_W
    return pl.pallas_call(
        _inproj_kernel,
        grid=grid,
        in_specs=in_specs,
        out_specs=out_specs,
        out_shape=out_shape,
        scratch_shapes=[pltpu.VMEM((D // LANES + 1, tm, LANES), F32),
                        pltpu.VMEM((n_f, LANES), F32),
                        pltpu.VMEM((2 * FOX_W, D), BF16), pltpu.VMEM((n_tail, D), BF16),
                        pltpu.VMEM((2, STAGE_ROWS, D), F32), pltpu.SemaphoreType.DMA((2,))],
        compiler_params=pltpu.CompilerParams(
            dimension_semantics=("arbitrary", "arbitrary"), vmem_limit_bytes=VMEM_LIMIT),
        name="inproj",
    )(x, g, w_all, w_vf, b_f, cos, slo, shi)


def _fox_kernel(q_ref, k_ref, kx_ref, vt_ref, o_ref):
    S = q_ref.shape[1]
    pw = 2 * HEAD_DIM
    pairs = q_ref.shape[2] // pw
    step = pl.program_id(1)
    lane = lax.broadcasted_iota(jnp.int32, (1, pw), 1)
    first = lane < HEAD_DIM
    zeros = jnp.zeros((TQ, pw), BF16)

    def sel(head):
        return jnp.broadcast_to(
            jnp.where((lane >= 3 * head) & (lane < 3 * head + 3), 1.0, 0.0), (TQ, pw)).astype(BF16)

    hq = TQ // 2
    key = lax.broadcasted_iota(jnp.int32, (hq, TQ), 0)
    qry = lax.broadcasted_iota(jnp.int32, (hq, TQ), 1) % hq
    causal = key <= qry

    def colmax(parts):
        m = jnp.max(parts[0], axis=0, keepdims=True)
        for t in parts[1:]:
            m = jnp.maximum(m, jnp.max(t, axis=0, keepdims=True))
        return m

    def scores(pair, qi):
        lanes = slice(pair * pw, (pair + 1) * pw)
        head0 = 2 * (pairs * step + pair)
        q2 = q_ref[0, qi * TQ:(qi + 1) * TQ, lanes]
        qa = jnp.concatenate([jnp.where(first, q2, zeros), sel(head0)], axis=1)
        qb = jnp.concatenate([jnp.where(first, zeros, q2), sel(head0 + 1)], axis=1)
        qmix = jnp.concatenate([qa[0:hq], qb[0:hq], qa[hq:TQ], qb[hq:TQ]], axis=0)
        n_k = (qi + 1) * TK
        n_lo = n_k - hq
        kaug = jnp.concatenate([k_ref[0, 0:n_k, lanes], kx_ref[0, 0:n_k, :]], axis=1)
        s = lax.dot_general(kaug[0:n_lo], qmix, _NT, preferred_element_type=F32)
        s_hi = lax.dot_general(kaug[n_lo:n_k], qmix[TQ:2 * TQ], _NT, preferred_element_type=F32)
        full = [s[c * TK:(c + 1) * TK] for c in range(qi)]
        diag = s[qi * TK:n_lo]
        lo = [t[:, 0:TQ] for t in full] + [jnp.where(causal, diag[:, 0:TQ], NEG_INF)]
        hi = ([t[:, TQ:2 * TQ] for t in full] + [diag[:, TQ:2 * TQ]]
              + [jnp.where(causal, s_hi, NEG_INF)])
        return (lo, colmax(lo)), (hi, colmax(hi))

    def outputs(pair, qi, lo, hi):
        for half, (chunks, m) in enumerate((lo, hi)):
            n = (qi + 1) * TK - (hq if half == 0 else 0)
            p = [jnp.exp2(s - m) for s in chunks]
            l = jnp.sum(p[0], axis=0, keepdims=True)
            for t in p[1:]:
                l = l + jnp.sum(t, axis=0, keepdims=True)
            pt = jnp.concatenate([t.astype(BF16) for t in p], axis=0)
            acc = jnp.dot(vt_ref[0, pair * pw:(pair + 1) * pw, 0:n], pt,
                          preferred_element_type=F32)
            ot = (acc / l).T
            r0 = qi * TQ + half * hq
            o_ref[0, r0:r0 + hq, pair * pw:(pair + 1) * pw] = (
                jnp.where(first, ot[0:hq], ot[hq:TQ]).astype(BF16))

    items = [(pair, qi) for pair in range(pairs) for qi in range(S // TQ)]
    nxt = scores(*items[0])
    for i, item in enumerate(items):
        cur = nxt
        if i + 1 < len(items):
            nxt = scores(*items[i + 1])
        outputs(*item, *cur)


def _fox(qk, vt, kx):
    B, S, _ = qk.shape
    pairs = FOX_PAIRS_PER_STEP
    steps = FOX_HEADS // (2 * pairs)
    w = pairs * 2 * HEAD_DIM
    spec = lambda off: pl.BlockSpec((1, S, w), lambda b, p: (b, 0, off + p))
    return pl.pallas_call(
        _fox_kernel,
        grid=(B, steps),
        in_specs=[spec(0), spec(steps),
                  pl.BlockSpec((1, S, LANES), lambda b, p: (b, 0, 0)),
                  pl.BlockSpec((1, w, S), lambda b, p: (b, p, 0))],
        out_specs=pl.BlockSpec((1, S, w), lambda b, p: (b, 0, p)),
        out_shape=jax.ShapeDtypeStruct((B, S, FOX_W), BF16),
        compiler_params=pltpu.CompilerParams(
            dimension_semantics=("arbitrary", "arbitrary"), vmem_limit_bytes=VMEM_LIMIT),
        name="fox",
    )(qk, qk, kx, vt)


def _dil_kernel(d0_ref, d1_ref, d2_ref, o_ref, acc_scr, m_scr, l_scr, nat_scr):
    S = d0_ref.shape[1]
    nh = DIL_HEADS_PER_GROUP
    blk = DIL_BLK
    n_blocks = S // blk
    lane_head = lax.broadcasted_iota(jnp.int32, (1, GROUP_W), 1) // HEAD_DIM
    a2 = lax.broadcasted_iota(jnp.int32, (nh * blk, 2 * blk), 0) % blk
    k2 = lax.broadcasted_iota(jnp.int32, (nh * blk, 2 * blk), 1)
    bias_two = jnp.where((k2 >= a2) & (k2 <= a2 + blk), 0.0, NEG_INF)
    a1 = lax.broadcasted_iota(jnp.int32, (nh * blk, blk), 0) % blk
    k1 = lax.broadcasted_iota(jnp.int32, (nh * blk, blk), 1)
    bias_one = jnp.where(k1 <= a1, 0.0, NEG_INF)

    def assemble(x):
        out = x[(nh - 1) * blk:nh * blk]
        for h in range(nh - 2, -1, -1):
            out = jnp.where(lane_head == h, x[h * blk:(h + 1) * blk], out)
        return out

    def logits(load_q, load_k, bias):
        qs = load_q()
        zero = jnp.zeros_like(qs)
        qstack = jnp.concatenate(
            [jnp.where(lane_head == h, qs, zero) for h in range(nh)], axis=0)
        s = lax.dot_general(qstack, load_k(), _NT, preferred_element_type=F32) + bias
        return s, jnp.max(s, axis=1, keepdims=True)

    def weigh(s, m, load_v):
        p = jnp.exp2(s - m)
        l = jnp.sum(p, axis=1, keepdims=True)
        pv = jnp.dot(p.astype(BF16), load_v(), preferred_element_type=F32)
        return (assemble(pv), assemble(jnp.broadcast_to(m, pv.shape)),
                assemble(jnp.broadcast_to(l, pv.shape)))

    def put(res, pieces, first):
        acc_b, m_b, l_b = res
        for rows, idx in pieces:
            for c in range(GROUP_W // LANES):
                lanes = slice(c * LANES, (c + 1) * LANES)
                if first:
                    acc_scr[c, idx, :] = acc_b[rows, lanes]
                    m_scr[c, idx, :] = m_b[rows, lanes]
                    l_scr[c, idx, :] = l_b[rows, lanes]
                else:
                    m_o = m_scr[c, idx, :]
                    m_n = jnp.maximum(m_o, m_b[rows, lanes])
                    e_o = jnp.exp2(m_o - m_n)
                    e_b = jnp.exp2(m_b[rows, lanes] - m_n)
                    acc_scr[c, idx, :] = acc_scr[c, idx, :] * e_o + acc_b[rows, lanes] * e_b
                    l_scr[c, idx, :] = l_scr[c, idx, :] * e_o + l_b[rows, lanes] * e_b
                    m_scr[c, idx, :] = m_n

    q_l, k_l, v_l = (slice(i * GROUP_W, (i + 1) * GROUP_W) for i in range(3))
    whole = slice(0, blk)
    n_res = 4
    slab = S // n_res
    per_slab = slab // blk

    items = []
    rows_of = lambda ref, r0, n, lanes: (lambda: ref[0, pl.ds(r0, n), lanes])

    for res in range(n_res):
        for pos in range(per_slab):
            r0 = res * slab + pos * blk
            k0, nk, bias = (r0, blk, bias_one) if pos == 0 else (r0 - blk, 2 * blk, bias_two)
            items.append((rows_of(d1_ref, r0, blk, q_l), rows_of(d1_ref, k0, nk, k_l), bias,
                          rows_of(d1_ref, k0, nk, v_l), [(whole, pl.ds(r0, blk))], True))

    sub = blk // n_res
    qa = lax.broadcasted_iota(jnp.int32, (nh * blk, 2 * blk), 0) % blk
    kk = lax.broadcasted_iota(jnp.int32, (nh * blk, 2 * blk), 1)
    tq = blk + n_res * (qa % sub) + qa // sub
    tk = n_res * (kk % (2 * sub)) + kk // (2 * sub)
    bias0_two = jnp.where((tk <= tq) & (tq - tk <= blk), 0.0, NEG_INF)
    qa1 = lax.broadcasted_iota(jnp.int32, (nh * blk, blk), 0) % blk
    kk1 = lax.broadcasted_iota(jnp.int32, (nh * blk, blk), 1)
    bias0_one = jnp.where(n_res * (kk1 % sub) + kk1 // sub <= n_res * (qa1 % sub) + qa1 // sub,
                          0.0, NEG_INF)

    def slabs_of(ref, start, n, lanes):
        return lambda: jnp.concatenate(
            [ref[0, pl.ds(r * slab + start, n), lanes] for r in range(n_res)], axis=0)

    for n in range(n_blocks):
        k0, nk, bias = (0, sub, bias0_one) if n == 0 else ((n - 1) * sub, 2 * sub, bias0_two)
        dst = [(slice(r * sub, (r + 1) * sub), pl.ds(r * slab + n * sub, sub)) for r in range(n_res)]
        items.append((slabs_of(d0_ref, n * sub, sub, q_l), slabs_of(d0_ref, k0, nk, k_l), bias,
                      slabs_of(d0_ref, k0, nk, v_l), dst, False))

    def gather_of(start, chunks):
        return lambda: jnp.concatenate(
            [d2_ref[0, c, pl.ds(start, blk, stride=n_res), :].astype(BF16) for c in chunks], axis=1)

    per_part = GROUP_W // LANES
    for r16 in range(n_blocks):
        start = (r16 % n_res) * slab + r16 // n_res
        dst = [(whole, pl.ds(start, blk, stride=n_res))]
        items.append((gather_of(start, range(0, per_part)),
                      gather_of(start, range(per_part, 2 * per_part)), bias_one,
                      gather_of(start, range(2 * per_part, 3 * per_part)), dst, False))

    nxt = logits(*items[0][0:3])
    for i, (_, _, _, load_v, dst, first_write) in enumerate(items):
        cur = nxt
        if i + 1 < len(items):
            nxt = logits(*items[i + 1][0:3])
        put(weigh(*cur, load_v), dst, first_write)

    def finish(res, _):
        for pos in range(per_slab):
            r0 = _aligned(res * slab + pos * blk, blk)
            for c in range(GROUP_W // LANES):
                nat_scr[c, pl.ds(n_res * pos * blk + res, blk, stride=n_res), :] = (
                    acc_scr[c, pl.ds(r0, blk), :] / l_scr[c, pl.ds(r0, blk), :])
        return 0

    for res in range(n_res):
        finish(res, 0)

    def emit(i, _):
        r0 = _aligned(i * 2 * blk, 2 * blk)
        for c in range(GROUP_W // LANES):
            o_ref[0, pl.ds(r0, 2 * blk), c * LANES:(c + 1) * LANES] = nat_scr[c, pl.ds(r0, 2 * blk), :].astype(BF16)
        return 0

    for i in range(n_blocks // 2):
        emit(i, 0)


def _dilated(d0, d1, d2):
    B, S, W = d0.shape
    spec = pl.BlockSpec((1, S, W), lambda b: (b, 0, 0))
    spec2 = pl.BlockSpec((1,) + d2.shape[1:], lambda b: (b, 0, 0, 0))
    return pl.pallas_call(
        _dil_kernel,
        grid=(B,),
        in_specs=[spec, spec, spec2],
        out_specs=pl.BlockSpec((1, S, GROUP_W), lambda b: (b, 0, 0)),
        out_shape=jax.ShapeDtypeStruct((B, S, GROUP_W), BF16),
        scratch_shapes=[pltpu.VMEM((GROUP_W // LANES, S, LANES), F32)] * 4,
        compiler_params=pltpu.CompilerParams(
            dimension_semantics=("arbitrary",), vmem_limit_bytes=VMEM_LIMIT),
        name="dilated",
    )(d0, d1, d2)


def _stage_weights(pairs, stage, sem):
    cols = stage.shape[2]
    chunks = []
    for src, dst in pairs:
        for r0 in range(0, src.shape[0], STAGE_ROWS):
            n = min(STAGE_ROWS, src.shape[0] - r0)
            for c0 in range(0, src.shape[1], cols):
                chunks.append((src, dst, r0, n, c0))

    def copy(k):
        src, _, r0, n, c0 = chunks[k]
        return pltpu.make_async_copy(src.at[r0:r0 + n, c0:c0 + cols], stage.at[k % 2, 0:n],
                                     sem.at[k % 2])

    copy(0).start()
    for k, (_, dst, r0, n, c0) in enumerate(chunks):
        if k + 1 < len(chunks):
            copy(k + 1).start()
        copy(k).wait()
        dst[r0:r0 + n, c0:c0 + cols] = stage[k % 2, 0:n].astype(BF16)


class _LazyStager:
    def __init__(self, stage, sem):
        self.stage, self.sem, self.slots = stage, sem, stage.shape[0]
        self.chunks, self.started, self.done = [], 0, 0

    def queue(self, src, dst, r0, n, c0):
        self.chunks.append((src, dst, r0, n, c0))
        return len(self.chunks)

    def _copy(self, k):
        src, _, r0, n, c0 = self.chunks[k]
        cols = self.stage.shape[2]
        return pltpu.make_async_copy(src.at[r0:r0 + n, c0:c0 + cols],
                                     self.stage.at[k % self.slots, 0:n],
                                     self.sem.at[k % self.slots])

    def _prefetch(self):
        while self.started < min(self.done + self.slots, len(self.chunks)):
            self._copy(self.started).start()
            self.started += 1

    def need(self, count):
        cols = self.stage.shape[2]
        while self.done < count:
            self._prefetch()
            k = self.done
            _, dst, r0, n, c0 = self.chunks[k]
            self._copy(k).wait()
            dst[r0:r0 + n, c0:c0 + cols] = self.stage[k % self.slots, 0:n].astype(BF16)
            self.done += 1
            self._prefetch()


def _outmlp_kernel(x_ref, oa_ref, ob_ref, gate_ref, wa_hbm, wb_hbm, wo_hbm, g2_ref,
                   wu_hbm, wdn_hbm, g3_ref, o_ref,
                   wa_ref, wb_ref, wo_ref, wu_ref, wdn_ref, stage, sem, *, final_norm):
    d = x_ref.shape[1]
    d_ff = wu_ref.shape[1]
    n_ff = d_ff // FF_CHUNK

    def body(need):
        need("branches")
        ya = jnp.dot(oa_ref[...], wa_ref[...], preferred_element_type=F32)
        yb = jnp.dot(ob_ref[...], wb_ref[...], preferred_element_type=F32)
        mixed = gate_ref[:, 0:d].astype(F32) * ya + gate_ref[:, d:2 * d].astype(F32) * yb
        need("out")
        x1 = x_ref[...] + jnp.dot(mixed.astype(BF16), wo_ref[...], preferred_element_type=F32)
        h = (x1 * g2_ref[...]).astype(BF16)
        inv = lax.rsqrt(jnp.mean(x1 * x1, axis=-1, keepdims=True) + EPS)
        mlp = None
        for c in range(n_ff):
            need(("up", c))
            u = jnp.dot(h, wu_ref[:, c * FF_CHUNK:(c + 1) * FF_CHUNK], preferred_element_type=F32)
            a = jnp.square(jnp.maximum(u, 0.0)).astype(BF16)
            need(("down", c))
            t = jnp.dot(a, wdn_ref[c * FF_CHUNK:(c + 1) * FF_CHUNK, :], preferred_element_type=F32)
            mlp = t if mlp is None else mlp + t
        x2 = x1 + (inv * inv) * mlp
        o_ref[...] = _rms(x2, g3_ref[...]) if final_norm else x2

    @pl.when(pl.program_id(0) == 0)
    def _first_step():
        stager = _LazyStager(stage, sem)
        ready = {}

        def queue_rows(src, dst, rows, c0=0):
            for r0 in range(rows.start, rows.stop, STAGE_ROWS):
                last = stager.queue(src, dst, r0, min(STAGE_ROWS, rows.stop - r0), c0)
            return last

        queue_rows(wa_hbm, wa_ref, range(0, wa_ref.shape[0]))
        ready["branches"] = queue_rows(wb_hbm, wb_ref, range(0, wb_ref.shape[0]))
        ready["out"] = queue_rows(wo_hbm, wo_ref, range(0, wo_ref.shape[0]))
        for c in range(n_ff):
            ready[("up", c)] = queue_rows(wu_hbm, wu_ref, range(0, wu_ref.shape[0]), c * FF_CHUNK)
            ready[("down", c)] = queue_rows(wdn_hbm, wdn_ref,
                                            range(c * FF_CHUNK, (c + 1) * FF_CHUNK))
        body(lambda tag: stager.need(ready[tag]))

    @pl.when(pl.program_id(0) != 0)
    def _other_steps():
        body(lambda tag: None)


def _outmlp(x, oa, ob, gates, w_a, w_b, w_o, g2, w_u, w_dn, g3, final_norm):
    N, D = x.shape
    tm = TM_OUT
    tok = lambda w: pl.BlockSpec((tm, w), lambda i: (i, 0))
    hbm = pl.BlockSpec(memory_space=pl.ANY)
    weights = (w_a, w_b, w_o, w_u, w_dn)
    return pl.pallas_call(
        functools.partial(_outmlp_kernel, final_norm=final_norm),
        grid=(N // tm,),
        in_specs=[tok(D), tok(oa.shape[1]), tok(ob.shape[1]), tok(gates.shape[1]),
                  hbm, hbm, hbm, _resident(g2.shape), hbm, hbm, _resident(g3.shape)],
        out_specs=tok(D),
        out_shape=jax.ShapeDtypeStruct((N, D), F32),
        scratch_shapes=[pltpu.VMEM(w.shape, BF16) for w in weights]
        + [pltpu.VMEM((OUT_STAGE_SLOTS, STAGE_ROWS, D), F32),
           pltpu.SemaphoreType.DMA((OUT_STAGE_SLOTS,))],
        compiler_params=pltpu.CompilerParams(
            dimension_semantics=("arbitrary",), vmem_limit_bytes=VMEM_LIMIT),
        name="outmlp",
    )(x, oa, ob, gates, w_a, w_b, w_o, g2, w_u, w_dn, g3)


def _rope_tables(S):
    inv_freq = jnp.power(jnp.float32(ROPE_THETA),
                         -jnp.arange(ROPE_HALF, dtype=F32) * 2.0 / ROPE_DIM)
    ang = jnp.arange(S).astype(F32)[:, None] * inv_freq[None, :]
    cos, sin = jnp.cos(ang), jnp.sin(ang)
    rest = HEAD_DIM - ROPE_DIM
    one = jnp.ones((S, rest), F32)
    zero = lambda w: jnp.zeros((S, w), F32)
    cos_h = jnp.concatenate([cos, cos, one], axis=1)
    slo_h = jnp.concatenate([-sin, zero(ROPE_HALF + rest)], axis=1)
    shi_h = jnp.concatenate([zero(ROPE_HALF), sin, zero(rest)], axis=1)
    rep = lambda t: jnp.tile(t, (1, 128 // HEAD_DIM))
    return rep(cos_h), rep(slo_h), rep(shi_h)


def kernel(x, norm_attn_g, w_in, b_forget, w_branch_a, w_branch_b, w_out, norm_mlp_g, w_up,
           w_down, norm_final_g):
    B, S, D = x.shape
    depth = w_in.shape[0]
    assert S % (16 * DIL_BLK) == 0 and S // 16 == DIL_BLK, "sequence length fixed by the dilation groups"
    cos, slo, shi = _rope_tables(S)
    sizes = (FOX_W, FOX_W, FOX_W, FOX_HEADS, DIL_W, DIL_W, DIL_W, D, D)
    offs = np.concatenate([[0], np.cumsum(sizes)])
    for l in range(depth):
        wt = jnp.swapaxes(w_in[l], 0, 1)
        w_all = wt.astype(F32)
        pad = -(3 * FOX_HEADS) % 16
        w_vf = jnp.concatenate(
            [wt[offs[2]:offs[3]], jnp.repeat(wt[offs[3]:offs[4]], 3, axis=0),
             jnp.zeros((pad, D), wt.dtype)], axis=0).astype(BF16)
        b_f = jnp.pad(jnp.repeat(b_forget[l].astype(F32), 3), (0, pad))[:, None]

        qk, vt, kx, d0, d1, d2, gates = _inproj(
            x, norm_attn_g[l][None, :], w_all, w_vf, b_f, cos, slo, shi)
        oa = _fox(qk, vt, kx)
        ob = _dilated(d0.reshape(B, S, -1), d1.reshape(B, S, -1),
                      d2.reshape(B, d2.shape[1], S, LANES))
        y = _outmlp(
            x.reshape(B * S, D), oa.reshape(B * S, -1), ob.reshape(B * S, -1),
            gates.reshape(B * S, -1),
            w_branch_a[l], w_branch_b[l], w_out[l], norm_mlp_g[l][None, :], w_up[l], w_down[l],
            norm_final_g[None, :], final_norm=(l == depth - 1))
        x = y.reshape(B, S, D)
    return x
```

```python
import functools

import numpy as np
import jax
import jax.numpy as jnp
from jax import lax
from jax.experimental import pallas as pl
from jax.experimental.pallas import tpu as pltpu

F32 = jnp.float32
BF16 = jnp.bfloat16

HEAD_DIM = 64
FOX_HEADS = 8
FOX_W = FOX_HEADS * HEAD_DIM
DIL_GROUPS = ((128, 1), (512, 4), (2048, 16))
DIL_HEADS_PER_GROUP = 4
GROUP_W = DIL_HEADS_PER_GROUP * HEAD_DIM
N_GROUPS = len(DIL_GROUPS)
DIL_W = N_GROUPS * GROUP_W
ROPE_THETA = 500000.0
ROPE_DIM = HEAD_DIM // 4
ROPE_HALF = ROPE_DIM // 2
EPS = 1e-6
NEG_INF = -1e30
Q_SCALE = HEAD_DIM ** -0.5
LOG2E = 1.4426950408889634

LANES = 128
DIL_BLK = 128
TM_IN = 512
TM_OUT = 512
TQ = 256
TK = 256
FOX_PAIRS_PER_STEP = 2
FF_CHUNK = 1024
STAGE_ROWS = 512
STAGE_SLOTS = 4
VMEM_LIMIT = 56 * 1024 * 1024

_NT = (((1,), (1,)), ((), ()))


def _resident(shape):
    nd = len(shape)
    return pl.BlockSpec(shape, lambda *_: (0,) * nd, pipeline_mode=pl.Buffered(1))


def _aligned(x, m):
    return x if isinstance(x, int) else pl.multiple_of(x, m)


def _rms(x, g):
    ms = jnp.mean(x * x, axis=-1, keepdims=True)
    return x * lax.rsqrt(ms + EPS) * g


def _rope(x, cos, sin_lo, sin_hi):
    w = x.shape[1]
    up = pltpu.roll(x, w - ROPE_HALF, 1)
    dn = pltpu.roll(x, ROPE_HALF, 1)
    return x * cos + up * sin_lo + dn * sin_hi


def _split3(x):
    hi = x.astype(BF16)
    r1 = x - hi.astype(F32)
    mid = r1.astype(BF16)
    lo = (r1 - mid.astype(F32)).astype(BF16)
    return hi, mid, lo


def _inproj_body(need, x_ref, g_ref, w_hbm, wvf_ref, bf_ref,
                 cos_ref, slo_ref, shi_ref,
                 qk_ref, vt_ref, kx_ref, d0_ref, d1_ref, d2_ref, gate_ref,
                 perm_scr, carry_scr, wqk_ref, wt_ref, stage, sem):
    tm = x_ref.shape[1]
    d = x_ref.shape[2]

    xt = x_ref[0]
    xg = xt * g_ref[...]
    h = xg.astype(BF16)
    inv = lax.rsqrt(jnp.mean(xt * xt, axis=-1, keepdims=True) + EPS)
    inv_b = jnp.broadcast_to(inv, (tm, LANES))
    inv_row = inv_b.T[0:1, :]
    proj = lambda ref, lo, hi: inv * lax.dot_general(h, ref[lo:hi, :], _NT,
                                                     preferred_element_type=F32)

    n_res = 4
    sub_t = tm // n_res

    def by_residue(ref, start=0):
        return jnp.concatenate(
            [ref[pl.ds(start + res, sub_t, stride=n_res), :] for res in range(n_res)], axis=0)

    for c in range(d // LANES):
        perm_scr[c] = xg[:, c * LANES:(c + 1) * LANES]
    perm_scr[d // LANES] = inv_b
    h4 = jnp.concatenate([by_residue(perm_scr.at[c]) for c in range(d // LANES)],
                         axis=1).astype(BF16)
    inv4 = by_residue(perm_scr.at[d // LANES])[:, 0:1]
    proj4 = lambda lo, hi: inv4 * lax.dot_general(h4, wt_ref[lo:hi, :], _NT,
                                                  preferred_element_type=F32)
    g_off = 3 * DIL_W
    lane = lax.broadcasted_iota(jnp.int32, (1, LANES), 1)

    n_f = wvf_ref.shape[0] - FOX_W

    def v_and_forget_logits():
        vf = lax.dot_general(wvf_ref[...], h, _NT, preferred_element_type=F32) * inv_row
        vt_ref[0] = vf[0:FOX_W].astype(BF16)
        z = vf[FOX_W:FOX_W + n_f] + bf_ref[...]
        row = lax.broadcasted_iota(jnp.int32, (n_f, 1), 0)
        log_f = jnp.where(row < 3 * FOX_HEADS,
                          -(jnp.maximum(-z, 0.0) + jnp.log1p(jnp.exp(-jnp.abs(z)))), 0.0)
        return _split3(log_f)

    def forget_cumsum(parts):
        row = lax.broadcasted_iota(jnp.int32, (tm, tm), 0)
        col = lax.broadcasted_iota(jnp.int32, (tm, tm), 1)
        tri = jnp.where(row <= col, 1.0, 0.0).astype(BF16)
        cum = carry_scr[:, 0:1]
        for part in parts:
            cum = cum + jnp.dot(part, tri, preferred_element_type=F32)
        carry_scr[...] = jnp.broadcast_to(cum[:, tm - 1:tm], carry_scr.shape)
        nf = jnp.concatenate([-LOG2E * cum, jnp.zeros((LANES - n_f, tm), F32)], axis=0).T
        hi, mid, lo = _split3(nf)
        kx_ref[0] = jnp.where(lane % 3 == 0, hi, jnp.where(lane % 3 == 1, mid, lo))

    def mixer_a_qk():
        need("qk")
        qk_ref[0, :, 0:FOX_W] = (proj(wqk_ref, 0, FOX_W) * (Q_SCALE * LOG2E)).astype(BF16)
        qk_ref[0, :, FOX_W:2 * FOX_W] = proj(wqk_ref, FOX_W, 2 * FOX_W).astype(BF16)

    outs = (d0_ref, d1_ref, d2_ref)

    def mixer_b(g):
        need(("b", g))
        dil = DIL_GROUPS[g][1]
        t0 = pl.multiple_of(pl.program_id(1) * tm, tm)
        cos, slo, shi = (by_residue(t, t0) for t in (cos_ref, slo_ref, shi_ref))
        part = lambda t: proj4(t * DIL_W + g * GROUP_W, t * DIL_W + (g + 1) * GROUP_W)
        q, k, v = part(0), part(1), part(2)
        halves = []
        for c in range(GROUP_W // LANES):
            halves.append(_rope(q[:, c * LANES:(c + 1) * LANES], cos, slo, shi) * (Q_SCALE * LOG2E))
        for c in range(GROUP_W // LANES):
            halves.append(_rope(k[:, c * LANES:(c + 1) * LANES], cos, slo, shi))
        halves += [v[:, c * LANES:(c + 1) * LANES] for c in range(GROUP_W // LANES)]
        for res in range(n_res):
            for c, t in enumerate(halves):
                rows_c = t[res * sub_t:(res + 1) * sub_t]
                if dil <= n_res:
                    outs[g][0, res, :, c * LANES:(c + 1) * LANES] = rows_c.astype(BF16)
                else:
                    outs[g][0, c, res] = rows_c

    def gates(c):
        need(("g", c))
        r = proj(wt_ref, g_off + c * 512, g_off + (c + 1) * 512)
        gate_ref[0, :, c * 512:(c + 1) * 512] = jax.nn.sigmoid(r).astype(BF16)

    parts = v_and_forget_logits()
    mixer_b(0)
    gates(0)
    gates(1)
    forget_cumsum(parts)
    mixer_b(1)
    gates(2)
    gates(3)
    mixer_b(2)
    mixer_a_qk()


def _inproj_kernel(*refs):
    w_hbm, carry_scr, wqk_ref, wt_ref, stage, sem = refs[2], *refs[16:21]

    @pl.when(pl.program_id(1) == 0)
    def _reset():
        carry_scr[...] = jnp.zeros_like(carry_scr)

    first = (pl.program_id(0) == 0) & (pl.program_id(1) == 0)

    @pl.when(first)
    def _first_step():
        stager = _LazyStager(stage, sem)
        tail = w_hbm.at[3 * FOX_W + FOX_HEADS:3 * FOX_W + FOX_HEADS + wt_ref.shape[0]]
        ready = {}
        for tag in (("b", 0), ("g", 0), ("g", 1), ("b", 1), ("g", 2), ("g", 3), ("b", 2)):
            kind, i = tag
            if kind == "b":
                for t in range(3):
                    ready[tag] = stager.queue(tail, wt_ref, t * DIL_W + i * GROUP_W, GROUP_W, 0)
            else:
                ready[tag] = stager.queue(tail, wt_ref, 3 * DIL_W + i * 512, 512, 0)
        for r0 in range(0, wqk_ref.shape[0], STAGE_ROWS):
            ready["qk"] = stager.queue(w_hbm.at[0:wqk_ref.shape[0]], wqk_ref, r0, STAGE_ROWS, 0)
        _inproj_body(lambda tag: stager.need(ready[tag]), *refs)

    @pl.when(jnp.logical_not(first))
    def _other_steps():
        _inproj_body(lambda tag: None, *refs)


def _inproj(x, g, w_all, w_vf, b_f, cos, slo, shi):
    B, S, D = x.shape
    n_tail = 3 * DIL_W + 2 * D
    tm = TM_IN
    grid = (B, S // tm)
    tok = lambda w: pl.BlockSpec((1, tm, w), lambda b, i: (b, i, 0))
    tab = _resident(cos.shape)
    out_shape = (
        jax.ShapeDtypeStruct((B, S, 2 * FOX_W), BF16),
        jax.ShapeDtypeStruct((B, FOX_W, S), BF16),
        jax.ShapeDtypeStruct((B, S, LANES), BF16),
        jax.ShapeDtypeStruct((B, 4, S // 4, 3 * GROUP_W), BF16),
        jax.ShapeDtypeStruct((B, 4, S // 4, 3 * GROUP_W), BF16),
        jax.ShapeDtypeStruct((B, 3 * GROUP_W // LANES, 4, S // 4, LANES), F32),
        jax.ShapeDtypeStruct((B, S, 2 * D), BF16),
    )
    out_specs = (
        tok(2 * FOX_W),
        pl.BlockSpec((1, FOX_W, tm), lambda b, i: (b, 0, i)),
        tok(LANES),
        pl.BlockSpec((1, 4, tm // 4, 3 * GROUP_W), lambda b, i: (b, 0, i, 0)),
        pl.BlockSpec((1, 4, tm // 4, 3 * GROUP_W), lambda b, i: (b, 0, i, 0)),
        pl.BlockSpec((1, 3 * GROUP_W // LANES, 4, tm // 4, LANES), lambda b, i: (b, 0, 0, i, 0)),
        tok(2 * D),
    )
    in_specs = [
        tok(D), _resident(g.shape), pl.BlockSpec(memory_space=pl.ANY), _resident(w_vf.shape),
        _resident(b_f.shape), tab, tab, tab,
    ]
    n_f = w_vf.shape[0] - FOX_W
    return pl.pallas_call(
        _inproj_kernel,
        grid=grid,
        in_specs=in_specs,
        out_specs=out_specs,
        out_shape=out_shape,
        scratch_shapes=[pltpu.VMEM((D // LANES + 1, tm, LANES), F32),
                        pltpu.VMEM((n_f, LANES), F32),
                        pltpu.VMEM((2 * FOX_W, D), BF16), pltpu.VMEM((n_tail, D), BF16),
                        pltpu.VMEM((STAGE_SLOTS, STAGE_ROWS, D), F32),
                        pltpu.SemaphoreType.DMA((STAGE_SLOTS,))],
        compiler_params=pltpu.CompilerParams(
            dimension_semantics=("arbitrary", "arbitrary"), vmem_limit_bytes=VMEM_LIMIT),
        name="inproj",
    )(x, g, w_all, w_vf, b_f, cos, slo, shi)


def _fox_kernel(q_ref, k_ref, kx_ref, vt_ref, o_ref):
    S = q_ref.shape[1]
    pw = 2 * HEAD_DIM
    pairs = q_ref.shape[2] // pw
    step = pl.program_id(1)
    lane = lax.broadcasted_iota(jnp.int32, (1, pw), 1)
    first = lane < HEAD_DIM
    zeros = jnp.zeros((TQ, pw), BF16)

    def sel(head):
        return jnp.broadcast_to(
            jnp.where((lane >= 3 * head) & (lane < 3 * head + 3), 1.0, 0.0), (TQ, pw)).astype(BF16)

    hq = TQ // 2
    key = lax.broadcasted_iota(jnp.int32, (hq, TQ), 0)
    qry = lax.broadcasted_iota(jnp.int32, (hq, TQ), 1) % hq
    causal = key <= qry

    def colmax(parts):
        m = jnp.max(parts[0], axis=0, keepdims=True)
        for t in parts[1:]:
            m = jnp.maximum(m, jnp.max(t, axis=0, keepdims=True))
        return m

    def scores(pair, qi):
        lanes = slice(pair * pw, (pair + 1) * pw)
        head0 = 2 * (pairs * step + pair)
        q2 = q_ref[0, qi * TQ:(qi + 1) * TQ, lanes]
        qa = jnp.concatenate([jnp.where(first, q2, zeros), sel(head0)], axis=1)
        qb = jnp.concatenate([jnp.where(first, zeros, q2), sel(head0 + 1)], axis=1)
        qmix = jnp.concatenate([qa[0:hq], qb[0:hq], qa[hq:TQ], qb[hq:TQ]], axis=0)
        n_k = (qi + 1) * TK
        n_lo = n_k - hq
        kaug = jnp.concatenate([k_ref[0, 0:n_k, lanes], kx_ref[0, 0:n_k, :]], axis=1)
        s = lax.dot_general(kaug[0:n_lo], qmix, _NT, preferred_element_type=F32)
        s_hi = lax.dot_general(kaug[n_lo:n_k], qmix[TQ:2 * TQ], _NT, preferred_element_type=F32)
        full = [s[c * TK:(c + 1) * TK] for c in range(qi)]
        diag = s[qi * TK:n_lo]
        lo = [t[:, 0:TQ] for t in full] + [jnp.where(causal, diag[:, 0:TQ], NEG_INF)]
        hi = ([t[:, TQ:2 * TQ] for t in full] + [diag[:, TQ:2 * TQ]]
              + [jnp.where(causal, s_hi, NEG_INF)])
        return (lo, colmax(lo)), (hi, colmax(hi))

    def outputs(pair, qi, lo, hi):
        for half, (chunks, m) in enumerate((lo, hi)):
            n = (qi + 1) * TK - (hq if half == 0 else 0)
            p = [jnp.exp2(s - m) for s in chunks]
            l = jnp.sum(p[0], axis=0, keepdims=True)
            for t in p[1:]:
                l = l + jnp.sum(t, axis=0, keepdims=True)
            pt = jnp.concatenate([t.astype(BF16) for t in p], axis=0)
            acc = jnp.dot(vt_ref[0, pair * pw:(pair + 1) * pw, 0:n], pt,
                          preferred_element_type=F32)
            ot = (acc / l).T
            r0 = qi * TQ + half * hq
            o_ref[0, r0:r0 + hq, pair * pw:(pair + 1) * pw] = (
                jnp.where(first, ot[0:hq], ot[hq:TQ]).astype(BF16))

    items = [(pair, qi) for pair in range(pairs) for qi in range(S // TQ)]
    nxt = scores(*items[0])
    for i, item in enumerate(items):
        cur = nxt
        if i + 1 < len(items):
            nxt = scores(*items[i + 1])
        outputs(*item, *cur)


def _fox(qk, vt, kx):
    B, S, _ = qk.shape
    pairs = FOX_PAIRS_PER_STEP
    steps = FOX_HEADS // (2 * pairs)
    w = pairs * 2 * HEAD_DIM
    spec = lambda off: pl.BlockSpec((1, S, w), lambda b, p: (b, 0, off + p))
    return pl.pallas_call(
        _fox_kernel,
        grid=(B, steps),
        in_specs=[spec(0), spec(steps),
                  pl.BlockSpec((1, S, LANES), lambda b, p: (b, 0, 0)),
                  pl.BlockSpec((1, w, S), lambda b, p: (b, p, 0))],
        out_specs=pl.BlockSpec((1, S, w), lambda b, p: (b, 0, p)),
        out_shape=jax.ShapeDtypeStruct((B, S, FOX_W), BF16),
        compiler_params=pltpu.CompilerParams(
            dimension_semantics=("arbitrary", "arbitrary"), vmem_limit_bytes=VMEM_LIMIT),
        name="fox",
    )(qk, qk, kx, vt)


def _dil_kernel(d0_ref, d1_ref, d2_ref, o_ref, acc_scr, m_scr, l_scr, nat_scr):
    S = d0_ref.shape[1]
    nh = DIL_HEADS_PER_GROUP
    blk = DIL_BLK
    n_blocks = S // blk
    lane_head = lax.broadcasted_iota(jnp.int32, (1, GROUP_W), 1) // HEAD_DIM
    a2 = lax.broadcasted_iota(jnp.int32, (nh * blk, 2 * blk), 0) % blk
    k2 = lax.broadcasted_iota(jnp.int32, (nh * blk, 2 * blk), 1)
    bias_two = jnp.where((k2 >= a2) & (k2 <= a2 + blk), 0.0, NEG_INF)
    a1 = lax.broadcasted_iota(jnp.int32, (nh * blk, blk), 0) % blk
    k1 = lax.broadcasted_iota(jnp.int32, (nh * blk, blk), 1)
    bias_one = jnp.where(k1 <= a1, 0.0, NEG_INF)

    def assemble(x):
        out = x[(nh - 1) * blk:nh * blk]
        for h in range(nh - 2, -1, -1):
            out = jnp.where(lane_head == h, x[h * blk:(h + 1) * blk], out)
        return out

    def logits(load_q, load_k, bias):
        qs = load_q()
        zero = jnp.zeros_like(qs)
        qstack = jnp.concatenate(
            [jnp.where(lane_head == h, qs, zero) for h in range(nh)], axis=0)
        s = lax.dot_general(qstack, load_k(), _NT, preferred_element_type=F32) + bias
        return s, jnp.max(s, axis=1, keepdims=True)

    def weigh(s, m, load_v):
        p = jnp.exp2(s - m)
        l = jnp.sum(p, axis=1, keepdims=True)
        pv = jnp.dot(p.astype(BF16), load_v(), preferred_element_type=F32)
        return (assemble(pv), assemble(jnp.broadcast_to(m, pv.shape)),
                assemble(jnp.broadcast_to(l, pv.shape)))

    def put(res, pieces, first):
        acc_b, m_b, l_b = res
        for rows, idx in pieces:
            for c in range(GROUP_W // LANES):
                lanes = slice(c * LANES, (c + 1) * LANES)
                if first:
                    acc_scr[c, idx, :] = acc_b[rows, lanes]
                    m_scr[c, idx, :] = m_b[rows, lanes]
                    l_scr[c, idx, :] = l_b[rows, lanes]
                else:
                    m_o = m_scr[c, idx, :]
                    m_n = jnp.maximum(m_o, m_b[rows, lanes])
                    e_o = jnp.exp2(m_o - m_n)
                    e_b = jnp.exp2(m_b[rows, lanes] - m_n)
                    acc_scr[c, idx, :] = acc_scr[c, idx, :] * e_o + acc_b[rows, lanes] * e_b
                    l_scr[c, idx, :] = l_scr[c, idx, :] * e_o + l_b[rows, lanes] * e_b
                    m_scr[c, idx, :] = m_n

    q_l, k_l, v_l = (slice(i * GROUP_W, (i + 1) * GROUP_W) for i in range(3))
    whole = slice(0, blk)
    n_res = 4
    slab = S // n_res
    per_slab = slab // blk

    items = []
    rows_of = lambda ref, r0, n, lanes: (lambda: ref[0, pl.ds(r0, n), lanes])

    for res in range(n_res):
        for pos in range(per_slab):
            r0 = res * slab + pos * blk
            k0, nk, bias = (r0, blk, bias_one) if pos == 0 else (r0 - blk, 2 * blk, bias_two)
            items.append((rows_of(d1_ref, r0, blk, q_l), rows_of(d1_ref, k0, nk, k_l), bias,
                          rows_of(d1_ref, k0, nk, v_l), [(whole, pl.ds(r0, blk))], True))

    sub = blk // n_res
    qa = lax.broadcasted_iota(jnp.int32, (nh * blk, 2 * blk), 0) % blk
    kk = lax.broadcasted_iota(jnp.int32, (nh * blk, 2 * blk), 1)
    tq = blk + n_res * (qa % sub) + qa // sub
    tk = n_res * (kk % (2 * sub)) + kk // (2 * sub)
    bias0_two = jnp.where((tk <= tq) & (tq - tk <= blk), 0.0, NEG_INF)
    qa1 = lax.broadcasted_iota(jnp.int32, (nh * blk, blk), 0) % blk
    kk1 = lax.broadcasted_iota(jnp.int32, (nh * blk, blk), 1)
    bias0_one = jnp.where(n_res * (kk1 % sub) + kk1 // sub <= n_res * (qa1 % sub) + qa1 // sub,
                          0.0, NEG_INF)

    def slabs_of(ref, start, n, lanes):
        return lambda: jnp.concatenate(
            [ref[0, pl.ds(r * slab + start, n), lanes] for r in range(n_res)], axis=0)

    for n in range(n_blocks):
        k0, nk, bias = (0, sub, bias0_one) if n == 0 else ((n - 1) * sub, 2 * sub, bias0_two)
        dst = [(slice(r * sub, (r + 1) * sub), pl.ds(r * slab + n * sub, sub)) for r in range(n_res)]
        items.append((slabs_of(d0_ref, n * sub, sub, q_l), slabs_of(d0_ref, k0, nk, k_l), bias,
                      slabs_of(d0_ref, k0, nk, v_l), dst, False))

    def gather_of(start, chunks):
        return lambda: jnp.concatenate(
            [d2_ref[0, c, pl.ds(start, blk, stride=n_res), :].astype(BF16) for c in chunks], axis=1)

    per_part = GROUP_W // LANES
    for r16 in range(n_blocks):
        start = (r16 % n_res) * slab + r16 // n_res
        dst = [(whole, pl.ds(start, blk, stride=n_res))]
        items.append((gather_of(start, range(0, per_part)),
                      gather_of(start, range(per_part, 2 * per_part)), bias_one,
                      gather_of(start, range(2 * per_part, 3 * per_part)), dst, False))

    nxt = logits(*items[0][0:3])
    for i, (_, _, _, load_v, dst, first_write) in enumerate(items):
        cur = nxt
        if i + 1 < len(items):
            nxt = logits(*items[i + 1][0:3])
        put(weigh(*cur, load_v), dst, first_write)

    def finish(res, _):
        for pos in range(per_slab):
            r0 = _aligned(res * slab + pos * blk, blk)
            for c in range(GROUP_W // LANES):
                nat_scr[c, pl.ds(n_res * pos * blk + res, blk, stride=n_res), :] = (
                    acc_scr[c, pl.ds(r0, blk), :] / l_scr[c, pl.ds(r0, blk), :])
        return 0

    for res in range(n_res):
        finish(res, 0)

    def emit(i, _):
        r0 = _aligned(i * 2 * blk, 2 * blk)
        for c in range(GROUP_W // LANES):
            o_ref[0, pl.ds(r0, 2 * blk), c * LANES:(c + 1) * LANES] = nat_scr[c, pl.ds(r0, 2 * blk), :].astype(BF16)
        return 0

    for i in range(n_blocks // 2):
        emit(i, 0)


def _dilated(d0, d1, d2):
    B, S, W = d0.shape
    spec = pl.BlockSpec((1, S, W), lambda b: (b, 0, 0))
    spec2 = pl.BlockSpec((1,) + d2.shape[1:], lambda b: (b, 0, 0, 0))
    return pl.pallas_call(
        _dil_kernel,
        grid=(B,),
        in_specs=[spec, spec, spec2],
        out_specs=pl.BlockSpec((1, S, GROUP_W), lambda b: (b, 0, 0)),
        out_shape=jax.ShapeDtypeStruct((B, S, GROUP_W), BF16),
        scratch_shapes=[pltpu.VMEM((GROUP_W // LANES, S, LANES), F32)] * 4,
        compiler_params=pltpu.CompilerParams(
            dimension_semantics=("arbitrary",), vmem_limit_bytes=VMEM_LIMIT),
        name="dilated",
    )(d0, d1, d2)


class _LazyStager:
    def __init__(self, stage, sem):
        self.stage, self.sem, self.slots = stage, sem, stage.shape[0]
        self.chunks, self.started, self.done = [], 0, 0

    def queue(self, src, dst, r0, n, c0):
        self.chunks.append((src, dst, r0, n, c0))
        return len(self.chunks)

    def _copy(self, k):
        src, _, r0, n, c0 = self.chunks[k]
        cols = self.stage.shape[2]
        return pltpu.make_async_copy(src.at[r0:r0 + n, c0:c0 + cols],
                                     self.stage.at[k % self.slots, 0:n],
                                     self.sem.at[k % self.slots])

    def _prefetch(self):
        while self.started < min(self.done + self.slots, len(self.chunks)):
            self._copy(self.started).start()
            self.started += 1

    def need(self, count):
        cols = self.stage.shape[2]
        while self.done < count:
            self._prefetch()
            k = self.done
            _, dst, r0, n, c0 = self.chunks[k]
            self._copy(k).wait()
            dst[r0:r0 + n, c0:c0 + cols] = self.stage[k % self.slots, 0:n].astype(BF16)
            self.done += 1
            self._prefetch()


def _outmlp_kernel(x_ref, oa_ref, ob_ref, gate_ref, wa_hbm, wb_hbm, wo_hbm, g2_ref,
                   wu_hbm, wdn_hbm, g3_ref, o_ref,
                   wa_ref, wb_ref, wo_ref, wu_ref, wdn_ref, stage, sem, *, final_norm):
    d = x_ref.shape[1]
    d_ff = wu_ref.shape[1]
    n_ff = d_ff // FF_CHUNK

    def body(need):
        need("branches")
        ya = jnp.dot(oa_ref[...], wa_ref[...], preferred_element_type=F32)
        yb = jnp.dot(ob_ref[...], wb_ref[...], preferred_element_type=F32)
        mixed = gate_ref[:, 0:d].astype(F32) * ya + gate_ref[:, d:2 * d].astype(F32) * yb
        need("out")
        x1 = x_ref[...] + jnp.dot(mixed.astype(BF16), wo_ref[...], preferred_element_type=F32)
        h = (x1 * g2_ref[...]).astype(BF16)
        inv = lax.rsqrt(jnp.mean(x1 * x1, axis=-1, keepdims=True) + EPS)
        mlp = None
        for c in range(n_ff):
            need(("up", c))
            u = jnp.dot(h, wu_ref[:, c * FF_CHUNK:(c + 1) * FF_CHUNK], preferred_element_type=F32)
            a = jnp.square(jnp.maximum(u, 0.0)).astype(BF16)
            need(("down", c))
            t = jnp.dot(a, wdn_ref[c * FF_CHUNK:(c + 1) * FF_CHUNK, :], preferred_element_type=F32)
            mlp = t if mlp is None else mlp + t
        x2 = x1 + (inv * inv) * mlp
        o_ref[...] = _rms(x2, g3_ref[...]) if final_norm else x2

    @pl.when(pl.program_id(0) == 0)
    def _first_step():
        stager = _LazyStager(stage, sem)
        ready = {}

        def queue_rows(src, dst, rows, c0=0):
            for r0 in range(rows.start, rows.stop, STAGE_ROWS):
                last = stager.queue(src, dst, r0, min(STAGE_ROWS, rows.stop - r0), c0)
            return last

        queue_rows(wa_hbm, wa_ref, range(0, wa_ref.shape[0]))
        ready["branches"] = queue_rows(wb_hbm, wb_ref, range(0, wb_ref.shape[0]))
        ready["out"] = queue_rows(wo_hbm, wo_ref, range(0, wo_ref.shape[0]))
        for c in range(n_ff):
            ready[("up", c)] = queue_rows(wu_hbm, wu_ref, range(0, wu_ref.shape[0]), c * FF_CHUNK)
            ready[("down", c)] = queue_rows(wdn_hbm, wdn_ref,
                                            range(c * FF_CHUNK, (c + 1) * FF_CHUNK))
        body(lambda tag: stager.need(ready[tag]))

    @pl.when(pl.program_id(0) != 0)
    def _other_steps():
        body(lambda tag: None)


def _outmlp(x, oa, ob, gates, w_a, w_b, w_o, g2, w_u, w_dn, g3, final_norm):
    N, D = x.shape
    tm = TM_OUT
    tok = lambda w: pl.BlockSpec((tm, w), lambda i: (i, 0))
    hbm = pl.BlockSpec(memory_space=pl.ANY)
    weights = (w_a, w_b, w_o, w_u, w_dn)
    return pl.pallas_call(
        functools.partial(_outmlp_kernel, final_norm=final_norm),
        grid=(N // tm,),
        in_specs=[tok(D), tok(oa.shape[1]), tok(ob.shape[1]), tok(gates.shape[1]),
                  hbm, hbm, hbm, _resident(g2.shape), hbm, hbm, _resident(g3.shape)],
        out_specs=tok(D),
        out_shape=jax.ShapeDtypeStruct((N, D), F32),
        scratch_shapes=[pltpu.VMEM(w.shape, BF16) for w in weights]
        + [pltpu.VMEM((STAGE_SLOTS, STAGE_ROWS, D), F32),
           pltpu.SemaphoreType.DMA((STAGE_SLOTS,))],
        compiler_params=pltpu.CompilerParams(
            dimension_semantics=("arbitrary",), vmem_limit_bytes=VMEM_LIMIT),
        name="outmlp",
    )(x, oa, ob, gates, w_a, w_b, w_o, g2, w_u, w_dn, g3)


def _rope_tables(S):
    inv_freq = jnp.power(jnp.float32(ROPE_THETA),
                         -jnp.arange(ROPE_HALF, dtype=F32) * 2.0 / ROPE_DIM)
    ang = jnp.arange(S).astype(F32)[:, None] * inv_freq[None, :]
    cos, sin = jnp.cos(ang), jnp.sin(ang)
    rest = HEAD_DIM - ROPE_DIM
    one = jnp.ones((S, rest), F32)
    zero = lambda w: jnp.zeros((S, w), F32)
    cos_h = jnp.concatenate([cos, cos, one], axis=1)
    slo_h = jnp.concatenate([-sin, zero(ROPE_HALF + rest)], axis=1)
    shi_h = jnp.concatenate([zero(ROPE_HALF), sin, zero(rest)], axis=1)
    rep = lambda t: jnp.tile(t, (1, 128 // HEAD_DIM))
    return rep(cos_h), rep(slo_h), rep(shi_h)


def kernel(x, norm_attn_g, w_in, b_forget, w_branch_a, w_branch_b, w_out, norm_mlp_g, w_up,
           w_down, norm_final_g):
    B, S, D = x.shape
    depth = w_in.shape[0]
    assert S % (16 * DIL_BLK) == 0 and S // 16 == DIL_BLK, "sequence length fixed by the dilation groups"
    cos, slo, shi = _rope_tables(S)
    sizes = (FOX_W, FOX_W, FOX_W, FOX_HEADS, DIL_W, DIL_W, DIL_W, D, D)
    offs = np.concatenate([[0], np.cumsum(sizes)])
    for l in range(depth):
        wt = jnp.swapaxes(w_in[l], 0, 1)
        w_all = wt.astype(F32)
        pad = -(3 * FOX_HEADS) % 16
        w_vf = jnp.concatenate(
            [wt[offs[2]:offs[3]], jnp.repeat(wt[offs[3]:offs[4]], 3, axis=0),
             jnp.zeros((pad, D), wt.dtype)], axis=0).astype(BF16)
        b_f = jnp.pad(jnp.repeat(b_forget[l].astype(F32), 3), (0, pad))[:, None]

        qk, vt, kx, d0, d1, d2, gates = _inproj(
            x, norm_attn_g[l][None, :], w_all, w_vf, b_f, cos, slo, shi)
        oa = _fox(qk, vt, kx)
        ob = _dilated(d0.reshape(B, S, -1), d1.reshape(B, S, -1),
                      d2.reshape(B, d2.shape[1], S, LANES))
        y = _outmlp(
            x.reshape(B * S, D), oa.reshape(B * S, -1), ob.reshape(B * S, -1),
            gates.reshape(B * S, -1),
            w_branch_a[l], w_branch_b[l], w_out[l], norm_mlp_g[l][None, :], w_up[l], w_down[l],
            norm_final_g[None, :], final_norm=(l == depth - 1))
        x = y.reshape(B, S, D)
    return x
```

```python
import functools

import numpy as np
import jax
import jax.numpy as jnp
from jax import lax
from jax.experimental import pallas as pl
from jax.experimental.pallas import tpu as pltpu

F32 = jnp.float32
BF16 = jnp.bfloat16

HEAD_DIM = 64
FOX_HEADS = 8
FOX_W = FOX_HEADS * HEAD_DIM
DIL_GROUPS = ((128, 1), (512, 4), (2048, 16))
DIL_HEADS_PER_GROUP = 4
GROUP_W = DIL_HEADS_PER_GROUP * HEAD_DIM
N_GROUPS = len(DIL_GROUPS)
DIL_W = N_GROUPS * GROUP_W
ROPE_THETA = 500000.0
ROPE_DIM = HEAD_DIM // 4
ROPE_HALF = ROPE_DIM // 2
EPS = 1e-6
NEG_INF = -1e30
Q_SCALE = HEAD_DIM ** -0.5
LOG2E = 1.4426950408889634

LANES = 128
DIL_BLK = 128
TM_IN = 512
TM_OUT = 512
TQ = 256
TK = 256
FOX_PAIRS_PER_STEP = 2
FF_CHUNK = 1024
STAGE_ROWS = 512
STAGE_SLOTS = 4
VMEM_LIMIT = 56 * 1024 * 1024

_NT = (((1,), (1,)), ((), ()))


def _resident(shape):
    nd = len(shape)
    return pl.BlockSpec(shape, lambda *_: (0,) * nd, pipeline_mode=pl.Buffered(1))


def _aligned(x, m):
    return x if isinstance(x, int) else pl.multiple_of(x, m)


def _rms(x, g):
    ms = jnp.mean(x * x, axis=-1, keepdims=True)
    return x * lax.rsqrt(ms + EPS) * g


def _rope(x, cos, sin_lo, sin_hi):
    w = x.shape[1]
    up = pltpu.roll(x, w - ROPE_HALF, 1)
    dn = pltpu.roll(x, ROPE_HALF, 1)
    return x * cos + up * sin_lo + dn * sin_hi


def _split3(x):
    hi = x.astype(BF16)
    r1 = x - hi.astype(F32)
    mid = r1.astype(BF16)
    lo = (r1 - mid.astype(F32)).astype(BF16)
    return hi, mid, lo


def _inproj_body(need, x_ref, g_ref, w_hbm, wvf_ref, bf_ref,
                 cos_ref, slo_ref, shi_ref,
                 qk_ref, vt_ref, kx_ref, d0_ref, d1_ref, d2_ref,
                 perm_scr, carry_scr, wqk_ref, wt_ref, stage, sem):
    tm = x_ref.shape[1]
    d = x_ref.shape[2]

    xt = x_ref[0]
    xg = xt * g_ref[...]
    h = xg.astype(BF16)
    inv = lax.rsqrt(jnp.mean(xt * xt, axis=-1, keepdims=True) + EPS)
    inv_b = jnp.broadcast_to(inv, (tm, LANES))
    inv_row = inv_b.T[0:1, :]
    proj = lambda ref, lo, hi: inv * lax.dot_general(h, ref[lo:hi, :], _NT,
                                                     preferred_element_type=F32)

    n_res = 4
    sub_t = tm // n_res

    def by_residue(ref, start=0):
        return jnp.concatenate(
            [ref[pl.ds(start + res, sub_t, stride=n_res), :] for res in range(n_res)], axis=0)

    for c in range(d // LANES):
        perm_scr[c] = xg[:, c * LANES:(c + 1) * LANES]
    perm_scr[d // LANES] = inv_b
    h4 = jnp.concatenate([by_residue(perm_scr.at[c]) for c in range(d // LANES)],
                         axis=1).astype(BF16)
    inv4 = by_residue(perm_scr.at[d // LANES])[:, 0:1]
    proj4 = lambda lo, hi: inv4 * lax.dot_general(h4, wt_ref[lo:hi, :], _NT,
                                                  preferred_element_type=F32)
    lane =lax.broadcasted_iota(jnp.int32, (1, LANES), 1)

    n_f = wvf_ref.shape[0] - FOX_W

    def v_and_forget_logits():
        vf = lax.dot_general(wvf_ref[...], h, _NT, preferred_element_type=F32) * inv_row
        vt_ref[0] = vf[0:FOX_W].astype(BF16)
        z = vf[FOX_W:FOX_W + n_f] + bf_ref[...]
        row = lax.broadcasted_iota(jnp.int32, (n_f, 1), 0)
        log_f = jnp.where(row < 3 * FOX_HEADS,
                          -(jnp.maximum(-z, 0.0) + jnp.log1p(jnp.exp(-jnp.abs(z)))), 0.0)
        return _split3(log_f)

    def forget_cumsum(parts):
        row = lax.broadcasted_iota(jnp.int32, (tm, tm), 0)
        col = lax.broadcasted_iota(jnp.int32, (tm, tm), 1)
        tri = jnp.where(row <= col, 1.0, 0.0).astype(BF16)
        cum = carry_scr[:, 0:1]
        for part in parts:
            cum = cum + jnp.dot(part, tri, preferred_element_type=F32)
        carry_scr[...] = jnp.broadcast_to(cum[:, tm - 1:tm], carry_scr.shape)
        nf = jnp.concatenate([-LOG2E * cum, jnp.zeros((LANES - n_f, tm), F32)], axis=0).T
        hi, mid, lo = _split3(nf)
        kx_ref[0] = jnp.where(lane % 3 == 0, hi, jnp.where(lane % 3 == 1, mid, lo))

    def mixer_a_qk():
        need("qk")
        qk_ref[0, :, 0:FOX_W] = (proj(wqk_ref, 0, FOX_W) * (Q_SCALE * LOG2E)).astype(BF16)
        qk_ref[0, :, FOX_W:2 * FOX_W] = proj(wqk_ref, FOX_W, 2 * FOX_W).astype(BF16)

    outs = (d0_ref, d1_ref, d2_ref)

    def mixer_b(g):
        need(("b", g))
        dil = DIL_GROUPS[g][1]
        t0 = pl.multiple_of(pl.program_id(1) * tm, tm)
        cos, slo, shi = (by_residue(t, t0) for t in (cos_ref, slo_ref, shi_ref))
        part = lambda t: proj4(t * DIL_W + g * GROUP_W, t * DIL_W + (g + 1) * GROUP_W)
        q, k, v = part(0), part(1), part(2)
        halves = []
        for c in range(GROUP_W // LANES):
            halves.append(_rope(q[:, c * LANES:(c + 1) * LANES], cos, slo, shi) * (Q_SCALE * LOG2E))
        for c in range(GROUP_W // LANES):
            halves.append(_rope(k[:, c * LANES:(c + 1) * LANES], cos, slo, shi))
        halves += [v[:, c * LANES:(c + 1) * LANES] for c in range(GROUP_W // LANES)]
        for res in range(n_res):
            for c, t in enumerate(halves):
                rows_c = t[res * sub_t:(res + 1) * sub_t]
                if dil <= n_res:
                    outs[g][0, res, :, c * LANES:(c + 1) * LANES] = rows_c.astype(BF16)
                else:
                    outs[g][0, c, res] = rows_c

    parts = v_and_forget_logits()
    mixer_b(0)
    forget_cumsum(parts)
    mixer_b(1)
    mixer_b(2)
    mixer_a_qk()


def _inproj_kernel(*refs):
    w_hbm, carry_scr, wqk_ref, wt_ref, stage, sem = refs[2], *refs[15:20]

    @pl.when(pl.program_id(1) == 0)
    def _reset():
        carry_scr[...] = jnp.zeros_like(carry_scr)

    first = (pl.program_id(0) == 0) & (pl.program_id(1) == 0)

    @pl.when(first)
    def _first_step():
        stager = _LazyStager(stage, sem)
        tail = w_hbm.at[3 * FOX_W + FOX_HEADS:3 * FOX_W + FOX_HEADS + wt_ref.shape[0]]
        ready = {}
        for i in range(len(DIL_GROUPS)):
            for t in range(3):
                ready[("b", i)] = stager.queue(tail, wt_ref, t * DIL_W + i * GROUP_W, GROUP_W, 0)
        for r0 in range(0, wqk_ref.shape[0], STAGE_ROWS):
            ready["qk"] = stager.queue(w_hbm.at[0:wqk_ref.shape[0]], wqk_ref, r0, STAGE_ROWS, 0)
        _inproj_body(lambda tag: stager.need(ready[tag]), *refs)

    @pl.when(jnp.logical_not(first))
    def _other_steps():
        _inproj_body(lambda tag: None, *refs)


def _inproj(x, g, w_all, w_vf, b_f, cos, slo, shi):
    B, S, D = x.shape
    n_tail = 3 * DIL_W
    tm = TM_IN
    grid = (B, S // tm)
    tok = lambda w: pl.BlockSpec((1, tm, w), lambda b, i: (b, i, 0))
    tab = _resident(cos.shape)
    out_shape = (
        jax.ShapeDtypeStruct((B, S, 2 * FOX_W), BF16),
        jax.ShapeDtypeStruct((B, FOX_W, S), BF16),
        jax.ShapeDtypeStruct((B, S, LANES), BF16),
        jax.ShapeDtypeStruct((B, 4, S // 4, 3 * GROUP_W), BF16),
        jax.ShapeDtypeStruct((B, 4, S // 4, 3 * GROUP_W), BF16),
        jax.ShapeDtypeStruct((B, 3 * GROUP_W // LANES, 4, S // 4, LANES), F32),
    )
    out_specs = (
        tok(2 * FOX_W),
        pl.BlockSpec((1, FOX_W, tm), lambda b, i: (b, 0, i)),
        tok(LANES),
        pl.BlockSpec((1, 4, tm // 4, 3 * GROUP_W), lambda b, i: (b, 0, i, 0)),
        pl.BlockSpec((1, 4, tm // 4, 3 * GROUP_W), lambda b, i: (b, 0, i, 0)),
        pl.BlockSpec((1, 3 * GROUP_W // LANES, 4, tm // 4, LANES), lambda b, i: (b, 0, 0, i, 0)),
    )
    in_specs = [
        tok(D), _resident(g.shape), pl.BlockSpec(memory_space=pl.ANY), _resident(w_vf.shape),
        _resident(b_f.shape), tab, tab, tab,
    ]
    n_f = w_vf.shape[0] - FOX_W
    return pl.pallas_call(
        _inproj_kernel,
        grid=grid,
        in_specs=in_specs,
        out_specs=out_specs,
        out_shape=out_shape,
        scratch_shapes=[pltpu.VMEM((D // LANES + 1, tm, LANES), F32),
                        pltpu.VMEM((n_f, LANES), F32),
                        pltpu.VMEM((2 * FOX_W, D), BF16), pltpu.VMEM((n_tail, D), BF16),
                        pltpu.VMEM((STAGE_SLOTS, STAGE_ROWS, D), F32),
                        pltpu.SemaphoreType.DMA((STAGE_SLOTS,))],
        compiler_params=pltpu.CompilerParams(
            dimension_semantics=("arbitrary", "arbitrary"), vmem_limit_bytes=VMEM_LIMIT),
        name="inproj",
    )(x, g, w_all, w_vf, b_f, cos, slo, shi)


def _fox_kernel(q_ref, k_ref, kx_ref, vt_ref, o_ref):
    S = q_ref.shape[1]
    pw = 2 * HEAD_DIM
    pairs = q_ref.shape[2] // pw
    step = pl.program_id(1)
    lane = lax.broadcasted_iota(jnp.int32, (1, pw), 1)
    first = lane < HEAD_DIM
    zeros = jnp.zeros((TQ, pw), BF16)

    def sel(head):
        return jnp.broadcast_to(
            jnp.where((lane >= 3 * head) & (lane < 3 * head + 3), 1.0, 0.0), (TQ, pw)).astype(BF16)

    hq = TQ // 2
    key = lax.broadcasted_iota(jnp.int32, (hq, TQ), 0)
    qry = lax.broadcasted_iota(jnp.int32, (hq, TQ), 1) % hq
    causal = key <= qry

    def colmax(parts):
        m = jnp.max(parts[0], axis=0, keepdims=True)
        for t in parts[1:]:
            m = jnp.maximum(m, jnp.max(t, axis=0, keepdims=True))
        return m

    def scores(pair, qi):
        lanes = slice(pair * pw, (pair + 1) * pw)
        head0 = 2 * (pairs * step + pair)
        q2 = q_ref[0, qi * TQ:(qi + 1) * TQ, lanes]
        qa = jnp.concatenate([jnp.where(first, q2, zeros), sel(head0)], axis=1)
        qb = jnp.concatenate([jnp.where(first, zeros, q2), sel(head0 + 1)], axis=1)
        qmix = jnp.concatenate([qa[0:hq], qb[0:hq], qa[hq:TQ], qb[hq:TQ]], axis=0)
        n_k = (qi + 1) * TK
        n_lo = n_k - hq
        kaug = jnp.concatenate([k_ref[0, 0:n_k, lanes], kx_ref[0, 0:n_k, :]], axis=1)
        s = lax.dot_general(kaug[0:n_lo], qmix, _NT, preferred_element_type=F32)
        s_hi = lax.dot_general(kaug[n_lo:n_k], qmix[TQ:2 * TQ], _NT, preferred_element_type=F32)
        full = [s[c * TK:(c + 1) * TK] for c in range(qi)]
        diag = s[qi * TK:n_lo]
        lo = [t[:, 0:TQ] for t in full] + [jnp.where(causal, diag[:, 0:TQ], NEG_INF)]
        hi = ([t[:, TQ:2 * TQ] for t in full] + [diag[:, TQ:2 * TQ]]
              + [jnp.where(causal, s_hi, NEG_INF)])
        return (lo, colmax(lo)), (hi, colmax(hi))

    def outputs(pair, qi, lo, hi):
        for half, (chunks, m) in enumerate((lo, hi)):
            n = (qi + 1) * TK - (hq if half == 0 else 0)
            p = [jnp.exp2(s - m) for s in chunks]
            l = jnp.sum(p[0], axis=0, keepdims=True)
            for t in p[1:]:
                l = l + jnp.sum(t, axis=0, keepdims=True)
            pt = jnp.concatenate([t.astype(BF16) for t in p], axis=0)
            acc = jnp.dot(vt_ref[0, pair * pw:(pair + 1) * pw, 0:n], pt,
                          preferred_element_type=F32)
            ot = (acc / l).T
            r0 = qi * TQ + half * hq
            o_ref[0, r0:r0 + hq, pair * pw:(pair + 1) * pw] = (
                jnp.where(first, ot[0:hq], ot[hq:TQ]).astype(BF16))

    items = [(pair, qi) for pair in range(pairs) for qi in range(S // TQ)]
    nxt = scores(*items[0])
    for i, item in enumerate(items):
        cur = nxt
        if i + 1 < len(items):
            nxt = scores(*items[i + 1])
        outputs(*item, *cur)


def _fox(qk, vt, kx):
    B, S, _ = qk.shape
    pairs = FOX_PAIRS_PER_STEP
    steps = FOX_HEADS // (2 * pairs)
    w = pairs * 2 * HEAD_DIM
    spec = lambda off: pl.BlockSpec((1, S, w), lambda b, p: (b, 0, off + p))
    return pl.pallas_call(
        _fox_kernel,
        grid=(B, steps),
        in_specs=[spec(0), spec(steps),
                  pl.BlockSpec((1, S, LANES), lambda b, p: (b, 0, 0)),
                  pl.BlockSpec((1, w, S), lambda b, p: (b, p, 0))],
        out_specs=pl.BlockSpec((1, S, w), lambda b, p: (b, 0, p)),
        out_shape=jax.ShapeDtypeStruct((B, S, FOX_W), BF16),
        compiler_params=pltpu.CompilerParams(
            dimension_semantics=("arbitrary", "arbitrary"), vmem_limit_bytes=VMEM_LIMIT),
        name="fox",
    )(qk, qk, kx, vt)


def _dil_kernel(d0_ref, d1_ref, d2_ref, o_ref, acc_scr, m_scr, l_scr, nat_scr):
    S = d0_ref.shape[1]
    nh = DIL_HEADS_PER_GROUP
    blk = DIL_BLK
    n_blocks = S // blk
    lane_head = lax.broadcasted_iota(jnp.int32, (1, GROUP_W), 1) // HEAD_DIM
    a2 = lax.broadcasted_iota(jnp.int32, (nh * blk, 2 * blk), 0) % blk
    k2 = lax.broadcasted_iota(jnp.int32, (nh * blk, 2 * blk), 1)
    bias_two = jnp.where((k2 >= a2) & (k2 <= a2 + blk), 0.0, NEG_INF)
    a1 = lax.broadcasted_iota(jnp.int32, (nh * blk, blk), 0) % blk
    k1 = lax.broadcasted_iota(jnp.int32, (nh * blk, blk), 1)
    bias_one = jnp.where(k1 <= a1, 0.0, NEG_INF)

    def assemble(x):
        out = x[(nh - 1) * blk:nh * blk]
        for h in range(nh - 2, -1, -1):
            out = jnp.where(lane_head == h, x[h * blk:(h + 1) * blk], out)
        return out

    def logits(load_q, load_k, bias):
        qs = load_q()
        zero = jnp.zeros_like(qs)
        qstack = jnp.concatenate(
            [jnp.where(lane_head == h, qs, zero) for h in range(nh)], axis=0)
        s = lax.dot_general(qstack, load_k(), _NT, preferred_element_type=F32) + bias
        return s, jnp.max(s, axis=1, keepdims=True)

    def weigh(s, m, load_v):
        p = jnp.exp2(s - m)
        l = jnp.sum(p, axis=1, keepdims=True)
        pv = jnp.dot(p.astype(BF16), load_v(), preferred_element_type=F32)
        return (assemble(pv), assemble(jnp.broadcast_to(m, pv.shape)),
                assemble(jnp.broadcast_to(l, pv.shape)))

    def put(res, pieces, first):
        acc_b, m_b, l_b = res
        for rows, idx in pieces:
            for c in range(GROUP_W // LANES):
                lanes = slice(c * LANES, (c + 1) * LANES)
                if first:
                    acc_scr[c, idx, :] = acc_b[rows, lanes]
                    m_scr[c, idx, :] = m_b[rows, lanes]
                    l_scr[c, idx, :] = l_b[rows, lanes]
                else:
                    m_o = m_scr[c, idx, :]
                    m_n = jnp.maximum(m_o, m_b[rows, lanes])
                    e_o = jnp.exp2(m_o - m_n)
                    e_b = jnp.exp2(m_b[rows, lanes] - m_n)
                    acc_scr[c, idx, :] = acc_scr[c, idx, :] * e_o + acc_b[rows, lanes] * e_b
                    l_scr[c, idx, :] = l_scr[c, idx, :] * e_o + l_b[rows, lanes] * e_b
                    m_scr[c, idx, :] = m_n

    q_l, k_l, v_l = (slice(i * GROUP_W, (i + 1) * GROUP_W) for i in range(3))
    whole = slice(0, blk)
    n_res = 4
    slab = S // n_res
    per_slab = slab // blk

    items = []
    rows_of = lambda ref, r0, n, lanes: (lambda: ref[0, pl.ds(r0, n), lanes])

    for res in range(n_res):
        for pos in range(per_slab):
            r0 = res * slab + pos * blk
            k0, nk, bias = (r0, blk, bias_one) if pos == 0 else (r0 - blk, 2 * blk, bias_two)
            items.append((rows_of(d1_ref, r0, blk, q_l), rows_of(d1_ref, k0, nk, k_l), bias,
                          rows_of(d1_ref, k0, nk, v_l), [(whole, pl.ds(r0, blk))], True))

    sub = blk // n_res
    qa = lax.broadcasted_iota(jnp.int32, (nh * blk, 2 * blk), 0) % blk
    kk = lax.broadcasted_iota(jnp.int32, (nh * blk, 2 * blk), 1)
    tq = blk + n_res * (qa % sub) + qa // sub
    tk = n_res * (kk % (2 * sub)) + kk // (2 * sub)
    bias0_two = jnp.where((tk <= tq) & (tq - tk <= blk), 0.0, NEG_INF)
    qa1 = lax.broadcasted_iota(jnp.int32, (nh * blk, blk), 0) % blk
    kk1 = lax.broadcasted_iota(jnp.int32, (nh * blk, blk), 1)
    bias0_one = jnp.where(n_res * (kk1 % sub) + kk1 // sub <= n_res * (qa1 % sub) + qa1 // sub,
                          0.0, NEG_INF)

    def slabs_of(ref, start, n, lanes):
        return lambda: jnp.concatenate(
            [ref[0, pl.ds(r * slab + start, n), lanes] for r in range(n_res)], axis=0)

    for n in range(n_blocks):
        k0, nk, bias = (0, sub, bias0_one) if n == 0 else ((n - 1) * sub, 2 * sub, bias0_two)
        dst = [(slice(r * sub, (r + 1) * sub), pl.ds(r * slab + n * sub, sub)) for r in range(n_res)]
        items.append((slabs_of(d0_ref, n * sub, sub, q_l), slabs_of(d0_ref, k0, nk, k_l), bias,
                      slabs_of(d0_ref, k0, nk, v_l), dst, False))

    def gather_of(start, chunks):
        return lambda: jnp.concatenate(
            [d2_ref[0, c, pl.ds(start, blk, stride=n_res), :].astype(BF16) for c in chunks], axis=1)

    per_part = GROUP_W // LANES
    for r16 in range(n_blocks):
        start = (r16 % n_res) * slab + r16 // n_res
        dst = [(whole, pl.ds(start, blk, stride=n_res))]
        items.append((gather_of(start, range(0, per_part)),
                      gather_of(start, range(per_part, 2 * per_part)), bias_one,
                      gather_of(start, range(2 * per_part, 3 * per_part)), dst, False))

    nxt = logits(*items[0][0:3])
    for i, (_, _, _, load_v, dst, first_write) in enumerate(items):
        cur = nxt
        if i + 1 < len(items):
            nxt = logits(*items[i + 1][0:3])
        put(weigh(*cur, load_v), dst, first_write)

    def finish(res, _):
        for pos in range(per_slab):
            r0 = _aligned(res * slab + pos * blk, blk)
            for c in range(GROUP_W // LANES):
                nat_scr[c, pl.ds(n_res * pos * blk + res, blk, stride=n_res), :] = (
                    acc_scr[c, pl.ds(r0, blk), :] / l_scr[c, pl.ds(r0, blk), :])
        return 0

    for res in range(n_res):
        finish(res, 0)

    def emit(i, _):
        r0 = _aligned(i * 2 * blk, 2 * blk)
        for c in range(GROUP_W // LANES):
            o_ref[0, pl.ds(r0, 2 * blk), c * LANES:(c + 1) * LANES] = nat_scr[c, pl.ds(r0, 2 * blk), :].astype(BF16)
        return 0

    for i in range(n_blocks // 2):
        emit(i, 0)


def _dilated(d0, d1, d2):
    B, S, W = d0.shape
    spec = pl.BlockSpec((1, S, W), lambda b: (b, 0, 0))
    spec2 = pl.BlockSpec((1,) + d2.shape[1:], lambda b: (b, 0, 0, 0))
    return pl.pallas_call(
        _dil_kernel,
        grid=(B,),
        in_specs=[spec, spec, spec2],
        out_specs=pl.BlockSpec((1, S, GROUP_W), lambda b: (b, 0, 0)),
        out_shape=jax.ShapeDtypeStruct((B, S, GROUP_W), BF16),
        scratch_shapes=[pltpu.VMEM((GROUP_W // LANES, S, LANES), F32)] * 4,
        compiler_params=pltpu.CompilerParams(
            dimension_semantics=("arbitrary",), vmem_limit_bytes=VMEM_LIMIT),
        name="dilated",
    )(d0, d1, d2)


class _LazyStager:
    def __init__(self, stage, sem):
        self.stage, self.sem, self.slots = stage, sem, stage.shape[0]
        self.chunks, self.started, self.done = [], 0, 0

    def queue(self, src, dst, r0, n, c0):
        self.chunks.append((src, dst, r0, n, c0))
        return len(self.chunks)

    def _copy(self, k):
        src, _, r0, n, c0 = self.chunks[k]
        cols = self.stage.shape[2]
        return pltpu.make_async_copy(src.at[r0:r0 + n, c0:c0 + cols],
                                     self.stage.at[k % self.slots, 0:n],
                                     self.sem.at[k % self.slots])

    def _prefetch(self):
        while self.started < min(self.done + self.slots, len(self.chunks)):
            self._copy(self.started).start()
            self.started += 1

    def need(self, count):
        cols = self.stage.shape[2]
        while self.done < count:
            self._prefetch()
            k = self.done
            _, dst, r0, n, c0 = self.chunks[k]
            self._copy(k).wait()
            dst[r0:r0 + n, c0:c0 + cols] = self.stage[k % self.slots, 0:n].astype(BF16)
            self.done += 1
            self._prefetch()


def _outmlp_kernel(x_ref, oa_ref, ob_ref, g1_ref, win_hbm, wa_hbm, wb_hbm, wo_hbm, g2_ref,
                   wu_hbm, wdn_hbm, g3_ref, o_ref,
                   wg_ref, wa_ref, wb_ref, wo_ref, wu_ref, wdn_ref, stage, sem, *, final_norm):
    d = x_ref.shape[1]
    d_ff = wu_ref.shape[1]
    n_ff = d_ff // FF_CHUNK

    def body(need):
        x0 = x_ref[...]
        h0 = (x0 * g1_ref[...]).astype(BF16)
        inv0 = lax.rsqrt(jnp.mean(x0 * x0, axis=-1, keepdims=True) + EPS)
        gate = lambda lo: jax.nn.sigmoid(inv0 * lax.dot_general(
            h0, wg_ref[lo:lo + d, :], _NT, preferred_element_type=F32))
        need("branches")
        ya = jnp.dot(oa_ref[...], wa_ref[...], preferred_element_type=F32)
        yb = jnp.dot(ob_ref[...], wb_ref[...], preferred_element_type=F32)
        need("gates")
        mixed = gate(0) * ya + gate(d) * yb
        need("out")
        x1 = x_ref[...] + jnp.dot(mixed.astype(BF16), wo_ref[...], preferred_element_type=F32)
        h = (x1 * g2_ref[...]).astype(BF16)
        inv = lax.rsqrt(jnp.mean(x1 * x1, axis=-1, keepdims=True) + EPS)
        mlp = None
        for c in range(n_ff):
            need(("up", c))
            u = jnp.dot(h, wu_ref[:, c * FF_CHUNK:(c + 1) * FF_CHUNK], preferred_element_type=F32)
            a = jnp.square(jnp.maximum(u, 0.0)).astype(BF16)
            need(("down", c))
            t = jnp.dot(a, wdn_ref[c * FF_CHUNK:(c + 1) * FF_CHUNK, :], preferred_element_type=F32)
            mlp = t if mlp is None else mlp + t
        x2 = x1 + (inv * inv) * mlp
        o_ref[...] = _rms(x2, g3_ref[...]) if final_norm else x2

    @pl.when(pl.program_id(0) == 0)
    def _first_step():
        stager = _LazyStager(stage, sem)
        ready = {}

        def queue_rows(src, dst, rows, c0=0):
            for r0 in range(rows.start, rows.stop, STAGE_ROWS):
                last = stager.queue(src, dst, r0, min(STAGE_ROWS, rows.stop - r0), c0)
            return last

        queue_rows(wa_hbm, wa_ref, range(0, wa_ref.shape[0]))
        ready["branches"] = queue_rows(wb_hbm, wb_ref, range(0, wb_ref.shape[0]))
        gate_row0 = 3 * FOX_W + FOX_HEADS + 3 * DIL_W
        ready["gates"] = queue_rows(win_hbm.at[gate_row0:gate_row0 + wg_ref.shape[0]], wg_ref,
                                    range(0, wg_ref.shape[0]))
        ready["out"] = queue_rows(wo_hbm, wo_ref, range(0, wo_ref.shape[0]))
        for c in range(n_ff):
            ready[("up", c)] = queue_rows(wu_hbm, wu_ref, range(0, wu_ref.shape[0]), c * FF_CHUNK)
            ready[("down", c)] = queue_rows(wdn_hbm, wdn_ref,
                                            range(c * FF_CHUNK, (c + 1) * FF_CHUNK))
        body(lambda tag: stager.need(ready[tag]))

    @pl.when(pl.program_id(0) != 0)
    def _other_steps():
        body(lambda tag: None)


def _outmlp(x, oa, ob, g1, w_in_t, w_a, w_b, w_o, g2, w_u, w_dn, g3, final_norm):
    N, D = x.shape
    tm = TM_OUT
    tok = lambda w: pl.BlockSpec((tm, w), lambda i: (i, 0))
    hbm = pl.BlockSpec(memory_space=pl.ANY)
    weights = (w_a, w_b, w_o, w_u, w_dn)
    return pl.pallas_call(
        functools.partial(_outmlp_kernel, final_norm=final_norm),
        grid=(N // tm,),
        in_specs=[tok(D), tok(oa.shape[1]), tok(ob.shape[1]), _resident(g1.shape), hbm,
                  hbm, hbm, hbm, _resident(g2.shape), hbm, hbm, _resident(g3.shape)],
        out_specs=tok(D),
        out_shape=jax.ShapeDtypeStruct((N, D), F32),
        scratch_shapes=[pltpu.VMEM((2 * D, D), BF16)] + [pltpu.VMEM(w.shape, BF16) for w in weights]
        + [pltpu.VMEM((STAGE_SLOTS, STAGE_ROWS, D), F32),
           pltpu.SemaphoreType.DMA((STAGE_SLOTS,))],
        compiler_params=pltpu.CompilerParams(
            dimension_semantics=("arbitrary",), vmem_limit_bytes=VMEM_LIMIT),
        name="outmlp",
    )(x, oa, ob, g1, w_in_t, w_a, w_b, w_o, g2, w_u, w_dn, g3)


def _rope_tables(S):
    inv_freq = jnp.power(jnp.float32(ROPE_THETA),
                         -jnp.arange(ROPE_HALF, dtype=F32) * 2.0 / ROPE_DIM)
    ang = jnp.arange(S).astype(F32)[:, None] * inv_freq[None, :]
    cos, sin = jnp.cos(ang), jnp.sin(ang)
    rest = HEAD_DIM - ROPE_DIM
    one = jnp.ones((S, rest), F32)
    zero = lambda w: jnp.zeros((S, w), F32)
    cos_h = jnp.concatenate([cos, cos, one], axis=1)
    slo_h = jnp.concatenate([-sin, zero(ROPE_HALF + rest)], axis=1)
    shi_h = jnp.concatenate([zero(ROPE_HALF), sin, zero(rest)], axis=1)
    rep = lambda t: jnp.tile(t, (1, 128 // HEAD_DIM))
    return rep(cos_h), rep(slo_h), rep(shi_h)


def kernel(x, norm_attn_g, w_in, b_forget, w_branch_a, w_branch_b, w_out, norm_mlp_g, w_up,
           w_down, norm_final_g):
    B, S, D = x.shape
    depth = w_in.shape[0]
    assert S % (16 * DIL_BLK) == 0 and S // 16 == DIL_BLK, "sequence length fixed by the dilation groups"
    cos, slo, shi = _rope_tables(S)
    sizes = (FOX_W, FOX_W, FOX_W, FOX_HEADS, DIL_W, DIL_W, DIL_W, D, D)
    offs = np.concatenate([[0], np.cumsum(sizes)])
    for l in range(depth):
        wt = jnp.swapaxes(w_in[l], 0, 1)
        w_all = wt.astype(F32)
        pad = -(3 * FOX_HEADS) % 16
        w_vf = jnp.concatenate(
            [wt[offs[2]:offs[3]], jnp.repeat(wt[offs[3]:offs[4]], 3, axis=0),
             jnp.zeros((pad, D), wt.dtype)], axis=0).astype(BF16)
        b_f = jnp.pad(jnp.repeat(b_forget[l].astype(F32), 3), (0, pad))[:, None]

        qk, vt, kx, d0, d1, d2 = _inproj(
            x, norm_attn_g[l][None, :], w_all, w_vf, b_f, cos, slo, shi)
        oa = _fox(qk, vt, kx)
        ob = _dilated(d0.reshape(B, S, -1), d1.reshape(B, S, -1),
                      d2.reshape(B, d2.shape[1], S, LANES))
        y = _outmlp(
            x.reshape(B * S, D), oa.reshape(B * S, -1), ob.reshape(B * S, -1),
            norm_attn_g[l][None, :], w_all,
            w_branch_a[l], w_branch_b[l], w_out[l], norm_mlp_g[l][None, :], w_up[l], w_down[l],
            norm_final_g[None, :], final_norm=(l == depth - 1))
        x = y.reshape(B, S, D)
    return x
```
